```python
import jax, jax.numpy as jnp
from jax import lax
import numpy as np

D_MODEL = 1024
BATCH = 4
SEQ = 8192
DEPTH = 1

GRID_W = 64
MEM_LEN = 256
MEM_HEADS = 4
MEM_HEAD_DIM = 128
MEM_DIM = MEM_HEADS * MEM_HEAD_DIM
N_FOURIER_GROUPS = 4
FOURIER_GROUP_DIM = 128
FOURIER_DIM = N_FOURIER_GROUPS * FOURIER_GROUP_DIM
NA_HEADS = 8
NA_HEAD_DIM = 64
NA_DIM = NA_HEADS * NA_HEAD_DIM
NA_ROW_WIN = 8
NA_COL_WIN = 16
N_BRANCHES = 3
IN_PROJ_DIM = FOURIER_DIM + 3 * NA_DIM + MEM_DIM
N_GROUPS = 8
EXPERTS_PER_GROUP = 8
N_EXPERTS = N_GROUPS * EXPERTS_PER_GROUP
TOP_K = 2
D_EXPERT = 512
MOE_BLOCK = 128
ALPHA = (2.0 * DEPTH) ** 0.25
BETA = (8.0 * DEPTH) ** -0.25
LN_EPS = 1e-5

kernel_name = 'hybrid_fnet_natten_memory_hmoe_encoder'


def layer_norm(x, g, b):
    xf = x.astype(jnp.float32)
    mu = xf.mean(-1, keepdims=True)
    var = jnp.square(xf - mu).mean(-1, keepdims=True)
    y = (xf - mu) * lax.rsqrt(var + LN_EPS) * g.astype(jnp.float32) + b.astype(jnp.float32)
    return y.astype(x.dtype)


def fourier_mix(u):
    b, s, _ = u.shape
    ug = u.astype(jnp.float32).reshape(b, s, N_FOURIER_GROUPS, FOURIER_GROUP_DIM)
    y = jnp.fft.fft2(ug, axes=(1, 3), norm='ortho').real
    return y.reshape(b, s, FOURIER_DIM).astype(u.dtype)


def neighbourhood_attention(q, k, v, rpb):
    b, s = q.shape[0], q.shape[1]
    rows = s // GRID_W
    kr = min(NA_ROW_WIN, rows)

    def to_grid(t):
        return t.reshape(b, rows, GRID_W, NA_HEADS, NA_HEAD_DIM).transpose(0, 3, 1, 2, 4)

    qg, kg, vg = to_grid(q), to_grid(k), to_grid(v)
    cols = np.arange(GRID_W)
    col_start = np.clip(cols - NA_COL_WIN // 2, 0, GRID_W - NA_COL_WIN)
    col_idx = col_start[:, None] + np.arange(NA_COL_WIN)[None, :]
    dc_idx = col_idx - cols[:, None] + (NA_COL_WIN - 1)
    bias_c = rpb[:, :, dc_idx]
    scale = NA_HEAD_DIM ** -0.5

    def one_row(r):
        rs = jnp.clip(r - kr // 2, 0, rows - kr)
        q_r = lax.dynamic_index_in_dim(qg, r, axis=2, keepdims=False)
        k_rows = lax.dynamic_slice_in_dim(kg, rs, kr, axis=2)
        v_rows = lax.dynamic_slice_in_dim(vg, rs, kr, axis=2)
        k_win = k_rows[:, :, :, col_idx]
        v_win = v_rows[:, :, :, col_idx]
        dr_idx = rs + jnp.arange(kr) - r + (NA_ROW_WIN - 1)
        bias = bias_c[:, dr_idx].transpose(0, 2, 1, 3)
        sc = jnp.einsum('bhcd,bhrckd->bhcrk', q_r, k_win).astype(jnp.float32) * scale
        sc = sc + bias[None].astype(jnp.float32)
        p = jax.nn.softmax(sc.reshape(b, NA_HEADS, GRID_W, kr * NA_COL_WIN), axis=-1)
        p = p.reshape(b, NA_HEADS, GRID_W, kr, NA_COL_WIN).astype(v.dtype)
        return jnp.einsum('bhcrk,bhrckd->bhcd', p, v_win)

    out = lax.map(one_row, jnp.arange(rows))
    return out.transpose(1, 0, 3, 2, 4).reshape(b, s, NA_DIM)


def memory_attention(q, k, v):
    b, s = q.shape[0], q.shape[1]
    sc = jnp.einsum('bshd,bmhd->bhsm', q, k).astype(jnp.float32) * (MEM_HEAD_DIM ** -0.5)
    p = jax.nn.softmax(sc, axis=-1).astype(v.dtype)
    return jnp.einsum('bhsm,bmhd->bshd', p, v).reshape(b, s, MEM_DIM)


def token_mixing(x, mem, w_in, w_gate, b_gate, w_mem_kv, rpb, w_fourier_o, w_na_o, w_mem_o, w_out):
    b, s, d = x.shape
    proj = x @ w_in
    o1 = FOURIER_DIM
    o2 = o1 + NA_DIM
    o3 = o2 + NA_DIM
    o4 = o3 + NA_DIM
    u_f = proj[..., :o1]
    q_na = proj[..., o1:o2].reshape(b, s, NA_HEADS, NA_HEAD_DIM)
    k_na = proj[..., o2:o3].reshape(b, s, NA_HEADS, NA_HEAD_DIM)
    v_na = proj[..., o3:o4].reshape(b, s, NA_HEADS, NA_HEAD_DIM)
    q_mem = proj[..., o4:].reshape(b, s, MEM_HEADS, MEM_HEAD_DIM)
    kv = mem @ w_mem_kv
    m = mem.shape[1]
    k_mem = kv[..., :MEM_DIM].reshape(b, m, MEM_HEADS, MEM_HEAD_DIM)
    v_mem = kv[..., MEM_DIM:].reshape(b, m, MEM_HEADS, MEM_HEAD_DIM)
    br_f = fourier_mix(u_f) @ w_fourier_o
    br_na = neighbourhood_attention(q_na, k_na, v_na, rpb) @ w_na_o
    br_mem = memory_attention(q_mem, k_mem, v_mem) @ w_mem_o
    gates = jax.nn.sigmoid(x @ w_gate + b_gate).reshape(b, s, N_BRANCHES, d)
    merged = gates[:, :, 0] * br_f + gates[:, :, 1] * br_na + gates[:, :, 2] * br_mem
    return merged @ w_out


def hierarchical_moe(h, w_rg, b_rg, w_re, b_re, w_eg, w_eu, w_ed):
    b, s, d = h.shape
    t = b * s
    tok = h.reshape(t, d)
    tf = tok.astype(jnp.float32)
    group_logits = tf @ w_rg.astype(jnp.float32) + b_rg.astype(jnp.float32)
    group_prob = jax.nn.softmax(group_logits, axis=-1)
    g_idx = jnp.argmax(group_logits, axis=-1).astype(jnp.int32)
    p_group = jnp.take_along_axis(group_prob, g_idx[:, None], axis=1)
    exp_logits = (tf @ w_re.astype(jnp.float32) + b_re.astype(jnp.float32)).reshape(t, N_GROUPS, EXPERTS_PER_GROUP)
    exp_logits = jnp.take_along_axis(exp_logits, g_idx[:, None, None], axis=1)[:, 0]
    top_val, top_idx = lax.top_k(exp_logits, TOP_K)
    weights = (jax.nn.softmax(top_val, axis=-1) * p_group).reshape(-1)
    expert = (g_idx[:, None] * EXPERTS_PER_GROUP + top_idx).reshape(-1).astype(jnp.int32)
    token = jnp.repeat(jnp.arange(t, dtype=jnp.int32), TOP_K)
    n_assign = t * TOP_K
    order = jnp.argsort(expert)
    e_sorted = expert[order]
    counts = jnp.bincount(expert, length=N_EXPERTS)
    start = jnp.cumsum(counts) - counts
    padded = (counts + MOE_BLOCK - 1) // MOE_BLOCK * MOE_BLOCK
    pad_end = jnp.cumsum(padded)
    pad_start = pad_end - padded
    dest = pad_start[e_sorted] + jnp.arange(n_assign, dtype=jnp.int32) - start[e_sorted]
    n_pad = n_assign + N_EXPERTS * MOE_BLOCK
    n_blocks = n_pad // MOE_BLOCK
    pad_tok = jnp.full((n_pad,), t, jnp.int32).at[dest].set(token[order])
    pad_w = jnp.zeros((n_pad,), jnp.float32).at[dest].set(weights[order])
    block_start = jnp.arange(n_blocks, dtype=jnp.int32) * MOE_BLOCK
    block_expert = jnp.clip(jnp.searchsorted(pad_end, block_start, side='right'), 0, N_EXPERTS - 1)
    tok_pad = jnp.concatenate([tok, jnp.zeros((1, d), tok.dtype)], axis=0)
    xg = tok_pad[pad_tok].reshape(n_blocks, MOE_BLOCK, d)

    def expert_block(args):
        xb, e = args
        return (jax.nn.silu(xb @ w_eg[e]) * (xb @ w_eu[e])) @ w_ed[e]

    yb = lax.map(expert_block, (xg, block_expert)).reshape(n_pad, d)
    y = jax.ops.segment_sum(yb * pad_w[:, None].astype(yb.dtype), pad_tok, num_segments=t + 1)[:t]
    return y.reshape(b, s, d)


def setup_inputs(seed: int = 0) -> dict:
    key = jax.random.key(seed)
    ks = jax.random.split(key, 32)
    L = DEPTH
    sd = D_MODEL ** -0.5

    def nrm(k, shape, scale):
        return jax.random.normal(k, shape, jnp.float32) * scale

    x = nrm(ks[0], (BATCH, SEQ, D_MODEL), 1.0)
    mem = nrm(ks[1], (BATCH, MEM_LEN, D_MODEL), 1.0)
    w_in = jnp.concatenate([
        nrm(ks[2], (L, D_MODEL, FOURIER_DIM + 2 * NA_DIM), sd),
        nrm(ks[3], (L, D_MODEL, NA_DIM), sd * BETA),
        nrm(ks[4], (L, D_MODEL, MEM_DIM), sd)], axis=-1)
    w_gate = nrm(ks[5], (L, D_MODEL, N_BRANCHES * D_MODEL), sd)
    b_gate = nrm(ks[6], (L, N_BRANCHES * D_MODEL), 0.02)
    w_mem_kv = jnp.concatenate([
        nrm(ks[7], (L, D_MODEL, MEM_DIM), sd),
        nrm(ks[8], (L, D_MODEL, MEM_DIM), sd * BETA)], axis=-1)
    rpb = nrm(ks[9], (L, NA_HEADS, 2 * NA_ROW_WIN - 1, 2 * NA_COL_WIN - 1), 0.02)
    w_fourier_o = nrm(ks[10], (L, FOURIER_DIM, D_MODEL), FOURIER_DIM ** -0.5 * BETA)
    w_na_o = nrm(ks[11], (L, NA_DIM, D_MODEL), NA_DIM ** -0.5 * BETA)
    w_mem_o = nrm(ks[12], (L, MEM_DIM, D_MODEL), MEM_DIM ** -0.5 * BETA)
    w_out = nrm(ks[13], (L, D_MODEL, D_MODEL), sd * BETA)
    ln1_g = 1.0 + nrm(ks[14], (L, D_MODEL), 0.02)
    ln1_b = nrm(ks[15], (L, D_MODEL), 0.02)
    w_router_group = nrm(ks[16], (L, D_MODEL, N_GROUPS), sd)
    b_router_group = nrm(ks[17], (L, N_GROUPS), 0.01)
    w_router_expert = nrm(ks[18], (L, D_MODEL, N_EXPERTS), sd)
    b_router_expert = nrm(ks[19], (L, N_EXPERTS), 0.01)
    w_exp_gate = nrm(ks[20], (L, N_EXPERTS, D_MODEL, D_EXPERT), sd)
    w_exp_up = nrm(ks[21], (L, N_EXPERTS, D_MODEL, D_EXPERT), sd * BETA)
    w_exp_down = nrm(ks[22], (L, N_EXPERTS, D_EXPERT, D_MODEL), D_EXPERT ** -0.5 * BETA)
    ln2_g = 1.0 + nrm(ks[23], (L, D_MODEL), 0.02)
    ln2_b = nrm(ks[24], (L, D_MODEL), 0.02)
    return {'x': x, 'mem': mem, 'w_in': w_in, 'w_gate': w_gate, 'b_gate': b_gate,
            'w_mem_kv': w_mem_kv, 'rpb': rpb, 'w_fourier_o': w_fourier_o, 'w_na_o': w_na_o,
            'w_mem_o': w_mem_o, 'w_out': w_out, 'ln1_g': ln1_g, 'ln1_b': ln1_b,
            'w_router_group': w_router_group, 'b_router_group': b_router_group,
            'w_router_expert': w_router_expert, 'b_router_expert': b_router_expert,
            'w_exp_gate': w_exp_gate, 'w_exp_up': w_exp_up, 'w_exp_down': w_exp_down,
            'ln2_g': ln2_g, 'ln2_b': ln2_b}


def reference(x, mem, w_in, w_gate, b_gate, w_mem_kv, rpb, w_fourier_o, w_na_o, w_mem_o,
              w_out, ln1_g, ln1_b, w_router_group, b_router_group, w_router_expert,
              b_router_expert, w_exp_gate, w_exp_up, w_exp_down, ln2_g, ln2_b):
    for l in range(DEPTH):
        mix = token_mixing(x, mem, w_in[l], w_gate[l], b_gate[l], w_mem_kv[l], rpb[l],
                           w_fourier_o[l], w_na_o[l], w_mem_o[l], w_out[l])
        h = layer_norm(ALPHA * x + mix, ln1_g[l], ln1_b[l])
        ffn = hierarchical_moe(h, w_router_group[l], b_router_group[l], w_router_expert[l],
                               b_router_expert[l], w_exp_gate[l], w_exp_up[l], w_exp_down[l])
        x = layer_norm(ALPHA * h + ffn, ln2_g[l], ln2_b[l])
    return x
```

```python
import functools

import numpy as np
import jax
import jax.numpy as jnp
from jax import lax
from jax.experimental import pallas as pl
from jax.experimental.pallas import tpu as pltpu

BF = jnp.bfloat16
F32 = jnp.float32
I32 = jnp.int32

GRID_W = 64
MEM_HEADS = 4
MEM_HEAD_DIM = 128
MEM_DIM = MEM_HEADS * MEM_HEAD_DIM
FOURIER_GROUPS = 4
FOURIER_GROUP_DIM = 128
FOURIER_DIM = FOURIER_GROUPS * FOURIER_GROUP_DIM
NA_HEADS = 8
NA_HEAD_DIM = 64
NA_DIM = NA_HEADS * NA_HEAD_DIM
NA_ROW_WIN = 8
NA_COL_WIN = 16
N_GROUPS = 8
EXPERTS_PER_GROUP = 8
N_EXPERTS = N_GROUPS * EXPERTS_PER_GROUP
DEPTH = 1
ALPHA = (2.0 * DEPTH) ** 0.25
LN_EPS = 1e-5
NA_SCALE = NA_HEAD_DIM ** -0.5
MEM_SCALE = MEM_HEAD_DIM ** -0.5
MASK_NEG = -1e30

LANES = 128
FFT_N2 = 128
TM_PROJ = 512
TM_MERGE = 512
TM_ROUTE = 256
TM_DISPATCH = 512
TM_COMBINE = 256
NA_ROWS_PER_STEP = 8
F1_COLS = 8192
F2_SLABS = 4
MOE_BLOCK = 256
ROUTE_LANES = 128
VMEM_LIMIT = 56 * 1024 * 1024


def _cparams(*sem):
    return pltpu.CompilerParams(dimension_semantics=sem, vmem_limit_bytes=VMEM_LIMIT)


def _dot(a, b):
    return jnp.dot(a, b, preferred_element_type=F32)


def _dot_nt(a, b):
    return lax.dot_general(a, b, (((1,), (1,)), ((), ())), preferred_element_type=F32)


def _layer_norm(y, g, b):
    mu = jnp.mean(y, axis=-1, keepdims=True)
    yc = y - mu
    var = jnp.mean(yc * yc, axis=-1, keepdims=True)
    return yc * lax.rsqrt(var + LN_EPS) * g + b


def _memkv_kernel(mem_ref, w_ref, k_ref, v_ref):
    kv = _dot(mem_ref[...].astype(BF), w_ref[...])
    k_ref[...] = kv[:, :MEM_DIM].astype(BF)
    v_ref[...] = kv[:, MEM_DIM:].astype(BF)


def _memkv(mem2d, w_kv, m_len):
    rows, d = mem2d.shape
    return pl.pallas_call(
        _memkv_kernel,
        grid=(rows // m_len,),
        in_specs=[pl.BlockSpec((m_len, d), lambda i: (i, 0)),
                  pl.BlockSpec((d, 2 * MEM_DIM), lambda i: (0, 0))],
        out_specs=[pl.BlockSpec((m_len, MEM_DIM), lambda i: (i, 0))] * 2,
        out_shape=[jax.ShapeDtypeStruct((rows, MEM_DIM), BF)] * 2,
        compiler_params=_cparams("parallel"),
        name="memkv",
    )(mem2d, w_kv)


def _proj_kernel(x_ref, w_ref, km_ref, vm_ref, uf_ref, q_ref, k_ref, v_ref, mo_ref):
    xb = x_ref[...].astype(BF)

    def seg(j):
        return _dot(xb, w_ref[:, j * 512:(j + 1) * 512])

    uf_ref[...] = seg(0).astype(BF)
    q_ref[...] = (seg(1) * NA_SCALE).astype(BF)
    k_ref[...] = seg(2).astype(BF)
    v_ref[...] = seg(3).astype(BF)
    qm = seg(4).astype(BF)
    for h in range(MEM_HEADS):
        sl = slice(h * MEM_HEAD_DIM, (h + 1) * MEM_HEAD_DIM)
        s = _dot_nt(qm[:, sl], km_ref[:, sl]) * MEM_SCALE
        m = jnp.max(s, axis=-1, keepdims=True)
        p = jnp.exp(s - m)
        l = jnp.sum(p, axis=-1, keepdims=True)
        o = _dot(p.astype(BF), vm_ref[:, sl])
        mo_ref[:, sl] = (o / l).astype(BF)


def _proj(x2d, w_in, k_mem, v_mem, seq, m_len):
    t, d = x2d.shape
    tm = TM_PROJ
    tiles_per_batch = seq // tm
    tok = lambda i: (i, 0)
    memb = lambda i: (i // tiles_per_batch, 0)
    return pl.pallas_call(
        _proj_kernel,
        grid=(t // tm,),
        in_specs=[pl.BlockSpec((tm, d), tok),
                  pl.BlockSpec(w_in.shape, lambda i: (0, 0)),
                  pl.BlockSpec((m_len, MEM_DIM), memb),
                  pl.BlockSpec((m_len, MEM_DIM), memb)],
        out_specs=[pl.BlockSpec((tm, 512), tok)] * 5,
        out_shape=[jax.ShapeDtypeStruct((t, 512), BF)] * 5,
        compiler_params=_cparams("parallel"),
        name="proj",
    )(x2d, w_in, k_mem, v_mem)


def _fourier_tables(n1, n2):
    n = n1 * n2
    k1 = np.arange(n1)
    ang1 = 2.0 * np.pi * ((k1[:, None] * k1[None, :]) % n1) / n1
    norm = 1.0 / np.sqrt(float(n) * FOURIER_GROUP_DIM)
    m1 = np.concatenate([np.cos(ang1), -np.sin(ang1)], axis=0) * norm
    kk = k1[:, None, None] + n1 * np.arange(n2)[None, :, None]
    nn = np.arange(n2)[None, None, :]
    ang2 = 2.0 * np.pi * ((kk * nn) % n) / n
    c2, s2 = np.cos(ang2), np.sin(ang2)
    m2 = np.concatenate([np.concatenate([c2, s2], axis=2),
                         np.concatenate([-s2, c2], axis=2)], axis=1)
    c = np.arange(FOURIER_GROUP_DIM)
    angc = 2.0 * np.pi * ((c[:, None] * c[None, :]) % FOURIER_GROUP_DIM) / FOURIER_GROUP_DIM
    mc = np.concatenate([np.cos(angc), np.sin(angc)], axis=0)
    as_bf = lambda a: jnp.asarray(a.astype(np.float32)).astype(BF)
    return as_bf(m1), as_bf(m2), as_bf(mc)


def _f1_kernel(u_ref, m_ref, a_ref):
    n1 = u_ref.shape[0]
    r = _dot(m_ref[...], u_ref[...])
    a_ref[0] = r[:n1].astype(BF)
    a_ref[1] = r[n1:].astype(BF)


def _f2_kernel(a_ref, m2_ref, mc_ref, o_ref):
    slabs = a_ref.shape[1]
    n2 = a_ref.shape[2]
    for t in range(slabs):
        a = jnp.concatenate([a_ref[0, t], a_ref[1, t]], axis=0)
        y = _dot(m2_ref[t], a)
        yr = y[:n2].astype(BF)
        yi = y[n2:].astype(BF)
        for g in range(FOURIER_GROUPS):
            sl = slice(g * LANES, (g + 1) * LANES)
            yy = jnp.concatenate([yr[:, sl], yi[:, sl]], axis=1)
            z = _dot(yy, mc_ref[...])
            o_ref[:, t * FOURIER_DIM + g * LANES: t * FOURIER_DIM + (g + 1) * LANES] = z.astype(BF)


def _fourier(u_f, batch, seq):
    n2 = FFT_N2
    n1 = seq // n2
    m1, m2, mc = _fourier_tables(n1, n2)
    cols = n2 * FOURIER_DIM
    tn = min(F1_COLS, cols)
    a = pl.pallas_call(
        _f1_kernel,
        grid=(batch, cols // tn),
        in_specs=[pl.BlockSpec((None, n1, tn), lambda b, j: (b, 0, j)),
                  pl.BlockSpec((2 * n1, n1), lambda b, j: (0, 0))],
        out_specs=pl.BlockSpec((None, 2, n1, tn), lambda b, j: (b, 0, 0, j)),
        out_shape=jax.ShapeDtypeStruct((batch, 2, n1, cols), BF),
        compiler_params=_cparams("parallel", "parallel"),
        name="fourier1",
    )(u_f.reshape(batch, n1, cols), m1)
    kb = min(F2_SLABS, n1)
    out = pl.pallas_call(
        _f2_kernel,
        grid=(batch, n1 // kb),
        in_specs=[pl.BlockSpec((None, 2, kb, n2, FOURIER_DIM), lambda b, j: (b, 0, j, 0, 0)),
                  pl.BlockSpec((kb, 2 * n2, 2 * n2), lambda b, j: (j, 0, 0)),
                  pl.BlockSpec((2 * FOURIER_GROUP_DIM, FOURIER_GROUP_DIM), lambda b, j: (0, 0))],
        out_specs=pl.BlockSpec((None, n2, kb * FOURIER_DIM), lambda b, j: (b, 0, j)),
        out_shape=jax.ShapeDtypeStruct((batch, n2, n1 * FOURIER_DIM), BF),
        compiler_params=_cparams("parallel", "parallel"),
        name="fourier2",
    )(a.reshape(batch, 2, n1, n2, FOURIER_DIM), m2, mc)
    return out.reshape(batch * seq, FOURIER_DIM)


def _na_bias_table(rpb):
    cols = np.arange(GRID_W)
    cs = np.clip(cols - NA_COL_WIN // 2, 0, GRID_W - NA_COL_WIN)
    kc = np.arange(GRID_W)
    in_win = (kc[None, :] >= cs[:, None]) & (kc[None, :] < cs[:, None] + NA_COL_WIN)
    dc = np.clip(kc[None, :] - cols[:, None] + (NA_COL_WIN - 1), 0, 2 * NA_COL_WIN - 2)
    full = jnp.where(in_win[None, None], rpb.astype(F32)[:, :, dc], MASK_NEG)
    tabs = []
    for d0 in range(NA_ROW_WIN):
        t = full[:, d0:d0 + NA_ROW_WIN]
        tabs.append(t.transpose(0, 2, 1, 3).reshape(NA_HEADS, GRID_W, NA_ROW_WIN * GRID_W))
    return jnp.stack(tabs, axis=0)


def _na_kernel(q_ref, k_ref, v_ref, tab_ref, o_ref, *, n_rows):
    i = pl.program_id(2)
    win = NA_ROW_WIN * GRID_W
    lane = lax.broadcasted_iota(I32, (GRID_W, 2 * NA_HEAD_DIM), 1)
    first_head = lane < NA_HEAD_DIM
    for j in range(NA_ROWS_PER_STEP):
        r = i * NA_ROWS_PER_STEP + j
        rs = jnp.clip(r - NA_ROW_WIN // 2, 0, n_rows - NA_ROW_WIN)
        d0 = rs - r + (NA_ROW_WIN - 1)
        start = pl.multiple_of(rs * GRID_W, GRID_W)
        q2 = q_ref[j * GRID_W:(j + 1) * GRID_W, :]
        kw = k_ref[pl.ds(start, win), :]
        vw = v_ref[pl.ds(start, win), :]
        outs = []
        for h in range(2):
            keep = first_head if h == 0 else jnp.logical_not(first_head)
            qh = jnp.where(keep, q2, jnp.zeros_like(q2))
            s = _dot_nt(qh, kw) + tab_ref[d0, h]
            m = jnp.max(s, axis=-1, keepdims=True)
            p = jnp.exp(s - m)
            l = jnp.sum(p, axis=-1, keepdims=True)
            outs.append(_dot(p.astype(BF), vw) / l)
        o_ref[j * GRID_W:(j + 1) * GRID_W, :] = jnp.where(first_head, outs[0], outs[1]).astype(BF)


def _natten(q, k, v, rpb, batch, seq):
    n_rows = seq // GRID_W
    rb = NA_ROWS_PER_STEP
    nrb = n_rows // rb
    hp = NA_HEADS // 2
    tab = _na_bias_table(rpb)
    qmap = lambda b, p, i: (b * nrb + i, p)
    kmap = lambda b, p, i: (b, p)
    return pl.pallas_call(
        functools.partial(_na_kernel, n_rows=n_rows),
        grid=(batch, hp, nrb),
        in_specs=[pl.BlockSpec((rb * GRID_W, LANES), qmap),
                  pl.BlockSpec((seq, LANES), kmap),
                  pl.BlockSpec((seq, LANES), kmap),
                  pl.BlockSpec((NA_ROW_WIN, 2, GRID_W, NA_ROW_WIN * GRID_W), lambda b, p, i: (0, p, 0, 0))],
        out_specs=pl.BlockSpec((rb * GRID_W, LANES), qmap),
        out_shape=jax.ShapeDtypeStruct((batch * seq, NA_DIM), BF),
        compiler_params=_cparams("parallel", "parallel", "parallel"),
        name="natten",
    )(q, k, v, tab)


def _merge_kernel(x_ref, fo_ref, na_ref, mo_ref, wg_ref, bg_ref, wf_ref, wn_ref, wm_ref, wo_ref,
                  g_ref, b_ref, h_ref):
    d = x_ref.shape[1]
    x = x_ref[...]
    xb = x.astype(BF)
    merged = None
    for j, (br_ref, w_ref) in enumerate(((fo_ref, wf_ref), (na_ref, wn_ref), (mo_ref, wm_ref))):
        z = _dot(xb, wg_ref[:, j * d:(j + 1) * d]) + bg_ref[:, j * d:(j + 1) * d]
        gate = 1.0 / (1.0 + jnp.exp(-z))
        term = gate * _dot(br_ref[...], w_ref[...])
        merged = term if merged is None else merged + term
    mix = _dot(merged.astype(BF), wo_ref[...])
    h_ref[...] = _layer_norm(ALPHA * x + mix, g_ref[...], b_ref[...])


def _merge(x2d, fo, na, mo, w_gate, b_gate, w_fo, w_na, w_mo, w_out, ln_g, ln_b):
    t, d = x2d.shape
    tm = TM_MERGE
    tok = lambda i: (i, 0)
    full = lambda a: pl.BlockSpec(a.shape, lambda i: (0, 0))
    return pl.pallas_call(
        _merge_kernel,
        grid=(t // tm,),
        in_specs=[pl.BlockSpec((tm, d), tok)] + [pl.BlockSpec((tm, 512), tok)] * 3
                 + [full(a) for a in (w_gate, b_gate, w_fo, w_na, w_mo, w_out, ln_g, ln_b)],
        out_specs=pl.BlockSpec((tm, d), tok),
        out_shape=jax.ShapeDtypeStruct((t, d), F32),
        compiler_params=_cparams("parallel"),
        name="merge",
    )(x2d, fo, na, mo, w_gate, b_gate, w_fo, w_na, w_mo, w_out, ln_g, ln_b)


def _router_kernel(h_ref, whi_ref, wlo_ref, b_ref, oi_ref, of_ref, cnt_ref, carry_ref):
    tm = h_ref.shape[0]

    @pl.when(pl.program_id(0) == 0)
    def _():
        carry_ref[...] = jnp.zeros_like(carry_ref)

    h = h_ref[...]
    hh = h.astype(BF)
    hl = (h - hh.astype(F32)).astype(BF)
    logits = _dot(hh, whi_ref[...]) + _dot(hh, wlo_ref[...]) + _dot(hl, whi_ref[...]) + b_ref[...]
    lane = lax.broadcasted_iota(I32, (tm, ROUTE_LANES), 1)
    is_group = lane < N_GROUPS
    gl = jnp.where(is_group, logits, MASK_NEG)
    gmax = jnp.max(gl, axis=-1, keepdims=True)
    g_idx = jnp.min(jnp.where(gl == gmax, lane, ROUTE_LANES), axis=-1, keepdims=True)
    p_group = 1.0 / jnp.sum(jnp.where(is_group, jnp.exp(gl - gmax), 0.0), axis=-1, keepdims=True)
    e_lane = lane - N_GROUPS
    in_group = (e_lane >= 0) & (e_lane < N_EXPERTS) & ((e_lane >> 3) == g_idx)
    el = jnp.where(in_group, logits, MASK_NEG)
    v1 = jnp.max(el, axis=-1, keepdims=True)
    i1 = jnp.min(jnp.where(el == v1, lane, ROUTE_LANES), axis=-1, keepdims=True)
    el2 = jnp.where(lane == i1, MASK_NEG, el)
    v2 = jnp.max(el2, axis=-1, keepdims=True)
    i2 = jnp.min(jnp.where(el2 == v2, lane, ROUTE_LANES), axis=-1, keepdims=True)
    tt = jnp.exp(v2 - v1)
    w1 = p_group / (1.0 + tt)
    w2 = p_group * tt / (1.0 + tt)
    sel1 = lane == i1
    sel2 = lane == i2
    onehot = jnp.where(sel1 | sel2, 1.0, 0.0)
    row = lax.broadcasted_iota(I32, (tm, tm), 0)
    col = lax.broadcasted_iota(I32, (tm, tm), 1)
    tri = jnp.where(row > col, 1.0, 0.0).astype(BF)
    prefix = _dot(tri, onehot.astype(BF)) + carry_ref[...]
    r1 = jnp.sum(jnp.where(sel1, prefix, 0.0), axis=-1, keepdims=True).astype(I32)
    r2 = jnp.sum(jnp.where(sel2, prefix, 0.0), axis=-1, keepdims=True).astype(I32)
    carry_ref[...] += jnp.sum(onehot, axis=0, keepdims=True)
    cnt_ref[...] = carry_ref[...]
    zi = jnp.zeros((tm, ROUTE_LANES), I32)
    oi = jnp.where(lane == 0, i1 - N_GROUPS, zi)
    oi = jnp.where(lane == 1, i2 - N_GROUPS, oi)
    oi = jnp.where(lane == 2, r1, oi)
    oi = jnp.where(lane == 3, r2, oi)
    oi_ref[...] = oi[:, :8]
    of = jnp.where(lane == 0, w1, jnp.where(lane == 1, w2, 0.0))
    of_ref[...] = of[:, :8]


def _router(h, w_rg, b_rg, w_re, b_re):
    t, d = h.shape
    tm = TM_ROUTE
    pad = ROUTE_LANES - N_GROUPS - N_EXPERTS
    w = jnp.concatenate([w_rg.astype(F32), w_re.astype(F32), jnp.zeros((d, pad), F32)], axis=1)
    b = jnp.concatenate([b_rg.astype(F32), b_re.astype(F32), jnp.zeros((pad,), F32)])[None, :]
    w_hi = w.astype(BF)
    w_lo = (w - w_hi.astype(F32)).astype(BF)
    tok = lambda i: (i, 0)
    const = lambda i: (0, 0)
    return pl.pallas_call(
        _router_kernel,
        grid=(t // tm,),
        in_specs=[pl.BlockSpec((tm, d), tok),
                  pl.BlockSpec((d, ROUTE_LANES), const),
                  pl.BlockSpec((d, ROUTE_LANES), const),
                  pl.BlockSpec((1, ROUTE_LANES), const)],
        out_specs=[pl.BlockSpec((tm, 8), tok), pl.BlockSpec((tm, 8), tok),
                   pl.BlockSpec((1, ROUTE_LANES), const)],
        out_shape=[jax.ShapeDtypeStruct((t, 8), I32), jax.ShapeDtypeStruct((t, 8), F32),
                   jax.ShapeDtypeStruct((1, ROUTE_LANES), F32)],
        scratch_shapes=[pltpu.VMEM((1, ROUTE_LANES), F32)],
        compiler_params=_cparams("arbitrary"),
        name="router",
    )(h, w_hi, w_lo, b)


def _row_copy(src_ref, src_row, dst_ref, dst_row, sem):
    return pltpu.make_async_copy(src_ref.at[pl.ds(src_row, 1)], dst_ref.at[pl.ds(dst_row, 1)], sem)


def _dispatch_kernel(d1_ref, d2_ref, h_ref, xg_in_ref, xg_ref, sem):
    del xg_in_ref
    tm = h_ref.shape[0]

    def issue(t, c):
        _row_copy(h_ref, t, xg_ref, d1_ref[t], sem).start()
        _row_copy(h_ref, t, xg_ref, d2_ref[t], sem).start()
        return c

    lax.fori_loop(0, tm, issue, 0)

    def drain(t, c):
        _row_copy(h_ref, 0, xg_ref, 0, sem).wait()
        _row_copy(h_ref, 0, xg_ref, 0, sem).wait()
        return c

    lax.fori_loop(0, tm, drain, 0)


def _dispatch(h, dest1, dest2, n_pad):
    t, d = h.shape
    tm = TM_DISPATCH
    smem = lambda: pl.BlockSpec((tm,), lambda i: (i,), memory_space=pltpu.SMEM)
    return pl.pallas_call(
        _dispatch_kernel,
        grid=(t // tm,),
        in_specs=[smem(), smem(), pl.BlockSpec((tm, d), lambda i: (i, 0)),
                  pl.BlockSpec(memory_space=pl.ANY)],
        out_specs=pl.BlockSpec(memory_space=pl.ANY),
        out_shape=jax.ShapeDtypeStruct((n_pad, d), F32),
        scratch_shapes=[pltpu.SemaphoreType.DMA(())],
        input_output_aliases={3: 0},
        compiler_params=_cparams("arbitrary"),
        name="dispatch",
    )(dest1, dest2, h, jnp.zeros((n_pad, d), F32))


def _expert_kernel(be_ref, nu_ref, x_ref, wg_ref, wu_ref, wd_ref, y_ref, wg_s, wu_s, wd_s):
    j = pl.program_id(0)

    @pl.when(j < nu_ref[0])
    def _():
        prev = be_ref[jnp.maximum(j - 1, 0)]

        @pl.when((j == 0) | (be_ref[j] != prev))
        def _():
            wg_s[...] = wg_ref[...].astype(BF)
            wu_s[...] = wu_ref[...].astype(BF)
            wd_s[...] = wd_ref[...].astype(BF)

        xb = x_ref[...].astype(BF)
        a = _dot(xb, wg_s[...])
        u = _dot(xb, wu_s[...])
        mid = (a / (1.0 + jnp.exp(-a))) * u
        y_ref[...] = _dot(mid.astype(BF), wd_s[...])

    @pl.when(j >= nu_ref[0])
    def _():
        y_ref[...] = jnp.zeros_like(y_ref)


def _experts(xg, blk_expert, n_used, w_eg, w_eu, w_ed):
    n_pad, d = xg.shape
    de = w_eg.shape[2]
    nb = n_pad // MOE_BLOCK
    rows = lambda j, be, nu: (jnp.minimum(j, nu[0] - 1), 0)
    wsel = lambda j, be, nu: (be[j], 0, 0)
    return pl.pallas_call(
        _expert_kernel,
        grid_spec=pltpu.PrefetchScalarGridSpec(
            num_scalar_prefetch=2,
            grid=(nb,),
            in_specs=[pl.BlockSpec((MOE_BLOCK, d), rows),
                      pl.BlockSpec((None, d, de), wsel),
                      pl.BlockSpec((None, d, de), wsel),
                      pl.BlockSpec((None, de, d), wsel)],
            out_specs=pl.BlockSpec((MOE_BLOCK, d), lambda j, be, nu: (j, 0)),
            scratch_shapes=[pltpu.VMEM((d, de), BF), pltpu.VMEM((d, de), BF), pltpu.VMEM((de, d), BF)],
        ),
        out_shape=jax.ShapeDtypeStruct((n_pad, d), F32),
        compiler_params=_cparams("arbitrary"),
        name="experts",
    )(blk_expert, n_used, xg, w_eg, w_eu, w_ed)


def _combine_kernel(d1_ref, d2_ref, wt_ref, h_ref, yb_ref, g_ref, b_ref, o_ref, buf1, buf2, sem):
    tm = h_ref.shape[0]

    def issue(t, c):
        _row_copy(yb_ref, d1_ref[t], buf1, t, sem).start()
        _row_copy(yb_ref, d2_ref[t], buf2, t, sem).start()
        return c

    lax.fori_loop(0, tm, issue, 0)

    def drain(t, c):
        _row_copy(yb_ref, 0, buf1, 0, sem).wait()
        _row_copy(yb_ref, 0, buf2, 0, sem).wait()
        return c

    lax.fori_loop(0, tm, drain, 0)
    wt = wt_ref[...]
    ffn = wt[:, 0:1] * buf1[...] + wt[:, 1:2] * buf2[...]
    o_ref[...] = _layer_norm(ALPHA * h_ref[...] + ffn, g_ref[...], b_ref[...])


def _combine(h, yb, dest1, dest2, wts, ln_g, ln_b):
    t, d = h.shape
    tm = TM_COMBINE
    smem = lambda: pl.BlockSpec((tm,), lambda i: (i,), memory_space=pltpu.SMEM)
    tok = lambda i: (i, 0)
    const = lambda i: (0, 0)
    return pl.pallas_call(
        _combine_kernel,
        grid=(t // tm,),
        in_specs=[smem(), smem(), pl.BlockSpec((tm, 8), tok), pl.BlockSpec((tm, d), tok),
                  pl.BlockSpec(memory_space=pl.ANY),
                  pl.BlockSpec((1, d), const), pl.BlockSpec((1, d), const)],
        out_specs=pl.BlockSpec((tm, d), tok),
        out_shape=jax.ShapeDtypeStruct((t, d), F32),
        scratch_shapes=[pltpu.VMEM((tm, d), F32), pltpu.VMEM((tm, d), F32), pltpu.SemaphoreType.DMA(())],
        compiler_params=_cparams("arbitrary"),
        name="combine",
    )(dest1, dest2, wts, h, yb, ln_g, ln_b)


def _token_mixing(x2d, mem2d, batch, seq, m_len, w_in, w_gate, b_gate, w_mem_kv, rpb, w_fo, w_na, w_mo, w_out,
                  ln_g, ln_b):
    k_mem, v_mem = _memkv(mem2d, w_mem_kv.astype(BF), m_len)
    u_f, q, k, v, mo = _proj(x2d, w_in.astype(BF), k_mem, v_mem, seq, m_len)
    fo = _fourier(u_f, batch, seq)
    na = _natten(q, k, v, rpb, batch, seq)
    return _merge(x2d, fo, na, mo, w_gate.astype(BF), b_gate.astype(F32)[None, :], w_fo.astype(BF),
                  w_na.astype(BF), w_mo.astype(BF), w_out.astype(BF), ln_g.astype(F32)[None, :],
                  ln_b.astype(F32)[None, :])


def _moe(h, w_rg, b_rg, w_re, b_re, w_eg, w_eu, w_ed, ln_g, ln_b):
    t, d = h.shape
    oi, wts, cnt = _router(h, w_rg, b_rg, w_re, b_re)
    counts = cnt[0, N_GROUPS:N_GROUPS + N_EXPERTS].astype(I32)
    padded = (counts + MOE_BLOCK - 1) // MOE_BLOCK * MOE_BLOCK
    pad_end = jnp.cumsum(padded)
    pad_start = pad_end - padded
    dest1 = pad_start[oi[:, 0]] + oi[:, 2]
    dest2 = pad_start[oi[:, 1]] + oi[:, 3]
    n_pad = 2 * t + N_EXPERTS * MOE_BLOCK
    nb = n_pad // MOE_BLOCK
    blk_start = jnp.arange(nb, dtype=I32) * MOE_BLOCK
    blk_expert = jnp.clip(jnp.searchsorted(pad_end, blk_start, side="right"), 0, N_EXPERTS - 1).astype(I32)
    n_used = (pad_end[-1:] // MOE_BLOCK).astype(I32)
    xg = _dispatch(h, dest1, dest2, n_pad)
    yb = _experts(xg, blk_expert, n_used, w_eg, w_eu, w_ed)
    return _combine(h, yb, dest1, dest2, wts, ln_g.astype(F32)[None, :], ln_b.astype(F32)[None, :])


def kernel(x, mem, w_in, w_gate, b_gate, w_mem_kv, rpb, w_fourier_o, w_na_o, w_mem_o, w_out, ln1_g, ln1_b,
           w_router_group, b_router_group, w_router_expert, b_router_expert, w_exp_gate, w_exp_up,
           w_exp_down, ln2_g, ln2_b):
    batch, seq, d = x.shape
    m_len = mem.shape[1]
    x2d = x.reshape(batch * seq, d)
    mem2d = mem.reshape(batch * m_len, d)
    for l in range(w_in.shape[0]):
        h = _token_mixing(x2d, mem2d, batch, seq, m_len, w_in[l], w_gate[l], b_gate[l], w_mem_kv[l], rpb[l],
                          w_fourier_o[l], w_na_o[l], w_mem_o[l], w_out[l], ln1_g[l], ln1_b[l])
        x2d = _moe(h, w_router_group[l], b_router_group[l], w_router_expert[l], b_router_expert[l],
                   w_exp_gate[l], w_exp_up[l], w_exp_down[l], ln2_g[l], ln2_b[l])
    return x2d.reshape(batch, seq, d)
```

```python
import functools

import numpy as np
import jax
import jax.numpy as jnp
from jax import lax
from jax.experimental import pallas as pl
from jax.experimental.pallas import tpu as pltpu

BF = jnp.bfloat16
F32 = jnp.float32
I32 = jnp.int32

GRID_W = 64
MEM_HEADS = 4
MEM_HEAD_DIM = 128
MEM_DIM = MEM_HEADS * MEM_HEAD_DIM
FOURIER_GROUPS = 4
FOURIER_GROUP_DIM = 128
FOURIER_DIM = FOURIER_GROUPS * FOURIER_GROUP_DIM
NA_HEADS = 8
NA_HEAD_DIM = 64
NA_DIM = NA_HEADS * NA_HEAD_DIM
NA_ROW_WIN = 8
NA_COL_WIN = 16
N_GROUPS = 8
EXPERTS_PER_GROUP = 8
N_EXPERTS = N_GROUPS * EXPERTS_PER_GROUP
DEPTH = 1
ALPHA = (2.0 * DEPTH) ** 0.25
LN_EPS = 1e-5
NA_SCALE = NA_HEAD_DIM ** -0.5
MEM_SCALE = MEM_HEAD_DIM ** -0.5
MASK_NEG = -1e30

LANES = 128
FFT_N2 = 128
TM_PROJ = 512
TM_MERGE = 512
TM_ROUTE = 256
TM_DISPATCH = 512
TM_COMBINE = 256
NA_ROWS_PER_STEP = 8
F1_COLS = 8192
F2_SLABS = 4
MOE_BLOCK = 256
ROUTE_LANES = 128
ROW_UNROLL = 8
VMEM_LIMIT = 56 * 1024 * 1024


def _cparams(*sem):
    return pltpu.CompilerParams(dimension_semantics=sem, vmem_limit_bytes=VMEM_LIMIT)


def _dot(a, b):
    return jnp.dot(a, b, preferred_element_type=F32)


def _dot_nt(a, b):
    return lax.dot_general(a, b, (((1,), (1,)), ((), ())), preferred_element_type=F32)


def _layer_norm(y, g, b):
    mu = jnp.mean(y, axis=-1, keepdims=True)
    yc = y - mu
    var = jnp.mean(yc * yc, axis=-1, keepdims=True)
    return yc * lax.rsqrt(var + LN_EPS) * g + b


def _memkv_kernel(mem_ref, w_ref, k_ref, v_ref):
    kv = _dot(mem_ref[...].astype(BF), w_ref[...])
    k_ref[...] = kv[:, :MEM_DIM].astype(BF)
    v_ref[...] = kv[:, MEM_DIM:].astype(BF)


def _memkv(mem2d, w_kv, m_len):
    rows, d = mem2d.shape
    return pl.pallas_call(
        _memkv_kernel,
        grid=(rows // m_len,),
        in_specs=[pl.BlockSpec((m_len, d), lambda i: (i, 0)),
                  pl.BlockSpec((d, 2 * MEM_DIM), lambda i: (0, 0))],
        out_specs=[pl.BlockSpec((m_len, MEM_DIM), lambda i: (i, 0))] * 2,
        out_shape=[jax.ShapeDtypeStruct((rows, MEM_DIM), BF)] * 2,
        compiler_params=_cparams("parallel"),
        name="memkv",
    )(mem2d, w_kv)


def _proj_kernel(x_ref, w_ref, km_ref, vm_ref, uf_ref, q_ref, k_ref, v_ref, mo_ref):
    xb = x_ref[...].astype(BF)

    def seg(j):
        return _dot(xb, w_ref[:, j * 512:(j + 1) * 512])

    uf_ref[...] = seg(0).astype(BF)
    q_ref[...] = (seg(1) * NA_SCALE).astype(BF)
    k_ref[...] = seg(2).astype(BF)
    v_ref[...] = seg(3).astype(BF)
    qm = seg(4).astype(BF)
    for h in range(MEM_HEADS):
        sl = slice(h * MEM_HEAD_DIM, (h + 1) * MEM_HEAD_DIM)
        s = _dot_nt(qm[:, sl], km_ref[:, sl]) * MEM_SCALE
        m = jnp.max(s, axis=-1, keepdims=True)
        p = jnp.exp(s - m)
        l = jnp.sum(p, axis=-1, keepdims=True)
        o = _dot(p.astype(BF), vm_ref[:, sl])
        mo_ref[:, sl] = (o / l).astype(BF)


def _proj(x2d, w_in, k_mem, v_mem, seq, m_len):
    t, d = x2d.shape
    tm = TM_PROJ
    tiles_per_batch = seq // tm
    tok = lambda i: (i, 0)
    memb = lambda i: (i // tiles_per_batch, 0)
    return pl.pallas_call(
        _proj_kernel,
        grid=(t // tm,),
        in_specs=[pl.BlockSpec((tm, d), tok),
                  pl.BlockSpec(w_in.shape, lambda i: (0, 0)),
                  pl.BlockSpec((m_len, MEM_DIM), memb),
                  pl.BlockSpec((m_len, MEM_DIM), memb)],
        out_specs=[pl.BlockSpec((tm, 512), tok)] * 5,
        out_shape=[jax.ShapeDtypeStruct((t, 512), BF)] * 5,
        compiler_params=_cparams("parallel"),
        name="proj",
    )(x2d, w_in, k_mem, v_mem)


def _fourier_tables(n1, n2):
    n = n1 * n2
    k1 = np.arange(n1)
    ang1 = 2.0 * np.pi * ((k1[:, None] * k1[None, :]) % n1) / n1
    norm = 1.0 / np.sqrt(float(n) * FOURIER_GROUP_DIM)
    m1 = np.concatenate([np.cos(ang1), -np.sin(ang1)], axis=0) * norm
    kk = k1[:, None, None] + n1 * np.arange(n2)[None, :, None]
    nn = np.arange(n2)[None, None, :]
    ang2 = 2.0 * np.pi * ((kk * nn) % n) / n
    c2, s2 = np.cos(ang2), np.sin(ang2)
    m2 = np.concatenate([np.concatenate([c2, s2], axis=2),
                         np.concatenate([-s2, c2], axis=2)], axis=1)
    c = np.arange(FOURIER_GROUP_DIM)
    angc = 2.0 * np.pi * ((c[:, None] * c[None, :]) % FOURIER_GROUP_DIM) / FOURIER_GROUP_DIM
    mc = np.concatenate([np.cos(angc), np.sin(angc)], axis=0)
    as_bf = lambda a: jnp.asarray(a.astype(np.float32)).astype(BF)
    return as_bf(m1), as_bf(m2), as_bf(mc)


def _f1_kernel(u_ref, m_ref, a_ref):
    n1 = u_ref.shape[0]
    r = _dot(m_ref[...], u_ref[...])
    a_ref[0] = r[:n1].astype(BF)
    a_ref[1] = r[n1:].astype(BF)


def _f2_kernel(a_ref, m2_ref, mc_ref, o_ref):
    slabs = a_ref.shape[1]
    n2 = a_ref.shape[2]
    for t in range(slabs):
        a = jnp.concatenate([a_ref[0, t], a_ref[1, t]], axis=0)
        y = _dot(m2_ref[t], a)
        yr = y[:n2].astype(BF)
        yi = y[n2:].astype(BF)
        for g in range(FOURIER_GROUPS):
            sl = slice(g * LANES, (g + 1) * LANES)
            yy = jnp.concatenate([yr[:, sl], yi[:, sl]], axis=1)
            z = _dot(yy, mc_ref[...])
            o_ref[:, t * FOURIER_DIM + g * LANES: t * FOURIER_DIM + (g + 1) * LANES] = z.astype(BF)


def _fourier(u_f, batch, seq):
    n2 = FFT_N2
    n1 = seq // n2
    m1, m2, mc = _fourier_tables(n1, n2)
    cols = n2 * FOURIER_DIM
    tn = min(F1_COLS, cols)
    a = pl.pallas_call(
        _f1_kernel,
        grid=(batch, cols // tn),
        in_specs=[pl.BlockSpec((None, n1, tn), lambda b, j: (b, 0, j)),
                  pl.BlockSpec((2 * n1, n1), lambda b, j: (0, 0))],
        out_specs=pl.BlockSpec((None, 2, n1, tn), lambda b, j: (b, 0, 0, j)),
        out_shape=jax.ShapeDtypeStruct((batch, 2, n1, cols), BF),
        compiler_params=_cparams("parallel", "parallel"),
        name="fourier1",
    )(u_f.reshape(batch, n1, cols), m1)
    kb = min(F2_SLABS, n1)
    out = pl.pallas_call(
        _f2_kernel,
        grid=(batch, n1 // kb),
        in_specs=[pl.BlockSpec((None, 2, kb, n2, FOURIER_DIM), lambda b, j: (b, 0, j, 0, 0)),
                  pl.BlockSpec((kb, 2 * n2, 2 * n2), lambda b, j: (j, 0, 0)),
                  pl.BlockSpec((2 * FOURIER_GROUP_DIM, FOURIER_GROUP_DIM), lambda b, j: (0, 0))],
        out_specs=pl.BlockSpec((None, n2, kb * FOURIER_DIM), lambda b, j: (b, 0, j)),
        out_shape=jax.ShapeDtypeStruct((batch, n2, n1 * FOURIER_DIM), BF),
        compiler_params=_cparams("parallel", "parallel"),
        name="fourier2",
    )(a.reshape(batch, 2, n1, n2, FOURIER_DIM), m2, mc)
    return out.reshape(batch * seq, FOURIER_DIM)


def _na_bias_table(rpb):
    cols = np.arange(GRID_W)
    cs = np.clip(cols - NA_COL_WIN // 2, 0, GRID_W - NA_COL_WIN)
    kc = np.arange(GRID_W)
    in_win = (kc[None, :] >= cs[:, None]) & (kc[None, :] < cs[:, None] + NA_COL_WIN)
    dc = np.clip(kc[None, :] - cols[:, None] + (NA_COL_WIN - 1), 0, 2 * NA_COL_WIN - 2)
    full = jnp.where(in_win[None, None], rpb.astype(F32)[:, :, dc], MASK_NEG)
    tabs = []
    for d0 in range(NA_ROW_WIN):
        t = full[:, d0:d0 + NA_ROW_WIN]
        tabs.append(t.transpose(0, 2, 1, 3).reshape(NA_HEADS, GRID_W, NA_ROW_WIN * GRID_W))
    return jnp.stack(tabs, axis=0)


def _na_kernel(q_ref, k_ref, v_ref, tab_ref, o_ref, p_buf, l_buf, *, n_rows):
    i = pl.program_id(2)
    win = NA_ROW_WIN * GRID_W
    lane = lax.broadcasted_iota(I32, (GRID_W, 2 * NA_HEAD_DIM), 1)
    first_head = lane < NA_HEAD_DIM
    starts = []
    for j in range(NA_ROWS_PER_STEP):
        r = i * NA_ROWS_PER_STEP + j
        rs = jnp.clip(r - NA_ROW_WIN // 2, 0, n_rows - NA_ROW_WIN)
        d0 = rs - r + (NA_ROW_WIN - 1)
        start = pl.multiple_of(rs * GRID_W, GRID_W)
        starts.append(start)
        q2 = q_ref[j * GRID_W:(j + 1) * GRID_W, :]
        kw = k_ref[pl.ds(start, win), :]
        for h in range(2):
            keep = first_head if h == 0 else jnp.logical_not(first_head)
            qh = jnp.where(keep, q2, jnp.zeros_like(q2))
            s = _dot_nt(qh, kw) + tab_ref[d0, h]
            m = jnp.max(s, axis=-1, keepdims=True)
            p = jnp.exp(s - m)
            l_buf[2 * j + h] = jnp.sum(p, axis=-1, keepdims=True)
            p_buf[2 * j + h] = p.astype(BF)
    for j in range(NA_ROWS_PER_STEP):
        vw = v_ref[pl.ds(starts[j], win), :]
        o0 = _dot(p_buf[2 * j], vw) / l_buf[2 * j]
        o1 = _dot(p_buf[2 * j + 1], vw) / l_buf[2 * j + 1]
        o_ref[j * GRID_W:(j + 1) * GRID_W, :] = jnp.where(first_head, o0, o1).astype(BF)


def _natten(q, k, v, rpb, batch, seq):
    n_rows = seq // GRID_W
    rb = NA_ROWS_PER_STEP
    nrb = n_rows // rb
    hp = NA_HEADS // 2
    tab = _na_bias_table(rpb)
    qmap = lambda b, p, i: (b * nrb + i, p)
    kmap = lambda b, p, i: (b, p)
    return pl.pallas_call(
        functools.partial(_na_kernel, n_rows=n_rows),
        grid=(batch, hp, nrb),
        in_specs=[pl.BlockSpec((rb * GRID_W, LANES), qmap),
                  pl.BlockSpec((seq, LANES), kmap),
                  pl.BlockSpec((seq, LANES), kmap),
                  pl.BlockSpec((NA_ROW_WIN, 2, GRID_W, NA_ROW_WIN * GRID_W), lambda b, p, i: (0, p, 0, 0))],
        out_specs=pl.BlockSpec((rb * GRID_W, LANES), qmap),
        out_shape=jax.ShapeDtypeStruct((batch * seq, NA_DIM), BF),
        scratch_shapes=[pltpu.VMEM((2 * rb, GRID_W, NA_ROW_WIN * GRID_W), BF),
                        pltpu.VMEM((2 * rb, GRID_W, 1), F32)],
        compiler_params=_cparams("parallel", "parallel", "parallel"),
        name="natten",
    )(q, k, v, tab)


def _merge_kernel(x_ref, fo_ref, na_ref, mo_ref, wg_ref, bg_ref, wf_ref, wn_ref, wm_ref, wo_ref,
                  g_ref, b_ref, h_ref):
    d = x_ref.shape[1]
    x = x_ref[...]
    xb = x.astype(BF)
    merged = None
    for j, (br_ref, w_ref) in enumerate(((fo_ref, wf_ref), (na_ref, wn_ref), (mo_ref, wm_ref))):
        z = _dot(xb, wg_ref[:, j * d:(j + 1) * d]) + bg_ref[:, j * d:(j + 1) * d]
        gate = 1.0 / (1.0 + jnp.exp(-z))
        term = gate * _dot(br_ref[...], w_ref[...])
        merged = term if merged is None else merged + term
    mix = _dot(merged.astype(BF), wo_ref[...])
    h_ref[...] = _layer_norm(ALPHA * x + mix, g_ref[...], b_ref[...])


def _merge(x2d, fo, na, mo, w_gate, b_gate, w_fo, w_na, w_mo, w_out, ln_g, ln_b):
    t, d = x2d.shape
    tm = TM_MERGE
    tok = lambda i: (i, 0)
    full = lambda a: pl.BlockSpec(a.shape, lambda i: (0, 0))
    return pl.pallas_call(
        _merge_kernel,
        grid=(t // tm,),
        in_specs=[pl.BlockSpec((tm, d), tok)] + [pl.BlockSpec((tm, 512), tok)] * 3
                 + [full(a) for a in (w_gate, b_gate, w_fo, w_na, w_mo, w_out, ln_g, ln_b)],
        out_specs=pl.BlockSpec((tm, d), tok),
        out_shape=jax.ShapeDtypeStruct((t, d), F32),
        compiler_params=_cparams("parallel"),
        name="merge",
    )(x2d, fo, na, mo, w_gate, b_gate, w_fo, w_na, w_mo, w_out, ln_g, ln_b)


def _router_kernel(h_ref, whi_ref, wlo_ref, b_ref, oi_ref, of_ref, cnt_ref, carry_ref):
    tm = h_ref.shape[0]

    @pl.when(pl.program_id(0) == 0)
    def _():
        carry_ref[...] = jnp.zeros_like(carry_ref)

    h = h_ref[...]
    hh = h.astype(BF)
    hl = (h - hh.astype(F32)).astype(BF)
    logits = _dot(hh, whi_ref[...]) + _dot(hh, wlo_ref[...]) + _dot(hl, whi_ref[...]) + b_ref[...]
    lane = lax.broadcasted_iota(I32, (tm, ROUTE_LANES), 1)
    is_group = lane < N_GROUPS
    gl = jnp.where(is_group, logits, MASK_NEG)
    gmax = jnp.max(gl, axis=-1, keepdims=True)
    g_idx = jnp.min(jnp.where(gl == gmax, lane, ROUTE_LANES), axis=-1, keepdims=True)
    p_group = 1.0 / jnp.sum(jnp.where(is_group, jnp.exp(gl - gmax), 0.0), axis=-1, keepdims=True)
    e_lane = lane - N_GROUPS
    in_group = (e_lane >= 0) & (e_lane < N_EXPERTS) & ((e_lane >> 3) == g_idx)
    el = jnp.where(in_group, logits, MASK_NEG)
    v1 = jnp.max(el, axis=-1, keepdims=True)
    i1 = jnp.min(jnp.where(el == v1, lane, ROUTE_LANES), axis=-1, keepdims=True)
    el2 = jnp.where(lane == i1, MASK_NEG, el)
    v2 = jnp.max(el2, axis=-1, keepdims=True)
    i2 = jnp.min(jnp.where(el2 == v2, lane, ROUTE_LANES), axis=-1, keepdims=True)
    tt = jnp.exp(v2 - v1)
    w1 = p_group / (1.0 + tt)
    w2 = p_group * tt / (1.0 + tt)
    sel1 = lane == i1
    sel2 = lane == i2
    onehot = jnp.where(sel1 | sel2, 1.0, 0.0)
    row = lax.broadcasted_iota(I32, (tm, tm), 0)
    col = lax.broadcasted_iota(I32, (tm, tm), 1)
    tri = jnp.where(row > col, 1.0, 0.0).astype(BF)
    prefix = _dot(tri, onehot.astype(BF)) + carry_ref[...]
    r1 = jnp.sum(jnp.where(sel1, prefix, 0.0), axis=-1, keepdims=True)
    r2 = jnp.sum(jnp.where(sel2, prefix, 0.0), axis=-1, keepdims=True)
    carry_ref[...] += jnp.sum(onehot, axis=0, keepdims=True)
    cnt_ref[...] = carry_ref[...]
    oi = jnp.where(lane == 0, (i1 - N_GROUPS).astype(F32), 0.0)
    oi = jnp.where(lane == 1, (i2 - N_GROUPS).astype(F32), oi)
    oi = jnp.where(lane == 2, r1, oi)
    oi = jnp.where(lane == 3, r2, oi)
    oi_ref[...] = jnp.transpose(oi)[:8, :].astype(I32)
    of = jnp.where(lane == 0, w1, jnp.where(lane == 1, w2, 0.0))
    of_ref[...] = of[:, :8]


def _router(h, w_rg, b_rg, w_re, b_re):
    t, d = h.shape
    tm = TM_ROUTE
    pad = ROUTE_LANES - N_GROUPS - N_EXPERTS
    w = jnp.concatenate([w_rg.astype(F32), w_re.astype(F32), jnp.zeros((d, pad), F32)], axis=1)
    b = jnp.concatenate([b_rg.astype(F32), b_re.astype(F32), jnp.zeros((pad,), F32)])[None, :]
    w_hi = w.astype(BF)
    w_lo = (w - w_hi.astype(F32)).astype(BF)
    tok = lambda i: (i, 0)
    const = lambda i: (0, 0)
    return pl.pallas_call(
        _router_kernel,
        grid=(t // tm,),
        in_specs=[pl.BlockSpec((tm, d), tok),
                  pl.BlockSpec((d, ROUTE_LANES), const),
                  pl.BlockSpec((d, ROUTE_LANES), const),
                  pl.BlockSpec((1, ROUTE_LANES), const)],
        out_specs=[pl.BlockSpec((8, tm), lambda i: (0, i)), pl.BlockSpec((tm, 8), tok),
                   pl.BlockSpec((1, ROUTE_LANES), const)],
        out_shape=[jax.ShapeDtypeStruct((8, t), I32), jax.ShapeDtypeStruct((t, 8), F32),
                   jax.ShapeDtypeStruct((1, ROUTE_LANES), F32)],
        scratch_shapes=[pltpu.VMEM((1, ROUTE_LANES), F32)],
        compiler_params=_cparams("arbitrary"),
        name="router",
    )(h, w_hi, w_lo, b)


def _row_copy(src_ref, src_row, dst_ref, dst_row, sem):
    return pltpu.make_async_copy(src_ref.at[pl.ds(src_row, 1)], dst_ref.at[pl.ds(dst_row, 1)], sem)


def _dest_row(ps_ref, oi_ref, slot, t):
    return ps_ref[oi_ref[slot, t]] + oi_ref[2 + slot, t]


def _dispatch_kernel(ps_ref, pe_ref, oi_ref, h_ref, xg_ref, zbuf, sem, zsem):
    tm = h_ref.shape[0]
    n_blocks = xg_ref.shape[0] // MOE_BLOCK

    @pl.when(pl.program_id(0) == 0)
    def _():
        zbuf[...] = jnp.zeros_like(zbuf)

        def zero_copy(row):
            return pltpu.make_async_copy(zbuf, xg_ref.at[pl.ds(pl.multiple_of(row, MOE_BLOCK), MOE_BLOCK)], zsem)

        def seg_issue(e, c):
            @pl.when(pe_ref[e] > ps_ref[e])
            def _():
                zero_copy(pe_ref[e] - MOE_BLOCK).start()
            return c

        def seg_drain(e, c):
            @pl.when(pe_ref[e] > ps_ref[e])
            def _():
                zero_copy(pe_ref[e] - MOE_BLOCK).wait()
            return c

        def tail_issue(b, c):
            zero_copy(b * MOE_BLOCK).start()
            return c

        def tail_drain(b, c):
            zero_copy(b * MOE_BLOCK).wait()
            return c

        first_tail = pe_ref[N_EXPERTS - 1] // MOE_BLOCK
        lax.fori_loop(0, N_EXPERTS, seg_issue, 0)
        lax.fori_loop(first_tail, n_blocks, tail_issue, 0)
        lax.fori_loop(0, N_EXPERTS, seg_drain, 0)
        lax.fori_loop(first_tail, n_blocks, tail_drain, 0)

    def issue(t, c):
        _row_copy(h_ref, t, xg_ref, _dest_row(ps_ref, oi_ref, 0, t), sem).start()
        _row_copy(h_ref, t, xg_ref, _dest_row(ps_ref, oi_ref, 1, t), sem).start()
        return c

    lax.fori_loop(0, tm, issue, 0, unroll=ROW_UNROLL)

    def drain(t, c):
        _row_copy(h_ref, 0, xg_ref, 0, sem).wait()
        _row_copy(h_ref, 0, xg_ref, 0, sem).wait()
        return c

    lax.fori_loop(0, tm, drain, 0, unroll=ROW_UNROLL)


def _dispatch(h, oi, pad_start, pad_end, n_pad):
    t, d = h.shape
    tm = TM_DISPATCH
    return pl.pallas_call(
        _dispatch_kernel,
        grid_spec=pltpu.PrefetchScalarGridSpec(
            num_scalar_prefetch=2,
            grid=(t // tm,),
            in_specs=[pl.BlockSpec((8, tm), lambda i, ps, pe: (0, i), memory_space=pltpu.SMEM),
                      pl.BlockSpec((tm, d), lambda i, ps, pe: (i, 0))],
            out_specs=pl.BlockSpec(memory_space=pl.ANY),
            scratch_shapes=[pltpu.VMEM((MOE_BLOCK, d), F32), pltpu.SemaphoreType.DMA(()),
                            pltpu.SemaphoreType.DMA(())],
        ),
        out_shape=jax.ShapeDtypeStruct((n_pad, d), F32),
        compiler_params=_cparams("arbitrary"),
        name="dispatch",
    )(pad_start, pad_end, oi, h)


def _expert_kernel(be_ref, nu_ref, x_ref, wg_ref, wu_ref, wd_ref, y_ref, wg_s, wu_s, wd_s):
    j = pl.program_id(0)

    @pl.when(j < nu_ref[0])
    def _():
        prev = be_ref[jnp.maximum(j - 1, 0)]

        @pl.when((j == 0) | (be_ref[j] != prev))
        def _():
            wg_s[...] = wg_ref[...].astype(BF)
            wu_s[...] = wu_ref[...].astype(BF)
            wd_s[...] = wd_ref[...].astype(BF)

        xb = x_ref[...].astype(BF)
        a = _dot(xb, wg_s[...])
        u = _dot(xb, wu_s[...])
        mid = (a / (1.0 + jnp.exp(-a))) * u
        y_ref[...] = _dot(mid.astype(BF), wd_s[...])

    @pl.when(j >= nu_ref[0])
    def _():
        y_ref[...] = jnp.zeros_like(y_ref)


def _experts(xg, blk_expert, n_used, w_eg, w_eu, w_ed):
    n_pad, d = xg.shape
    de = w_eg.shape[2]
    nb = n_pad // MOE_BLOCK
    rows = lambda j, be, nu: (jnp.minimum(j, nu[0] - 1), 0)
    wsel = lambda j, be, nu: (be[j], 0, 0)
    return pl.pallas_call(
        _expert_kernel,
        grid_spec=pltpu.PrefetchScalarGridSpec(
            num_scalar_prefetch=2,
            grid=(nb,),
            in_specs=[pl.BlockSpec((MOE_BLOCK, d), rows),
                      pl.BlockSpec((None, d, de), wsel),
                      pl.BlockSpec((None, d, de), wsel),
                      pl.BlockSpec((None, de, d), wsel)],
            out_specs=pl.BlockSpec((MOE_BLOCK, d), lambda j, be, nu: (j, 0)),
            scratch_shapes=[pltpu.VMEM((d, de), BF), pltpu.VMEM((d, de), BF), pltpu.VMEM((de, d), BF)],
        ),
        out_shape=jax.ShapeDtypeStruct((n_pad, d), F32),
        compiler_params=_cparams("arbitrary"),
        name="experts",
    )(blk_expert, n_used, xg, w_eg, w_eu, w_ed)


def _combine_kernel(ps_ref, oi_ref, wt_ref, h_ref, yb_ref, g_ref, b_ref, o_ref, buf1, buf2, sem):
    tm = h_ref.shape[0]

    def issue(t, c):
        _row_copy(yb_ref, _dest_row(ps_ref, oi_ref, 0, t), buf1, t, sem).start()
        _row_copy(yb_ref, _dest_row(ps_ref, oi_ref, 1, t), buf2, t, sem).start()
        return c

    lax.fori_loop(0, tm, issue, 0, unroll=ROW_UNROLL)

    def drain(t, c):
        _row_copy(yb_ref, 0, buf1, 0, sem).wait()
        _row_copy(yb_ref, 0, buf2, 0, sem).wait()
        return c

    lax.fori_loop(0, tm, drain, 0, unroll=ROW_UNROLL)
    wt = wt_ref[...]
    ffn = wt[:, 0:1] * buf1[...] + wt[:, 1:2] * buf2[...]
    o_ref[...] = _layer_norm(ALPHA * h_ref[...] + ffn, g_ref[...], b_ref[...])


def _combine(h, yb, oi, pad_start, wts, ln_g, ln_b):
    t, d = h.shape
    tm = TM_COMBINE
    tok = lambda i, ps: (i, 0)
    const = lambda i, ps: (0, 0)
    return pl.pallas_call(
        _combine_kernel,
        grid_spec=pltpu.PrefetchScalarGridSpec(
            num_scalar_prefetch=1,
            grid=(t // tm,),
            in_specs=[pl.BlockSpec((8, tm), lambda i, ps: (0, i), memory_space=pltpu.SMEM),
                      pl.BlockSpec((tm, 8), tok), pl.BlockSpec((tm, d), tok),
                      pl.BlockSpec(memory_space=pl.ANY),
                      pl.BlockSpec((1, d), const), pl.BlockSpec((1, d), const)],
            out_specs=pl.BlockSpec((tm, d), tok),
            scratch_shapes=[pltpu.VMEM((tm, d), F32), pltpu.VMEM((tm, d), F32), pltpu.SemaphoreType.DMA(())],
        ),
        out_shape=jax.ShapeDtypeStruct((t, d), F32),
        compiler_params=_cparams("arbitrary"),
        name="combine",
    )(pad_start, oi, wts, h, yb, ln_g, ln_b)


def _token_mixing(x2d, mem2d, batch, seq, m_len, w_in, w_gate, b_gate, w_mem_kv, rpb, w_fo, w_na, w_mo, w_out,
                  ln_g, ln_b):
    k_mem, v_mem = _memkv(mem2d, w_mem_kv.astype(BF), m_len)
    u_f, q, k, v, mo = _proj(x2d, w_in.astype(BF), k_mem, v_mem, seq, m_len)
    fo = _fourier(u_f, batch, seq)
    na = _natten(q, k, v, rpb, batch, seq)
    return _merge(x2d, fo, na, mo, w_gate.astype(BF), b_gate.astype(F32)[None, :], w_fo.astype(BF),
                  w_na.astype(BF), w_mo.astype(BF), w_out.astype(BF), ln_g.astype(F32)[None, :],
                  ln_b.astype(F32)[None, :])


def _moe(h, w_rg, b_rg, w_re, b_re, w_eg, w_eu, w_ed, ln_g, ln_b):
    t, d = h.shape
    oi, wts, cnt = _router(h, w_rg, b_rg, w_re, b_re)
    counts = cnt[0, N_GROUPS:N_GROUPS + N_EXPERTS].astype(I32)
    padded = (counts + MOE_BLOCK - 1) // MOE_BLOCK * MOE_BLOCK
    pad_end = jnp.cumsum(padded).astype(I32)
    pad_start = pad_end - padded
    n_pad = 2 * t + N_EXPERTS * MOE_BLOCK
    nb = n_pad // MOE_BLOCK
    blk_start = jnp.arange(nb, dtype=I32) * MOE_BLOCK
    blk_expert = jnp.minimum(jnp.sum((pad_end[None, :] <= blk_start[:, None]).astype(I32), axis=1), N_EXPERTS - 1)
    n_used = pad_end[-1:] // MOE_BLOCK
    xg = _dispatch(h, oi, pad_start, pad_end, n_pad)
    yb = _experts(xg, blk_expert, n_used, w_eg, w_eu, w_ed)
    return _combine(h, yb, oi, pad_start, wts, ln_g.astype(F32)[None, :], ln_b.astype(F32)[None, :])


def kernel(x, mem, w_in, w_gate, b_gate, w_mem_kv, rpb, w_fourier_o, w_na_o, w_mem_o, w_out, ln1_g, ln1_b,
           w_router_group, b_router_group, w_router_expert, b_router_expert, w_exp_gate, w_exp_up,
           w_exp_down, ln2_g, ln2_b):
    batch, seq, d = x.shape
    m_len = mem.shape[1]
    x2d = x.reshape(batch * seq, d)
    mem2d = mem.reshape(batch * m_len, d)
    for l in range(w_in.shape[0]):
        h = _token_mixing(x2d, mem2d, batch, seq, m_len, w_in[l], w_gate[l], b_gate[l], w_mem_kv[l], rpb[l],
                          w_fourier_o[l], w_na_o[l], w_mem_o[l], w_out[l], ln1_g[l], ln1_b[l])
        x2d = _moe(h, w_router_group[l], b_router_group[l], w_router_expert[l], b_router_expert[l],
                   w_exp_gate[l], w_exp_up[l], w_exp_down[l], ln2_g[l], ln2_b[l])
    return x2d.reshape(batch, seq, d)
```

```python
import functools

import numpy as np
import jax
import jax.numpy as jnp
from jax import lax
from jax.experimental import pallas as pl
from jax.experimental.pallas import tpu as pltpu

BF = jnp.bfloat16
F32 = jnp.float32
I32 = jnp.int32

GRID_W = 64
MEM_HEADS = 4
MEM_HEAD_DIM = 128
MEM_DIM = MEM_HEADS * MEM_HEAD_DIM
FOURIER_GROUPS = 4
FOURIER_GROUP_DIM = 128
FOURIER_DIM = FOURIER_GROUPS * FOURIER_GROUP_DIM
NA_HEADS = 8
NA_HEAD_DIM = 64
NA_DIM = NA_HEADS * NA_HEAD_DIM
NA_ROW_WIN = 8
NA_COL_WIN = 16
N_GROUPS = 8
EXPERTS_PER_GROUP = 8
N_EXPERTS = N_GROUPS * EXPERTS_PER_GROUP
DEPTH = 1
ALPHA = (2.0 * DEPTH) ** 0.25
LN_EPS = 1e-5
NA_SCALE = NA_HEAD_DIM ** -0.5
MEM_SCALE = MEM_HEAD_DIM ** -0.5
MASK_NEG = -1e30

LANES = 128
SUBLANES = 8
FFT_N2 = 128
TM_PROJ = 512
TM_MERGE = 512
TM_ROUTE = 256
TM_DISPATCH = 512
TM_COMBINE = 256
NA_ROWS_PER_STEP = 8
F1_COLS = 8192
F2_SLABS = 4
MOE_BLOCK = 256
ROUTE_LANES = 128
ROUTE_STEPS_PER_PLANE = SUBLANES * LANES // TM_ROUTE
ROW_GROUP = 8
VMEM_LIMIT = 56 * 1024 * 1024


def _cparams(*sem):
    return pltpu.CompilerParams(dimension_semantics=sem, vmem_limit_bytes=VMEM_LIMIT)


def _dot(a, b):
    return jnp.dot(a, b, preferred_element_type=F32)


def _dot_nt(a, b):
    return lax.dot_general(a, b, (((1,), (1,)), ((), ())), preferred_element_type=F32)


def _layer_norm(y, g, b):
    mu = jnp.mean(y, axis=-1, keepdims=True)
    yc = y - mu
    var = jnp.mean(yc * yc, axis=-1, keepdims=True)
    return yc * lax.rsqrt(var + LN_EPS) * g + b


def _store_tile_rows(ref, val):
    n = val.shape[0]
    for s in range(SUBLANES):
        ref[pl.ds(s, n, stride=SUBLANES), :] = val[:, s * LANES:(s + 1) * LANES]


def _load_tile_rows(ref):
    n = ref.shape[0] // SUBLANES
    return jnp.concatenate([ref[pl.ds(s, n, stride=SUBLANES), :] for s in range(SUBLANES)], axis=1)


def _memkv_kernel(mem_ref, w_ref, k_ref, v_ref):
    kv = _dot(mem_ref[...].astype(BF), w_ref[...])
    k_ref[...] = kv[:, :MEM_DIM].astype(BF)
    v_ref[...] = kv[:, MEM_DIM:].astype(BF)


def _memkv(mem2d, w_kv, m_len):
    rows, d = mem2d.shape
    return pl.pallas_call(
        _memkv_kernel,
        grid=(rows // m_len,),
        in_specs=[pl.BlockSpec((m_len, d), lambda i: (i, 0)),
                  pl.BlockSpec((d, 2 * MEM_DIM), lambda i: (0, 0))],
        out_specs=[pl.BlockSpec((m_len, MEM_DIM), lambda i: (i, 0))] * 2,
        out_shape=[jax.ShapeDtypeStruct((rows, MEM_DIM), BF)] * 2,
        compiler_params=_cparams("parallel"),
        name="memkv",
    )(mem2d, w_kv)


def _proj_kernel(x_ref, w_ref, km_ref, vm_ref, uf_ref, q_ref, k_ref, v_ref, mo_ref):
    xb = x_ref[...].astype(BF)

    def seg(j):
        return _dot(xb, w_ref[:, j * 512:(j + 1) * 512])

    uf_ref[...] = seg(0).astype(BF)
    q_ref[...] = (seg(1) * NA_SCALE).astype(BF)
    k_ref[...] = seg(2).astype(BF)
    v_ref[...] = seg(3).astype(BF)
    qm = seg(4).astype(BF)
    for h in range(MEM_HEADS):
        sl = slice(h * MEM_HEAD_DIM, (h + 1) * MEM_HEAD_DIM)
        s = _dot_nt(qm[:, sl], km_ref[:, sl]) * MEM_SCALE
        m = jnp.max(s, axis=-1, keepdims=True)
        p = jnp.exp(s - m)
        l = jnp.sum(p, axis=-1, keepdims=True)
        o = _dot(p.astype(BF), vm_ref[:, sl])
        mo_ref[:, sl] = (o / l).astype(BF)


def _proj(x2d, w_in, k_mem, v_mem, seq, m_len):
    t, d = x2d.shape
    tm = TM_PROJ
    tiles_per_batch = seq // tm
    tok = lambda i: (i, 0)
    memb = lambda i: (i // tiles_per_batch, 0)
    return pl.pallas_call(
        _proj_kernel,
        grid=(t // tm,),
        in_specs=[pl.BlockSpec((tm, d), tok),
                  pl.BlockSpec(w_in.shape, lambda i: (0, 0)),
                  pl.BlockSpec((m_len, MEM_DIM), memb),
                  pl.BlockSpec((m_len, MEM_DIM), memb)],
        out_specs=[pl.BlockSpec((tm, 512), tok)] * 5,
        out_shape=[jax.ShapeDtypeStruct((t, 512), BF)] * 5,
        compiler_params=_cparams("parallel"),
        name="proj",
    )(x2d, w_in, k_mem, v_mem)


def _fourier_tables(n1, n2):
    n = n1 * n2
    k1 = np.arange(n1)
    ang1 = 2.0 * np.pi * ((k1[:, None] * k1[None, :]) % n1) / n1
    norm = 1.0 / np.sqrt(float(n) * FOURIER_GROUP_DIM)
    m1 = np.concatenate([np.cos(ang1), -np.sin(ang1)], axis=0) * norm
    kk = k1[:, None, None] + n1 * np.arange(n2)[None, :, None]
    nn = np.arange(n2)[None, None, :]
    ang2 = 2.0 * np.pi * ((kk * nn) % n) / n
    c2, s2 = np.cos(ang2), np.sin(ang2)
    m2 = np.concatenate([np.concatenate([c2, s2], axis=2),
                         np.concatenate([-s2, c2], axis=2)], axis=1)
    c = np.arange(FOURIER_GROUP_DIM)
    angc = 2.0 * np.pi * ((c[:, None] * c[None, :]) % FOURIER_GROUP_DIM) / FOURIER_GROUP_DIM
    mc = np.concatenate([np.cos(angc), np.sin(angc)], axis=0)
    as_bf = lambda a: jnp.asarray(a.astype(np.float32)).astype(BF)
    return as_bf(m1), as_bf(m2), as_bf(mc)


def _f1_kernel(u_ref, m_ref, a_ref):
    n1 = u_ref.shape[0]
    r = _dot(m_ref[...], u_ref[...])
    a_ref[0] = r[:n1].astype(BF)
    a_ref[1] = r[n1:].astype(BF)


def _f2_kernel(a_ref, m2_ref, mc_ref, o_ref):
    slabs = a_ref.shape[1]
    n2 = a_ref.shape[2]
    for t in range(slabs):
        a = jnp.concatenate([a_ref[0, t], a_ref[1, t]], axis=0)
        y = _dot(m2_ref[t], a)
        yr = y[:n2].astype(BF)
        yi = y[n2:].astype(BF)
        for g in range(FOURIER_GROUPS):
            sl = slice(g * LANES, (g + 1) * LANES)
            yy = jnp.concatenate([yr[:, sl], yi[:, sl]], axis=1)
            z = _dot(yy, mc_ref[...])
            o_ref[:, t * FOURIER_DIM + g * LANES: t * FOURIER_DIM + (g + 1) * LANES] = z.astype(BF)


def _fourier(u_f, batch, seq):
    n2 = FFT_N2
    n1 = seq // n2
    m1, m2, mc = _fourier_tables(n1, n2)
    cols = n2 * FOURIER_DIM
    tn = min(F1_COLS, cols)
    a = pl.pallas_call(
        _f1_kernel,
        grid=(batch, cols // tn),
        in_specs=[pl.BlockSpec((None, n1, tn), lambda b, j: (b, 0, j)),
                  pl.BlockSpec((2 * n1, n1), lambda b, j: (0, 0))],
        out_specs=pl.BlockSpec((None, 2, n1, tn), lambda b, j: (b, 0, 0, j)),
        out_shape=jax.ShapeDtypeStruct((batch, 2, n1, cols), BF),
        compiler_params=_cparams("parallel", "parallel"),
        name="fourier1",
    )(u_f.reshape(batch, n1, cols), m1)
    kb = min(F2_SLABS, n1)
    out = pl.pallas_call(
        _f2_kernel,
        grid=(batch, n1 // kb),
        in_specs=[pl.BlockSpec((None, 2, kb, n2, FOURIER_DIM), lambda b, j: (b, 0, j, 0, 0)),
                  pl.BlockSpec((kb, 2 * n2, 2 * n2), lambda b, j: (j, 0, 0)),
                  pl.BlockSpec((2 * FOURIER_GROUP_DIM, FOURIER_GROUP_DIM), lambda b, j: (0, 0))],
        out_specs=pl.BlockSpec((None, n2, kb * FOURIER_DIM), lambda b, j: (b, 0, j)),
        out_shape=jax.ShapeDtypeStruct((batch, n2, n1 * FOURIER_DIM), BF),
        compiler_params=_cparams("parallel", "parallel"),
        name="fourier2",
    )(a.reshape(batch, 2, n1, n2, FOURIER_DIM), m2, mc)
    return out.reshape(batch * seq, FOURIER_DIM)


def _na_bias_table(rpb):
    cols = np.arange(GRID_W)
    cs = np.clip(cols - NA_COL_WIN // 2, 0, GRID_W - NA_COL_WIN)
    kc = np.arange(GRID_W)
    in_win = (kc[None, :] >= cs[:, None]) & (kc[None, :] < cs[:, None] + NA_COL_WIN)
    dc = kc[None, :] - cols[:, None] + (NA_COL_WIN - 1)
    n_dc = 2 * NA_COL_WIN - 1
    pick = ((dc.reshape(-1)[None, :] == np.arange(n_dc)[:, None]) & in_win.reshape(-1)[None, :])
    picked = jnp.dot(rpb.astype(F32).reshape(-1, n_dc), jnp.asarray(pick.astype(np.float32)),
                     precision=lax.Precision.HIGHEST)
    picked = picked.reshape(NA_HEADS, 2 * NA_ROW_WIN - 1, GRID_W, GRID_W)
    full = jnp.where(in_win[None, None], picked, MASK_NEG)
    tabs = []
    for d0 in range(NA_ROW_WIN):
        t = full[:, d0:d0 + NA_ROW_WIN]
        tabs.append(t.transpose(0, 2, 1, 3).reshape(NA_HEADS, GRID_W, NA_ROW_WIN * GRID_W))
    return jnp.stack(tabs, axis=0)


def _na_kernel(q_ref, k_ref, v_ref, tab_ref, o_ref, p_buf, l_buf, *, n_rows):
    i = pl.program_id(2)
    win = NA_ROW_WIN * GRID_W
    lane = lax.broadcasted_iota(I32, (GRID_W, 2 * NA_HEAD_DIM), 1)
    first_head = lane < NA_HEAD_DIM
    starts = []
    for j in range(NA_ROWS_PER_STEP):
        r = i * NA_ROWS_PER_STEP + j
        rs = jnp.clip(r - NA_ROW_WIN // 2, 0, n_rows - NA_ROW_WIN)
        d0 = rs - r + (NA_ROW_WIN - 1)
        start = pl.multiple_of(rs * GRID_W, GRID_W)
        starts.append(start)
        q2 = q_ref[j * GRID_W:(j + 1) * GRID_W, :]
        kw = k_ref[pl.ds(start, win), :]
        for h in range(2):
            keep = first_head if h == 0 else jnp.logical_not(first_head)
            qh = jnp.where(keep, q2, jnp.zeros_like(q2))
            s = _dot_nt(qh, kw) + tab_ref[d0, h]
            m = jnp.max(s, axis=-1, keepdims=True)
            p = jnp.exp(s - m)
            l_buf[2 * j + h] = jnp.sum(p, axis=-1, keepdims=True)
            p_buf[2 * j + h] = p.astype(BF)
    for j in range(NA_ROWS_PER_STEP):
        vw = v_ref[pl.ds(starts[j], win), :]
        o0 = _dot(p_buf[2 * j], vw) / l_buf[2 * j]
        o1 = _dot(p_buf[2 * j + 1], vw) / l_buf[2 * j + 1]
        o_ref[j * GRID_W:(j + 1) * GRID_W, :] = jnp.where(first_head, o0, o1).astype(BF)


def _natten(q, k, v, rpb, batch, seq):
    n_rows = seq // GRID_W
    rb = NA_ROWS_PER_STEP
    nrb = n_rows // rb
    hp = NA_HEADS // 2
    tab = _na_bias_table(rpb)
    qmap = lambda b, p, i: (b * nrb + i, p)
    kmap = lambda b, p, i: (b, p)
    return pl.pallas_call(
        functools.partial(_na_kernel, n_rows=n_rows),
        grid=(batch, hp, nrb),
        in_specs=[pl.BlockSpec((rb * GRID_W, LANES), qmap),
                  pl.BlockSpec((seq, LANES), kmap),
                  pl.BlockSpec((seq, LANES), kmap),
                  pl.BlockSpec((NA_ROW_WIN, 2, GRID_W, NA_ROW_WIN * GRID_W), lambda b, p, i: (0, p, 0, 0))],
        out_specs=pl.BlockSpec((rb * GRID_W, LANES), qmap),
        out_shape=jax.ShapeDtypeStruct((batch * seq, NA_DIM), BF),
        scratch_shapes=[pltpu.VMEM((2 * rb, GRID_W, NA_ROW_WIN * GRID_W), BF),
                        pltpu.VMEM((2 * rb, GRID_W, 1), F32)],
        compiler_params=_cparams("parallel", "parallel", "parallel"),
        name="natten",
    )(q, k, v, tab)


def _merge_kernel(x_ref, fo_ref, na_ref, mo_ref, wg_ref, bg_ref, wf_ref, wn_ref, wm_ref, wo_ref,
                  g_ref, b_ref, ht_ref):
    d = x_ref.shape[1]
    x = x_ref[...]
    xb = x.astype(BF)
    merged = None
    for j, (br_ref, w_ref) in enumerate(((fo_ref, wf_ref), (na_ref, wn_ref), (mo_ref, wm_ref))):
        z = _dot(xb, wg_ref[:, j * d:(j + 1) * d]) + bg_ref[:, j * d:(j + 1) * d]
        gate = 1.0 / (1.0 + jnp.exp(-z))
        term = gate * _dot(br_ref[...], w_ref[...])
        merged = term if merged is None else merged + term
    mix = _dot(merged.astype(BF), wo_ref[...])
    _store_tile_rows(ht_ref, _layer_norm(ALPHA * x + mix, g_ref[...], b_ref[...]))


def _merge(x2d, fo, na, mo, w_gate, b_gate, w_fo, w_na, w_mo, w_out, ln_g, ln_b):
    t, d = x2d.shape
    assert d == SUBLANES * LANES
    tm = TM_MERGE
    tok = lambda i: (i, 0)
    full = lambda a: pl.BlockSpec(a.shape, lambda i: (0, 0))
    return pl.pallas_call(
        _merge_kernel,
        grid=(t // tm,),
        in_specs=[pl.BlockSpec((tm, d), tok)] + [pl.BlockSpec((tm, 512), tok)] * 3
                 + [full(a) for a in (w_gate, b_gate, w_fo, w_na, w_mo, w_out, ln_g, ln_b)],
        out_specs=pl.BlockSpec((tm * SUBLANES, LANES), tok),
        out_shape=jax.ShapeDtypeStruct((t * SUBLANES, LANES), F32),
        compiler_params=_cparams("parallel"),
        name="merge",
    )(x2d, fo, na, mo, w_gate, b_gate, w_fo, w_na, w_mo, w_out, ln_g, ln_b)


def _router_kernel(ht_ref, whi_ref, wlo_ref, b_ref, oi_ref, of_ref, cnt_ref, carry_ref):
    tm = ht_ref.shape[0] // SUBLANES

    @pl.when(pl.program_id(0) == 0)
    def _():
        carry_ref[...] = jnp.zeros_like(carry_ref)

    h = _load_tile_rows(ht_ref)
    hh = h.astype(BF)
    hl = (h - hh.astype(F32)).astype(BF)
    logits = _dot(hh, whi_ref[...]) + _dot(hh, wlo_ref[...]) + _dot(hl, whi_ref[...]) + b_ref[...]
    lane = lax.broadcasted_iota(I32, (tm, ROUTE_LANES), 1)
    is_group = lane < N_GROUPS
    gl = jnp.where(is_group, logits, MASK_NEG)
    gmax = jnp.max(gl, axis=-1, keepdims=True)
    g_idx = jnp.min(jnp.where(gl == gmax, lane, ROUTE_LANES), axis=-1, keepdims=True)
    p_group = 1.0 / jnp.sum(jnp.where(is_group, jnp.exp(gl - gmax), 0.0), axis=-1, keepdims=True)
    e_lane = lane - N_GROUPS
    in_group = (e_lane >= 0) & (e_lane < N_EXPERTS) & ((e_lane >> 3) == g_idx)
    el = jnp.where(in_group, logits, MASK_NEG)
    v1 = jnp.max(el, axis=-1, keepdims=True)
    i1 = jnp.min(jnp.where(el == v1, lane, ROUTE_LANES), axis=-1, keepdims=True)
    el2 = jnp.where(lane == i1, MASK_NEG, el)
    v2 = jnp.max(el2, axis=-1, keepdims=True)
    i2 = jnp.min(jnp.where(el2 == v2, lane, ROUTE_LANES), axis=-1, keepdims=True)
    tt = jnp.exp(v2 - v1)
    w1 = p_group / (1.0 + tt)
    w2 = p_group * tt / (1.0 + tt)
    sel1 = lane == i1
    sel2 = lane == i2
    onehot = jnp.where(sel1 | sel2, 1.0, 0.0)
    row = lax.broadcasted_iota(I32, (tm, tm), 0)
    col = lax.broadcasted_iota(I32, (tm, tm), 1)
    tri = jnp.where(row > col, 1.0, 0.0).astype(BF)
    prefix = _dot(tri, onehot.astype(BF)) + carry_ref[...]
    r1 = jnp.sum(jnp.where(sel1, prefix, 0.0), axis=-1, keepdims=True)
    r2 = jnp.sum(jnp.where(sel2, prefix, 0.0), axis=-1, keepdims=True)
    carry_ref[...] += jnp.sum(onehot, axis=0, keepdims=True)
    cnt_ref[...] = carry_ref[...]
    sub = pl.program_id(0) % ROUTE_STEPS_PER_PLANE
    diag = lane == (lax.broadcasted_iota(I32, (tm, ROUTE_LANES), 0) % ROUTE_LANES)
    for f, col in enumerate(((i1 - N_GROUPS).astype(F32), (i2 - N_GROUPS).astype(F32), r1, r2)):
        plane = jnp.sum(jnp.where(diag, col, 0.0).reshape(tm // ROUTE_LANES, ROUTE_LANES, ROUTE_LANES), axis=1)
        oi_ref[f, pl.ds(sub * (tm // ROUTE_LANES), tm // ROUTE_LANES), :] = plane.astype(I32)
    of = jnp.where(lane == 0, w1, jnp.where(lane == 1, w2, 0.0))
    of_ref[...] = of[:, :8]


def _router(ht, w_rg, b_rg, w_re, b_re):
    t = ht.shape[0] // SUBLANES
    d = w_rg.shape[0]
    tm = TM_ROUTE
    pad = ROUTE_LANES - N_GROUPS - N_EXPERTS
    w = jnp.concatenate([w_rg.astype(F32), w_re.astype(F32), jnp.zeros((d, pad), F32)], axis=1)
    b = jnp.concatenate([b_rg.astype(F32), b_re.astype(F32), jnp.zeros((pad,), F32)])[None, :]
    w_hi = w.astype(BF)
    w_lo = (w - w_hi.astype(F32)).astype(BF)
    tok = lambda i: (i, 0)
    const = lambda i: (0, 0)
    return pl.pallas_call(
        _router_kernel,
        grid=(t // tm,),
        in_specs=[pl.BlockSpec((tm * SUBLANES, LANES), tok),
                  pl.BlockSpec((d, ROUTE_LANES), const),
                  pl.BlockSpec((d, ROUTE_LANES), const),
                  pl.BlockSpec((1, ROUTE_LANES), const)],
        out_specs=[pl.BlockSpec((4, SUBLANES, LANES), lambda i: (0, i // ROUTE_STEPS_PER_PLANE, 0)),
                   pl.BlockSpec((tm, 8), tok),
                   pl.BlockSpec((1, ROUTE_LANES), const)],
        out_shape=[jax.ShapeDtypeStruct((4, t // LANES, LANES), I32), jax.ShapeDtypeStruct((t, 8), F32),
                   jax.ShapeDtypeStruct((1, ROUTE_LANES), F32)],
        scratch_shapes=[pltpu.VMEM((1, ROUTE_LANES), F32)],
        compiler_params=_cparams("arbitrary"),
        name="router",
    )(ht, w_hi, w_lo, b)


def _tile_copy(src_ref, src_row, dst_ref, dst_row, sem):
    src = pl.multiple_of(src_row * SUBLANES, SUBLANES)
    dst = pl.multiple_of(dst_row * SUBLANES, SUBLANES)
    return pltpu.make_async_copy(src_ref.at[pl.ds(src, SUBLANES)], dst_ref.at[pl.ds(dst, SUBLANES)], sem)


def _row_groups(n_rows, copies_of_row, start):
    def group(g, c):
        copies = []
        for u in range(ROW_GROUP):
            copies.extend(copies_of_row(g * ROW_GROUP + u))
        for cp in copies:
            if start:
                cp.start()
            else:
                cp.wait()
        return c

    lax.fori_loop(0, n_rows // ROW_GROUP, group, 0)


def _dests_kernel(ps_ref, idx_ref, d_ref):
    for a in range(2):
        e = idx_ref[a]
        d = idx_ref[2 + a]
        for k in range(N_EXPERTS):
            d = d + jnp.where(e == k, ps_ref[k], 0)
        d_ref[a] = d


def _dests(idx, pad_start):
    planes = idx.shape[1]
    out = pl.pallas_call(
        _dests_kernel,
        grid_spec=pltpu.PrefetchScalarGridSpec(
            num_scalar_prefetch=1,
            grid=(1,),
            in_specs=[pl.BlockSpec(idx.shape, lambda i, ps: (0, 0, 0))],
            out_specs=pl.BlockSpec((2, planes, LANES), lambda i, ps: (0, 0, 0)),
        ),
        out_shape=jax.ShapeDtypeStruct((2, planes, LANES), I32),
        compiler_params=_cparams("arbitrary"),
        name="dests",
    )(pad_start, idx)
    return out[0].reshape(-1), out[1].reshape(-1)


def _dispatch_kernel(ps_ref, pe_ref, d1_ref, d2_ref, ht_ref, xg_ref, zbuf, sem, zsem):
    i = pl.program_id(0)
    tm = d1_ref.shape[0]
    blk = MOE_BLOCK * SUBLANES
    n_blocks = xg_ref.shape[0] // blk

    @pl.when(i == 0)
    def _():
        zbuf[...] = jnp.zeros_like(zbuf)

        def zero_copy(block):
            return pltpu.make_async_copy(zbuf, xg_ref.at[pl.ds(pl.multiple_of(block * blk, blk), blk)], zsem)

        def seg_issue(e, c):
            @pl.when(pe_ref[e] > ps_ref[e])
            def _():
                zero_copy(pe_ref[e] // MOE_BLOCK - 1).start()
            return c

        def seg_drain(e, c):
            @pl.when(pe_ref[e] > ps_ref[e])
            def _():
                zero_copy(pe_ref[e] // MOE_BLOCK - 1).wait()
            return c

        def tail_issue(b, c):
            zero_copy(b).start()
            return c

        def tail_drain(b, c):
            zero_copy(b).wait()
            return c

        first_tail = pe_ref[N_EXPERTS - 1] // MOE_BLOCK
        lax.fori_loop(0, N_EXPERTS, seg_issue, 0)
        lax.fori_loop(first_tail, n_blocks, tail_issue, 0)
        lax.fori_loop(0, N_EXPERTS, seg_drain, 0)
        lax.fori_loop(first_tail, n_blocks, tail_drain, 0)

    _row_groups(tm, lambda t: [_tile_copy(ht_ref, i * tm + t, xg_ref, d1_ref[t], sem),
                               _tile_copy(ht_ref, i * tm + t, xg_ref, d2_ref[t], sem)], True)
    _row_groups(tm, lambda t: [_tile_copy(ht_ref, 0, xg_ref, 0, sem)] * 2, False)


def _dispatch(ht, d1, d2, pad_start, pad_end, n_pad):
    t = d1.shape[0]
    tm = TM_DISPATCH
    smem = lambda: pl.BlockSpec((tm,), lambda i, ps, pe: (i,), memory_space=pltpu.SMEM)
    return pl.pallas_call(
        _dispatch_kernel,
        grid_spec=pltpu.PrefetchScalarGridSpec(
            num_scalar_prefetch=2,
            grid=(t // tm,),
            in_specs=[smem(), smem(), pl.BlockSpec(memory_space=pl.ANY)],
            out_specs=pl.BlockSpec(memory_space=pl.ANY),
            scratch_shapes=[pltpu.VMEM((MOE_BLOCK * SUBLANES, LANES), F32), pltpu.SemaphoreType.DMA(()),
                            pltpu.SemaphoreType.DMA(())],
        ),
        out_shape=jax.ShapeDtypeStruct((n_pad * SUBLANES, LANES), F32),
        compiler_params=_cparams("arbitrary"),
        name="dispatch",
    )(pad_start, pad_end, d1, d2, ht)


def _expert_kernel(be_ref, nu_ref, x_ref, wg_ref, wu_ref, wd_ref, y_ref, wg_s, wu_s, wd_s):
    j = pl.program_id(0)

    @pl.when(j < nu_ref[0])
    def _():
        prev = be_ref[jnp.maximum(j - 1, 0)]

        @pl.when((j == 0) | (be_ref[j] != prev))
        def _():
            wg_s[...] = wg_ref[...].astype(BF)
            wu_s[...] = wu_ref[...].astype(BF)
            wd_s[...] = wd_ref[...].astype(BF)

        xb = _load_tile_rows(x_ref).astype(BF)
        a = _dot(xb, wg_s[...])
        u = _dot(xb, wu_s[...])
        mid = (a / (1.0 + jnp.exp(-a))) * u
        _store_tile_rows(y_ref, _dot(mid.astype(BF), wd_s[...]))

    @pl.when(j >= nu_ref[0])
    def _():
        y_ref[...] = jnp.zeros_like(y_ref)


def _experts(xg, blk_expert, n_used, w_eg, w_eu, w_ed):
    blk = MOE_BLOCK * SUBLANES
    nb = xg.shape[0] // blk
    d, de = w_eg.shape[1], w_eg.shape[2]
    rows = lambda j, be, nu: (jnp.minimum(j, nu[0] - 1), 0)
    wsel = lambda j, be, nu: (be[j], 0, 0)
    return pl.pallas_call(
        _expert_kernel,
        grid_spec=pltpu.PrefetchScalarGridSpec(
            num_scalar_prefetch=2,
            grid=(nb,),
            in_specs=[pl.BlockSpec((blk, LANES), rows),
                      pl.BlockSpec((None, d, de), wsel),
                      pl.BlockSpec((None, d, de), wsel),
                      pl.BlockSpec((None, de, d), wsel)],
            out_specs=pl.BlockSpec((blk, LANES), lambda j, be, nu: (j, 0)),
            scratch_shapes=[pltpu.VMEM((d, de), BF), pltpu.VMEM((d, de), BF), pltpu.VMEM((de, d), BF)],
        ),
        out_shape=jax.ShapeDtypeStruct(xg.shape, F32),
        compiler_params=_cparams("arbitrary"),
        name="experts",
    )(blk_expert, n_used, xg, w_eg, w_eu, w_ed)


def _combine_kernel(d1_ref, d2_ref, d1_next_ref, d2_next_ref, wt_ref, ht_ref, yb_ref, g_ref, b_ref, o_ref,
                    buf, sems):
    i = pl.program_id(0)
    n = pl.num_programs(0)
    tm = d1_ref.shape[0]

    def gather_tile(da_ref, db_ref, slot, start):
        def copies(t):
            if not start:
                return [_tile_copy(yb_ref, 0, buf.at[slot, a], 0, sems.at[slot]) for a in range(2)]
            return [_tile_copy(yb_ref, da_ref[t], buf.at[slot, 0], t, sems.at[slot]),
                    _tile_copy(yb_ref, db_ref[t], buf.at[slot, 1], t, sems.at[slot])]
        _row_groups(tm, copies, start)

    slot = i % 2

    @pl.when(i == 0)
    def _():
        gather_tile(d1_ref, d2_ref, 0, True)

    @pl.when(i + 1 < n)
    def _():
        gather_tile(d1_next_ref, d2_next_ref, 1 - slot, True)

    gather_tile(d1_ref, d2_ref, slot, False)
    wt = wt_ref[...]
    ffn = wt[:, 0:1] * _load_tile_rows(buf.at[slot, 0]) + wt[:, 1:2] * _load_tile_rows(buf.at[slot, 1])
    o_ref[...] = _layer_norm(ALPHA * _load_tile_rows(ht_ref) + ffn, g_ref[...], b_ref[...])


def _combine(ht, yb, d1, d2, wts, ln_g, ln_b):
    t = d1.shape[0]
    d = SUBLANES * LANES
    tm = TM_COMBINE
    n = t // tm
    tok = lambda i: (i, 0)
    const = lambda i: (0, 0)
    cur = lambda: pl.BlockSpec((tm,), lambda i: (i,), memory_space=pltpu.SMEM)
    nxt = lambda: pl.BlockSpec((tm,), lambda i: (jnp.minimum(i + 1, n - 1),), memory_space=pltpu.SMEM)
    return pl.pallas_call(
        _combine_kernel,
        grid=(n,),
        in_specs=[cur(), cur(), nxt(), nxt(),
                  pl.BlockSpec((tm, 8), tok),
                  pl.BlockSpec((tm * SUBLANES, LANES), tok),
                  pl.BlockSpec(memory_space=pl.ANY),
                  pl.BlockSpec((1, d), const), pl.BlockSpec((1, d), const)],
        out_specs=pl.BlockSpec((tm, d), tok),
        out_shape=jax.ShapeDtypeStruct((t, d), F32),
        scratch_shapes=[pltpu.VMEM((2, 2, tm * SUBLANES, LANES), F32), pltpu.SemaphoreType.DMA((2,))],
        compiler_params=_cparams("arbitrary"),
        name="combine",
    )(d1, d2, d1, d2, wts, ht, yb, ln_g, ln_b)


def _token_mixing(x2d, mem2d, batch, seq, m_len, w_in, w_gate, b_gate, w_mem_kv, rpb, w_fo, w_na, w_mo, w_out,
                  ln_g, ln_b):
    k_mem, v_mem = _memkv(mem2d, w_mem_kv.astype(BF), m_len)
    u_f, q, k, v, mo = _proj(x2d, w_in.astype(BF), k_mem, v_mem, seq, m_len)
    fo = _fourier(u_f, batch, seq)
    na = _natten(q, k, v, rpb, batch, seq)
    return _merge(x2d, fo, na, mo, w_gate.astype(BF), b_gate.astype(F32)[None, :], w_fo.astype(BF),
                  w_na.astype(BF), w_mo.astype(BF), w_out.astype(BF), ln_g.astype(F32)[None, :],
                  ln_b.astype(F32)[None, :])


def _moe(ht, w_rg, b_rg, w_re, b_re, w_eg, w_eu, w_ed, ln_g, ln_b):
    t = ht.shape[0] // SUBLANES
    idx, wts, cnt = _router(ht, w_rg, b_rg, w_re, b_re)
    counts = cnt[0, N_GROUPS:N_GROUPS + N_EXPERTS].astype(I32)
    padded = (counts + MOE_BLOCK - 1) // MOE_BLOCK * MOE_BLOCK
    pad_end = jnp.cumsum(padded).astype(I32)
    pad_start = pad_end - padded
    n_pad = 2 * t + N_EXPERTS * MOE_BLOCK
    nb = n_pad // MOE_BLOCK
    blk_start = jnp.arange(nb, dtype=I32) * MOE_BLOCK
    blk_expert = jnp.minimum(jnp.sum((pad_end[None, :] <= blk_start[:, None]).astype(I32), axis=1), N_EXPERTS - 1)
    n_used = pad_end[-1:] // MOE_BLOCK
    d1, d2 = _dests(idx, pad_start)
    xg = _dispatch(ht, d1, d2, pad_start, pad_end, n_pad)
    yb = _experts(xg, blk_expert, n_used, w_eg, w_eu, w_ed)
    return _combine(ht, yb, d1, d2, wts, ln_g.astype(F32)[None, :], ln_b.astype(F32)[None, :])


def kernel(x, mem, w_in, w_gate, b_gate, w_mem_kv, rpb, w_fourier_o, w_na_o, w_mem_o, w_out, ln1_g, ln1_b,
           w_router_group, b_router_group, w_router_expert, b_router_expert, w_exp_gate, w_exp_up,
           w_exp_down, ln2_g, ln2_b):
    batch, seq, d = x.shape
    m_len = mem.shape[1]
    x2d = x.reshape(batch * seq, d)
    mem2d = mem.reshape(batch * m_len, d)
    for l in range(w_in.shape[0]):
        ht = _token_mixing(x2d, mem2d, batch, seq, m_len, w_in[l], w_gate[l], b_gate[l], w_mem_kv[l], rpb[l],
                           w_fourier_o[l], w_na_o[l], w_mem_o[l], w_out[l], ln1_g[l], ln1_b[l])
        x2d = _moe(ht, w_router_group[l], b_router_group[l], w_router_expert[l], b_router_expert[l],
                   w_exp_gate[l], w_exp_up[l], w_exp_down[l], ln2_g[l], ln2_b[l])
    return x2d.reshape(batch, seq, d)
```

```python
import functools

import numpy as np
import jax
import jax.numpy as jnp
from jax import lax
from jax.experimental import pallas as pl
from jax.experimental.pallas import tpu as pltpu

BF = jnp.bfloat16
F32 = jnp.float32
I32 = jnp.int32

GRID_W = 64
MEM_HEADS = 4
MEM_HEAD_DIM = 128
MEM_DIM = MEM_HEADS * MEM_HEAD_DIM
FOURIER_GROUPS = 4
FOURIER_GROUP_DIM = 128
FOURIER_DIM = FOURIER_GROUPS * FOURIER_GROUP_DIM
NA_HEADS = 8
NA_HEAD_DIM = 64
NA_DIM = NA_HEADS * NA_HEAD_DIM
NA_ROW_WIN = 8
NA_COL_WIN = 16
N_GROUPS = 8
EXPERTS_PER_GROUP = 8
N_EXPERTS = N_GROUPS * EXPERTS_PER_GROUP
DEPTH = 1
ALPHA = (2.0 * DEPTH) ** 0.25
LN_EPS = 1e-5
NA_SCALE = NA_HEAD_DIM ** -0.5
MEM_SCALE = MEM_HEAD_DIM ** -0.5
MASK_NEG = -1e30

LANES = 128
SUBLANES = 8
FFT_N2 = 128
TM_PROJ = 512
TM_MERGE = 512
TM_ROUTE = 256
TM_DISPATCH = 1024
TM_COMBINE = 256
NA_ROWS_PER_STEP = 8
F1_COLS = 8192
F2_SLABS = 4
MOE_BLOCK = 256
ROUTE_LANES = 128
ROUTE_STEPS_PER_PLANE = SUBLANES * LANES // TM_ROUTE
ROW_GROUP = 8
VMEM_LIMIT = 56 * 1024 * 1024


def _cparams(*sem):
    return pltpu.CompilerParams(dimension_semantics=sem, vmem_limit_bytes=VMEM_LIMIT)


def _dot(a, b):
    return jnp.dot(a, b, preferred_element_type=F32)


def _dot_nt(a, b):
    return lax.dot_general(a, b, (((1,), (1,)), ((), ())), preferred_element_type=F32)


def _layer_norm(y, g, b):
    mu = jnp.mean(y, axis=-1, keepdims=True)
    yc = y - mu
    var = jnp.mean(yc * yc, axis=-1, keepdims=True)
    return yc * lax.rsqrt(var + LN_EPS) * g + b


def _store_tile_rows(ref, val):
    n = val.shape[0]
    for s in range(SUBLANES):
        ref[pl.ds(s, n, stride=SUBLANES), :] = val[:, s * LANES:(s + 1) * LANES]


def _load_tile_rows(ref):
    n = ref.shape[0] // SUBLANES
    return jnp.concatenate([ref[pl.ds(s, n, stride=SUBLANES), :] for s in range(SUBLANES)], axis=1)


def _memkv_kernel(mem_ref, w_ref, k_ref, v_ref):
    kv = _dot(mem_ref[...].astype(BF), w_ref[...])
    k_ref[...] = kv[:, :MEM_DIM].astype(BF)
    v_ref[...] = kv[:, MEM_DIM:].astype(BF)


def _memkv(mem2d, w_kv, m_len):
    rows, d = mem2d.shape
    return pl.pallas_call(
        _memkv_kernel,
        grid=(rows // m_len,),
        in_specs=[pl.BlockSpec((m_len, d), lambda i: (i, 0)),
                  pl.BlockSpec((d, 2 * MEM_DIM), lambda i: (0, 0))],
        out_specs=[pl.BlockSpec((m_len, MEM_DIM), lambda i: (i, 0))] * 2,
        out_shape=[jax.ShapeDtypeStruct((rows, MEM_DIM), BF)] * 2,
        compiler_params=_cparams("parallel"),
        name="memkv",
    )(mem2d, w_kv)


def _proj_kernel(x_ref, w_ref, km_ref, vm_ref, uf_ref, q_ref, k_ref, v_ref, mo_ref):
    xb = x_ref[...].astype(BF)

    def seg(j):
        return _dot(xb, w_ref[:, j * 512:(j + 1) * 512])

    uf_ref[...] = seg(0).astype(BF)
    q_ref[...] = (seg(1) * NA_SCALE).astype(BF)
    k_ref[...] = seg(2).astype(BF)
    v_ref[...] = seg(3).astype(BF)
    qm = seg(4).astype(BF)
    for h in range(MEM_HEADS):
        sl = slice(h * MEM_HEAD_DIM, (h + 1) * MEM_HEAD_DIM)
        s = _dot_nt(qm[:, sl], km_ref[:, sl]) * MEM_SCALE
        m = jnp.max(s, axis=-1, keepdims=True)
        p = jnp.exp(s - m)
        l = jnp.sum(p, axis=-1, keepdims=True)
        o = _dot(p.astype(BF), vm_ref[:, sl])
        mo_ref[:, sl] = (o / l).astype(BF)


def _proj(x2d, w_in, k_mem, v_mem, seq, m_len):
    t, d = x2d.shape
    tm = TM_PROJ
    tiles_per_batch = seq // tm
    tok = lambda i: (i, 0)
    memb = lambda i: (i // tiles_per_batch, 0)
    return pl.pallas_call(
        _proj_kernel,
        grid=(t // tm,),
        in_specs=[pl.BlockSpec((tm, d), tok),
                  pl.BlockSpec(w_in.shape, lambda i: (0, 0)),
                  pl.BlockSpec((m_len, MEM_DIM), memb),
                  pl.BlockSpec((m_len, MEM_DIM), memb)],
        out_specs=[pl.BlockSpec((tm, 512), tok)] * 5,
        out_shape=[jax.ShapeDtypeStruct((t, 512), BF)] * 5,
        compiler_params=_cparams("parallel"),
        name="proj",
    )(x2d, w_in, k_mem, v_mem)


def _fourier_tables(n1, n2):
    n = n1 * n2
    k1 = np.arange(n1)
    ang1 = 2.0 * np.pi * ((k1[:, None] * k1[None, :]) % n1) / n1
    norm = 1.0 / np.sqrt(float(n) * FOURIER_GROUP_DIM)
    m1 = np.concatenate([np.cos(ang1), -np.sin(ang1)], axis=0) * norm
    kk = k1[:, None, None] + n1 * np.arange(n2)[None, :, None]
    nn = np.arange(n2)[None, None, :]
    ang2 = 2.0 * np.pi * ((kk * nn) % n) / n
    c2, s2 = np.cos(ang2), np.sin(ang2)
    m2 = np.concatenate([np.concatenate([c2, s2], axis=2),
                         np.concatenate([-s2, c2], axis=2)], axis=1)
    c = np.arange(FOURIER_GROUP_DIM)
    angc = 2.0 * np.pi * ((c[:, None] * c[None, :]) % FOURIER_GROUP_DIM) / FOURIER_GROUP_DIM
    mc = np.concatenate([np.cos(angc), np.sin(angc)], axis=0)
    as_bf = lambda a: jnp.asarray(a.astype(np.float32)).astype(BF)
    return as_bf(m1), as_bf(m2), as_bf(mc)


def _f1_kernel(u_ref, m_ref, a_ref):
    n1 = u_ref.shape[0]
    r = _dot(m_ref[...], u_ref[...])
    a_ref[0] = r[:n1].astype(BF)
    a_ref[1] = r[n1:].astype(BF)


def _f2_kernel(a_ref, m2_ref, mc_ref, o_ref):
    slabs = a_ref.shape[1]
    n2 = a_ref.shape[2]
    for t in range(slabs):
        a = jnp.concatenate([a_ref[0, t], a_ref[1, t]], axis=0)
        y = _dot(m2_ref[t], a)
        yr = y[:n2].astype(BF)
        yi = y[n2:].astype(BF)
        for g in range(FOURIER_GROUPS):
            sl = slice(g * LANES, (g + 1) * LANES)
            yy = jnp.concatenate([yr[:, sl], yi[:, sl]], axis=1)
            z = _dot(yy, mc_ref[...])
            o_ref[:, t * FOURIER_DIM + g * LANES: t * FOURIER_DIM + (g + 1) * LANES] = z.astype(BF)


def _fourier(u_f, batch, seq):
    n2 = FFT_N2
    n1 = seq // n2
    m1, m2, mc = _fourier_tables(n1, n2)
    cols = n2 * FOURIER_DIM
    tn = min(F1_COLS, cols)
    a = pl.pallas_call(
        _f1_kernel,
        grid=(batch, cols // tn),
        in_specs=[pl.BlockSpec((None, n1, tn), lambda b, j: (b, 0, j)),
                  pl.BlockSpec((2 * n1, n1), lambda b, j: (0, 0))],
        out_specs=pl.BlockSpec((None, 2, n1, tn), lambda b, j: (b, 0, 0, j)),
        out_shape=jax.ShapeDtypeStruct((batch, 2, n1, cols), BF),
        compiler_params=_cparams("parallel", "parallel"),
        name="fourier1",
    )(u_f.reshape(batch, n1, cols), m1)
    kb = min(F2_SLABS, n1)
    out = pl.pallas_call(
        _f2_kernel,
        grid=(batch, n1 // kb),
        in_specs=[pl.BlockSpec((None, 2, kb, n2, FOURIER_DIM), lambda b, j: (b, 0, j, 0, 0)),
                  pl.BlockSpec((kb, 2 * n2, 2 * n2), lambda b, j: (j, 0, 0)),
                  pl.BlockSpec((2 * FOURIER_GROUP_DIM, FOURIER_GROUP_DIM), lambda b, j: (0, 0))],
        out_specs=pl.BlockSpec((None, n2, kb * FOURIER_DIM), lambda b, j: (b, 0, j)),
        out_shape=jax.ShapeDtypeStruct((batch, n2, n1 * FOURIER_DIM), BF),
        compiler_params=_cparams("parallel", "parallel"),
        name="fourier2",
    )(a.reshape(batch, 2, n1, n2, FOURIER_DIM), m2, mc)
    return out.reshape(batch * seq, FOURIER_DIM)


def _na_bias_table(rpb):
    cols = np.arange(GRID_W)
    cs = np.clip(cols - NA_COL_WIN // 2, 0, GRID_W - NA_COL_WIN)
    kc = np.arange(GRID_W)
    in_win = (kc[None, :] >= cs[:, None]) & (kc[None, :] < cs[:, None] + NA_COL_WIN)
    dc = kc[None, :] - cols[:, None] + (NA_COL_WIN - 1)
    n_dc = 2 * NA_COL_WIN - 1
    pick = ((dc.reshape(-1)[None, :] == np.arange(n_dc)[:, None]) & in_win.reshape(-1)[None, :])
    picked = jnp.dot(rpb.astype(F32).reshape(-1, n_dc), jnp.asarray(pick.astype(np.float32)),
                     precision=lax.Precision.HIGHEST)
    picked = picked.reshape(NA_HEADS, 2 * NA_ROW_WIN - 1, GRID_W, GRID_W)
    full = jnp.where(in_win[None, None], picked, MASK_NEG)
    tabs = []
    for d0 in range(NA_ROW_WIN):
        t = full[:, d0:d0 + NA_ROW_WIN]
        tabs.append(t.transpose(0, 2, 1, 3).reshape(NA_HEADS, GRID_W, NA_ROW_WIN * GRID_W))
    return jnp.stack(tabs, axis=0)


def _na_kernel(q_ref, k_ref, v_ref, tab_ref, o_ref, p_buf, l_buf, *, n_rows):
    i = pl.program_id(2)
    win = NA_ROW_WIN * GRID_W
    lane = lax.broadcasted_iota(I32, (GRID_W, 2 * NA_HEAD_DIM), 1)
    first_head = lane < NA_HEAD_DIM
    starts = []
    for j in range(NA_ROWS_PER_STEP):
        r = i * NA_ROWS_PER_STEP + j
        rs = jnp.clip(r - NA_ROW_WIN // 2, 0, n_rows - NA_ROW_WIN)
        d0 = rs - r + (NA_ROW_WIN - 1)
        start = pl.multiple_of(rs * GRID_W, GRID_W)
        starts.append(start)
        q2 = q_ref[j * GRID_W:(j + 1) * GRID_W, :]
        kw = k_ref[pl.ds(start, win), :]
        for h in range(2):
            keep = first_head if h == 0 else jnp.logical_not(first_head)
            qh = jnp.where(keep, q2, jnp.zeros_like(q2))
            s = _dot_nt(qh, kw) + tab_ref[d0, h]
            m = jnp.max(s, axis=-1, keepdims=True)
            p = jnp.exp(s - m)
            l_buf[2 * j + h] = jnp.sum(p, axis=-1, keepdims=True)
            p_buf[2 * j + h] = p.astype(BF)
    for j in range(NA_ROWS_PER_STEP):
        vw = v_ref[pl.ds(starts[j], win), :]
        o0 = _dot(p_buf[2 * j], vw) / l_buf[2 * j]
        o1 = _dot(p_buf[2 * j + 1], vw) / l_buf[2 * j + 1]
        o_ref[j * GRID_W:(j + 1) * GRID_W, :] = jnp.where(first_head, o0, o1).astype(BF)


def _natten(q, k, v, rpb, batch, seq):
    n_rows = seq // GRID_W
    rb = NA_ROWS_PER_STEP
    nrb = n_rows // rb
    hp = NA_HEADS // 2
    tab = _na_bias_table(rpb)
    qmap = lambda b, p, i: (b * nrb + i, p)
    kmap = lambda b, p, i: (b, p)
    return pl.pallas_call(
        functools.partial(_na_kernel, n_rows=n_rows),
        grid=(batch, hp, nrb),
        in_specs=[pl.BlockSpec((rb * GRID_W, LANES), qmap),
                  pl.BlockSpec((seq, LANES), kmap),
                  pl.BlockSpec((seq, LANES), kmap),
                  pl.BlockSpec((NA_ROW_WIN, 2, GRID_W, NA_ROW_WIN * GRID_W), lambda b, p, i: (0, p, 0, 0))],
        out_specs=pl.BlockSpec((rb * GRID_W, LANES), qmap),
        out_shape=jax.ShapeDtypeStruct((batch * seq, NA_DIM), BF),
        scratch_shapes=[pltpu.VMEM((2 * rb, GRID_W, NA_ROW_WIN * GRID_W), BF),
                        pltpu.VMEM((2 * rb, GRID_W, 1), F32)],
        compiler_params=_cparams("parallel", "parallel", "parallel"),
        name="natten",
    )(q, k, v, tab)


def _merge_kernel(x_ref, fo_ref, na_ref, mo_ref, wg_ref, bg_ref, wf_ref, wn_ref, wm_ref, wo_ref,
                  g_ref, b_ref, ht_ref):
    d = x_ref.shape[1]
    x = x_ref[...]
    xb = x.astype(BF)
    merged = None
    for j, (br_ref, w_ref) in enumerate(((fo_ref, wf_ref), (na_ref, wn_ref), (mo_ref, wm_ref))):
        z = _dot(xb, wg_ref[:, j * d:(j + 1) * d]) + bg_ref[:, j * d:(j + 1) * d]
        gate = 1.0 / (1.0 + jnp.exp(-z))
        term = gate * _dot(br_ref[...], w_ref[...])
        merged = term if merged is None else merged + term
    mix = _dot(merged.astype(BF), wo_ref[...])
    _store_tile_rows(ht_ref, _layer_norm(ALPHA * x + mix, g_ref[...], b_ref[...]))


def _merge(x2d, fo, na, mo, w_gate, b_gate, w_fo, w_na, w_mo, w_out, ln_g, ln_b):
    t, d = x2d.shape
    assert d == SUBLANES * LANES
    tm = TM_MERGE
    tok = lambda i: (i, 0)
    full = lambda a: pl.BlockSpec(a.shape, lambda i: (0, 0))
    return pl.pallas_call(
        _merge_kernel,
        grid=(t // tm,),
        in_specs=[pl.BlockSpec((tm, d), tok)] + [pl.BlockSpec((tm, 512), tok)] * 3
                 + [full(a) for a in (w_gate, b_gate, w_fo, w_na, w_mo, w_out, ln_g, ln_b)],
        out_specs=pl.BlockSpec((tm * SUBLANES, LANES), tok),
        out_shape=jax.ShapeDtypeStruct((t * SUBLANES, LANES), F32),
        compiler_params=_cparams("parallel"),
        name="merge",
    )(x2d, fo, na, mo, w_gate, b_gate, w_fo, w_na, w_mo, w_out, ln_g, ln_b)


def _router_kernel(ht_ref, whi_ref, wlo_ref, b_ref, oi_ref, of_ref, cnt_ref, carry_ref):
    tm = ht_ref.shape[0] // SUBLANES

    @pl.when(pl.program_id(0) == 0)
    def _():
        carry_ref[...] = jnp.zeros_like(carry_ref)

    h = _load_tile_rows(ht_ref)
    hh = h.astype(BF)
    hl = (h - hh.astype(F32)).astype(BF)
    logits = _dot(hh, whi_ref[...]) + _dot(hh, wlo_ref[...]) + _dot(hl, whi_ref[...]) + b_ref[...]
    lane = lax.broadcasted_iota(I32, (tm, ROUTE_LANES), 1)
    is_group = lane < N_GROUPS
    gl = jnp.where(is_group, logits, MASK_NEG)
    gmax = jnp.max(gl, axis=-1, keepdims=True)
    g_idx = jnp.min(jnp.where(gl == gmax, lane, ROUTE_LANES), axis=-1, keepdims=True)
    p_group = 1.0 / jnp.sum(jnp.where(is_group, jnp.exp(gl - gmax), 0.0), axis=-1, keepdims=True)
    e_lane = lane - N_GROUPS
    in_group = (e_lane >= 0) & (e_lane < N_EXPERTS) & ((e_lane >> 3) == g_idx)
    el = jnp.where(in_group, logits, MASK_NEG)
    v1 = jnp.max(el, axis=-1, keepdims=True)
    i1 = jnp.min(jnp.where(el == v1, lane, ROUTE_LANES), axis=-1, keepdims=True)
    el2 = jnp.where(lane == i1, MASK_NEG, el)
    v2 = jnp.max(el2, axis=-1, keepdims=True)
    i2 = jnp.min(jnp.where(el2 == v2, lane, ROUTE_LANES), axis=-1, keepdims=True)
    tt = jnp.exp(v2 - v1)
    w1 = p_group / (1.0 + tt)
    w2 = p_group * tt / (1.0 + tt)
    sel1 = lane == i1
    sel2 = lane == i2
    onehot = jnp.where(sel1 | sel2, 1.0, 0.0)
    row = lax.broadcasted_iota(I32, (tm, tm), 0)
    col = lax.broadcasted_iota(I32, (tm, tm), 1)
    tri = jnp.where(row > col, 1.0, 0.0).astype(BF)
    prefix = _dot(tri, onehot.astype(BF)) + carry_ref[...]
    r1 = jnp.sum(jnp.where(sel1, prefix, 0.0), axis=-1, keepdims=True)
    r2 = jnp.sum(jnp.where(sel2, prefix, 0.0), axis=-1, keepdims=True)
    carry_ref[...] += jnp.sum(onehot, axis=0, keepdims=True)
    cnt_ref[...] = carry_ref[...]
    sub = pl.program_id(0) % ROUTE_STEPS_PER_PLANE
    diag = lane == (lax.broadcasted_iota(I32, (tm, ROUTE_LANES), 0) % ROUTE_LANES)
    for f, col in enumerate(((i1 - N_GROUPS).astype(F32), (i2 - N_GROUPS).astype(F32), r1, r2)):
        plane = jnp.sum(jnp.where(diag, col, 0.0).reshape(tm // ROUTE_LANES, ROUTE_LANES, ROUTE_LANES), axis=1)
        oi_ref[f, pl.ds(sub * (tm // ROUTE_LANES), tm // ROUTE_LANES), :] = plane.astype(I32)
    of = jnp.where(lane == 0, w1, jnp.where(lane == 1, w2, 0.0))
    of_ref[...] = of[:, :8]


def _router(ht, w_rg, b_rg, w_re, b_re):
    t = ht.shape[0] // SUBLANES
    d = w_rg.shape[0]
    tm = TM_ROUTE
    pad = ROUTE_LANES - N_GROUPS - N_EXPERTS
    w = jnp.concatenate([w_rg.astype(F32), w_re.astype(F32), jnp.zeros((d, pad), F32)], axis=1)
    b = jnp.concatenate([b_rg.astype(F32), b_re.astype(F32), jnp.zeros((pad,), F32)])[None, :]
    w_hi = w.astype(BF)
    w_lo = (w - w_hi.astype(F32)).astype(BF)
    tok = lambda i: (i, 0)
    const = lambda i: (0, 0)
    return pl.pallas_call(
        _router_kernel,
        grid=(t // tm,),
        in_specs=[pl.BlockSpec((tm * SUBLANES, LANES), tok),
                  pl.BlockSpec((d, ROUTE_LANES), const),
                  pl.BlockSpec((d, ROUTE_LANES), const),
                  pl.BlockSpec((1, ROUTE_LANES), const)],
        out_specs=[pl.BlockSpec((4, SUBLANES, LANES), lambda i: (0, i // ROUTE_STEPS_PER_PLANE, 0)),
                   pl.BlockSpec((tm, 8), tok),
                   pl.BlockSpec((1, ROUTE_LANES), const)],
        out_shape=[jax.ShapeDtypeStruct((4, t // LANES, LANES), I32), jax.ShapeDtypeStruct((t, 8), F32),
                   jax.ShapeDtypeStruct((1, ROUTE_LANES), F32)],
        scratch_shapes=[pltpu.VMEM((1, ROUTE_LANES), F32)],
        compiler_params=_cparams("arbitrary"),
        name="router",
    )(ht, w_hi, w_lo, b)


def _tile_copy(src_ref, src_row, dst_ref, dst_row, sem):
    src = pl.multiple_of(src_row * SUBLANES, SUBLANES)
    dst = pl.multiple_of(dst_row * SUBLANES, SUBLANES)
    return pltpu.make_async_copy(src_ref.at[pl.ds(src, SUBLANES)], dst_ref.at[pl.ds(dst, SUBLANES)], sem)


def _row_groups(n_rows, copies_of_row, start):
    def group(g, c):
        copies = []
        for u in range(ROW_GROUP):
            copies.extend(copies_of_row(g * ROW_GROUP + u))
        for cp in copies:
            if start:
                cp.start()
            else:
                cp.wait()
        return c

    lax.fori_loop(0, n_rows // ROW_GROUP, group, 0)


def _dests_kernel(ps_ref, idx_ref, d_ref):
    for a in range(2):
        e = idx_ref[a]
        d = idx_ref[2 + a]
        for k in range(N_EXPERTS):
            d = d + jnp.where(e == k, ps_ref[k], 0)
        d_ref[a] = d


def _dests(idx, pad_start):
    planes = idx.shape[1]
    out = pl.pallas_call(
        _dests_kernel,
        grid_spec=pltpu.PrefetchScalarGridSpec(
            num_scalar_prefetch=1,
            grid=(1,),
            in_specs=[pl.BlockSpec(idx.shape, lambda i, ps: (0, 0, 0))],
            out_specs=pl.BlockSpec((2, planes, LANES), lambda i, ps: (0, 0, 0)),
        ),
        out_shape=jax.ShapeDtypeStruct((2, planes, LANES), I32),
        compiler_params=_cparams("arbitrary"),
        name="dests",
    )(pad_start, idx)
    return out[0].reshape(-1), out[1].reshape(-1)


def _dispatch_kernel(ps_ref, pe_ref, d1_ref, d2_ref, ht_ref, xg_ref, zbuf, sem, zsem):
    i = pl.program_id(0)
    tm = d1_ref.shape[0]
    blk = MOE_BLOCK * SUBLANES
    n_blocks = xg_ref.shape[0] // blk

    @pl.when(i == 0)
    def _():
        zbuf[...] = jnp.zeros_like(zbuf)

        def zero_copy(block):
            return pltpu.make_async_copy(zbuf, xg_ref.at[pl.ds(pl.multiple_of(block * blk, blk), blk)], zsem)

        def seg_issue(e, c):
            @pl.when(pe_ref[e] > ps_ref[e])
            def _():
                zero_copy(pe_ref[e] // MOE_BLOCK - 1).start()
            return c

        def seg_drain(e, c):
            @pl.when(pe_ref[e] > ps_ref[e])
            def _():
                zero_copy(pe_ref[e] // MOE_BLOCK - 1).wait()
            return c

        def tail_issue(b, c):
            zero_copy(b).start()
            return c

        def tail_drain(b, c):
            zero_copy(b).wait()
            return c

        first_tail = pe_ref[N_EXPERTS - 1] // MOE_BLOCK
        lax.fori_loop(0, N_EXPERTS, seg_issue, 0)
        lax.fori_loop(first_tail, n_blocks, tail_issue, 0)
        lax.fori_loop(0, N_EXPERTS, seg_drain, 0)
        lax.fori_loop(first_tail, n_blocks, tail_drain, 0)

    _row_groups(tm, lambda t: [_tile_copy(ht_ref, t, xg_ref, d1_ref[t], sem),
                               _tile_copy(ht_ref, t, xg_ref, d2_ref[t], sem)], True)
    _row_groups(tm, lambda t: [_tile_copy(ht_ref, 0, xg_ref, 0, sem)] * 2, False)


def _dispatch(ht, d1, d2, pad_start, pad_end, n_pad):
    t = d1.shape[0]
    tm = TM_DISPATCH
    smem = lambda: pl.BlockSpec((tm,), lambda i, ps, pe: (i,), memory_space=pltpu.SMEM)
    return pl.pallas_call(
        _dispatch_kernel,
        grid_spec=pltpu.PrefetchScalarGridSpec(
            num_scalar_prefetch=2,
            grid=(t // tm,),
            in_specs=[smem(), smem(), pl.BlockSpec((tm * SUBLANES, LANES), lambda i, ps, pe: (i, 0))],
            out_specs=pl.BlockSpec(memory_space=pl.ANY),
            scratch_shapes=[pltpu.VMEM((MOE_BLOCK * SUBLANES, LANES), F32), pltpu.SemaphoreType.DMA(()),
                            pltpu.SemaphoreType.DMA(())],
        ),
        out_shape=jax.ShapeDtypeStruct((n_pad * SUBLANES, LANES), F32),
        compiler_params=_cparams("arbitrary"),
        name="dispatch",
    )(pad_start, pad_end, d1, d2, ht)


def _expert_kernel(be_ref, nu_ref, x_ref, wg_ref, wu_ref, wd_ref, y_ref, wg_s, wu_s, wd_s):
    j = pl.program_id(0)

    @pl.when(j < nu_ref[0])
    def _():
        prev = be_ref[jnp.maximum(j - 1, 0)]

        @pl.when((j == 0) | (be_ref[j] != prev))
        def _():
            wg_s[...] = wg_ref[...].astype(BF)
            wu_s[...] = wu_ref[...].astype(BF)
            wd_s[...] = wd_ref[...].astype(BF)

        xb = _load_tile_rows(x_ref).astype(BF)
        a = _dot(xb, wg_s[...])
        u = _dot(xb, wu_s[...])
        mid = (a / (1.0 + jnp.exp(-a))) * u
        _store_tile_rows(y_ref, _dot(mid.astype(BF), wd_s[...]))

    @pl.when(j >= nu_ref[0])
    def _():
        y_ref[...] = jnp.zeros_like(y_ref)


def _experts(xg, blk_expert, n_used, w_eg, w_eu, w_ed):
    blk = MOE_BLOCK * SUBLANES
    nb = xg.shape[0] // blk
    d, de = w_eg.shape[1], w_eg.shape[2]
    rows = lambda j, be, nu: (jnp.minimum(j, nu[0] - 1), 0)
    wsel = lambda j, be, nu: (be[j], 0, 0)
    return pl.pallas_call(
        _expert_kernel,
        grid_spec=pltpu.PrefetchScalarGridSpec(
            num_scalar_prefetch=2,
            grid=(nb,),
            in_specs=[pl.BlockSpec((blk, LANES), rows),
                      pl.BlockSpec((None, d, de), wsel),
                      pl.BlockSpec((None, d, de), wsel),
                      pl.BlockSpec((None, de, d), wsel)],
            out_specs=pl.BlockSpec((blk, LANES), lambda j, be, nu: (j, 0)),
            scratch_shapes=[pltpu.VMEM((d, de), BF), pltpu.VMEM((d, de), BF), pltpu.VMEM((de, d), BF)],
        ),
        out_shape=jax.ShapeDtypeStruct(xg.shape, F32),
        compiler_params=_cparams("arbitrary"),
        name="experts",
    )(blk_expert, n_used, xg, w_eg, w_eu, w_ed)


def _combine_kernel(d1_ref, d2_ref, d1_next_ref, d2_next_ref, wt_ref, ht_ref, yb_ref, g_ref, b_ref, o_ref,
                    buf, sems):
    i = pl.program_id(0)
    n = pl.num_programs(0)
    tm = d1_ref.shape[0]

    def gather_tile(da_ref, db_ref, slot, start):
        def copies(t):
            if not start:
                return [_tile_copy(yb_ref, 0, buf.at[slot, a], 0, sems.at[slot]) for a in range(2)]
            return [_tile_copy(yb_ref, da_ref[t], buf.at[slot, 0], t, sems.at[slot]),
                    _tile_copy(yb_ref, db_ref[t], buf.at[slot, 1], t, sems.at[slot])]
        _row_groups(tm, copies, start)

    slot = i % 2

    @pl.when(i == 0)
    def _():
        gather_tile(d1_ref, d2_ref, 0, True)

    @pl.when(i + 1 < n)
    def _():
        gather_tile(d1_next_ref, d2_next_ref, 1 - slot, True)

    gather_tile(d1_ref, d2_ref, slot, False)
    wt = wt_ref[...]
    ffn = wt[:, 0:1] * _load_tile_rows(buf.at[slot, 0]) + wt[:, 1:2] * _load_tile_rows(buf.at[slot, 1])
    o_ref[...] = _layer_norm(ALPHA * _load_tile_rows(ht_ref) + ffn, g_ref[...], b_ref[...])


def _combine(ht, yb, d1, d2, wts, ln_g, ln_b):
    t = d1.shape[0]
    d = SUBLANES * LANES
    tm = TM_COMBINE
    n = t // tm
    tok = lambda i: (i, 0)
    const = lambda i: (0, 0)
    cur = lambda: pl.BlockSpec((tm,), lambda i: (i,), memory_space=pltpu.SMEM)
    nxt = lambda: pl.BlockSpec((tm,), lambda i: (jnp.minimum(i + 1, n - 1),), memory_space=pltpu.SMEM)
    return pl.pallas_call(
        _combine_kernel,
        grid=(n,),
        in_specs=[cur(), cur(), nxt(), nxt(),
                  pl.BlockSpec((tm, 8), tok),
                  pl.BlockSpec((tm * SUBLANES, LANES), tok),
                  pl.BlockSpec(memory_space=pl.ANY),
                  pl.BlockSpec((1, d), const), pl.BlockSpec((1, d), const)],
        out_specs=pl.BlockSpec((tm, d), tok),
        out_shape=jax.ShapeDtypeStruct((t, d), F32),
        scratch_shapes=[pltpu.VMEM((2, 2, tm * SUBLANES, LANES), F32), pltpu.SemaphoreType.DMA((2,))],
        compiler_params=_cparams("arbitrary"),
        name="combine",
    )(d1, d2, d1, d2, wts, ht, yb, ln_g, ln_b)


def _token_mixing(x2d, mem2d, batch, seq, m_len, w_in, w_gate, b_gate, w_mem_kv, rpb, w_fo, w_na, w_mo, w_out,
                  ln_g, ln_b):
    k_mem, v_mem = _memkv(mem2d, w_mem_kv.astype(BF), m_len)
    u_f, q, k, v, mo = _proj(x2d, w_in.astype(BF), k_mem, v_mem, seq, m_len)
    fo = _fourier(u_f, batch, seq)
    na = _natten(q, k, v, rpb, batch, seq)
    return _merge(x2d, fo, na, mo, w_gate.astype(BF), b_gate.astype(F32)[None, :], w_fo.astype(BF),
                  w_na.astype(BF), w_mo.astype(BF), w_out.astype(BF), ln_g.astype(F32)[None, :],
                  ln_b.astype(F32)[None, :])


def _moe(ht, w_rg, b_rg, w_re, b_re, w_eg, w_eu, w_ed, ln_g, ln_b):
    t = ht.shape[0] // SUBLANES
    idx, wts, cnt = _router(ht, w_rg, b_rg, w_re, b_re)
    counts = cnt[0, N_GROUPS:N_GROUPS + N_EXPERTS].astype(I32)
    padded = (counts + MOE_BLOCK - 1) // MOE_BLOCK * MOE_BLOCK
    pad_end = jnp.cumsum(padded).astype(I32)
    pad_start = pad_end - padded
    n_pad = 2 * t + N_EXPERTS * MOE_BLOCK
    nb = n_pad // MOE_BLOCK
    blk_start = jnp.arange(nb, dtype=I32) * MOE_BLOCK
    blk_expert = jnp.minimum(jnp.sum((pad_end[None, :] <= blk_start[:, None]).astype(I32), axis=1), N_EXPERTS - 1)
    n_used = pad_end[-1:] // MOE_BLOCK
    d1, d2 = _dests(idx, pad_start)
    xg = _dispatch(ht, d1, d2, pad_start, pad_end, n_pad)
    yb = _experts(xg, blk_expert, n_used, w_eg, w_eu, w_ed)
    return _combine(ht, yb, d1, d2, wts, ln_g.astype(F32)[None, :], ln_b.astype(F32)[None, :])


def kernel(x, mem, w_in, w_gate, b_gate, w_mem_kv, rpb, w_fourier_o, w_na_o, w_mem_o, w_out, ln1_g, ln1_b,
           w_router_group, b_router_group, w_router_expert, b_router_expert, w_exp_gate, w_exp_up,
           w_exp_down, ln2_g, ln2_b):
    batch, seq, d = x.shape
    m_len = mem.shape[1]
    x2d = x.reshape(batch * seq, d)
    mem2d = mem.reshape(batch * m_len, d)
    for l in range(w_in.shape[0]):
        ht = _token_mixing(x2d, mem2d, batch, seq, m_len, w_in[l], w_gate[l], b_gate[l], w_mem_kv[l], rpb[l],
                           w_fourier_o[l], w_na_o[l], w_mem_o[l], w_out[l], ln1_g[l], ln1_b[l])
        x2d = _moe(ht, w_router_group[l], b_router_group[l], w_router_expert[l], b_router_expert[l],
                   w_exp_gate[l], w_exp_up[l], w_exp_down[l], ln2_g[l], ln2_b[l])
    return x2d.reshape(batch, seq, d)
```

```python
import functools

import numpy as np
import jax
import jax.numpy as jnp
from jax import lax
from jax.experimental import pallas as pl
from jax.experimental.pallas import tpu as pltpu

BF = jnp.bfloat16
F32 = jnp.float32
I32 = jnp.int32

GRID_W = 64
MEM_HEADS = 4
MEM_HEAD_DIM = 128
MEM_DIM = MEM_HEADS * MEM_HEAD_DIM
FOURIER_GROUPS = 4
FOURIER_GROUP_DIM = 128
FOURIER_DIM = FOURIER_GROUPS * FOURIER_GROUP_DIM
NA_HEADS = 8
NA_HEAD_DIM = 64
NA_DIM = NA_HEADS * NA_HEAD_DIM
NA_ROW_WIN = 8
NA_COL_WIN = 16
N_GROUPS = 8
EXPERTS_PER_GROUP = 8
N_EXPERTS = N_GROUPS * EXPERTS_PER_GROUP
DEPTH = 1
ALPHA = (2.0 * DEPTH) ** 0.25
LN_EPS = 1e-5
NA_SCALE = NA_HEAD_DIM ** -0.5
MEM_SCALE = MEM_HEAD_DIM ** -0.5
MASK_NEG = -1e30

LANES = 128
SUBLANES = 8
FFT_N2 = 128
TM_PROJ = 512
TM_MERGE = 512
TM_ROUTE = 256
TM_DISPATCH = 1024
TM_COMBINE = 256
NA_GROUP_ROWS = 4
NA_GROUP_WIN = NA_GROUP_ROWS + NA_ROW_WIN
NA_GROUPS_PER_STEP = 4
NA_GROUP_UNROLL = 4
F1_COLS = 8192
F2_SLABS = 4
MOE_BLOCK = 256
ROUTE_LANES = 128
ROUTE_STEPS_PER_PLANE = SUBLANES * LANES // TM_ROUTE
ROW_GROUP = 8
VMEM_LIMIT = 56 * 1024 * 1024


def _cparams(*sem):
    return pltpu.CompilerParams(dimension_semantics=sem, vmem_limit_bytes=VMEM_LIMIT)


def _dot(a, b):
    return jnp.dot(a, b, preferred_element_type=F32)


def _dot_nt(a, b):
    return lax.dot_general(a, b, (((1,), (1,)), ((), ())), preferred_element_type=F32)


def _layer_norm(y, g, b):
    mu = jnp.mean(y, axis=-1, keepdims=True)
    yc = y - mu
    var = jnp.mean(yc * yc, axis=-1, keepdims=True)
    return yc * lax.rsqrt(var + LN_EPS) * g + b


def _store_tile_rows(ref, val):
    n = val.shape[0]
    for s in range(SUBLANES):
        ref[pl.ds(s, n, stride=SUBLANES), :] = val[:, s * LANES:(s + 1) * LANES]


def _load_tile_rows(ref):
    n = ref.shape[0] // SUBLANES
    return jnp.concatenate([ref[pl.ds(s, n, stride=SUBLANES), :] for s in range(SUBLANES)], axis=1)


def _memkv_kernel(mem_ref, w_ref, k_ref, v_ref):
    kv = _dot(mem_ref[...].astype(BF), w_ref[...])
    k_ref[...] = kv[:, :MEM_DIM].astype(BF)
    v_ref[...] = kv[:, MEM_DIM:].astype(BF)


def _memkv(mem2d, w_kv, m_len):
    rows, d = mem2d.shape
    return pl.pallas_call(
        _memkv_kernel,
        grid=(rows // m_len,),
        in_specs=[pl.BlockSpec((m_len, d), lambda i: (i, 0)),
                  pl.BlockSpec((d, 2 * MEM_DIM), lambda i: (0, 0))],
        out_specs=[pl.BlockSpec((m_len, MEM_DIM), lambda i: (i, 0))] * 2,
        out_shape=[jax.ShapeDtypeStruct((rows, MEM_DIM), BF)] * 2,
        compiler_params=_cparams("parallel"),
        name="memkv",
    )(mem2d, w_kv)


def _proj_kernel(x_ref, w_ref, wt_ref, km_ref, vm_ref, uf_ref, k_ref, q3_ref, v3_ref, mo_ref):
    xb = x_ref[...].astype(BF)

    def seg(j):
        return _dot(xb, w_ref[:, j * 512:(j + 1) * 512])

    uf_ref[...] = seg(0).astype(BF)
    k_ref[...] = seg(1).astype(BF)
    q_t = _dot_nt(wt_ref[:NA_DIM, :], xb) * NA_SCALE
    v_t = _dot_nt(wt_ref[NA_DIM:, :], xb)
    for s in range(q3_ref.shape[0]):
        q3_ref[s] = q_t[:, s * LANES:(s + 1) * LANES].astype(BF)
        v3_ref[s] = v_t[:, s * LANES:(s + 1) * LANES].astype(BF)
    qm = seg(2).astype(BF)
    for h in range(MEM_HEADS):
        sl = slice(h * MEM_HEAD_DIM, (h + 1) * MEM_HEAD_DIM)
        s = _dot_nt(qm[:, sl], km_ref[:, sl]) * MEM_SCALE
        m = jnp.max(s, axis=-1, keepdims=True)
        p = jnp.exp(s - m)
        l = jnp.sum(p, axis=-1, keepdims=True)
        o = _dot(p.astype(BF), vm_ref[:, sl])
        mo_ref[:, sl] = (o / l).astype(BF)


def _proj(x2d, w_in, k_mem, v_mem, seq, m_len):
    t, d = x2d.shape
    tm = TM_PROJ
    tiles_per_batch = seq // tm
    w_tok = jnp.concatenate([w_in[:, 0:512], w_in[:, 1024:1536], w_in[:, 2048:2560]], axis=1).astype(BF)
    w_chan = jnp.concatenate([w_in[:, 512:1024], w_in[:, 1536:2048]], axis=1).T.astype(BF)
    tok = lambda i: (i, 0)
    slab = lambda i: (i, 0, 0)
    memb = lambda i: (i // tiles_per_batch, 0)
    const = lambda i: (0, 0)
    tok_out = pl.BlockSpec((tm, 512), tok)
    slab_out = pl.BlockSpec((tm // LANES, NA_DIM, LANES), slab)
    slab_shape = jax.ShapeDtypeStruct((t // LANES, NA_DIM, LANES), BF)
    return pl.pallas_call(
        _proj_kernel,
        grid=(t // tm,),
        in_specs=[pl.BlockSpec((tm, d), tok),
                  pl.BlockSpec(w_tok.shape, const),
                  pl.BlockSpec(w_chan.shape, const),
                  pl.BlockSpec((m_len, MEM_DIM), memb),
                  pl.BlockSpec((m_len, MEM_DIM), memb)],
        out_specs=[tok_out, tok_out, slab_out, slab_out, tok_out],
        out_shape=[jax.ShapeDtypeStruct((t, 512), BF)] * 2 + [slab_shape] * 2
                  + [jax.ShapeDtypeStruct((t, 512), BF)],
        compiler_params=_cparams("parallel"),
        name="proj",
    )(x2d, w_tok, w_chan, k_mem, v_mem)


def _fourier_tables(n1, n2):
    n = n1 * n2
    k1 = np.arange(n1)
    ang1 = 2.0 * np.pi * ((k1[:, None] * k1[None, :]) % n1) / n1
    norm = 1.0 / np.sqrt(float(n) * FOURIER_GROUP_DIM)
    m1 = np.concatenate([np.cos(ang1), -np.sin(ang1)], axis=0) * norm
    kk = k1[:, None, None] + n1 * np.arange(n2)[None, :, None]
    nn = np.arange(n2)[None, None, :]
    ang2 = 2.0 * np.pi * ((kk * nn) % n) / n
    c2, s2 = np.cos(ang2), np.sin(ang2)
    m2 = np.concatenate([np.concatenate([c2, s2], axis=2),
                         np.concatenate([-s2, c2], axis=2)], axis=1)
    c = np.arange(FOURIER_GROUP_DIM)
    angc = 2.0 * np.pi * ((c[:, None] * c[None, :]) % FOURIER_GROUP_DIM) / FOURIER_GROUP_DIM
    mc = np.concatenate([np.cos(angc), np.sin(angc)], axis=0)
    as_bf = lambda a: jnp.asarray(a.astype(np.float32)).astype(BF)
    return as_bf(m1), as_bf(m2), as_bf(mc)


def _f1_kernel(u_ref, m_ref, a_ref):
    n1 = u_ref.shape[0]
    r = _dot(m_ref[...], u_ref[...])
    a_ref[0] = r[:n1].astype(BF)
    a_ref[1] = r[n1:].astype(BF)


def _f2_kernel(a_ref, m2_ref, mc_ref, o_ref):
    slabs = a_ref.shape[1]
    n2 = a_ref.shape[2]
    for t in range(slabs):
        a = jnp.concatenate([a_ref[0, t], a_ref[1, t]], axis=0)
        y = _dot(m2_ref[t], a)
        yr = y[:n2].astype(BF)
        yi = y[n2:].astype(BF)
        for g in range(FOURIER_GROUPS):
            sl = slice(g * LANES, (g + 1) * LANES)
            yy = jnp.concatenate([yr[:, sl], yi[:, sl]], axis=1)
            z = _dot(yy, mc_ref[...])
            o_ref[:, t * FOURIER_DIM + g * LANES: t * FOURIER_DIM + (g + 1) * LANES] = z.astype(BF)


def _fourier(u_f, batch, seq):
    n2 = FFT_N2
    n1 = seq // n2
    m1, m2, mc = _fourier_tables(n1, n2)
    cols = n2 * FOURIER_DIM
    tn = min(F1_COLS, cols)
    a = pl.pallas_call(
        _f1_kernel,
        grid=(batch, cols // tn),
        in_specs=[pl.BlockSpec((None, n1, tn), lambda b, j: (b, 0, j)),
                  pl.BlockSpec((2 * n1, n1), lambda b, j: (0, 0))],
        out_specs=pl.BlockSpec((None, 2, n1, tn), lambda b, j: (b, 0, 0, j)),
        out_shape=jax.ShapeDtypeStruct((batch, 2, n1, cols), BF),
        compiler_params=_cparams("parallel", "parallel"),
        name="fourier1",
    )(u_f.reshape(batch, n1, cols), m1)
    kb = min(F2_SLABS, n1)
    out = pl.pallas_call(
        _f2_kernel,
        grid=(batch, n1 // kb),
        in_specs=[pl.BlockSpec((None, 2, kb, n2, FOURIER_DIM), lambda b, j: (b, 0, j, 0, 0)),
                  pl.BlockSpec((kb, 2 * n2, 2 * n2), lambda b, j: (j, 0, 0)),
                  pl.BlockSpec((2 * FOURIER_GROUP_DIM, FOURIER_GROUP_DIM), lambda b, j: (0, 0))],
        out_specs=pl.BlockSpec((None, n2, kb * FOURIER_DIM), lambda b, j: (b, 0, j)),
        out_shape=jax.ShapeDtypeStruct((batch, n2, n1 * FOURIER_DIM), BF),
        compiler_params=_cparams("parallel", "parallel"),
        name="fourier2",
    )(a.reshape(batch, 2, n1, n2, FOURIER_DIM), m2, mc)
    return out.reshape(batch * seq, FOURIER_DIM)


def _na_group_start(r0, n_rows):
    rs0 = min(max(r0 - NA_ROW_WIN // 2, 0), n_rows - NA_ROW_WIN)
    return min(rs0, n_rows - NA_GROUP_WIN)


def _na_bias_table(rpb, n_rows):
    g = NA_GROUP_ROWS
    assert n_rows >= 2 * NA_GROUP_WIN and n_rows % g == 0 and (NA_ROW_WIN // 2) % g == 0
    cols = np.arange(GRID_W)
    cs = np.clip(cols - NA_COL_WIN // 2, 0, GRID_W - NA_COL_WIN)
    kc = np.arange(GRID_W)
    in_win = (kc[None, :] >= cs[:, None]) & (kc[None, :] < cs[:, None] + NA_COL_WIN)
    dc = kc[None, :] - cols[:, None] + (NA_COL_WIN - 1)
    n_dc = 2 * NA_COL_WIN - 1
    pick = ((dc.reshape(-1)[None, :] == np.arange(n_dc)[:, None]) & in_win.reshape(-1)[None, :])
    picked = jnp.dot(rpb.astype(F32).reshape(-1, n_dc), jnp.asarray(pick.astype(np.float32)),
                     precision=lax.Precision.HIGHEST)
    picked = picked.reshape(NA_HEADS, 2 * NA_ROW_WIN - 1, GRID_W, GRID_W)
    full_t = jnp.where(in_win[None, None], picked, MASK_NEG).transpose(0, 1, 3, 2)
    masked = jnp.full((NA_HEADS, GRID_W, GRID_W), MASK_NEG, F32)
    half = NA_ROW_WIN // 2
    group_rows = list(range(0, half, g)) + [half] + list(range(n_rows - half, n_rows, g))
    tabs = []
    for r0 in group_rows:
        start = _na_group_start(r0, n_rows)
        key_rows = []
        for kr in range(NA_GROUP_WIN):
            blocks = []
            for rr in range(g):
                r = r0 + rr
                rs = min(max(r - half, 0), n_rows - NA_ROW_WIN)
                in_rows = rs <= start + kr < rs + NA_ROW_WIN
                blocks.append(full_t[:, start + kr - r + NA_ROW_WIN - 1] if in_rows else masked)
            key_rows.append(jnp.concatenate(blocks, axis=2))
        tabs.append(jnp.concatenate(key_rows, axis=1))
    return jnp.stack(tabs, axis=0)


def _na_kernel(q3_ref, k_ref, v3_ref, tab_ref, o_ref, *, n_rows):
    i = pl.program_id(2)
    g = NA_GROUP_ROWS
    slabs_per_group = g * GRID_W // LANES
    groups = q3_ref.shape[0] // slabs_per_group
    win = NA_GROUP_WIN * GRID_W
    tokens = g * GRID_W
    half = NA_ROW_WIN // 2
    n_top = half // g
    chan = lax.broadcasted_iota(I32, (2 * NA_HEAD_DIM, tokens), 0)
    first_head = chan < NA_HEAD_DIM

    def group(jj, carry):
        r0 = (i * groups + jj) * g
        start = jnp.minimum(jnp.clip(r0 - half, 0, n_rows - NA_ROW_WIN), n_rows - NA_GROUP_WIN)
        cfg = jnp.where(r0 < half, r0 // g,
                        jnp.where(r0 >= n_rows - half, n_top + 1 + (r0 - (n_rows - half)) // g, n_top))
        kw = k_ref[pl.ds(pl.multiple_of(start * GRID_W, LANES), win), :]
        qt = jnp.concatenate([q3_ref[jj * slabs_per_group + t] for t in range(slabs_per_group)], axis=1)
        zero = jnp.zeros_like(qt)
        q_bd = jnp.concatenate([jnp.where(first_head, qt, zero), jnp.where(first_head, zero, qt)], axis=1)
        s = _dot(kw, q_bd) + jnp.concatenate([tab_ref[cfg, 0], tab_ref[cfg, 1]], axis=1)
        p = jnp.exp(s - jnp.max(s, axis=0, keepdims=True))
        l = jnp.sum(p, axis=0, keepdims=True)
        slab = start // 2
        vw = jnp.concatenate([v3_ref[slab + t] for t in range(NA_GROUP_WIN // 2)], axis=1)
        o_t = _dot(vw, p.astype(BF)) / l
        o = jnp.where(first_head, o_t[:, :tokens], o_t[:, tokens:])
        o_ref[pl.ds(pl.multiple_of(jj * tokens, tokens), tokens), :] = jnp.transpose(o).astype(BF)
        return carry

    lax.fori_loop(0, groups, group, 0, unroll=NA_GROUP_UNROLL)


def _natten(q3, k, v3, rpb, batch, seq):
    assert 2 * GRID_W == LANES and 2 * NA_HEAD_DIM == LANES and NA_GROUP_ROWS % 2 == 0
    n_rows = seq // GRID_W
    slabs = n_rows // 2
    step_slabs = min(NA_GROUPS_PER_STEP * NA_GROUP_ROWS // 2, slabs)
    steps = slabs // step_slabs
    hp = NA_HEADS // 2
    tab = _na_bias_table(rpb, n_rows)
    return pl.pallas_call(
        functools.partial(_na_kernel, n_rows=n_rows),
        grid=(batch, hp, steps),
        in_specs=[pl.BlockSpec((step_slabs, LANES, LANES), lambda b, p, i: (b * steps + i, p, 0)),
                  pl.BlockSpec((seq, LANES), lambda b, p, i: (b, p)),
                  pl.BlockSpec((slabs, LANES, LANES), lambda b, p, i: (b, p, 0)),
                  pl.BlockSpec((tab.shape[0], 2) + tab.shape[2:], lambda b, p, i: (0, p, 0, 0))],
        out_specs=pl.BlockSpec((step_slabs * LANES, LANES), lambda b, p, i: (b * steps + i, p)),
        out_shape=jax.ShapeDtypeStruct((batch * seq, NA_DIM), BF),
        compiler_params=_cparams("parallel", "parallel", "parallel"),
        name="natten",
    )(q3, k, v3, tab)


def _merge_kernel(x_ref, fo_ref, na_ref, mo_ref, wg_ref, bg_ref, wf_ref, wn_ref, wm_ref, wo_ref,
                  g_ref, b_ref, ht_ref):
    d = x_ref.shape[1]
    x = x_ref[...]
    xb = x.astype(BF)
    merged = None
    for j, (br_ref, w_ref) in enumerate(((fo_ref, wf_ref), (na_ref, wn_ref), (mo_ref, wm_ref))):
        z = _dot(xb, wg_ref[:, j * d:(j + 1) * d]) + bg_ref[:, j * d:(j + 1) * d]
        gate = 1.0 / (1.0 + jnp.exp(-z))
        term = gate * _dot(br_ref[...], w_ref[...])
        merged = term if merged is None else merged + term
    mix = _dot(merged.astype(BF), wo_ref[...])
    _store_tile_rows(ht_ref, _layer_norm(ALPHA * x + mix, g_ref[...], b_ref[...]))


def _merge(x2d, fo, na, mo, w_gate, b_gate, w_fo, w_na, w_mo, w_out, ln_g, ln_b):
    t, d = x2d.shape
    assert d == SUBLANES * LANES
    tm = TM_MERGE
    tok = lambda i: (i, 0)
    full = lambda a: pl.BlockSpec(a.shape, lambda i: (0, 0))
    return pl.pallas_call(
        _merge_kernel,
        grid=(t // tm,),
        in_specs=[pl.BlockSpec((tm, d), tok)] + [pl.BlockSpec((tm, 512), tok)] * 3
                 + [full(a) for a in (w_gate, b_gate, w_fo, w_na, w_mo, w_out, ln_g, ln_b)],
        out_specs=pl.BlockSpec((tm * SUBLANES, LANES), tok),
        out_shape=jax.ShapeDtypeStruct((t * SUBLANES, LANES), F32),
        compiler_params=_cparams("parallel"),
        name="merge",
    )(x2d, fo, na, mo, w_gate, b_gate, w_fo, w_na, w_mo, w_out, ln_g, ln_b)


def _router_kernel(ht_ref, whi_ref, wlo_ref, b_ref, oi_ref, of_ref, cnt_ref, carry_ref):
    tm = ht_ref.shape[0] // SUBLANES

    @pl.when(pl.program_id(0) == 0)
    def _():
        carry_ref[...] = jnp.zeros_like(carry_ref)

    h = _load_tile_rows(ht_ref)
    hh = h.astype(BF)
    hl = (h - hh.astype(F32)).astype(BF)
    logits = _dot(hh, whi_ref[...]) + _dot(hh, wlo_ref[...]) + _dot(hl, whi_ref[...]) + b_ref[...]
    lane = lax.broadcasted_iota(I32, (tm, ROUTE_LANES), 1)
    is_group = lane < N_GROUPS
    gl = jnp.where(is_group, logits, MASK_NEG)
    gmax = jnp.max(gl, axis=-1, keepdims=True)
    g_idx = jnp.min(jnp.where(gl == gmax, lane, ROUTE_LANES), axis=-1, keepdims=True)
    p_group = 1.0 / jnp.sum(jnp.where(is_group, jnp.exp(gl - gmax), 0.0), axis=-1, keepdims=True)
    e_lane = lane - N_GROUPS
    in_group = (e_lane >= 0) & (e_lane < N_EXPERTS) & ((e_lane >> 3) == g_idx)
    el = jnp.where(in_group, logits, MASK_NEG)
    v1 = jnp.max(el, axis=-1, keepdims=True)
    i1 = jnp.min(jnp.where(el == v1, lane, ROUTE_LANES), axis=-1, keepdims=True)
    el2 = jnp.where(lane == i1, MASK_NEG, el)
    v2 = jnp.max(el2, axis=-1, keepdims=True)
    i2 = jnp.min(jnp.where(el2 == v2, lane, ROUTE_LANES), axis=-1, keepdims=True)
    tt = jnp.exp(v2 - v1)
    w1 = p_group / (1.0 + tt)
    w2 = p_group * tt / (1.0 + tt)
    sel1 = lane == i1
    sel2 = lane == i2
    onehot = jnp.where(sel1 | sel2, 1.0, 0.0)
    row = lax.broadcasted_iota(I32, (tm, tm), 0)
    col = lax.broadcasted_iota(I32, (tm, tm), 1)
    tri = jnp.where(row > col, 1.0, 0.0).astype(BF)
    prefix = _dot(tri, onehot.astype(BF)) + carry_ref[...]
    r1 = jnp.sum(jnp.where(sel1, prefix, 0.0), axis=-1, keepdims=True)
    r2 = jnp.sum(jnp.where(sel2, prefix, 0.0), axis=-1, keepdims=True)
    carry_ref[...] += jnp.sum(onehot, axis=0, keepdims=True)
    cnt_ref[...] = carry_ref[...]
    sub = pl.program_id(0) % ROUTE_STEPS_PER_PLANE
    diag = lane == (lax.broadcasted_iota(I32, (tm, ROUTE_LANES), 0) % ROUTE_LANES)
    for f, col in enumerate(((i1 - N_GROUPS).astype(F32), (i2 - N_GROUPS).astype(F32), r1, r2)):
        plane = jnp.sum(jnp.where(diag, col, 0.0).reshape(tm // ROUTE_LANES, ROUTE_LANES, ROUTE_LANES), axis=1)
        oi_ref[f, pl.ds(sub * (tm // ROUTE_LANES), tm // ROUTE_LANES), :] = plane.astype(I32)
    of = jnp.where(lane == 0, w1, jnp.where(lane == 1, w2, 0.0))
    of_ref[...] = of[:, :8]


def _router(ht, w_rg, b_rg, w_re, b_re):
    t = ht.shape[0] // SUBLANES
    d = w_rg.shape[0]
    tm = TM_ROUTE
    pad = ROUTE_LANES - N_GROUPS - N_EXPERTS
    w = jnp.concatenate([w_rg.astype(F32), w_re.astype(F32), jnp.zeros((d, pad), F32)], axis=1)
    b = jnp.concatenate([b_rg.astype(F32), b_re.astype(F32), jnp.zeros((pad,), F32)])[None, :]
    w_hi = w.astype(BF)
    w_lo = (w - w_hi.astype(F32)).astype(BF)
    tok = lambda i: (i, 0)
    const = lambda i: (0, 0)
    return pl.pallas_call(
        _router_kernel,
        grid=(t // tm,),
        in_specs=[pl.BlockSpec((tm * SUBLANES, LANES), tok),
                  pl.BlockSpec((d, ROUTE_LANES), const),
                  pl.BlockSpec((d, ROUTE_LANES), const),
                  pl.BlockSpec((1, ROUTE_LANES), const)],
        out_specs=[pl.BlockSpec((4, SUBLANES, LANES), lambda i: (0, i // ROUTE_STEPS_PER_PLANE, 0)),
                   pl.BlockSpec((tm, 8), tok),
                   pl.BlockSpec((1, ROUTE_LANES), const)],
        out_shape=[jax.ShapeDtypeStruct((4, t // LANES, LANES), I32), jax.ShapeDtypeStruct((t, 8), F32),
                   jax.ShapeDtypeStruct((1, ROUTE_LANES), F32)],
        scratch_shapes=[pltpu.VMEM((1, ROUTE_LANES), F32)],
        compiler_params=_cparams("arbitrary"),
        name="router",
    )(ht, w_hi, w_lo, b)


def _tile_copy(src_ref, src_row, dst_ref, dst_row, sem):
    src = pl.multiple_of(src_row * SUBLANES, SUBLANES)
    dst = pl.multiple_of(dst_row * SUBLANES, SUBLANES)
    return pltpu.make_async_copy(src_ref.at[pl.ds(src, SUBLANES)], dst_ref.at[pl.ds(dst, SUBLANES)], sem)


def _row_groups(n_rows, copies_of_row, start):
    def group(g, c):
        copies = []
        for u in range(ROW_GROUP):
            copies.extend(copies_of_row(g * ROW_GROUP + u))
        for cp in copies:
            if start:
                cp.start()
            else:
                cp.wait()
        return c

    lax.fori_loop(0, n_rows // ROW_GROUP, group, 0)


def _dests_kernel(ps_ref, idx_ref, d_ref):
    for a in range(2):
        e = idx_ref[a]
        d = idx_ref[2 + a]
        for k in range(N_EXPERTS):
            d = d + jnp.where(e == k, ps_ref[k], 0)
        d_ref[a] = d


def _dests(idx, pad_start):
    planes = idx.shape[1]
    out = pl.pallas_call(
        _dests_kernel,
        grid_spec=pltpu.PrefetchScalarGridSpec(
            num_scalar_prefetch=1,
            grid=(1,),
            in_specs=[pl.BlockSpec(idx.shape, lambda i, ps: (0, 0, 0))],
            out_specs=pl.BlockSpec((2, planes, LANES), lambda i, ps: (0, 0, 0)),
        ),
        out_shape=jax.ShapeDtypeStruct((2, planes, LANES), I32),
        compiler_params=_cparams("arbitrary"),
        name="dests",
    )(pad_start, idx)
    return out[0].reshape(-1), out[1].reshape(-1)


def _dispatch_kernel(ps_ref, pe_ref, d1_ref, d2_ref, ht_ref, xg_ref, zbuf, sem, zsem):
    i = pl.program_id(0)
    tm = d1_ref.shape[0]
    blk = MOE_BLOCK * SUBLANES
    n_blocks = xg_ref.shape[0] // blk

    @pl.when(i == 0)
    def _():
        zbuf[...] = jnp.zeros_like(zbuf)

        def zero_copy(block):
            return pltpu.make_async_copy(zbuf, xg_ref.at[pl.ds(pl.multiple_of(block * blk, blk), blk)], zsem)

        def seg_issue(e, c):
            @pl.when(pe_ref[e] > ps_ref[e])
            def _():
                zero_copy(pe_ref[e] // MOE_BLOCK - 1).start()
            return c

        def seg_drain(e, c):
            @pl.when(pe_ref[e] > ps_ref[e])
            def _():
                zero_copy(pe_ref[e] // MOE_BLOCK - 1).wait()
            return c

        def tail_issue(b, c):
            zero_copy(b).start()
            return c

        def tail_drain(b, c):
            zero_copy(b).wait()
            return c

        first_tail = pe_ref[N_EXPERTS - 1] // MOE_BLOCK
        lax.fori_loop(0, N_EXPERTS, seg_issue, 0)
        lax.fori_loop(first_tail, n_blocks, tail_issue, 0)
        lax.fori_loop(0, N_EXPERTS, seg_drain, 0)
        lax.fori_loop(first_tail, n_blocks, tail_drain, 0)

    _row_groups(tm, lambda t: [_tile_copy(ht_ref, t, xg_ref, d1_ref[t], sem),
                               _tile_copy(ht_ref, t, xg_ref, d2_ref[t], sem)], True)
    _row_groups(tm, lambda t: [_tile_copy(ht_ref, 0, xg_ref, 0, sem)] * 2, False)


def _dispatch(ht, d1, d2, pad_start, pad_end, n_pad):
    t = d1.shape[0]
    tm = TM_DISPATCH
    smem = lambda: pl.BlockSpec((tm,), lambda i, ps, pe: (i,), memory_space=pltpu.SMEM)
    return pl.pallas_call(
        _dispatch_kernel,
        grid_spec=pltpu.PrefetchScalarGridSpec(
            num_scalar_prefetch=2,
            grid=(t // tm,),
            in_specs=[smem(), smem(), pl.BlockSpec((tm * SUBLANES, LANES), lambda i, ps, pe: (i, 0))],
            out_specs=pl.BlockSpec(memory_space=pl.ANY),
            scratch_shapes=[pltpu.VMEM((MOE_BLOCK * SUBLANES, LANES), F32), pltpu.SemaphoreType.DMA(()),
                            pltpu.SemaphoreType.DMA(())],
        ),
        out_shape=jax.ShapeDtypeStruct((n_pad * SUBLANES, LANES), F32),
        compiler_params=_cparams("arbitrary"),
        name="dispatch",
    )(pad_start, pad_end, d1, d2, ht)


def _expert_kernel(be_ref, nu_ref, x_ref, wg_ref, wu_ref, wd_ref, y_ref, wg_s, wu_s, wd_s):
    j = pl.program_id(0)

    @pl.when(j < nu_ref[0])
    def _():
        prev = be_ref[jnp.maximum(j - 1, 0)]

        @pl.when((j == 0) | (be_ref[j] != prev))
        def _():
            wg_s[...] = wg_ref[...].astype(BF)
            wu_s[...] = wu_ref[...].astype(BF)
            wd_s[...] = wd_ref[...].astype(BF)

        xb = _load_tile_rows(x_ref).astype(BF)
        a = _dot(xb, wg_s[...])
        u = _dot(xb, wu_s[...])
        mid = (a / (1.0 + jnp.exp(-a))) * u
        _store_tile_rows(y_ref, _dot(mid.astype(BF), wd_s[...]))

    @pl.when(j >= nu_ref[0])
    def _():
        y_ref[...] = jnp.zeros_like(y_ref)


def _experts(xg, blk_expert, n_used, w_eg, w_eu, w_ed):
    blk = MOE_BLOCK * SUBLANES
    nb = xg.shape[0] // blk
    d, de = w_eg.shape[1], w_eg.shape[2]
    rows = lambda j, be, nu: (jnp.minimum(j, nu[0] - 1), 0)
    wsel = lambda j, be, nu: (be[j], 0, 0)
    return pl.pallas_call(
        _expert_kernel,
        grid_spec=pltpu.PrefetchScalarGridSpec(
            num_scalar_prefetch=2,
            grid=(nb,),
            in_specs=[pl.BlockSpec((blk, LANES), rows),
                      pl.BlockSpec((None, d, de), wsel),
                      pl.BlockSpec((None, d, de), wsel),
                      pl.BlockSpec((None, de, d), wsel)],
            out_specs=pl.BlockSpec((blk, LANES), lambda j, be, nu: (j, 0)),
            scratch_shapes=[pltpu.VMEM((d, de), BF), pltpu.VMEM((d, de), BF), pltpu.VMEM((de, d), BF)],
        ),
        out_shape=jax.ShapeDtypeStruct(xg.shape, F32),
        compiler_params=_cparams("arbitrary"),
        name="experts",
    )(blk_expert, n_used, xg, w_eg, w_eu, w_ed)


def _combine_kernel(d1_ref, d2_ref, d1_next_ref, d2_next_ref, wt_ref, ht_ref, yb_ref, g_ref, b_ref, o_ref,
                    buf, sems):
    i = pl.program_id(0)
    n = pl.num_programs(0)
    tm = d1_ref.shape[0]

    def gather_tile(da_ref, db_ref, slot, start):
        def copies(t):
            if not start:
                return [_tile_copy(yb_ref, 0, buf.at[slot, a], 0, sems.at[slot]) for a in range(2)]
            return [_tile_copy(yb_ref, da_ref[t], buf.at[slot, 0], t, sems.at[slot]),
                    _tile_copy(yb_ref, db_ref[t], buf.at[slot, 1], t, sems.at[slot])]
        _row_groups(tm, copies, start)

    slot = i % 2

    @pl.when(i == 0)
    def _():
        gather_tile(d1_ref, d2_ref, 0, True)

    @pl.when(i + 1 < n)
    def _():
        gather_tile(d1_next_ref, d2_next_ref, 1 - slot, True)

    gather_tile(d1_ref, d2_ref, slot, False)
    wt = wt_ref[...]
    ffn = wt[:, 0:1] * _load_tile_rows(buf.at[slot, 0]) + wt[:, 1:2] * _load_tile_rows(buf.at[slot, 1])
    o_ref[...] = _layer_norm(ALPHA * _load_tile_rows(ht_ref) + ffn, g_ref[...], b_ref[...])


def _combine(ht, yb, d1, d2, wts, ln_g, ln_b):
    t = d1.shape[0]
    d = SUBLANES * LANES
    tm = TM_COMBINE
    n = t // tm
    tok = lambda i: (i, 0)
    const = lambda i: (0, 0)
    cur = lambda: pl.BlockSpec((tm,), lambda i: (i,), memory_space=pltpu.SMEM)
    nxt = lambda: pl.BlockSpec((tm,), lambda i: (jnp.minimum(i + 1, n - 1),), memory_space=pltpu.SMEM)
    return pl.pallas_call(
        _combine_kernel,
        grid=(n,),
        in_specs=[cur(), cur(), nxt(), nxt(),
                  pl.BlockSpec((tm, 8), tok),
                  pl.BlockSpec((tm * SUBLANES, LANES), tok),
                  pl.BlockSpec(memory_space=pl.ANY),
                  pl.BlockSpec((1, d), const), pl.BlockSpec((1, d), const)],
        out_specs=pl.BlockSpec((tm, d), tok),
        out_shape=jax.ShapeDtypeStruct((t, d), F32),
        scratch_shapes=[pltpu.VMEM((2, 2, tm * SUBLANES, LANES), F32), pltpu.SemaphoreType.DMA((2,))],
        compiler_params=_cparams("arbitrary"),
        name="combine",
    )(d1, d2, d1, d2, wts, ht, yb, ln_g, ln_b)


def _token_mixing(x2d, mem2d, batch, seq, m_len, w_in, w_gate, b_gate, w_mem_kv, rpb, w_fo, w_na, w_mo, w_out,
                  ln_g, ln_b):
    k_mem, v_mem = _memkv(mem2d, w_mem_kv.astype(BF), m_len)
    u_f, k, q3, v3, mo = _proj(x2d, w_in, k_mem, v_mem, seq, m_len)
    fo = _fourier(u_f, batch, seq)
    na = _natten(q3, k, v3, rpb, batch, seq)
    return _merge(x2d, fo, na, mo, w_gate.astype(BF), b_gate.astype(F32)[None, :], w_fo.astype(BF),
                  w_na.astype(BF), w_mo.astype(BF), w_out.astype(BF), ln_g.astype(F32)[None, :],
                  ln_b.astype(F32)[None, :])


def _moe(ht, w_rg, b_rg, w_re, b_re, w_eg, w_eu, w_ed, ln_g, ln_b):
    t = ht.shape[0] // SUBLANES
    idx, wts, cnt = _router(ht, w_rg, b_rg, w_re, b_re)
    counts = cnt[0, N_GROUPS:N_GROUPS + N_EXPERTS].astype(I32)
    padded = (counts + MOE_BLOCK - 1) // MOE_BLOCK * MOE_BLOCK
    pad_end = jnp.cumsum(padded).astype(I32)
    pad_start = pad_end - padded
    n_pad = 2 * t + N_EXPERTS * MOE_BLOCK
    nb = n_pad // MOE_BLOCK
    blk_start = jnp.arange(nb, dtype=I32) * MOE_BLOCK
    blk_expert = jnp.minimum(jnp.sum((pad_end[None, :] <= blk_start[:, None]).astype(I32), axis=1), N_EXPERTS - 1)
    n_used = pad_end[-1:] // MOE_BLOCK
    d1, d2 = _dests(idx, pad_start)
    xg = _dispatch(ht, d1, d2, pad_start, pad_end, n_pad)
    yb = _experts(xg, blk_expert, n_used, w_eg, w_eu, w_ed)
    return _combine(ht, yb, d1, d2, wts, ln_g.astype(F32)[None, :], ln_b.astype(F32)[None, :])


def kernel(x, mem, w_in, w_gate, b_gate, w_mem_kv, rpb, w_fourier_o, w_na_o, w_mem_o, w_out, ln1_g, ln1_b,
           w_router_group, b_router_group, w_router_expert, b_router_expert, w_exp_gate, w_exp_up,
           w_exp_down, ln2_g, ln2_b):
    batch, seq, d = x.shape
    m_len = mem.shape[1]
    x2d = x.reshape(batch * seq, d)
    mem2d = mem.reshape(batch * m_len, d)
    for l in range(w_in.shape[0]):
        ht = _token_mixing(x2d, mem2d, batch, seq, m_len, w_in[l], w_gate[l], b_gate[l], w_mem_kv[l], rpb[l],
                           w_fourier_o[l], w_na_o[l], w_mem_o[l], w_out[l], ln1_g[l], ln1_b[l])
        x2d = _moe(ht, w_router_group[l], b_router_group[l], w_router_expert[l], b_router_expert[l],
                   w_exp_gate[l], w_exp_up[l], w_exp_down[l], ln2_g[l], ln2_b[l])
    return x2d.reshape(batch, seq, d)
```

```python
import functools

import numpy as np
import jax
import jax.numpy as jnp
from jax import lax
from jax.experimental import pallas as pl
from jax.experimental.pallas import tpu as pltpu

BF = jnp.bfloat16
F32 = jnp.float32
I32 = jnp.int32

GRID_W = 64
MEM_HEADS = 4
MEM_HEAD_DIM = 128
MEM_DIM = MEM_HEADS * MEM_HEAD_DIM
FOURIER_GROUPS = 4
FOURIER_GROUP_DIM = 128
FOURIER_DIM = FOURIER_GROUPS * FOURIER_GROUP_DIM
NA_HEADS = 8
NA_HEAD_DIM = 64
NA_DIM = NA_HEADS * NA_HEAD_DIM
NA_ROW_WIN = 8
NA_COL_WIN = 16
N_GROUPS = 8
EXPERTS_PER_GROUP = 8
N_EXPERTS = N_GROUPS * EXPERTS_PER_GROUP
DEPTH = 1
ALPHA = (2.0 * DEPTH) ** 0.25
LN_EPS = 1e-5
NA_SCALE = NA_HEAD_DIM ** -0.5
MEM_SCALE = MEM_HEAD_DIM ** -0.5
MASK_NEG = -1e30

LANES = 128
SUBLANES = 8
FFT_N2 = 128
TM_PROJ = 512
TM_MERGE = 512
TM_ROUTE = 512
TM_DISPATCH = 1024
TM_COMBINE = 256
NA_GROUP_ROWS = 4
NA_GROUP_WIN = NA_GROUP_ROWS + NA_ROW_WIN
NA_GROUPS_PER_STEP = 4
NA_GROUP_UNROLL = 4
F1_COLS = 8192
F2_SLABS = 4
MOE_BLOCK = 256
ROUTE_ROWS = 128
ROUTE_STEPS_PER_PLANE = SUBLANES * LANES // TM_ROUTE
ROW_GROUP = 8
VMEM_LIMIT = 56 * 1024 * 1024


def _cparams(*sem):
    return pltpu.CompilerParams(dimension_semantics=sem, vmem_limit_bytes=VMEM_LIMIT)


def _dot(a, b):
    return jnp.dot(a, b, preferred_element_type=F32)


def _dot_nt(a, b):
    return lax.dot_general(a, b, (((1,), (1,)), ((), ())), preferred_element_type=F32)


def _layer_norm(y, g, b):
    mu = jnp.mean(y, axis=-1, keepdims=True)
    yc = y - mu
    var = jnp.mean(yc * yc, axis=-1, keepdims=True)
    return yc * lax.rsqrt(var + LN_EPS) * g + b


def _store_tile_rows(ref, val):
    n = val.shape[0]
    for s in range(SUBLANES):
        ref[pl.ds(s, n, stride=SUBLANES), :] = val[:, s * LANES:(s + 1) * LANES]


def _load_tile_rows(ref):
    n = ref.shape[0] // SUBLANES
    return jnp.concatenate([ref[pl.ds(s, n, stride=SUBLANES), :] for s in range(SUBLANES)], axis=1)


def _memkv_kernel(mem_ref, w_ref, k_ref, v_ref):
    kv = _dot(mem_ref[...].astype(BF), w_ref[...])
    k_ref[...] = kv[:, :MEM_DIM].astype(BF)
    v_ref[...] = kv[:, MEM_DIM:].astype(BF)


def _memkv(mem2d, w_kv, m_len):
    rows, d = mem2d.shape
    return pl.pallas_call(
        _memkv_kernel,
        grid=(rows // m_len,),
        in_specs=[pl.BlockSpec((m_len, d), lambda i: (i, 0)),
                  pl.BlockSpec((d, 2 * MEM_DIM), lambda i: (0, 0))],
        out_specs=[pl.BlockSpec((m_len, MEM_DIM), lambda i: (i, 0))] * 2,
        out_shape=[jax.ShapeDtypeStruct((rows, MEM_DIM), BF)] * 2,
        compiler_params=_cparams("parallel"),
        name="memkv",
    )(mem2d, w_kv)


def _proj_kernel(x_ref, w_ref, wt_ref, km_ref, vm_ref, uf_ref, k_ref, q3_ref, v3_ref, mo_ref):
    xb = x_ref[...].astype(BF)

    def seg(j):
        return _dot(xb, w_ref[:, j * 512:(j + 1) * 512])

    uf_ref[...] = seg(0).astype(BF)
    k_ref[...] = seg(1).astype(BF)
    q_t = _dot_nt(wt_ref[:NA_DIM, :], xb) * NA_SCALE
    v_t = _dot_nt(wt_ref[NA_DIM:, :], xb)
    for s in range(q3_ref.shape[0]):
        q3_ref[s] = q_t[:, s * LANES:(s + 1) * LANES].astype(BF)
        v3_ref[s] = v_t[:, s * LANES:(s + 1) * LANES].astype(BF)
    qm = seg(2).astype(BF)
    for h in range(MEM_HEADS):
        sl = slice(h * MEM_HEAD_DIM, (h + 1) * MEM_HEAD_DIM)
        s = _dot_nt(qm[:, sl], km_ref[:, sl]) * MEM_SCALE
        m = jnp.max(s, axis=-1, keepdims=True)
        p = jnp.exp(s - m)
        l = jnp.sum(p, axis=-1, keepdims=True)
        o = _dot(p.astype(BF), vm_ref[:, sl])
        mo_ref[:, sl] = (o / l).astype(BF)


def _proj(x2d, w_in, k_mem, v_mem, seq, m_len):
    t, d = x2d.shape
    tm = TM_PROJ
    tiles_per_batch = seq // tm
    w_tok = jnp.concatenate([w_in[:, 0:512], w_in[:, 1024:1536], w_in[:, 2048:2560]], axis=1).astype(BF)
    w_chan = jnp.concatenate([w_in[:, 512:1024], w_in[:, 1536:2048]], axis=1).T.astype(BF)
    tok = lambda i: (i, 0)
    slab = lambda i: (i, 0, 0)
    memb = lambda i: (i // tiles_per_batch, 0)
    const = lambda i: (0, 0)
    tok_out = pl.BlockSpec((tm, 512), tok)
    slab_out = pl.BlockSpec((tm // LANES, NA_DIM, LANES), slab)
    slab_shape = jax.ShapeDtypeStruct((t // LANES, NA_DIM, LANES), BF)
    return pl.pallas_call(
        _proj_kernel,
        grid=(t // tm,),
        in_specs=[pl.BlockSpec((tm, d), tok),
                  pl.BlockSpec(w_tok.shape, const),
                  pl.BlockSpec(w_chan.shape, const),
                  pl.BlockSpec((m_len, MEM_DIM), memb),
                  pl.BlockSpec((m_len, MEM_DIM), memb)],
        out_specs=[tok_out, tok_out, slab_out, slab_out, tok_out],
        out_shape=[jax.ShapeDtypeStruct((t, 512), BF)] * 2 + [slab_shape] * 2
                  + [jax.ShapeDtypeStruct((t, 512), BF)],
        compiler_params=_cparams("parallel"),
        name="proj",
    )(x2d, w_tok, w_chan, k_mem, v_mem)


def _fourier_tables(n1, n2):
    n = n1 * n2
    k1 = np.arange(n1)
    ang1 = 2.0 * np.pi * ((k1[:, None] * k1[None, :]) % n1) / n1
    norm = 1.0 / np.sqrt(float(n) * FOURIER_GROUP_DIM)
    m1 = np.concatenate([np.cos(ang1), -np.sin(ang1)], axis=0) * norm
    kk = k1[:, None, None] + n1 * np.arange(n2)[None, :, None]
    nn = np.arange(n2)[None, None, :]
    ang2 = 2.0 * np.pi * ((kk * nn) % n) / n
    c2, s2 = np.cos(ang2), np.sin(ang2)
    m2 = np.concatenate([np.concatenate([c2, s2], axis=2),
                         np.concatenate([-s2, c2], axis=2)], axis=1)
    c = np.arange(FOURIER_GROUP_DIM)
    angc = 2.0 * np.pi * ((c[:, None] * c[None, :]) % FOURIER_GROUP_DIM) / FOURIER_GROUP_DIM
    mc = np.concatenate([np.cos(angc), np.sin(angc)], axis=0)
    as_bf = lambda a: jnp.asarray(a.astype(np.float32)).astype(BF)
    return as_bf(m1), as_bf(m2), as_bf(mc)


def _f1_kernel(u_ref, m_ref, a_ref):
    n1 = u_ref.shape[0]
    r = _dot(m_ref[...], u_ref[...])
    a_ref[0] = r[:n1].astype(BF)
    a_ref[1] = r[n1:].astype(BF)


def _f2_kernel(a_ref, m2_ref, mc_ref, o_ref):
    slabs = a_ref.shape[1]
    n2 = a_ref.shape[2]
    for t in range(slabs):
        a = jnp.concatenate([a_ref[0, t], a_ref[1, t]], axis=0)
        y = _dot(m2_ref[t], a)
        yr = y[:n2].astype(BF)
        yi = y[n2:].astype(BF)
        for g in range(FOURIER_GROUPS):
            sl = slice(g * LANES, (g + 1) * LANES)
            yy = jnp.concatenate([yr[:, sl], yi[:, sl]], axis=1)
            z = _dot(yy, mc_ref[...])
            o_ref[:, t * FOURIER_DIM + g * LANES: t * FOURIER_DIM + (g + 1) * LANES] = z.astype(BF)


def _fourier(u_f, batch, seq):
    n2 = FFT_N2
    n1 = seq // n2
    m1, m2, mc = _fourier_tables(n1, n2)
    cols = n2 * FOURIER_DIM
    tn = min(F1_COLS, cols)
    a = pl.pallas_call(
        _f1_kernel,
        grid=(batch, cols // tn),
        in_specs=[pl.BlockSpec((None, n1, tn), lambda b, j: (b, 0, j)),
                  pl.BlockSpec((2 * n1, n1), lambda b, j: (0, 0))],
        out_specs=pl.BlockSpec((None, 2, n1, tn), lambda b, j: (b, 0, 0, j)),
        out_shape=jax.ShapeDtypeStruct((batch, 2, n1, cols), BF),
        compiler_params=_cparams("parallel", "parallel"),
        name="fourier1",
    )(u_f.reshape(batch, n1, cols), m1)
    kb = min(F2_SLABS, n1)
    out = pl.pallas_call(
        _f2_kernel,
        grid=(batch, n1 // kb),
        in_specs=[pl.BlockSpec((None, 2, kb, n2, FOURIER_DIM), lambda b, j: (b, 0, j, 0, 0)),
                  pl.BlockSpec((kb, 2 * n2, 2 * n2), lambda b, j: (j, 0, 0)),
                  pl.BlockSpec((2 * FOURIER_GROUP_DIM, FOURIER_GROUP_DIM), lambda b, j: (0, 0))],
        out_specs=pl.BlockSpec((None, n2, kb * FOURIER_DIM), lambda b, j: (b, 0, j)),
        out_shape=jax.ShapeDtypeStruct((batch, n2, n1 * FOURIER_DIM), BF),
        compiler_params=_cparams("parallel", "parallel"),
        name="fourier2",
    )(a.reshape(batch, 2, n1, n2, FOURIER_DIM), m2, mc)
    return out.reshape(batch * seq, FOURIER_DIM)


def _na_group_start(r0, n_rows):
    rs0 = min(max(r0 - NA_ROW_WIN // 2, 0), n_rows - NA_ROW_WIN)
    return min(rs0, n_rows - NA_GROUP_WIN)


def _na_bias_table(rpb, n_rows):
    g = NA_GROUP_ROWS
    assert n_rows >= 2 * NA_GROUP_WIN and n_rows % g == 0 and (NA_ROW_WIN // 2) % g == 0
    cols = np.arange(GRID_W)
    cs = np.clip(cols - NA_COL_WIN // 2, 0, GRID_W - NA_COL_WIN)
    kc = np.arange(GRID_W)
    in_win = (kc[None, :] >= cs[:, None]) & (kc[None, :] < cs[:, None] + NA_COL_WIN)
    dc = kc[None, :] - cols[:, None] + (NA_COL_WIN - 1)
    n_dc = 2 * NA_COL_WIN - 1
    pick = ((dc.reshape(-1)[None, :] == np.arange(n_dc)[:, None]) & in_win.reshape(-1)[None, :])
    picked = jnp.dot(rpb.astype(F32).reshape(-1, n_dc), jnp.asarray(pick.astype(np.float32)),
                     precision=lax.Precision.HIGHEST)
    picked = picked.reshape(NA_HEADS, 2 * NA_ROW_WIN - 1, GRID_W, GRID_W)
    full_t = jnp.where(in_win[None, None], picked, MASK_NEG).transpose(0, 1, 3, 2)
    masked = jnp.full((NA_HEADS, GRID_W, GRID_W), MASK_NEG, F32)
    half = NA_ROW_WIN // 2
    group_rows = list(range(0, half, g)) + [half] + list(range(n_rows - half, n_rows, g))
    tabs = []
    for r0 in group_rows:
        start = _na_group_start(r0, n_rows)
        key_rows = []
        for kr in range(NA_GROUP_WIN):
            blocks = []
            for rr in range(g):
                r = r0 + rr
                rs = min(max(r - half, 0), n_rows - NA_ROW_WIN)
                in_rows = rs <= start + kr < rs + NA_ROW_WIN
                blocks.append(full_t[:, start + kr - r + NA_ROW_WIN - 1] if in_rows else masked)
            key_rows.append(jnp.concatenate(blocks, axis=2))
        tabs.append(jnp.concatenate(key_rows, axis=1))
    return jnp.stack(tabs, axis=0)


def _na_kernel(q3_ref, k_ref, v3_ref, tab_ref, o_ref, *, n_rows):
    i = pl.program_id(2)
    g = NA_GROUP_ROWS
    slabs_per_group = g * GRID_W // LANES
    groups = q3_ref.shape[0] // slabs_per_group
    win = NA_GROUP_WIN * GRID_W
    tokens = g * GRID_W
    half = NA_ROW_WIN // 2
    n_top = half // g
    chan = lax.broadcasted_iota(I32, (2 * NA_HEAD_DIM, tokens), 0)
    first_head = chan < NA_HEAD_DIM

    def group(jj, carry):
        r0 = (i * groups + jj) * g
        start = jnp.minimum(jnp.clip(r0 - half, 0, n_rows - NA_ROW_WIN), n_rows - NA_GROUP_WIN)
        cfg = jnp.where(r0 < half, r0 // g,
                        jnp.where(r0 >= n_rows - half, n_top + 1 + (r0 - (n_rows - half)) // g, n_top))
        kw = k_ref[pl.ds(pl.multiple_of(start * GRID_W, LANES), win), :]
        qt = jnp.concatenate([q3_ref[jj * slabs_per_group + t] for t in range(slabs_per_group)], axis=1)
        zero = jnp.zeros_like(qt)
        q_bd = jnp.concatenate([jnp.where(first_head, qt, zero), jnp.where(first_head, zero, qt)], axis=1)
        s = _dot(kw, q_bd) + jnp.concatenate([tab_ref[cfg, 0], tab_ref[cfg, 1]], axis=1)
        p = jnp.exp(s - jnp.max(s, axis=0, keepdims=True))
        l = jnp.sum(p, axis=0, keepdims=True)
        slab = start // 2
        vw = jnp.concatenate([v3_ref[slab + t] for t in range(NA_GROUP_WIN // 2)], axis=1)
        o_t = _dot(vw, p.astype(BF)) / l
        o = jnp.where(first_head, o_t[:, :tokens], o_t[:, tokens:])
        o_ref[pl.ds(pl.multiple_of(jj * tokens, tokens), tokens), :] = jnp.transpose(o).astype(BF)
        return carry

    lax.fori_loop(0, groups, group, 0, unroll=NA_GROUP_UNROLL)


def _natten(q3, k, v3, rpb, batch, seq):
    assert 2 * GRID_W == LANES and 2 * NA_HEAD_DIM == LANES and NA_GROUP_ROWS % 2 == 0
    n_rows = seq // GRID_W
    slabs = n_rows // 2
    step_slabs = min(NA_GROUPS_PER_STEP * NA_GROUP_ROWS // 2, slabs)
    steps = slabs // step_slabs
    hp = NA_HEADS // 2
    tab = _na_bias_table(rpb, n_rows)
    return pl.pallas_call(
        functools.partial(_na_kernel, n_rows=n_rows),
        grid=(batch, hp, steps),
        in_specs=[pl.BlockSpec((step_slabs, LANES, LANES), lambda b, p, i: (b * steps + i, p, 0)),
                  pl.BlockSpec((seq, LANES), lambda b, p, i: (b, p)),
                  pl.BlockSpec((slabs, LANES, LANES), lambda b, p, i: (b, p, 0)),
                  pl.BlockSpec((tab.shape[0], 2) + tab.shape[2:], lambda b, p, i: (0, p, 0, 0))],
        out_specs=pl.BlockSpec((step_slabs * LANES, LANES), lambda b, p, i: (b * steps + i, p)),
        out_shape=jax.ShapeDtypeStruct((batch * seq, NA_DIM), BF),
        compiler_params=_cparams("parallel", "parallel", "parallel"),
        name="natten",
    )(q3, k, v3, tab)


def _merge_kernel(x_ref, fo_ref, na_ref, mo_ref, wg_ref, bg_ref, wf_ref, wn_ref, wm_ref, wo_ref,
                  g_ref, b_ref, ht_ref):
    d = x_ref.shape[1]
    x = x_ref[...]
    xb = x.astype(BF)
    merged = None
    for j, (br_ref, w_ref) in enumerate(((fo_ref, wf_ref), (na_ref, wn_ref), (mo_ref, wm_ref))):
        z = _dot(xb, wg_ref[:, j * d:(j + 1) * d]) + bg_ref[:, j * d:(j + 1) * d]
        gate = 1.0 / (1.0 + jnp.exp(-z))
        term = gate * _dot(br_ref[...], w_ref[...])
        merged = term if merged is None else merged + term
    mix = _dot(merged.astype(BF), wo_ref[...])
    _store_tile_rows(ht_ref, _layer_norm(ALPHA * x + mix, g_ref[...], b_ref[...]))


def _merge(x2d, fo, na, mo, w_gate, b_gate, w_fo, w_na, w_mo, w_out, ln_g, ln_b):
    t, d = x2d.shape
    assert d == SUBLANES * LANES
    tm = TM_MERGE
    tok = lambda i: (i, 0)
    full = lambda a: pl.BlockSpec(a.shape, lambda i: (0, 0))
    return pl.pallas_call(
        _merge_kernel,
        grid=(t // tm,),
        in_specs=[pl.BlockSpec((tm, d), tok)] + [pl.BlockSpec((tm, 512), tok)] * 3
                 + [full(a) for a in (w_gate, b_gate, w_fo, w_na, w_mo, w_out, ln_g, ln_b)],
        out_specs=pl.BlockSpec((tm * SUBLANES, LANES), tok),
        out_shape=jax.ShapeDtypeStruct((t * SUBLANES, LANES), F32),
        compiler_params=_cparams("parallel"),
        name="merge",
    )(x2d, fo, na, mo, w_gate, b_gate, w_fo, w_na, w_mo, w_out, ln_g, ln_b)


def _router_kernel(ht_ref, whi_ref, wlo_ref, b_ref, oi_ref, of_ref, cnt_ref, carry_ref):
    tm = ht_ref.shape[0] // SUBLANES

    @pl.when(pl.program_id(0) == 0)
    def _():
        carry_ref[...] = jnp.zeros_like(carry_ref)

    h = _load_tile_rows(ht_ref)
    hh = h.astype(BF)
    hl = (h - hh.astype(F32)).astype(BF)
    logits = (_dot_nt(whi_ref[...], hh) + _dot_nt(wlo_ref[...], hh) + _dot_nt(whi_ref[...], hl)
              + b_ref[...])
    gl = logits[:N_GROUPS, :]
    g_row = lax.broadcasted_iota(I32, (N_GROUPS, tm), 0)
    gmax = jnp.max(gl, axis=0, keepdims=True)
    g_idx = jnp.min(jnp.where(gl == gmax, g_row, N_GROUPS), axis=0, keepdims=True)
    p_group = 1.0 / jnp.sum(jnp.exp(gl - gmax), axis=0, keepdims=True)
    e_row = lax.broadcasted_iota(I32, (N_EXPERTS, tm), 0)
    el = jnp.where((e_row >> 3) == g_idx, logits[N_GROUPS:N_GROUPS + N_EXPERTS, :], MASK_NEG)
    v1 = jnp.max(el, axis=0, keepdims=True)
    i1 = jnp.min(jnp.where(el == v1, e_row, N_EXPERTS), axis=0, keepdims=True)
    el2 = jnp.where(e_row == i1, MASK_NEG, el)
    v2 = jnp.max(el2, axis=0, keepdims=True)
    i2 = jnp.min(jnp.where(el2 == v2, e_row, N_EXPERTS), axis=0, keepdims=True)
    tt = jnp.exp(v2 - v1)
    w1 = p_group / (1.0 + tt)
    w2 = p_group * tt / (1.0 + tt)
    sel1 = e_row == i1
    sel2 = e_row == i2
    onehot = jnp.where(sel1 | sel2, 1.0, 0.0)
    t_in = lax.broadcasted_iota(I32, (tm, tm), 0)
    t_out = lax.broadcasted_iota(I32, (tm, tm), 1)
    tri = jnp.where(t_in < t_out, 1.0, 0.0).astype(BF)
    prefix = _dot(onehot.astype(BF), tri) + carry_ref[...]
    r1 = jnp.sum(jnp.where(sel1, prefix, 0.0), axis=0, keepdims=True)
    r2 = jnp.sum(jnp.where(sel2, prefix, 0.0), axis=0, keepdims=True)
    carry_ref[...] += jnp.sum(onehot, axis=1, keepdims=True)
    cnt_ref[...] = carry_ref[...]
    sub = pl.program_id(0) % ROUTE_STEPS_PER_PLANE
    for f, vals in enumerate((i1, i2, r1.astype(I32), r2.astype(I32))):
        for c in range(tm // LANES):
            oi_ref[f, pl.ds(sub * (tm // LANES) + c, 1), :] = vals[:, c * LANES:(c + 1) * LANES]
    w_row = lax.broadcasted_iota(I32, (LANES, tm), 0)
    w_rows = jnp.where(w_row == 0, w1, jnp.where(w_row == 1, w2, 0.0))
    of_ref[...] = jnp.transpose(w_rows)[:, :8]


def _router(ht, w_rg, b_rg, w_re, b_re):
    t = ht.shape[0] // SUBLANES
    d = w_rg.shape[0]
    tm = TM_ROUTE
    pad = ROUTE_ROWS - N_GROUPS - N_EXPERTS
    w = jnp.concatenate([w_rg.astype(F32), w_re.astype(F32), jnp.zeros((d, pad), F32)], axis=1).T
    b = jnp.concatenate([b_rg.astype(F32), b_re.astype(F32), jnp.zeros((pad,), F32)])[:, None]
    w_hi = w.astype(BF)
    w_lo = (w - w_hi.astype(F32)).astype(BF)
    tok = lambda i: (i, 0)
    const = lambda i: (0, 0)
    return pl.pallas_call(
        _router_kernel,
        grid=(t // tm,),
        in_specs=[pl.BlockSpec((tm * SUBLANES, LANES), tok),
                  pl.BlockSpec((ROUTE_ROWS, d), const),
                  pl.BlockSpec((ROUTE_ROWS, d), const),
                  pl.BlockSpec((ROUTE_ROWS, 1), const)],
        out_specs=[pl.BlockSpec((4, SUBLANES, LANES), lambda i: (0, i // ROUTE_STEPS_PER_PLANE, 0)),
                   pl.BlockSpec((tm, 8), tok),
                   pl.BlockSpec((N_EXPERTS, 1), const)],
        out_shape=[jax.ShapeDtypeStruct((4, t // LANES, LANES), I32), jax.ShapeDtypeStruct((t, 8), F32),
                   jax.ShapeDtypeStruct((N_EXPERTS, 1), F32)],
        scratch_shapes=[pltpu.VMEM((N_EXPERTS, 1), F32)],
        compiler_params=_cparams("arbitrary"),
        name="router",
    )(ht, w_hi, w_lo, b)


def _tile_copy(src_ref, src_row, dst_ref, dst_row, sem):
    src = pl.multiple_of(src_row * SUBLANES, SUBLANES)
    dst = pl.multiple_of(dst_row * SUBLANES, SUBLANES)
    return pltpu.make_async_copy(src_ref.at[pl.ds(src, SUBLANES)], dst_ref.at[pl.ds(dst, SUBLANES)], sem)


def _row_groups(n_rows, copies_of_row, start):
    def group(g, c):
        copies = []
        for u in range(ROW_GROUP):
            copies.extend(copies_of_row(g * ROW_GROUP + u))
        for cp in copies:
            if start:
                cp.start()
            else:
                cp.wait()
        return c

    lax.fori_loop(0, n_rows // ROW_GROUP, group, 0)


def _dests_kernel(ps_ref, idx_ref, d_ref):
    for a in range(2):
        e = idx_ref[a]
        d = idx_ref[2 + a]
        for k in range(N_EXPERTS):
            d = d + jnp.where(e == k, ps_ref[k], 0)
        d_ref[a] = d


def _dests(idx, pad_start):
    planes = idx.shape[1]
    out = pl.pallas_call(
        _dests_kernel,
        grid_spec=pltpu.PrefetchScalarGridSpec(
            num_scalar_prefetch=1,
            grid=(1,),
            in_specs=[pl.BlockSpec(idx.shape, lambda i, ps: (0, 0, 0))],
            out_specs=pl.BlockSpec((2, planes, LANES), lambda i, ps: (0, 0, 0)),
        ),
        out_shape=jax.ShapeDtypeStruct((2, planes, LANES), I32),
        compiler_params=_cparams("arbitrary"),
        name="dests",
    )(pad_start, idx)
    return out[0].reshape(-1), out[1].reshape(-1)


def _dispatch_kernel(ps_ref, pe_ref, d1_ref, d2_ref, ht_ref, xg_ref, zbuf, sem, zsem):
    i = pl.program_id(0)
    tm = d1_ref.shape[0]
    blk = MOE_BLOCK * SUBLANES
    n_blocks = xg_ref.shape[0] // blk

    @pl.when(i == 0)
    def _():
        zbuf[...] = jnp.zeros_like(zbuf)

        def zero_copy(block):
            return pltpu.make_async_copy(zbuf, xg_ref.at[pl.ds(pl.multiple_of(block * blk, blk), blk)], zsem)

        def seg_issue(e, c):
            @pl.when(pe_ref[e] > ps_ref[e])
            def _():
                zero_copy(pe_ref[e] // MOE_BLOCK - 1).start()
            return c

        def seg_drain(e, c):
            @pl.when(pe_ref[e] > ps_ref[e])
            def _():
                zero_copy(pe_ref[e] // MOE_BLOCK - 1).wait()
            return c

        def tail_issue(b, c):
            zero_copy(b).start()
            return c

        def tail_drain(b, c):
            zero_copy(b).wait()
            return c

        first_tail = pe_ref[N_EXPERTS - 1] // MOE_BLOCK
        lax.fori_loop(0, N_EXPERTS, seg_issue, 0)
        lax.fori_loop(first_tail, n_blocks, tail_issue, 0)
        lax.fori_loop(0, N_EXPERTS, seg_drain, 0)
        lax.fori_loop(first_tail, n_blocks, tail_drain, 0)

    _row_groups(tm, lambda t: [_tile_copy(ht_ref, t, xg_ref, d1_ref[t], sem),
                               _tile_copy(ht_ref, t, xg_ref, d2_ref[t], sem)], True)
    _row_groups(tm, lambda t: [_tile_copy(ht_ref, 0, xg_ref, 0, sem)] * 2, False)


def _dispatch(ht, d1, d2, pad_start, pad_end, n_pad):
    t = d1.shape[0]
    tm = TM_DISPATCH
    smem = lambda: pl.BlockSpec((tm,), lambda i, ps, pe: (i,), memory_space=pltpu.SMEM)
    return pl.pallas_call(
        _dispatch_kernel,
        grid_spec=pltpu.PrefetchScalarGridSpec(
            num_scalar_prefetch=2,
            grid=(t // tm,),
            in_specs=[smem(), smem(), pl.BlockSpec((tm * SUBLANES, LANES), lambda i, ps, pe: (i, 0))],
            out_specs=pl.BlockSpec(memory_space=pl.ANY),
            scratch_shapes=[pltpu.VMEM((MOE_BLOCK * SUBLANES, LANES), F32), pltpu.SemaphoreType.DMA(()),
                            pltpu.SemaphoreType.DMA(())],
        ),
        out_shape=jax.ShapeDtypeStruct((n_pad * SUBLANES, LANES), F32),
        compiler_params=_cparams("arbitrary"),
        name="dispatch",
    )(pad_start, pad_end, d1, d2, ht)


def _expert_kernel(be_ref, nx_ref, nu_ref, x_ref, wg_hbm, wu_hbm, wd_hbm, y_ref,
                   wg_f, wu_f, wd_f, wg_s, wu_s, wd_s, sems):
    j = pl.program_id(0)
    stage = ((wg_hbm, wg_f, wg_s), (wu_hbm, wu_f, wu_s), (wd_hbm, wd_f, wd_s))

    def weight_copies(e):
        return [pltpu.make_async_copy(hbm.at[e], buf, sems.at[k]) for k, (hbm, buf, _) in enumerate(stage)]

    @pl.when(j < nu_ref[0])
    def _():
        e = be_ref[j]

        @pl.when(j == 0)
        def _():
            for cp in weight_copies(e):
                cp.start()

        @pl.when((j == 0) | (e != be_ref[jnp.maximum(j - 1, 0)]))
        def _():
            for cp, (_, buf, dst) in zip(weight_copies(e), stage):
                cp.wait()
                dst[...] = buf[...].astype(BF)
            nxt = nx_ref[e]

            @pl.when(nxt >= 0)
            def _():
                for cp in weight_copies(nxt):
                    cp.start()

        xb = _load_tile_rows(x_ref).astype(BF)
        a = _dot(xb, wg_s[...])
        u = _dot(xb, wu_s[...])
        mid = (a / (1.0 + jnp.exp(-a))) * u
        _store_tile_rows(y_ref, _dot(mid.astype(BF), wd_s[...]))

    @pl.when(j >= nu_ref[0])
    def _():
        y_ref[...] = jnp.zeros_like(y_ref)


def _experts(xg, blk_expert, next_expert, n_used, w_eg, w_eu, w_ed):
    blk = MOE_BLOCK * SUBLANES
    nb = xg.shape[0] // blk
    d, de = w_eg.shape[1], w_eg.shape[2]
    rows = lambda j, be, nx, nu: (jnp.minimum(j, nu[0] - 1), 0)
    hbm = pl.BlockSpec(memory_space=pl.ANY)
    return pl.pallas_call(
        _expert_kernel,
        grid_spec=pltpu.PrefetchScalarGridSpec(
            num_scalar_prefetch=3,
            grid=(nb,),
            in_specs=[pl.BlockSpec((blk, LANES), rows), hbm, hbm, hbm],
            out_specs=pl.BlockSpec((blk, LANES), lambda j, be, nx, nu: (j, 0)),
            scratch_shapes=[pltpu.VMEM((d, de), F32), pltpu.VMEM((d, de), F32), pltpu.VMEM((de, d), F32),
                            pltpu.VMEM((d, de), BF), pltpu.VMEM((d, de), BF), pltpu.VMEM((de, d), BF),
                            pltpu.SemaphoreType.DMA((3,))],
        ),
        out_shape=jax.ShapeDtypeStruct(xg.shape, F32),
        compiler_params=_cparams("arbitrary"),
        name="experts",
    )(blk_expert, next_expert, n_used, xg, w_eg, w_eu, w_ed)


def _combine_kernel(d1_ref, d2_ref, d1_next_ref, d2_next_ref, wt_ref, ht_ref, yb_ref, g_ref, b_ref, o_ref,
                    buf, sems):
    i = pl.program_id(0)
    n = pl.num_programs(0)
    tm = d1_ref.shape[0]

    def gather_tile(da_ref, db_ref, slot, start):
        def copies(t):
            if not start:
                return [_tile_copy(yb_ref, 0, buf.at[slot, a], 0, sems.at[slot]) for a in range(2)]
            return [_tile_copy(yb_ref, da_ref[t], buf.at[slot, 0], t, sems.at[slot]),
                    _tile_copy(yb_ref, db_ref[t], buf.at[slot, 1], t, sems.at[slot])]
        _row_groups(tm, copies, start)

    slot = i % 2

    @pl.when(i == 0)
    def _():
        gather_tile(d1_ref, d2_ref, 0, True)

    @pl.when(i + 1 < n)
    def _():
        gather_tile(d1_next_ref, d2_next_ref, 1 - slot, True)

    gather_tile(d1_ref, d2_ref, slot, False)
    wt = wt_ref[...]
    ffn = wt[:, 0:1] * _load_tile_rows(buf.at[slot, 0]) + wt[:, 1:2] * _load_tile_rows(buf.at[slot, 1])
    o_ref[...] = _layer_norm(ALPHA * _load_tile_rows(ht_ref) + ffn, g_ref[...], b_ref[...])


def _combine(ht, yb, d1, d2, wts, ln_g, ln_b):
    t = d1.shape[0]
    d = SUBLANES * LANES
    tm = TM_COMBINE
    n = t // tm
    tok = lambda i: (i, 0)
    const = lambda i: (0, 0)
    cur = lambda: pl.BlockSpec((tm,), lambda i: (i,), memory_space=pltpu.SMEM)
    nxt = lambda: pl.BlockSpec((tm,), lambda i: (jnp.minimum(i + 1, n - 1),), memory_space=pltpu.SMEM)
    return pl.pallas_call(
        _combine_kernel,
        grid=(n,),
        in_specs=[cur(), cur(), nxt(), nxt(),
                  pl.BlockSpec((tm, 8), tok),
                  pl.BlockSpec((tm * SUBLANES, LANES), tok),
                  pl.BlockSpec(memory_space=pl.ANY),
                  pl.BlockSpec((1, d), const), pl.BlockSpec((1, d), const)],
        out_specs=pl.BlockSpec((tm, d), tok),
        out_shape=jax.ShapeDtypeStruct((t, d), F32),
        scratch_shapes=[pltpu.VMEM((2, 2, tm * SUBLANES, LANES), F32), pltpu.SemaphoreType.DMA((2,))],
        compiler_params=_cparams("arbitrary"),
        name="combine",
    )(d1, d2, d1, d2, wts, ht, yb, ln_g, ln_b)


def _token_mixing(x2d, mem2d, batch, seq, m_len, w_in, w_gate, b_gate, w_mem_kv, rpb, w_fo, w_na, w_mo, w_out,
                  ln_g, ln_b):
    k_mem, v_mem = _memkv(mem2d, w_mem_kv.astype(BF), m_len)
    u_f, k, q3, v3, mo = _proj(x2d, w_in, k_mem, v_mem, seq, m_len)
    fo = _fourier(u_f, batch, seq)
    na = _natten(q3, k, v3, rpb, batch, seq)
    return _merge(x2d, fo, na, mo, w_gate.astype(BF), b_gate.astype(F32)[None, :], w_fo.astype(BF),
                  w_na.astype(BF), w_mo.astype(BF), w_out.astype(BF), ln_g.astype(F32)[None, :],
                  ln_b.astype(F32)[None, :])


def _moe(ht, w_rg, b_rg, w_re, b_re, w_eg, w_eu, w_ed, ln_g, ln_b):
    t = ht.shape[0] // SUBLANES
    idx, wts, cnt = _router(ht, w_rg, b_rg, w_re, b_re)
    counts = cnt[:, 0].astype(I32)
    padded = (counts + MOE_BLOCK - 1) // MOE_BLOCK * MOE_BLOCK
    pad_end = jnp.cumsum(padded).astype(I32)
    pad_start = pad_end - padded
    n_pad = 2 * t + N_EXPERTS * MOE_BLOCK
    nb = n_pad // MOE_BLOCK
    blk_start = jnp.arange(nb, dtype=I32) * MOE_BLOCK
    blk_expert = jnp.minimum(jnp.sum((pad_end[None, :] <= blk_start[:, None]).astype(I32), axis=1), N_EXPERTS - 1)
    n_used = pad_end[-1:] // MOE_BLOCK
    ids = jnp.arange(N_EXPERTS, dtype=I32)
    later_used = (padded[None, :] > 0) & (ids[None, :] > ids[:, None])
    next_expert = jnp.min(jnp.where(later_used, ids[None, :], N_EXPERTS), axis=1)
    next_expert = jnp.where(next_expert == N_EXPERTS, -1, next_expert).astype(I32)
    d1, d2 = _dests(idx, pad_start)
    xg = _dispatch(ht, d1, d2, pad_start, pad_end, n_pad)
    yb = _experts(xg, blk_expert, next_expert, n_used, w_eg, w_eu, w_ed)
    return _combine(ht, yb, d1, d2, wts, ln_g.astype(F32)[None, :], ln_b.astype(F32)[None, :])


def kernel(x, mem, w_in, w_gate, b_gate, w_mem_kv, rpb, w_fourier_o, w_na_o, w_mem_o, w_out, ln1_g, ln1_b,
           w_router_group, b_router_group, w_router_expert, b_router_expert, w_exp_gate, w_exp_up,
           w_exp_down, ln2_g, ln2_b):
    batch, seq, d = x.shape
    m_len = mem.shape[1]
    x2d = x.reshape(batch * seq, d)
    mem2d = mem.reshape(batch * m_len, d)
    for l in range(w_in.shape[0]):
        ht = _token_mixing(x2d, mem2d, batch, seq, m_len, w_in[l], w_gate[l], b_gate[l], w_mem_kv[l], rpb[l],
                           w_fourier_o[l], w_na_o[l], w_mem_o[l], w_out[l], ln1_g[l], ln1_b[l])
        x2d = _moe(ht, w_router_group[l], b_router_group[l], w_router_expert[l], b_router_expert[l],
                   w_exp_gate[l], w_exp_up[l], w_exp_down[l], ln2_g[l], ln2_b[l])
    return x2d.reshape(batch, seq, d)
```

```python
import functools

import numpy as np
import jax
import jax.numpy as jnp
from jax import lax
from jax.experimental import pallas as pl
from jax.experimental.pallas import tpu as pltpu

BF = jnp.bfloat16
F32 = jnp.float32
I32 = jnp.int32

GRID_W = 64
MEM_HEADS = 4
MEM_HEAD_DIM = 128
MEM_DIM = MEM_HEADS * MEM_HEAD_DIM
FOURIER_GROUPS = 4
FOURIER_GROUP_DIM = 128
FOURIER_DIM = FOURIER_GROUPS * FOURIER_GROUP_DIM
NA_HEADS = 8
NA_HEAD_DIM = 64
NA_DIM = NA_HEADS * NA_HEAD_DIM
NA_ROW_WIN = 8
NA_COL_WIN = 16
N_GROUPS = 8
EXPERTS_PER_GROUP = 8
N_EXPERTS = N_GROUPS * EXPERTS_PER_GROUP
DEPTH = 1
ALPHA = (2.0 * DEPTH) ** 0.25
LN_EPS = 1e-5
NA_SCALE = NA_HEAD_DIM ** -0.5
MEM_SCALE = MEM_HEAD_DIM ** -0.5
MASK_NEG = -1e30

LANES = 128
SUBLANES = 8
FFT_N2 = 128
TM_PROJ = 512
TM_MERGE = 512
TM_ROUTE = 512
TM_DISPATCH = 1024
TM_COMBINE = 256
NA_GROUP_ROWS = 4
NA_GROUP_WIN = NA_GROUP_ROWS + NA_ROW_WIN
NA_GROUPS_PER_STEP = 8
F1_POS = 16
F2_SLABS = 8
MOE_BLOCK = 256
EXPERT_SUBBLOCKS = 2
ROUTE_ROWS = 128
ROUTE_STEPS_PER_PLANE = SUBLANES * LANES // TM_ROUTE
ROW_GROUP = 8
VMEM_LIMIT = 56 * 1024 * 1024


def _cparams(*sem):
    return pltpu.CompilerParams(dimension_semantics=sem, vmem_limit_bytes=VMEM_LIMIT)


def _dot(a, b):
    return jnp.dot(a, b, preferred_element_type=F32)


def _dot_nt(a, b):
    return lax.dot_general(a, b, (((1,), (1,)), ((), ())), preferred_element_type=F32)


def _layer_norm(y, g, b):
    mu = jnp.mean(y, axis=-1, keepdims=True)
    yc = y - mu
    var = jnp.mean(yc * yc, axis=-1, keepdims=True)
    return yc * lax.rsqrt(var + LN_EPS) * g + b


def _store_tile_rows(ref, val):
    n = val.shape[0]
    for s in range(SUBLANES):
        ref[pl.ds(s, n, stride=SUBLANES), :] = val[:, s * LANES:(s + 1) * LANES]


def _load_tile_rows(ref):
    n = ref.shape[0] // SUBLANES
    return jnp.concatenate([ref[pl.ds(s, n, stride=SUBLANES), :] for s in range(SUBLANES)], axis=1)


def _memkv_kernel(mem_ref, w_ref, k_ref, v_ref):
    kv = _dot(mem_ref[...].astype(BF), w_ref[...])
    k_ref[...] = kv[:, :MEM_DIM].astype(BF)
    v_ref[...] = kv[:, MEM_DIM:].astype(BF)


def _memkv(mem2d, w_kv, m_len):
    rows, d = mem2d.shape
    return pl.pallas_call(
        _memkv_kernel,
        grid=(rows // m_len,),
        in_specs=[pl.BlockSpec((m_len, d), lambda i: (i, 0)),
                  pl.BlockSpec((d, 2 * MEM_DIM), lambda i: (0, 0))],
        out_specs=[pl.BlockSpec((m_len, MEM_DIM), lambda i: (i, 0))] * 2,
        out_shape=[jax.ShapeDtypeStruct((rows, MEM_DIM), BF)] * 2,
        compiler_params=_cparams("parallel"),
        name="memkv",
    )(mem2d, w_kv)


def _proj_kernel(x_ref, w_ref, wt_ref, km_ref, vm_ref, uf_ref, k_ref, q3_ref, v3_ref, mo_ref):
    xb = x_ref[...].astype(BF)

    def seg(j):
        return _dot(xb, w_ref[:, j * 512:(j + 1) * 512])

    uf_ref[...] = seg(0)
    k_ref[...] = seg(1).astype(BF)
    q_t = _dot_nt(wt_ref[:NA_DIM, :], xb) * NA_SCALE
    v_t = _dot_nt(wt_ref[NA_DIM:, :], xb)
    for s in range(q3_ref.shape[0]):
        q3_ref[s] = q_t[:, s * LANES:(s + 1) * LANES].astype(BF)
        v3_ref[s] = v_t[:, s * LANES:(s + 1) * LANES].astype(BF)
    qm = seg(2).astype(BF)
    for h in range(MEM_HEADS):
        sl = slice(h * MEM_HEAD_DIM, (h + 1) * MEM_HEAD_DIM)
        s = _dot_nt(qm[:, sl], km_ref[:, sl]) * MEM_SCALE
        m = jnp.max(s, axis=-1, keepdims=True)
        p = jnp.exp(s - m)
        l = jnp.sum(p, axis=-1, keepdims=True)
        o = _dot(p.astype(BF), vm_ref[:, sl])
        mo_ref[:, sl] = (o / l).astype(BF)


def _proj(x2d, w_in, k_mem, v_mem, seq, m_len):
    t, d = x2d.shape
    tm = TM_PROJ
    tiles_per_batch = seq // tm
    w_tok = jnp.concatenate([w_in[:, 0:512], w_in[:, 1024:1536], w_in[:, 2048:2560]], axis=1).astype(BF)
    w_chan = jnp.concatenate([w_in[:, 512:1024], w_in[:, 1536:2048]], axis=1).T.astype(BF)
    tok = lambda i: (i, 0)
    slab = lambda i: (i, 0, 0)
    memb = lambda i: (i // tiles_per_batch, 0)
    const = lambda i: (0, 0)
    tok_out = pl.BlockSpec((tm, 512), tok)
    slab_out = pl.BlockSpec((tm // LANES, NA_DIM, LANES), slab)
    slab_shape = jax.ShapeDtypeStruct((t // LANES, NA_DIM, LANES), BF)
    return pl.pallas_call(
        _proj_kernel,
        grid=(t // tm,),
        in_specs=[pl.BlockSpec((tm, d), tok),
                  pl.BlockSpec(w_tok.shape, const),
                  pl.BlockSpec(w_chan.shape, const),
                  pl.BlockSpec((m_len, MEM_DIM), memb),
                  pl.BlockSpec((m_len, MEM_DIM), memb)],
        out_specs=[tok_out, tok_out, slab_out, slab_out, tok_out],
        out_shape=[jax.ShapeDtypeStruct((t, 512), F32), jax.ShapeDtypeStruct((t, 512), BF)] + [slab_shape] * 2
                  + [jax.ShapeDtypeStruct((t, 512), BF)],
        compiler_params=_cparams("parallel"),
        name="proj",
    )(x2d, w_tok, w_chan, k_mem, v_mem)


def _fourier_tables(n1, n2):
    n = n1 * n2
    k1 = np.arange(n1)
    ang1 = 2.0 * np.pi * ((k1[:, None] * k1[None, :]) % n1) / n1
    norm = 1.0 / np.sqrt(float(n) * FOURIER_GROUP_DIM)
    m1 = np.concatenate([np.cos(ang1), -np.sin(ang1)], axis=0) * norm
    kk = k1[:, None, None] + n1 * np.arange(n2)[None, :, None]
    nn = np.arange(n2)[None, None, :]
    ang2 = 2.0 * np.pi * ((kk * nn) % n) / n
    c2, s2 = np.cos(ang2), np.sin(ang2)
    m2 = np.concatenate([np.concatenate([c2, s2], axis=2),
                         np.concatenate([-s2, c2], axis=2)], axis=1)
    c = np.arange(FOURIER_GROUP_DIM)
    angc = 2.0 * np.pi * ((c[:, None] * c[None, :]) % FOURIER_GROUP_DIM) / FOURIER_GROUP_DIM
    mc = np.concatenate([np.cos(angc), np.sin(angc)], axis=0)
    as_bf = lambda a: jnp.asarray(a.astype(np.float32)).astype(BF)
    return as_bf(m1), as_bf(m2), as_bf(mc)


def _f1_kernel(u_ref, m_ref, a_ref):
    n1, pos, _ = u_ref.shape
    x = jnp.concatenate([u_ref[:, p, :] for p in range(pos)], axis=1).astype(BF)
    r = _dot(m_ref[...], x)
    a_ref[0] = r[:n1].astype(BF)
    a_ref[1] = r[n1:].astype(BF)


def _f2_kernel(a_ref, m2_ref, mc_ref, o_ref):
    slabs = a_ref.shape[1]
    n2 = a_ref.shape[2]
    for t in range(slabs):
        a = jnp.concatenate([a_ref[0, t], a_ref[1, t]], axis=0)
        y = _dot(m2_ref[t], a)
        yr = y[:n2].astype(BF)
        yi = y[n2:].astype(BF)
        for g in range(FOURIER_GROUPS):
            sl = slice(g * LANES, (g + 1) * LANES)
            yy = jnp.concatenate([yr[:, sl], yi[:, sl]], axis=1)
            o_ref[:, t, sl] = _dot(yy, mc_ref[...])


def _fourier(u_f, batch, seq):
    n2 = FFT_N2
    n1 = seq // n2
    m1, m2, mc = _fourier_tables(n1, n2)
    cols = n2 * FOURIER_DIM
    pos = min(F1_POS, n2)
    a = pl.pallas_call(
        _f1_kernel,
        grid=(batch, n2 // pos),
        in_specs=[pl.BlockSpec((None, n1, pos, FOURIER_DIM), lambda b, j: (b, 0, j, 0)),
                  pl.BlockSpec((2 * n1, n1), lambda b, j: (0, 0))],
        out_specs=pl.BlockSpec((None, 2, n1, pos * FOURIER_DIM), lambda b, j: (b, 0, 0, j)),
        out_shape=jax.ShapeDtypeStruct((batch, 2, n1, cols), BF),
        compiler_params=_cparams("parallel", "parallel"),
        name="fourier1",
    )(u_f.reshape(batch, n1, n2, FOURIER_DIM), m1)
    kb = min(F2_SLABS, n1)
    out = pl.pallas_call(
        _f2_kernel,
        grid=(batch, n1 // kb),
        in_specs=[pl.BlockSpec((None, 2, kb, n2, FOURIER_DIM), lambda b, j: (b, 0, j, 0, 0)),
                  pl.BlockSpec((kb, 2 * n2, 2 * n2), lambda b, j: (j, 0, 0)),
                  pl.BlockSpec((2 * FOURIER_GROUP_DIM, FOURIER_GROUP_DIM), lambda b, j: (0, 0))],
        out_specs=pl.BlockSpec((None, n2, kb, FOURIER_DIM), lambda b, j: (b, 0, j, 0)),
        out_shape=jax.ShapeDtypeStruct((batch, n2, n1, FOURIER_DIM), F32),
        compiler_params=_cparams("parallel", "parallel"),
        name="fourier2",
    )(a.reshape(batch, 2, n1, n2, FOURIER_DIM), m2, mc)
    return out.reshape(batch * seq, FOURIER_DIM)


def _na_group_start(r0, n_rows):
    rs0 = min(max(r0 - NA_ROW_WIN // 2, 0), n_rows - NA_ROW_WIN)
    return min(rs0, n_rows - NA_GROUP_WIN)


def _na_bias_table(rpb, n_rows):
    g = NA_GROUP_ROWS
    assert n_rows >= 2 * NA_GROUP_WIN and n_rows % g == 0 and (NA_ROW_WIN // 2) % g == 0
    cols = np.arange(GRID_W)
    cs = np.clip(cols - NA_COL_WIN // 2, 0, GRID_W - NA_COL_WIN)
    kc = np.arange(GRID_W)
    in_win = (kc[None, :] >= cs[:, None]) & (kc[None, :] < cs[:, None] + NA_COL_WIN)
    dc = kc[None, :] - cols[:, None] + (NA_COL_WIN - 1)
    n_dc = 2 * NA_COL_WIN - 1
    pick = ((dc.reshape(-1)[None, :] == np.arange(n_dc)[:, None]) & in_win.reshape(-1)[None, :])
    picked = jnp.dot(rpb.astype(F32).reshape(-1, n_dc), jnp.asarray(pick.astype(np.float32)),
                     precision=lax.Precision.HIGHEST)
    picked = picked.reshape(NA_HEADS, 2 * NA_ROW_WIN - 1, GRID_W, GRID_W)
    full_t = jnp.where(in_win[None, None], picked, MASK_NEG).transpose(0, 1, 3, 2)
    masked = jnp.full((NA_HEADS, GRID_W, GRID_W), MASK_NEG, F32)
    half = NA_ROW_WIN // 2
    group_rows = list(range(0, half, g)) + [half] + list(range(n_rows - half, n_rows, g))
    tabs = []
    for r0 in group_rows:
        start = _na_group_start(r0, n_rows)
        key_rows = []
        for kr in range(NA_GROUP_WIN):
            blocks = []
            for rr in range(g):
                r = r0 + rr
                rs = min(max(r - half, 0), n_rows - NA_ROW_WIN)
                in_rows = rs <= start + kr < rs + NA_ROW_WIN
                blocks.append(full_t[:, start + kr - r + NA_ROW_WIN - 1] if in_rows else masked)
            key_rows.append(jnp.concatenate(blocks, axis=2))
        tabs.append(jnp.concatenate(key_rows, axis=1))
    return jnp.stack(tabs, axis=0)


def _na_kernel(q3_ref, k_ref, v3_ref, tab_ref, o_ref, *, n_rows):
    i = pl.program_id(2)
    g = NA_GROUP_ROWS
    slabs_per_group = g * GRID_W // LANES
    groups = q3_ref.shape[0] // slabs_per_group
    win = NA_GROUP_WIN * GRID_W
    tokens = g * GRID_W
    half = NA_ROW_WIN // 2
    n_top = half // g
    chan = lax.broadcasted_iota(I32, (2 * NA_HEAD_DIM, tokens), 0)
    first_head = chan < NA_HEAD_DIM

    def scores(jj):
        r0 = (i * groups + jj) * g
        start = jnp.minimum(jnp.clip(r0 - half, 0, n_rows - NA_ROW_WIN), n_rows - NA_GROUP_WIN)
        cfg = jnp.where(r0 < half, r0 // g,
                        jnp.where(r0 >= n_rows - half, n_top + 1 + (r0 - (n_rows - half)) // g, n_top))
        kw = k_ref[pl.ds(pl.multiple_of(start * GRID_W, LANES), win), :]
        qt = jnp.concatenate([q3_ref[jj * slabs_per_group + t] for t in range(slabs_per_group)], axis=1)
        zero = jnp.zeros_like(qt)
        q_bd = jnp.concatenate([jnp.where(first_head, qt, zero), jnp.where(first_head, zero, qt)], axis=1)
        s = _dot(kw, q_bd) + jnp.concatenate([tab_ref[cfg, 0], tab_ref[cfg, 1]], axis=1)
        return s, start

    def outputs(jj, p, l, start):
        slab = start // 2
        vw = jnp.concatenate([v3_ref[slab + t] for t in range(NA_GROUP_WIN // 2)], axis=1)
        o_t = _dot(vw, p) / l
        o = jnp.where(first_head, o_t[:, :tokens], o_t[:, tokens:])
        o_ref[jj * tokens:(jj + 1) * tokens, :] = jnp.transpose(o).astype(BF)

    pending = None
    for jj in range(groups):
        s, start = scores(jj)
        if pending is not None:
            outputs(*pending)
        p = jnp.exp(s - jnp.max(s, axis=0, keepdims=True))
        pending = (jj, p.astype(BF), jnp.sum(p, axis=0, keepdims=True), start)
    outputs(*pending)


def _natten(q3, k, v3, rpb, batch, seq):
    assert 2 * GRID_W == LANES and 2 * NA_HEAD_DIM == LANES and NA_GROUP_ROWS % 2 == 0
    n_rows = seq // GRID_W
    slabs = n_rows // 2
    step_slabs = min(NA_GROUPS_PER_STEP * NA_GROUP_ROWS // 2, slabs)
    steps = slabs // step_slabs
    hp = NA_HEADS // 2
    tab = _na_bias_table(rpb, n_rows)
    return pl.pallas_call(
        functools.partial(_na_kernel, n_rows=n_rows),
        grid=(batch, hp, steps),
        in_specs=[pl.BlockSpec((step_slabs, LANES, LANES), lambda b, p, i: (b * steps + i, p, 0)),
                  pl.BlockSpec((seq, LANES), lambda b, p, i: (b, p)),
                  pl.BlockSpec((slabs, LANES, LANES), lambda b, p, i: (b, p, 0)),
                  pl.BlockSpec((tab.shape[0], 2) + tab.shape[2:], lambda b, p, i: (0, p, 0, 0))],
        out_specs=pl.BlockSpec((step_slabs * LANES, LANES), lambda b, p, i: (b * steps + i, p)),
        out_shape=jax.ShapeDtypeStruct((batch * seq, NA_DIM), BF),
        compiler_params=_cparams("parallel", "parallel", "parallel"),
        name="natten",
    )(q3, k, v3, tab)


def _merge_kernel(x_ref, fo_ref, na_ref, mo_ref, wg_ref, bg_ref, wf_ref, wn_ref, wm_ref, wo_ref,
                  g_ref, b_ref, ht_ref):
    d = x_ref.shape[1]
    x = x_ref[...]
    xb = x.astype(BF)
    merged = None
    for j, (br_ref, w_ref) in enumerate(((fo_ref, wf_ref), (na_ref, wn_ref), (mo_ref, wm_ref))):
        z = _dot(xb, wg_ref[:, j * d:(j + 1) * d]) + bg_ref[:, j * d:(j + 1) * d]
        gate = 1.0 / (1.0 + jnp.exp(-z))
        term = gate * _dot(br_ref[...].astype(BF), w_ref[...])
        merged = term if merged is None else merged + term
    mix = _dot(merged.astype(BF), wo_ref[...])
    _store_tile_rows(ht_ref, _layer_norm(ALPHA * x + mix, g_ref[...], b_ref[...]))


def _merge(x2d, fo, na, mo, w_gate, b_gate, w_fo, w_na, w_mo, w_out, ln_g, ln_b):
    t, d = x2d.shape
    assert d == SUBLANES * LANES
    tm = TM_MERGE
    tok = lambda i: (i, 0)
    full = lambda a: pl.BlockSpec(a.shape, lambda i: (0, 0))
    return pl.pallas_call(
        _merge_kernel,
        grid=(t // tm,),
        in_specs=[pl.BlockSpec((tm, d), tok)] + [pl.BlockSpec((tm, 512), tok)] * 3
                 + [full(a) for a in (w_gate, b_gate, w_fo, w_na, w_mo, w_out, ln_g, ln_b)],
        out_specs=pl.BlockSpec((tm * SUBLANES, LANES), tok),
        out_shape=jax.ShapeDtypeStruct((t * SUBLANES, LANES), F32),
        compiler_params=_cparams("parallel"),
        name="merge",
    )(x2d, fo, na, mo, w_gate, b_gate, w_fo, w_na, w_mo, w_out, ln_g, ln_b)


def _router_kernel(ht_ref, whi_ref, wlo_ref, b_ref, oi_ref, of_ref, cnt_ref, carry_ref):
    tm = ht_ref.shape[0] // SUBLANES

    @pl.when(pl.program_id(0) == 0)
    def _():
        carry_ref[...] = jnp.zeros_like(carry_ref)

    h = _load_tile_rows(ht_ref)
    hh = h.astype(BF)
    hl = (h - hh.astype(F32)).astype(BF)
    logits = (_dot_nt(whi_ref[...], hh) + _dot_nt(wlo_ref[...], hh) + _dot_nt(whi_ref[...], hl)
              + b_ref[...])
    gl = logits[:N_GROUPS, :]
    g_row = lax.broadcasted_iota(I32, (N_GROUPS, tm), 0)
    gmax = jnp.max(gl, axis=0, keepdims=True)
    g_idx = jnp.min(jnp.where(gl == gmax, g_row, N_GROUPS), axis=0, keepdims=True)
    p_group = 1.0 / jnp.sum(jnp.exp(gl - gmax), axis=0, keepdims=True)
    e_row = lax.broadcasted_iota(I32, (N_EXPERTS, tm), 0)
    el = jnp.where((e_row >> 3) == g_idx, logits[N_GROUPS:N_GROUPS + N_EXPERTS, :], MASK_NEG)
    v1 = jnp.max(el, axis=0, keepdims=True)
    i1 = jnp.min(jnp.where(el == v1, e_row, N_EXPERTS), axis=0, keepdims=True)
    el2 = jnp.where(e_row == i1, MASK_NEG, el)
    v2 = jnp.max(el2, axis=0, keepdims=True)
    i2 = jnp.min(jnp.where(el2 == v2, e_row, N_EXPERTS), axis=0, keepdims=True)
    tt = jnp.exp(v2 - v1)
    w1 = p_group / (1.0 + tt)
    w2 = p_group * tt / (1.0 + tt)
    sel1 = e_row == i1
    sel2 = e_row == i2
    onehot = jnp.where(sel1 | sel2, 1.0, 0.0)
    t_in = lax.broadcasted_iota(I32, (tm, tm), 0)
    t_out = lax.broadcasted_iota(I32, (tm, tm), 1)
    tri = jnp.where(t_in < t_out, 1.0, 0.0).astype(BF)
    prefix = _dot(onehot.astype(BF), tri) + carry_ref[...]
    r1 = jnp.sum(jnp.where(sel1, prefix, 0.0), axis=0, keepdims=True)
    r2 = jnp.sum(jnp.where(sel2, prefix, 0.0), axis=0, keepdims=True)
    carry_ref[...] += jnp.sum(onehot, axis=1, keepdims=True)
    cnt_ref[...] = carry_ref[...]
    sub = pl.program_id(0) % ROUTE_STEPS_PER_PLANE
    for f, vals in enumerate((i1, i2, r1.astype(I32), r2.astype(I32))):
        for c in range(tm // LANES):
            oi_ref[f, pl.ds(sub * (tm // LANES) + c, 1), :] = vals[:, c * LANES:(c + 1) * LANES]
    w_row = lax.broadcasted_iota(I32, (LANES, tm), 0)
    w_rows = jnp.where(w_row == 0, w1, jnp.where(w_row == 1, w2, 0.0))
    of_ref[...] = jnp.transpose(w_rows)[:, :8]


def _router(ht, w_rg, b_rg, w_re, b_re):
    t = ht.shape[0] // SUBLANES
    d = w_rg.shape[0]
    tm = TM_ROUTE
    pad = ROUTE_ROWS - N_GROUPS - N_EXPERTS
    w = jnp.concatenate([w_rg.astype(F32), w_re.astype(F32), jnp.zeros((d, pad), F32)], axis=1).T
    b = jnp.concatenate([b_rg.astype(F32), b_re.astype(F32), jnp.zeros((pad,), F32)])[:, None]
    w_hi = w.astype(BF)
    w_lo = (w - w_hi.astype(F32)).astype(BF)
    tok = lambda i: (i, 0)
    const = lambda i: (0, 0)
    return pl.pallas_call(
        _router_kernel,
        grid=(t // tm,),
        in_specs=[pl.BlockSpec((tm * SUBLANES, LANES), tok),
                  pl.BlockSpec((ROUTE_ROWS, d), const),
                  pl.BlockSpec((ROUTE_ROWS, d), const),
                  pl.BlockSpec((ROUTE_ROWS, 1), const)],
        out_specs=[pl.BlockSpec((4, SUBLANES, LANES), lambda i: (0, i // ROUTE_STEPS_PER_PLANE, 0)),
                   pl.BlockSpec((tm, 8), tok),
                   pl.BlockSpec((N_EXPERTS, 1), const)],
        out_shape=[jax.ShapeDtypeStruct((4, t // LANES, LANES), I32), jax.ShapeDtypeStruct((t, 8), F32),
                   jax.ShapeDtypeStruct((N_EXPERTS, 1), F32)],
        scratch_shapes=[pltpu.VMEM((N_EXPERTS, 1), F32)],
        compiler_params=_cparams("arbitrary"),
        name="router",
    )(ht, w_hi, w_lo, b)


def _tile_copy(src_ref, src_row, dst_ref, dst_row, sem):
    src = pl.multiple_of(src_row * SUBLANES, SUBLANES)
    dst = pl.multiple_of(dst_row * SUBLANES, SUBLANES)
    return pltpu.make_async_copy(src_ref.at[pl.ds(src, SUBLANES)], dst_ref.at[pl.ds(dst, SUBLANES)], sem)


def _row_groups(n_rows, copies_of_row, start):
    def group(g, c):
        copies = []
        for u in range(ROW_GROUP):
            copies.extend(copies_of_row(g * ROW_GROUP + u))
        for cp in copies:
            if start:
                cp.start()
            else:
                cp.wait()
        return c

    lax.fori_loop(0, n_rows // ROW_GROUP, group, 0)


def _dests_kernel(ps_ref, idx_ref, d_ref):
    for a in range(2):
        e = idx_ref[a]
        d = idx_ref[2 + a]
        for k in range(N_EXPERTS):
            d = d + jnp.where(e == k, ps_ref[k], 0)
        d_ref[a] = d


def _dests(idx, pad_start):
    planes = idx.shape[1]
    out = pl.pallas_call(
        _dests_kernel,
        grid_spec=pltpu.PrefetchScalarGridSpec(
            num_scalar_prefetch=1,
            grid=(1,),
            in_specs=[pl.BlockSpec(idx.shape, lambda i, ps: (0, 0, 0))],
            out_specs=pl.BlockSpec((2, planes, LANES), lambda i, ps: (0, 0, 0)),
        ),
        out_shape=jax.ShapeDtypeStruct((2, planes, LANES), I32),
        compiler_params=_cparams("arbitrary"),
        name="dests",
    )(pad_start, idx)
    return out[0].reshape(-1), out[1].reshape(-1)


def _dispatch_kernel(ps_ref, pe_ref, d1_ref, d2_ref, ht_ref, xg_ref, zbuf, sem, zsem):
    i = pl.program_id(0)
    tm = d1_ref.shape[0]
    blk = MOE_BLOCK * SUBLANES
    n_blocks = xg_ref.shape[0] // blk

    @pl.when(i == 0)
    def _():
        zbuf[...] = jnp.zeros_like(zbuf)

        def zero_copy(block):
            return pltpu.make_async_copy(zbuf, xg_ref.at[pl.ds(pl.multiple_of(block * blk, blk), blk)], zsem)

        def seg_issue(e, c):
            @pl.when(pe_ref[e] > ps_ref[e])
            def _():
                zero_copy(pe_ref[e] // MOE_BLOCK - 1).start()
            return c

        def seg_drain(e, c):
            @pl.when(pe_ref[e] > ps_ref[e])
            def _():
                zero_copy(pe_ref[e] // MOE_BLOCK - 1).wait()
            return c

        def tail_issue(b, c):
            zero_copy(b).start()
            return c

        def tail_drain(b, c):
            zero_copy(b).wait()
            return c

        first_tail = pe_ref[N_EXPERTS - 1] // MOE_BLOCK
        lax.fori_loop(0, N_EXPERTS, seg_issue, 0)
        lax.fori_loop(first_tail, n_blocks, tail_issue, 0)
        lax.fori_loop(0, N_EXPERTS, seg_drain, 0)
        lax.fori_loop(first_tail, n_blocks, tail_drain, 0)

    _row_groups(tm, lambda t: [_tile_copy(ht_ref, t, xg_ref, d1_ref[t], sem),
                               _tile_copy(ht_ref, t, xg_ref, d2_ref[t], sem)], True)
    _row_groups(tm, lambda t: [_tile_copy(ht_ref, 0, xg_ref, 0, sem)] * 2, False)


def _dispatch(ht, d1, d2, pad_start, pad_end, n_pad):
    t = d1.shape[0]
    tm = TM_DISPATCH
    smem = lambda: pl.BlockSpec((tm,), lambda i, ps, pe: (i,), memory_space=pltpu.SMEM)
    return pl.pallas_call(
        _dispatch_kernel,
        grid_spec=pltpu.PrefetchScalarGridSpec(
            num_scalar_prefetch=2,
            grid=(t // tm,),
            in_specs=[smem(), smem(), pl.BlockSpec((tm * SUBLANES, LANES), lambda i, ps, pe: (i, 0))],
            out_specs=pl.BlockSpec(memory_space=pl.ANY),
            scratch_shapes=[pltpu.VMEM((MOE_BLOCK * SUBLANES, LANES), F32), pltpu.SemaphoreType.DMA(()),
                            pltpu.SemaphoreType.DMA(())],
        ),
        out_shape=jax.ShapeDtypeStruct((n_pad * SUBLANES, LANES), F32),
        compiler_params=_cparams("arbitrary"),
        name="dispatch",
    )(pad_start, pad_end, d1, d2, ht)


def _expert_kernel(be_ref, nx_ref, nu_ref, x_ref, wg_hbm, wu_hbm, wd_hbm, y_ref,
                   wg_f, wu_f, wd_f, wg_s, wu_s, wd_s, sems):
    j = pl.program_id(0)
    stage = ((wg_hbm, wg_f, wg_s), (wu_hbm, wu_f, wu_s), (wd_hbm, wd_f, wd_s))

    def weight_copies(e):
        return [pltpu.make_async_copy(hbm.at[e], buf, sems.at[k]) for k, (hbm, buf, _) in enumerate(stage)]

    @pl.when(j < nu_ref[0])
    def _():
        e = be_ref[j]

        @pl.when(j == 0)
        def _():
            for cp in weight_copies(e):
                cp.start()

        @pl.when((j == 0) | (e != be_ref[jnp.maximum(j - 1, 0)]))
        def _():
            for cp, (_, buf, dst) in zip(weight_copies(e), stage):
                cp.wait()
                dst[...] = buf[...].astype(BF)
            nxt = nx_ref[e]

            @pl.when(nxt >= 0)
            def _():
                for cp in weight_copies(nxt):
                    cp.start()

        rows = x_ref.shape[0] // EXPERT_SUBBLOCKS

        def up(h):
            xb = _load_tile_rows(x_ref.at[pl.ds(h * rows, rows)]).astype(BF)
            a = _dot(xb, wg_s[...])
            u = _dot(xb, wu_s[...])
            return ((a / (1.0 + jnp.exp(-a))) * u).astype(BF)

        mids = [up(h) for h in range(EXPERT_SUBBLOCKS)]
        for h, mid in enumerate(mids):
            _store_tile_rows(y_ref.at[pl.ds(h * rows, rows)], _dot(mid, wd_s[...]))

    @pl.when(j >= nu_ref[0])
    def _():
        y_ref[...] = jnp.zeros_like(y_ref)


def _experts(xg, blk_expert, next_expert, n_used, w_eg, w_eu, w_ed):
    blk = MOE_BLOCK * SUBLANES
    nb = xg.shape[0] // blk
    d, de = w_eg.shape[1], w_eg.shape[2]
    rows = lambda j, be, nx, nu: (jnp.minimum(j, nu[0] - 1), 0)
    hbm = pl.BlockSpec(memory_space=pl.ANY)
    return pl.pallas_call(
        _expert_kernel,
        grid_spec=pltpu.PrefetchScalarGridSpec(
            num_scalar_prefetch=3,
            grid=(nb,),
            in_specs=[pl.BlockSpec((blk, LANES), rows), hbm, hbm, hbm],
            out_specs=pl.BlockSpec((blk, LANES), lambda j, be, nx, nu: (j, 0)),
            scratch_shapes=[pltpu.VMEM((d, de), F32), pltpu.VMEM((d, de), F32), pltpu.VMEM((de, d), F32),
                            pltpu.VMEM((d, de), BF), pltpu.VMEM((d, de), BF), pltpu.VMEM((de, d), BF),
                            pltpu.SemaphoreType.DMA((3,))],
        ),
        out_shape=jax.ShapeDtypeStruct(xg.shape, F32),
        compiler_params=_cparams("arbitrary"),
        name="experts",
    )(blk_expert, next_expert, n_used, xg, w_eg, w_eu, w_ed)


def _combine_kernel(d1_ref, d2_ref, d1_next_ref, d2_next_ref, wt_ref, ht_ref, yb_ref, g_ref, b_ref, o_ref,
                    buf, sems):
    i = pl.program_id(0)
    n = pl.num_programs(0)
    tm = d1_ref.shape[0]

    def gather_tile(da_ref, db_ref, slot, start):
        def copies(t):
            if not start:
                return [_tile_copy(yb_ref, 0, buf.at[slot, a], 0, sems.at[slot]) for a in range(2)]
            return [_tile_copy(yb_ref, da_ref[t], buf.at[slot, 0], t, sems.at[slot]),
                    _tile_copy(yb_ref, db_ref[t], buf.at[slot, 1], t, sems.at[slot])]
        _row_groups(tm, copies, start)

    slot = i % 2

    @pl.when(i == 0)
    def _():
        gather_tile(d1_ref, d2_ref, 0, True)

    @pl.when(i + 1 < n)
    def _():
        gather_tile(d1_next_ref, d2_next_ref, 1 - slot, True)

    gather_tile(d1_ref, d2_ref, slot, False)
    wt = wt_ref[...]
    ffn = wt[:, 0:1] * _load_tile_rows(buf.at[slot, 0]) + wt[:, 1:2] * _load_tile_rows(buf.at[slot, 1])
    o_ref[...] = _layer_norm(ALPHA * _load_tile_rows(ht_ref) + ffn, g_ref[...], b_ref[...])


def _combine(ht, yb, d1, d2, wts, ln_g, ln_b):
    t = d1.shape[0]
    d = SUBLANES * LANES
    tm = TM_COMBINE
    n = t // tm
    tok = lambda i: (i, 0)
    const = lambda i: (0, 0)
    cur = lambda: pl.BlockSpec((tm,), lambda i: (i,), memory_space=pltpu.SMEM)
    nxt = lambda: pl.BlockSpec((tm,), lambda i: (jnp.minimum(i + 1, n - 1),), memory_space=pltpu.SMEM)
    return pl.pallas_call(
        _combine_kernel,
        grid=(n,),
        in_specs=[cur(), cur(), nxt(), nxt(),
                  pl.BlockSpec((tm, 8), tok),
                  pl.BlockSpec((tm * SUBLANES, LANES), tok),
                  pl.BlockSpec(memory_space=pl.ANY),
                  pl.BlockSpec((1, d), const), pl.BlockSpec((1, d), const)],
        out_specs=pl.BlockSpec((tm, d), tok),
        out_shape=jax.ShapeDtypeStruct((t, d), F32),
        scratch_shapes=[pltpu.VMEM((2, 2, tm * SUBLANES, LANES), F32), pltpu.SemaphoreType.DMA((2,))],
        compiler_params=_cparams("arbitrary"),
        name="combine",
    )(d1, d2, d1, d2, wts, ht, yb, ln_g, ln_b)


def _token_mixing(x2d, mem2d, batch, seq, m_len, w_in, w_gate, b_gate, w_mem_kv, rpb, w_fo, w_na, w_mo, w_out,
                  ln_g, ln_b):
    k_mem, v_mem = _memkv(mem2d, w_mem_kv.astype(BF), m_len)
    u_f, k, q3, v3, mo = _proj(x2d, w_in, k_mem, v_mem, seq, m_len)
    fo = _fourier(u_f, batch, seq)
    na = _natten(q3, k, v3, rpb, batch, seq)
    return _merge(x2d, fo, na, mo, w_gate.astype(BF), b_gate.astype(F32)[None, :], w_fo.astype(BF),
                  w_na.astype(BF), w_mo.astype(BF), w_out.astype(BF), ln_g.astype(F32)[None, :],
                  ln_b.astype(F32)[None, :])


def _moe(ht, w_rg, b_rg, w_re, b_re, w_eg, w_eu, w_ed, ln_g, ln_b):
    t = ht.shape[0] // SUBLANES
    idx, wts, cnt = _router(ht, w_rg, b_rg, w_re, b_re)
    counts = cnt[:, 0].astype(I32)
    padded = (counts + MOE_BLOCK - 1) // MOE_BLOCK * MOE_BLOCK
    pad_end = jnp.cumsum(padded).astype(I32)
    pad_start = pad_end - padded
    n_pad = 2 * t + N_EXPERTS * MOE_BLOCK
    nb = n_pad // MOE_BLOCK
    blk_start = jnp.arange(nb, dtype=I32) * MOE_BLOCK
    blk_expert = jnp.minimum(jnp.sum((pad_end[None, :] <= blk_start[:, None]).astype(I32), axis=1), N_EXPERTS - 1)
    n_used = pad_end[-1:] // MOE_BLOCK
    ids = jnp.arange(N_EXPERTS, dtype=I32)
    later_used = (padded[None, :] > 0) & (ids[None, :] > ids[:, None])
    next_expert = jnp.min(jnp.where(later_used, ids[None, :], N_EXPERTS), axis=1)
    next_expert = jnp.where(next_expert == N_EXPERTS, -1, next_expert).astype(I32)
    d1, d2 = _dests(idx, pad_start)
    xg = _dispatch(ht, d1, d2, pad_start, pad_end, n_pad)
    yb = _experts(xg, blk_expert, next_expert, n_used, w_eg, w_eu, w_ed)
    return _combine(ht, yb, d1, d2, wts, ln_g.astype(F32)[None, :], ln_b.astype(F32)[None, :])


def kernel(x, mem, w_in, w_gate, b_gate, w_mem_kv, rpb, w_fourier_o, w_na_o, w_mem_o, w_out, ln1_g, ln1_b,
           w_router_group, b_router_group, w_router_expert, b_router_expert, w_exp_gate, w_exp_up,
           w_exp_down, ln2_g, ln2_b):
    batch, seq, d = x.shape
    m_len = mem.shape[1]
    x2d = x.reshape(batch * seq, d)
    mem2d = mem.reshape(batch * m_len, d)
    for l in range(w_in.shape[0]):
        ht = _token_mixing(x2d, mem2d, batch, seq, m_len, w_in[l], w_gate[l], b_gate[l], w_mem_kv[l], rpb[l],
                           w_fourier_o[l], w_na_o[l], w_mem_o[l], w_out[l], ln1_g[l], ln1_b[l])
        x2d = _moe(ht, w_router_group[l], b_router_group[l], w_router_expert[l], b_router_expert[l],
                   w_exp_gate[l], w_exp_up[l], w_exp_down[l], ln2_g[l], ln2_b[l])
    return x2d.reshape(batch, seq, d)
```

```python
import functools

import numpy as np
import jax
import jax.numpy as jnp
from jax import lax
from jax.experimental import pallas as pl
from jax.experimental.pallas import tpu as pltpu

BF = jnp.bfloat16
F32 = jnp.float32
I32 = jnp.int32
U32 = jnp.uint32

GRID_W = 64
MEM_HEADS = 4
MEM_HEAD_DIM = 128
MEM_DIM = MEM_HEADS * MEM_HEAD_DIM
FOURIER_GROUPS = 4
FOURIER_GROUP_DIM = 128
FOURIER_DIM = FOURIER_GROUPS * FOURIER_GROUP_DIM
NA_HEADS = 8
NA_HEAD_DIM = 64
NA_DIM = NA_HEADS * NA_HEAD_DIM
NA_ROW_WIN = 8
NA_COL_WIN = 16
N_GROUPS = 8
EXPERTS_PER_GROUP = 8
N_EXPERTS = N_GROUPS * EXPERTS_PER_GROUP
DEPTH = 1
ALPHA = (2.0 * DEPTH) ** 0.25
LN_EPS = 1e-5
NA_SCALE = NA_HEAD_DIM ** -0.5
LOG2_E = 1.4426950408889634
MEM_SCALE = MEM_HEAD_DIM ** -0.5
MASK_NEG = -1e30

LANES = 128
SUBLANES = 8
FFT_N2 = 128
TM_PROJ = 512
TM_MERGE = 512
TM_ROUTE = 512
TM_DISPATCH = 1024
TM_COMBINE = 256
NA_GROUP_ROWS = 4
NA_GROUP_WIN = NA_GROUP_ROWS + NA_ROW_WIN
NA_GROUPS_PER_STEP = 8
NA_SUM_ROWS = 16
F1_POS = 16
F2_SLABS = 8
MOE_BLOCK = 256
PACK_ROWS = 4
EXPERT_SUBBLOCKS = 2
ROUTE_ROWS = 128
ROUTE_STEPS_PER_PLANE = SUBLANES * LANES // TM_ROUTE
ROW_GROUP = 8
VMEM_LIMIT = 56 * 1024 * 1024


def _cparams(*sem):
    return pltpu.CompilerParams(dimension_semantics=sem, vmem_limit_bytes=VMEM_LIMIT)


def _dot(a, b):
    return jnp.dot(a, b, preferred_element_type=F32)


def _dot_nt(a, b):
    return lax.dot_general(a, b, (((1,), (1,)), ((), ())), preferred_element_type=F32)


def _layer_norm(y, g, b):
    mu = jnp.mean(y, axis=-1, keepdims=True)
    yc = y - mu
    var = jnp.mean(yc * yc, axis=-1, keepdims=True)
    return yc * lax.rsqrt(var + LN_EPS) * g + b


def _store_tile_rows(ref, val):
    n = val.shape[0]
    for s in range(SUBLANES):
        ref[pl.ds(s, n, stride=SUBLANES), :] = val[:, s * LANES:(s + 1) * LANES]


def _load_tile_rows(ref):
    n = ref.shape[0] // SUBLANES
    return jnp.concatenate([ref[pl.ds(s, n, stride=SUBLANES), :] for s in range(SUBLANES)], axis=1)


def _memkv_kernel(mem_ref, w_ref, k_ref, v_ref):
    kv = _dot(mem_ref[...].astype(BF), w_ref[...])
    k_ref[...] = kv[:, :MEM_DIM].astype(BF)
    v_ref[...] = kv[:, MEM_DIM:].astype(BF)


def _memkv(mem2d, w_kv, m_len):
    rows, d = mem2d.shape
    return pl.pallas_call(
        _memkv_kernel,
        grid=(rows // m_len,),
        in_specs=[pl.BlockSpec((m_len, d), lambda i: (i, 0)),
                  pl.BlockSpec((d, 2 * MEM_DIM), lambda i: (0, 0))],
        out_specs=[pl.BlockSpec((m_len, MEM_DIM), lambda i: (i, 0))] * 2,
        out_shape=[jax.ShapeDtypeStruct((rows, MEM_DIM), BF)] * 2,
        compiler_params=_cparams("parallel"),
        name="memkv",
    )(mem2d, w_kv)


def _proj_kernel(x_ref, w_ref, wt_ref, km_ref, vm_ref, uf_ref, k_ref, q3_ref, v3_ref, mo_ref):
    xb = x_ref[...].astype(BF)

    def seg(j):
        return _dot(xb, w_ref[:, j * 512:(j + 1) * 512])

    uf_ref[...] = seg(0)
    k_ref[...] = seg(1).astype(BF)
    q_t = _dot_nt(wt_ref[:NA_DIM, :], xb) * (NA_SCALE * LOG2_E)
    v_t = _dot_nt(wt_ref[NA_DIM:, :], xb)
    for s in range(q3_ref.shape[0]):
        q3_ref[s] = q_t[:, s * LANES:(s + 1) * LANES].astype(BF)
        v3_ref[s] = v_t[:, s * LANES:(s + 1) * LANES].astype(BF)
    qm = seg(2).astype(BF)
    for h in range(MEM_HEADS):
        sl = slice(h * MEM_HEAD_DIM, (h + 1) * MEM_HEAD_DIM)
        s = _dot_nt(qm[:, sl], km_ref[:, sl]) * MEM_SCALE
        m = jnp.max(s, axis=-1, keepdims=True)
        p = jnp.exp(s - m)
        l = jnp.sum(p, axis=-1, keepdims=True)
        o = _dot(p.astype(BF), vm_ref[:, sl])
        mo_ref[:, sl] = (o / l).astype(BF)


def _proj(x2d, w_in, k_mem, v_mem, seq, m_len):
    t, d = x2d.shape
    tm = TM_PROJ
    tiles_per_batch = seq // tm
    w_tok = jnp.concatenate([w_in[:, 0:512], w_in[:, 1024:1536], w_in[:, 2048:2560]], axis=1).astype(BF)
    w_chan = jnp.concatenate([w_in[:, 512:1024], w_in[:, 1536:2048]], axis=1).T.astype(BF)
    tok = lambda i: (i, 0)
    slab = lambda i: (i, 0, 0)
    memb = lambda i: (i // tiles_per_batch, 0)
    const = lambda i: (0, 0)
    tok_out = pl.BlockSpec((tm, 512), tok)
    slab_out = pl.BlockSpec((tm // LANES, NA_DIM, LANES), slab)
    slab_shape = jax.ShapeDtypeStruct((t // LANES, NA_DIM, LANES), BF)
    return pl.pallas_call(
        _proj_kernel,
        grid=(t // tm,),
        in_specs=[pl.BlockSpec((tm, d), tok),
                  pl.BlockSpec(w_tok.shape, const),
                  pl.BlockSpec(w_chan.shape, const),
                  pl.BlockSpec((m_len, MEM_DIM), memb),
                  pl.BlockSpec((m_len, MEM_DIM), memb)],
        out_specs=[tok_out, tok_out, slab_out, slab_out, tok_out],
        out_shape=[jax.ShapeDtypeStruct((t, 512), F32), jax.ShapeDtypeStruct((t, 512), BF)] + [slab_shape] * 2
                  + [jax.ShapeDtypeStruct((t, 512), BF)],
        compiler_params=_cparams("parallel"),
        name="proj",
    )(x2d, w_tok, w_chan, k_mem, v_mem)


def _fourier_tables(n1, n2):
    n = n1 * n2
    k1 = np.arange(n1)
    ang1 = 2.0 * np.pi * ((k1[:, None] * k1[None, :]) % n1) / n1
    norm = 1.0 / np.sqrt(float(n) * FOURIER_GROUP_DIM)
    m1 = np.concatenate([np.cos(ang1), -np.sin(ang1)], axis=0) * norm
    kk = k1[:, None, None] + n1 * np.arange(n2)[None, :, None]
    nn = np.arange(n2)[None, None, :]
    ang2 = 2.0 * np.pi * ((kk * nn) % n) / n
    c2, s2 = np.cos(ang2), np.sin(ang2)
    m2 = np.concatenate([np.concatenate([c2, s2], axis=2),
                         np.concatenate([-s2, c2], axis=2)], axis=1)
    c = np.arange(FOURIER_GROUP_DIM)
    angc = 2.0 * np.pi * ((c[:, None] * c[None, :]) % FOURIER_GROUP_DIM) / FOURIER_GROUP_DIM
    mc = np.concatenate([np.cos(angc), np.sin(angc)], axis=0)
    as_bf = lambda a: jnp.asarray(a.astype(np.float32)).astype(BF)
    return as_bf(m1), as_bf(m2), as_bf(mc)


def _f1_kernel(u_ref, m_ref, a_ref):
    n1, pos, _ = u_ref.shape
    x = jnp.concatenate([u_ref[:, p, :] for p in range(pos)], axis=1).astype(BF)
    r = _dot(m_ref[...], x)
    a_ref[0] = r[:n1].astype(BF)
    a_ref[1] = r[n1:].astype(BF)


def _f2_kernel(a_ref, m2_ref, mc_ref, o_ref):
    slabs = a_ref.shape[1]
    n2 = a_ref.shape[2]
    for t in range(slabs):
        a = jnp.concatenate([a_ref[0, t], a_ref[1, t]], axis=0)
        y = _dot(m2_ref[t], a)
        yr = y[:n2].astype(BF)
        yi = y[n2:].astype(BF)
        for g in range(FOURIER_GROUPS):
            sl = slice(g * LANES, (g + 1) * LANES)
            yy = jnp.concatenate([yr[:, sl], yi[:, sl]], axis=1)
            o_ref[:, t, sl] = _dot(yy, mc_ref[...])


def _fourier(u_f, batch, seq):
    n2 = FFT_N2
    n1 = seq // n2
    m1, m2, mc = _fourier_tables(n1, n2)
    cols = n2 * FOURIER_DIM
    pos = min(F1_POS, n2)
    a = pl.pallas_call(
        _f1_kernel,
        grid=(batch, n2 // pos),
        in_specs=[pl.BlockSpec((None, n1, pos, FOURIER_DIM), lambda b, j: (b, 0, j, 0)),
                  pl.BlockSpec((2 * n1, n1), lambda b, j: (0, 0))],
        out_specs=pl.BlockSpec((None, 2, n1, pos * FOURIER_DIM), lambda b, j: (b, 0, 0, j)),
        out_shape=jax.ShapeDtypeStruct((batch, 2, n1, cols), BF),
        compiler_params=_cparams("parallel", "parallel"),
        name="fourier1",
    )(u_f.reshape(batch, n1, n2, FOURIER_DIM), m1)
    kb = min(F2_SLABS, n1)
    out = pl.pallas_call(
        _f2_kernel,
        grid=(batch, n1 // kb),
        in_specs=[pl.BlockSpec((None, 2, kb, n2, FOURIER_DIM), lambda b, j: (b, 0, j, 0, 0)),
                  pl.BlockSpec((kb, 2 * n2, 2 * n2), lambda b, j: (j, 0, 0)),
                  pl.BlockSpec((2 * FOURIER_GROUP_DIM, FOURIER_GROUP_DIM), lambda b, j: (0, 0))],
        out_specs=pl.BlockSpec((None, n2, kb, FOURIER_DIM), lambda b, j: (b, 0, j, 0)),
        out_shape=jax.ShapeDtypeStruct((batch, n2, n1, FOURIER_DIM), F32),
        compiler_params=_cparams("parallel", "parallel"),
        name="fourier2",
    )(a.reshape(batch, 2, n1, n2, FOURIER_DIM), m2, mc)
    return out.reshape(batch * seq, FOURIER_DIM)


def _na_group_start(r0, n_rows):
    rs0 = min(max(r0 - NA_ROW_WIN // 2, 0), n_rows - NA_ROW_WIN)
    return min(rs0, n_rows - NA_GROUP_WIN)


def _na_bias_table(rpb, n_rows):
    g = NA_GROUP_ROWS
    assert n_rows >= 2 * NA_GROUP_WIN and n_rows % g == 0 and (NA_ROW_WIN // 2) % g == 0
    cols = np.arange(GRID_W)
    cs = np.clip(cols - NA_COL_WIN // 2, 0, GRID_W - NA_COL_WIN)
    kc = np.arange(GRID_W)
    in_win = (kc[None, :] >= cs[:, None]) & (kc[None, :] < cs[:, None] + NA_COL_WIN)
    dc = kc[None, :] - cols[:, None] + (NA_COL_WIN - 1)
    n_dc = 2 * NA_COL_WIN - 1
    pick = ((dc.reshape(-1)[None, :] == np.arange(n_dc)[:, None]) & in_win.reshape(-1)[None, :])
    picked = jnp.dot(rpb.astype(F32).reshape(-1, n_dc), jnp.asarray(pick.astype(np.float32)),
                     precision=lax.Precision.HIGHEST)
    picked = picked.reshape(NA_HEADS, 2 * NA_ROW_WIN - 1, GRID_W, GRID_W)
    full_t = jnp.where(in_win[None, None], picked * LOG2_E, MASK_NEG).transpose(0, 1, 3, 2)
    masked = jnp.full((NA_HEADS, GRID_W, GRID_W), MASK_NEG, F32)
    half = NA_ROW_WIN // 2
    group_rows = list(range(0, half, g)) + [half] + list(range(n_rows - half, n_rows, g))
    tabs = []
    for r0 in group_rows:
        start = _na_group_start(r0, n_rows)
        key_rows = []
        for kr in range(NA_GROUP_WIN):
            blocks = []
            for rr in range(g):
                r = r0 + rr
                rs = min(max(r - half, 0), n_rows - NA_ROW_WIN)
                in_rows = rs <= start + kr < rs + NA_ROW_WIN
                blocks.append(full_t[:, start + kr - r + NA_ROW_WIN - 1] if in_rows else masked)
            key_rows.append(jnp.concatenate(blocks, axis=2))
        tabs.append(jnp.concatenate(key_rows, axis=1))
    return jnp.stack(tabs, axis=0)


def _na_kernel(q3_ref, k_ref, v3_ref, tab_ref, o_ref, *, n_rows):
    i = pl.program_id(2)
    g = NA_GROUP_ROWS
    slabs_per_group = g * GRID_W // LANES
    groups = q3_ref.shape[0] // slabs_per_group
    win = NA_GROUP_WIN * GRID_W
    tokens = g * GRID_W
    half = NA_ROW_WIN // 2
    n_top = half // g
    chan = lax.broadcasted_iota(I32, (2 * NA_HEAD_DIM, tokens), 0)
    first_head = chan < NA_HEAD_DIM

    def scores(jj):
        r0 = (i * groups + jj) * g
        start = jnp.minimum(jnp.clip(r0 - half, 0, n_rows - NA_ROW_WIN), n_rows - NA_GROUP_WIN)
        cfg = jnp.where(r0 < half, r0 // g,
                        jnp.where(r0 >= n_rows - half, n_top + 1 + (r0 - (n_rows - half)) // g, n_top))
        kw = k_ref[pl.ds(pl.multiple_of(start * GRID_W, LANES), win), :]
        qt = jnp.concatenate([q3_ref[jj * slabs_per_group + t] for t in range(slabs_per_group)], axis=1)
        zero = jnp.zeros_like(qt)
        q_bd = jnp.concatenate([jnp.where(first_head, qt, zero), jnp.where(first_head, zero, qt)], axis=1)
        s = _dot(kw, q_bd) + jnp.concatenate([tab_ref[cfg, 0], tab_ref[cfg, 1]], axis=1)
        return s, start

    ones_rows = jnp.ones((NA_SUM_ROWS, win), BF)

    def outputs(jj, p, start):
        slab = start // 2
        vw = jnp.concatenate([v3_ref[slab + t] for t in range(NA_GROUP_WIN // 2)] , axis=1)
        o_l = _dot(jnp.concatenate([vw, ones_rows], axis=0), p)
        o_t = o_l[:2 * NA_HEAD_DIM] / o_l[2 * NA_HEAD_DIM:2 * NA_HEAD_DIM + 1]
        o = jnp.where(first_head, o_t[:, :tokens], o_t[:, tokens:])
        o_ref[jj * tokens:(jj + 1) * tokens, :] = jnp.transpose(o).astype(BF)

    pending = None
    for jj in range(groups):
        s, start = scores(jj)
        if pending is not None:
            outputs(*pending)
        p = jnp.exp2(s - jnp.max(s, axis=0, keepdims=True))
        pending = (jj, p.astype(BF), start)
    outputs(*pending)


def _natten(q3, k, v3, rpb, batch, seq):
    assert 2 * GRID_W == LANES and 2 * NA_HEAD_DIM == LANES and NA_GROUP_ROWS % 2 == 0
    n_rows = seq // GRID_W
    slabs = n_rows // 2
    step_slabs = min(NA_GROUPS_PER_STEP * NA_GROUP_ROWS // 2, slabs)
    steps = slabs // step_slabs
    hp = NA_HEADS // 2
    tab = _na_bias_table(rpb, n_rows)
    return pl.pallas_call(
        functools.partial(_na_kernel, n_rows=n_rows),
        grid=(batch, hp, steps),
        in_specs=[pl.BlockSpec((step_slabs, LANES, LANES), lambda b, p, i: (b * steps + i, p, 0)),
                  pl.BlockSpec((seq, LANES), lambda b, p, i: (b, p)),
                  pl.BlockSpec((slabs, LANES, LANES), lambda b, p, i: (b, p, 0)),
                  pl.BlockSpec((tab.shape[0], 2) + tab.shape[2:], lambda b, p, i: (0, p, 0, 0))],
        out_specs=pl.BlockSpec((step_slabs * LANES, LANES), lambda b, p, i: (b * steps + i, p)),
        out_shape=jax.ShapeDtypeStruct((batch * seq, NA_DIM), BF),
        compiler_params=_cparams("parallel", "parallel", "parallel"),
        name="natten",
    )(q3, k, v3, tab)


def _merge_kernel(x_ref, fo_ref, na_ref, mo_ref, wg_ref, bg_ref, wf_ref, wn_ref, wm_ref, wo_ref,
                  g_ref, b_ref, ht_ref):
    d = x_ref.shape[1]
    x = x_ref[...]
    xb = x.astype(BF)
    merged = None
    for j, (br_ref, w_ref) in enumerate(((fo_ref, wf_ref), (na_ref, wn_ref), (mo_ref, wm_ref))):
        z = _dot(xb, wg_ref[:, j * d:(j + 1) * d]) + bg_ref[:, j * d:(j + 1) * d]
        gate = 1.0 / (1.0 + jnp.exp(-z))
        term = gate * _dot(br_ref[...].astype(BF), w_ref[...])
        merged = term if merged is None else merged + term
    mix = _dot(merged.astype(BF), wo_ref[...])
    _store_tile_rows(ht_ref, _layer_norm(ALPHA * x + mix, g_ref[...], b_ref[...]))


def _merge(x2d, fo, na, mo, w_gate, b_gate, w_fo, w_na, w_mo, w_out, ln_g, ln_b):
    t, d = x2d.shape
    assert d == SUBLANES * LANES
    tm = TM_MERGE
    tok = lambda i: (i, 0)
    full = lambda a: pl.BlockSpec(a.shape, lambda i: (0, 0))
    return pl.pallas_call(
        _merge_kernel,
        grid=(t // tm,),
        in_specs=[pl.BlockSpec((tm, d), tok)] + [pl.BlockSpec((tm, 512), tok)] * 3
                 + [full(a) for a in (w_gate, b_gate, w_fo, w_na, w_mo, w_out, ln_g, ln_b)],
        out_specs=pl.BlockSpec((tm * SUBLANES, LANES), tok),
        out_shape=jax.ShapeDtypeStruct((t * SUBLANES, LANES), F32),
        compiler_params=_cparams("parallel"),
        name="merge",
    )(x2d, fo, na, mo, w_gate, b_gate, w_fo, w_na, w_mo, w_out, ln_g, ln_b)


def _router_kernel(ht_ref, whi_ref, wlo_ref, b_ref, oi_ref, of_ref, cnt_ref, carry_ref):
    tm = ht_ref.shape[0] // SUBLANES

    @pl.when(pl.program_id(0) == 0)
    def _():
        carry_ref[...] = jnp.zeros_like(carry_ref)

    h = _load_tile_rows(ht_ref)
    hh = h.astype(BF)
    hl = (h - hh.astype(F32)).astype(BF)
    logits = (_dot_nt(whi_ref[...], hh) + _dot_nt(wlo_ref[...], hh) + _dot_nt(whi_ref[...], hl)
              + b_ref[...])
    gl = logits[:N_GROUPS, :]
    g_row = lax.broadcasted_iota(I32, (N_GROUPS, tm), 0)
    gmax = jnp.max(gl, axis=0, keepdims=True)
    g_idx = jnp.min(jnp.where(gl == gmax, g_row, N_GROUPS), axis=0, keepdims=True)
    p_group = 1.0 / jnp.sum(jnp.exp(gl - gmax), axis=0, keepdims=True)
    e_row = lax.broadcasted_iota(I32, (N_EXPERTS, tm), 0)
    el = jnp.where((e_row >> 3) == g_idx, logits[N_GROUPS:N_GROUPS + N_EXPERTS, :], MASK_NEG)
    v1 = jnp.max(el, axis=0, keepdims=True)
    i1 = jnp.min(jnp.where(el == v1, e_row, N_EXPERTS), axis=0, keepdims=True)
    el2 = jnp.where(e_row == i1, MASK_NEG, el)
    v2 = jnp.max(el2, axis=0, keepdims=True)
    i2 = jnp.min(jnp.where(el2 == v2, e_row, N_EXPERTS), axis=0, keepdims=True)
    tt = jnp.exp(v2 - v1)
    w1 = p_group / (1.0 + tt)
    w2 = p_group * tt / (1.0 + tt)
    sel1 = e_row == i1
    sel2 = e_row == i2
    onehot = jnp.where(sel1 | sel2, 1.0, 0.0)
    t_in = lax.broadcasted_iota(I32, (tm, tm), 0)
    t_out = lax.broadcasted_iota(I32, (tm, tm), 1)
    tri = jnp.where(t_in < t_out, 1.0, 0.0).astype(BF)
    prefix = _dot(onehot.astype(BF), tri) + carry_ref[...]
    r1 = jnp.sum(jnp.where(sel1, prefix, 0.0), axis=0, keepdims=True)
    r2 = jnp.sum(jnp.where(sel2, prefix, 0.0), axis=0, keepdims=True)
    carry_ref[...] += jnp.sum(onehot, axis=1, keepdims=True)
    cnt_ref[...] = carry_ref[...]
    sub = pl.program_id(0) % ROUTE_STEPS_PER_PLANE
    for f, vals in enumerate((i1, i2, r1.astype(I32), r2.astype(I32))):
        for c in range(tm // LANES):
            oi_ref[f, pl.ds(sub * (tm // LANES) + c, 1), :] = vals[:, c * LANES:(c + 1) * LANES]
    w_row = lax.broadcasted_iota(I32, (LANES, tm), 0)
    w_rows = jnp.where(w_row == 0, w1, jnp.where(w_row == 1, w2, 0.0))
    of_ref[...] = jnp.transpose(w_rows)[:, :8]


def _router(ht, w_rg, b_rg, w_re, b_re):
    t = ht.shape[0] // SUBLANES
    d = w_rg.shape[0]
    tm = TM_ROUTE
    pad = ROUTE_ROWS - N_GROUPS - N_EXPERTS
    w = jnp.concatenate([w_rg.astype(F32), w_re.astype(F32), jnp.zeros((d, pad), F32)], axis=1).T
    b = jnp.concatenate([b_rg.astype(F32), b_re.astype(F32), jnp.zeros((pad,), F32)])[:, None]
    w_hi = w.astype(BF)
    w_lo = (w - w_hi.astype(F32)).astype(BF)
    tok = lambda i: (i, 0)
    const = lambda i: (0, 0)
    return pl.pallas_call(
        _router_kernel,
        grid=(t // tm,),
        in_specs=[pl.BlockSpec((tm * SUBLANES, LANES), tok),
                  pl.BlockSpec((ROUTE_ROWS, d), const),
                  pl.BlockSpec((ROUTE_ROWS, d), const),
                  pl.BlockSpec((ROUTE_ROWS, 1), const)],
        out_specs=[pl.BlockSpec((4, SUBLANES, LANES), lambda i: (0, i // ROUTE_STEPS_PER_PLANE, 0)),
                   pl.BlockSpec((tm, 8), tok),
                   pl.BlockSpec((N_EXPERTS, 1), const)],
        out_shape=[jax.ShapeDtypeStruct((4, t // LANES, LANES), I32), jax.ShapeDtypeStruct((t, 8), F32),
                   jax.ShapeDtypeStruct((N_EXPERTS, 1), F32)],
        scratch_shapes=[pltpu.VMEM((N_EXPERTS, 1), F32)],
        compiler_params=_cparams("arbitrary"),
        name="router",
    )(ht, w_hi, w_lo, b)


def _row_groups(n_rows, copies_of_row, start):
    def group(g, c):
        copies = []
        for u in range(ROW_GROUP):
            copies.extend(copies_of_row(g * ROW_GROUP + u))
        for cp in copies:
            if start:
                cp.start()
            else:
                cp.wait()
        return c

    lax.fori_loop(0, n_rows // ROW_GROUP, group, 0)


def _dests_kernel(ps_ref, idx_ref, d_ref):
    for a in range(2):
        e = idx_ref[a]
        d = idx_ref[2 + a]
        for k in range(N_EXPERTS):
            d = d + jnp.where(e == k, ps_ref[k], 0)
        d_ref[a] = d


def _dests(idx, pad_start):
    planes = idx.shape[1]
    out = pl.pallas_call(
        _dests_kernel,
        grid_spec=pltpu.PrefetchScalarGridSpec(
            num_scalar_prefetch=1,
            grid=(1,),
            in_specs=[pl.BlockSpec(idx.shape, lambda i, ps: (0, 0, 0))],
            out_specs=pl.BlockSpec((2, planes, LANES), lambda i, ps: (0, 0, 0)),
        ),
        out_shape=jax.ShapeDtypeStruct((2, planes, LANES), I32),
        compiler_params=_cparams("arbitrary"),
        name="dests",
    )(pad_start, idx)
    return out[0].reshape(-1), out[1].reshape(-1)


def _pack_words(val):
    words = []
    for c in range(PACK_ROWS):
        hi = pltpu.bitcast(val[:, c * LANES:(c + 1) * LANES].astype(BF).astype(F32), U32)
        lo = pltpu.bitcast(val[:, (c + PACK_ROWS) * LANES:(c + PACK_ROWS + 1) * LANES].astype(BF).astype(F32), U32)
        words.append(hi | (lo >> 16))
    return words


def _unpack_words(words):
    his = [pltpu.bitcast(w & jnp.uint32(0xFFFF0000), F32) for w in words]
    los = [pltpu.bitcast(w << 16, F32) for w in words]
    return jnp.concatenate(his + los, axis=1)


def _load_packed(ref):
    n = ref.shape[0] // PACK_ROWS
    return [ref[pl.ds(c, n, stride=PACK_ROWS), :] for c in range(PACK_ROWS)]


def _store_packed(ref, words):
    n = ref.shape[0] // PACK_ROWS
    for c, w in enumerate(words):
        ref[pl.ds(c, n, stride=PACK_ROWS), :] = w


def _packed_row_copy(src_ref, src_row, dst_ref, dst_row, sem):
    src = pl.multiple_of(src_row * PACK_ROWS, PACK_ROWS)
    dst = pl.multiple_of(dst_row * PACK_ROWS, PACK_ROWS)
    return pltpu.make_async_copy(src_ref.at[pl.ds(src, PACK_ROWS)], dst_ref.at[pl.ds(dst, PACK_ROWS)], sem)


def _dispatch_kernel(ps_ref, pe_ref, d1_ref, d2_ref, ht_ref, xg_ref, pk, zbuf, sem, zsem):
    i = pl.program_id(0)
    tm = d1_ref.shape[0]
    blk = MOE_BLOCK * PACK_ROWS
    n_blocks = xg_ref.shape[0] // blk

    @pl.when(i == 0)
    def _():
        zbuf[...] = jnp.zeros_like(zbuf)

        def zero_copy(block):
            return pltpu.make_async_copy(zbuf, xg_ref.at[pl.ds(pl.multiple_of(block * blk, blk), blk)], zsem)

        def seg_issue(e, c):
            @pl.when(pe_ref[e] > ps_ref[e])
            def _():
                zero_copy(pe_ref[e] // MOE_BLOCK - 1).start()
            return c

        def seg_drain(e, c):
            @pl.when(pe_ref[e] > ps_ref[e])
            def _():
                zero_copy(pe_ref[e] // MOE_BLOCK - 1).wait()
            return c

        def tail_issue(b, c):
            zero_copy(b).start()
            return c

        def tail_drain(b, c):
            zero_copy(b).wait()
            return c

        first_tail = pe_ref[N_EXPERTS - 1] // MOE_BLOCK
        lax.fori_loop(0, N_EXPERTS, seg_issue, 0)
        lax.fori_loop(first_tail, n_blocks, tail_issue, 0)
        lax.fori_loop(0, N_EXPERTS, seg_drain, 0)
        lax.fori_loop(first_tail, n_blocks, tail_drain, 0)

    _store_packed(pk, _pack_words(_load_tile_rows(ht_ref)))
    _row_groups(tm, lambda t: [_packed_row_copy(pk, t, xg_ref, d1_ref[t], sem),
                               _packed_row_copy(pk, t, xg_ref, d2_ref[t], sem)], True)
    _row_groups(tm, lambda t: [_packed_row_copy(pk, 0, xg_ref, 0, sem)] * 2, False)


def _dispatch(ht, d1, d2, pad_start, pad_end, n_pad):
    t = d1.shape[0]
    tm = TM_DISPATCH
    smem = lambda: pl.BlockSpec((tm,), lambda i, ps, pe: (i,), memory_space=pltpu.SMEM)
    return pl.pallas_call(
        _dispatch_kernel,
        grid_spec=pltpu.PrefetchScalarGridSpec(
            num_scalar_prefetch=2,
            grid=(t // tm,),
            in_specs=[smem(), smem(), pl.BlockSpec((tm * SUBLANES, LANES), lambda i, ps, pe: (i, 0))],
            out_specs=pl.BlockSpec(memory_space=pl.ANY),
            scratch_shapes=[pltpu.VMEM((tm * PACK_ROWS, LANES), U32), pltpu.VMEM((MOE_BLOCK * PACK_ROWS, LANES), U32),
                            pltpu.SemaphoreType.DMA(()), pltpu.SemaphoreType.DMA(())],
        ),
        out_shape=jax.ShapeDtypeStruct((n_pad * PACK_ROWS, LANES), U32),
        compiler_params=_cparams("arbitrary"),
        name="dispatch",
    )(pad_start, pad_end, d1, d2, ht)


def _expert_kernel(be_ref, nx_ref, nu_ref, x_ref, wg_hbm, wu_hbm, wd_hbm, y_ref,
                   wg_f, wu_f, wd_f, wg_s, wu_s, wd_s, sems):
    j = pl.program_id(0)
    stage = ((wg_hbm, wg_f, wg_s), (wu_hbm, wu_f, wu_s), (wd_hbm, wd_f, wd_s))

    def weight_copies(e):
        return [pltpu.make_async_copy(hbm.at[e], buf, sems.at[k]) for k, (hbm, buf, _) in enumerate(stage)]

    @pl.when(j < nu_ref[0])
    def _():
        e = be_ref[j]

        @pl.when(j == 0)
        def _():
            for cp in weight_copies(e):
                cp.start()

        @pl.when((j == 0) | (e != be_ref[jnp.maximum(j - 1, 0)]))
        def _():
            for cp, (_, buf, dst) in zip(weight_copies(e), stage):
                cp.wait()
                dst[...] = buf[...].astype(BF)
            nxt = nx_ref[e]

            @pl.when(nxt >= 0)
            def _():
                for cp in weight_copies(nxt):
                    cp.start()

        rows = x_ref.shape[0] // EXPERT_SUBBLOCKS

        def up(h):
            xb = _unpack_words(_load_packed(x_ref.at[pl.ds(h * rows, rows)])).astype(BF)
            a = _dot(xb, wg_s[...])
            u = _dot(xb, wu_s[...])
            return ((a / (1.0 + jnp.exp(-a))) * u).astype(BF)

        mids = [up(h) for h in range(EXPERT_SUBBLOCKS)]
        for h, mid in enumerate(mids):
            _store_packed(y_ref.at[pl.ds(h * rows, rows)], _pack_words(_dot(mid, wd_s[...])))

    @pl.when(j >= nu_ref[0])
    def _():
        y_ref[...] = jnp.zeros_like(y_ref)


def _experts(xg, blk_expert, next_expert, n_used, w_eg, w_eu, w_ed):
    blk = MOE_BLOCK * PACK_ROWS
    nb = xg.shape[0] // blk
    d, de = w_eg.shape[1], w_eg.shape[2]
    rows = lambda j, be, nx, nu: (jnp.minimum(j, nu[0] - 1), 0)
    hbm = pl.BlockSpec(memory_space=pl.ANY)
    return pl.pallas_call(
        _expert_kernel,
        grid_spec=pltpu.PrefetchScalarGridSpec(
            num_scalar_prefetch=3,
            grid=(nb,),
            in_specs=[pl.BlockSpec((blk, LANES), rows), hbm, hbm, hbm],
            out_specs=pl.BlockSpec((blk, LANES), lambda j, be, nx, nu: (j, 0)),
            scratch_shapes=[pltpu.VMEM((d, de), F32), pltpu.VMEM((d, de), F32), pltpu.VMEM((de, d), F32),
                            pltpu.VMEM((d, de), BF), pltpu.VMEM((d, de), BF), pltpu.VMEM((de, d), BF),
                            pltpu.SemaphoreType.DMA((3,))],
        ),
        out_shape=jax.ShapeDtypeStruct(xg.shape, U32),
        compiler_params=_cparams("arbitrary"),
        name="experts",
    )(blk_expert, next_expert, n_used, xg, w_eg, w_eu, w_ed)


def _combine_kernel(d1_ref, d2_ref, d1_next_ref, d2_next_ref, wt_ref, ht_ref, yb_ref, g_ref, b_ref, o_ref,
                    buf, sems):
    i = pl.program_id(0)
    n = pl.num_programs(0)
    tm = d1_ref.shape[0]

    def gather_tile(da_ref, db_ref, slot, start):
        def copies(t):
            if not start:
                return [_packed_row_copy(yb_ref, 0, buf.at[slot, a], 0, sems.at[slot]) for a in range(2)]
            return [_packed_row_copy(yb_ref, da_ref[t], buf.at[slot, 0], t, sems.at[slot]),
                    _packed_row_copy(yb_ref, db_ref[t], buf.at[slot, 1], t, sems.at[slot])]
        _row_groups(tm, copies, start)

    def gathered(slot, a):
        return _unpack_words(_load_packed(buf.at[slot, a]))

    slot = i % 2

    @pl.when(i == 0)
    def _():
        gather_tile(d1_ref, d2_ref, 0, True)

    @pl.when(i + 1 < n)
    def _():
        gather_tile(d1_next_ref, d2_next_ref, 1 - slot, True)

    gather_tile(d1_ref, d2_ref, slot, False)
    wt = wt_ref[...]
    ffn = wt[:, 0:1] * gathered(slot, 0) + wt[:, 1:2] * gathered(slot, 1)
    o_ref[...] = _layer_norm(ALPHA * _load_tile_rows(ht_ref) + ffn, g_ref[...], b_ref[...])


def _combine(ht, yb, d1, d2, wts, ln_g, ln_b):
    t = d1.shape[0]
    d = SUBLANES * LANES
    tm = TM_COMBINE
    n = t // tm
    tok = lambda i: (i, 0)
    const = lambda i: (0, 0)
    cur = lambda: pl.BlockSpec((tm,), lambda i: (i,), memory_space=pltpu.SMEM)
    nxt = lambda: pl.BlockSpec((tm,), lambda i: (jnp.minimum(i + 1, n - 1),), memory_space=pltpu.SMEM)
    return pl.pallas_call(
        _combine_kernel,
        grid=(n,),
        in_specs=[cur(), cur(), nxt(), nxt(),
                  pl.BlockSpec((tm, 8), tok),
                  pl.BlockSpec((tm * SUBLANES, LANES), tok),
                  pl.BlockSpec(memory_space=pl.ANY),
                  pl.BlockSpec((1, d), const), pl.BlockSpec((1, d), const)],
        out_specs=pl.BlockSpec((tm, d), tok),
        out_shape=jax.ShapeDtypeStruct((t, d), F32),
        scratch_shapes=[pltpu.VMEM((2, 2, tm * PACK_ROWS, LANES), U32), pltpu.SemaphoreType.DMA((2,))],
        compiler_params=_cparams("arbitrary"),
        name="combine",
    )(d1, d2, d1, d2, wts, ht, yb, ln_g, ln_b)


def _token_mixing(x2d, mem2d, batch, seq, m_len, w_in, w_gate, b_gate, w_mem_kv, rpb, w_fo, w_na, w_mo, w_out,
                  ln_g, ln_b):
    k_mem, v_mem = _memkv(mem2d, w_mem_kv.astype(BF), m_len)
    u_f, k, q3, v3, mo = _proj(x2d, w_in, k_mem, v_mem, seq, m_len)
    fo = _fourier(u_f, batch, seq)
    na = _natten(q3, k, v3, rpb, batch, seq)
    return _merge(x2d, fo, na, mo, w_gate.astype(BF), b_gate.astype(F32)[None, :], w_fo.astype(BF),
                  w_na.astype(BF), w_mo.astype(BF), w_out.astype(BF), ln_g.astype(F32)[None, :],
                  ln_b.astype(F32)[None, :])


def _moe(ht, w_rg, b_rg, w_re, b_re, w_eg, w_eu, w_ed, ln_g, ln_b):
    t = ht.shape[0] // SUBLANES
    idx, wts, cnt = _router(ht, w_rg, b_rg, w_re, b_re)
    counts = cnt[:, 0].astype(I32)
    padded = (counts + MOE_BLOCK - 1) // MOE_BLOCK * MOE_BLOCK
    pad_end = jnp.cumsum(padded).astype(I32)
    pad_start = pad_end - padded
    n_pad = 2 * t + N_EXPERTS * MOE_BLOCK
    nb = n_pad // MOE_BLOCK
    blk_start = jnp.arange(nb, dtype=I32) * MOE_BLOCK
    blk_expert = jnp.minimum(jnp.sum((pad_end[None, :] <= blk_start[:, None]).astype(I32), axis=1), N_EXPERTS - 1)
    n_used = pad_end[-1:] // MOE_BLOCK
    ids = jnp.arange(N_EXPERTS, dtype=I32)
    later_used = (padded[None, :] > 0) & (ids[None, :] > ids[:, None])
    next_expert = jnp.min(jnp.where(later_used, ids[None, :], N_EXPERTS), axis=1)
    next_expert = jnp.where(next_expert == N_EXPERTS, -1, next_expert).astype(I32)
    d1, d2 = _dests(idx, pad_start)
    xg = _dispatch(ht, d1, d2, pad_start, pad_end, n_pad)
    yb = _experts(xg, blk_expert, next_expert, n_used, w_eg, w_eu, w_ed)
    return _combine(ht, yb, d1, d2, wts, ln_g.astype(F32)[None, :], ln_b.astype(F32)[None, :])


def kernel(x, mem, w_in, w_gate, b_gate, w_mem_kv, rpb, w_fourier_o, w_na_o, w_mem_o, w_out, ln1_g, ln1_b,
           w_router_group, b_router_group, w_router_expert, b_router_expert, w_exp_gate, w_exp_up,
           w_exp_down, ln2_g, ln2_b):
    batch, seq, d = x.shape
    m_len = mem.shape[1]
    x2d = x.reshape(batch * seq, d)
    mem2d = mem.reshape(batch * m_len, d)
    for l in range(w_in.shape[0]):
        ht = _token_mixing(x2d, mem2d, batch, seq, m_len, w_in[l], w_gate[l], b_gate[l], w_mem_kv[l], rpb[l],
                           w_fourier_o[l], w_na_o[l], w_mem_o[l], w_out[l], ln1_g[l], ln1_b[l])
        x2d = _moe(ht, w_router_group[l], b_router_group[l], w_router_expert[l], b_router_expert[l],
                   w_exp_gate[l], w_exp_up[l], w_exp_down[l], ln2_g[l], ln2_b[l])
    return x2d.reshape(batch, seq, d)
```

```python
import functools

import numpy as np
import jax
import jax.numpy as jnp
from jax import lax
from jax.experimental import pallas as pl
from jax.experimental.pallas import tpu as pltpu

BF = jnp.bfloat16
F32 = jnp.float32
I32 = jnp.int32
U32 = jnp.uint32

GRID_W = 64
MEM_HEADS = 4
MEM_HEAD_DIM = 128
MEM_DIM = MEM_HEADS * MEM_HEAD_DIM
FOURIER_GROUPS = 4
FOURIER_GROUP_DIM = 128
FOURIER_DIM = FOURIER_GROUPS * FOURIER_GROUP_DIM
NA_HEADS = 8
NA_HEAD_DIM = 64
NA_DIM = NA_HEADS * NA_HEAD_DIM
NA_ROW_WIN = 8
NA_COL_WIN = 16
N_GROUPS = 8
EXPERTS_PER_GROUP = 8
N_EXPERTS = N_GROUPS * EXPERTS_PER_GROUP
DEPTH = 1
ALPHA = (2.0 * DEPTH) ** 0.25
LN_EPS = 1e-5
NA_SCALE = NA_HEAD_DIM ** -0.5
LOG2_E = 1.4426950408889634
MEM_SCALE = MEM_HEAD_DIM ** -0.5
MASK_NEG = -1e30

LANES = 128
SUBLANES = 8
FFT_N2 = 128
TM_PROJ = 512
TM_MERGE = 512
TM_ROUTE = 512
TM_DISPATCH = 1024
TM_COMBINE = 512
NA_GROUP_ROWS = 4
NA_GROUP_WIN = NA_GROUP_ROWS + NA_ROW_WIN
NA_GROUPS_PER_STEP = 8
NA_SUM_ROWS = 16
F1_POS = 16
F2_SLABS = 8
MOE_BLOCK = 512
EXPERT_PARTS = 2
PACK_ROWS = 4
EXPERT_SUBBLOCKS = 2
ROUTE_ROWS = 128
ROUTE_STEPS_PER_PLANE = SUBLANES * LANES // TM_ROUTE
ROW_GROUP = 8
VMEM_LIMIT = 56 * 1024 * 1024


def _cparams(*sem):
    return pltpu.CompilerParams(dimension_semantics=sem, vmem_limit_bytes=VMEM_LIMIT)


def _dot(a, b):
    return jnp.dot(a, b, preferred_element_type=F32)


def _dot_nt(a, b):
    return lax.dot_general(a, b, (((1,), (1,)), ((), ())), preferred_element_type=F32)


def _layer_norm(y, g, b):
    mu = jnp.mean(y, axis=-1, keepdims=True)
    yc = y - mu
    var = jnp.mean(yc * yc, axis=-1, keepdims=True)
    return yc * lax.rsqrt(var + LN_EPS) * g + b


def _store_tile_rows(ref, val):
    n = val.shape[0]
    for s in range(SUBLANES):
        ref[pl.ds(s, n, stride=SUBLANES), :] = val[:, s * LANES:(s + 1) * LANES]


def _load_tile_rows(ref):
    n = ref.shape[0] // SUBLANES
    return jnp.concatenate([ref[pl.ds(s, n, stride=SUBLANES), :] for s in range(SUBLANES)], axis=1)


def _memkv_kernel(mem_ref, w_ref, k_ref, v_ref):
    kv = _dot(mem_ref[...].astype(BF), w_ref[...])
    k_ref[...] = kv[:, :MEM_DIM].astype(BF)
    v_ref[...] = kv[:, MEM_DIM:].astype(BF)


def _memkv(mem2d, w_kv, m_len):
    rows, d = mem2d.shape
    return pl.pallas_call(
        _memkv_kernel,
        grid=(rows // m_len,),
        in_specs=[pl.BlockSpec((m_len, d), lambda i: (i, 0)),
                  pl.BlockSpec((d, 2 * MEM_DIM), lambda i: (0, 0))],
        out_specs=[pl.BlockSpec((m_len, MEM_DIM), lambda i: (i, 0))] * 2,
        out_shape=[jax.ShapeDtypeStruct((rows, MEM_DIM), BF)] * 2,
        compiler_params=_cparams("parallel"),
        name="memkv",
    )(mem2d, w_kv)


def _proj_kernel(x_ref, w_ref, wt_ref, km_ref, vm_ref, uf_ref, k_ref, q3_ref, v3_ref, mo_ref):
    xb = x_ref[...].astype(BF)

    def seg(j):
        return _dot(xb, w_ref[:, j * 512:(j + 1) * 512])

    uf_ref[...] = seg(0)
    k_ref[...] = seg(1).astype(BF)
    q_t = _dot_nt(wt_ref[:NA_DIM, :], xb) * (NA_SCALE * LOG2_E)
    v_t = _dot_nt(wt_ref[NA_DIM:, :], xb)
    for s in range(q3_ref.shape[0]):
        q3_ref[s] = q_t[:, s * LANES:(s + 1) * LANES].astype(BF)
        v3_ref[s] = v_t[:, s * LANES:(s + 1) * LANES].astype(BF)
    qm = seg(2).astype(BF)
    for h in range(MEM_HEADS):
        sl = slice(h * MEM_HEAD_DIM, (h + 1) * MEM_HEAD_DIM)
        s = _dot_nt(qm[:, sl], km_ref[:, sl]) * MEM_SCALE
        m = jnp.max(s, axis=-1, keepdims=True)
        p = jnp.exp(s - m)
        l = jnp.sum(p, axis=-1, keepdims=True)
        o = _dot(p.astype(BF), vm_ref[:, sl])
        mo_ref[:, sl] = (o / l).astype(BF)


def _proj(x2d, w_in, k_mem, v_mem, seq, m_len):
    t, d = x2d.shape
    tm = TM_PROJ
    tiles_per_batch = seq // tm
    w_tok = jnp.concatenate([w_in[:, 0:512], w_in[:, 1024:1536], w_in[:, 2048:2560]], axis=1).astype(BF)
    w_chan = jnp.concatenate([w_in[:, 512:1024], w_in[:, 1536:2048]], axis=1).T.astype(BF)
    tok = lambda i: (i, 0)
    slab = lambda i: (i, 0, 0)
    memb = lambda i: (i // tiles_per_batch, 0)
    const = lambda i: (0, 0)
    tok_out = pl.BlockSpec((tm, 512), tok)
    slab_out = pl.BlockSpec((tm // LANES, NA_DIM, LANES), slab)
    slab_shape = jax.ShapeDtypeStruct((t // LANES, NA_DIM, LANES), BF)
    return pl.pallas_call(
        _proj_kernel,
        grid=(t // tm,),
        in_specs=[pl.BlockSpec((tm, d), tok),
                  pl.BlockSpec(w_tok.shape, const),
                  pl.BlockSpec(w_chan.shape, const),
                  pl.BlockSpec((m_len, MEM_DIM), memb),
                  pl.BlockSpec((m_len, MEM_DIM), memb)],
        out_specs=[tok_out, tok_out, slab_out, slab_out, tok_out],
        out_shape=[jax.ShapeDtypeStruct((t, 512), F32), jax.ShapeDtypeStruct((t, 512), BF)] + [slab_shape] * 2
                  + [jax.ShapeDtypeStruct((t, 512), BF)],
        compiler_params=_cparams("parallel"),
        name="proj",
    )(x2d, w_tok, w_chan, k_mem, v_mem)


def _fourier_tables(n1, n2):
    n = n1 * n2
    k1 = np.arange(n1)
    ang1 = 2.0 * np.pi * ((k1[:, None] * k1[None, :]) % n1) / n1
    norm = 1.0 / np.sqrt(float(n) * FOURIER_GROUP_DIM)
    m1 = np.concatenate([np.cos(ang1), -np.sin(ang1)], axis=0) * norm
    kk = k1[:, None, None] + n1 * np.arange(n2)[None, :, None]
    nn = np.arange(n2)[None, None, :]
    ang2 = 2.0 * np.pi * ((kk * nn) % n) / n
    c2, s2 = np.cos(ang2), np.sin(ang2)
    m2 = np.concatenate([np.concatenate([c2, s2], axis=2),
                         np.concatenate([-s2, c2], axis=2)], axis=1)
    c = np.arange(FOURIER_GROUP_DIM)
    angc = 2.0 * np.pi * ((c[:, None] * c[None, :]) % FOURIER_GROUP_DIM) / FOURIER_GROUP_DIM
    mc = np.concatenate([np.cos(angc), np.sin(angc)], axis=0)
    as_bf = lambda a: jnp.asarray(a.astype(np.float32)).astype(BF)
    return as_bf(m1), as_bf(m2), as_bf(mc)


def _f1_kernel(u_ref, m_ref, a_ref):
    n1, pos, _ = u_ref.shape
    x = jnp.concatenate([u_ref[:, p, :] for p in range(pos)], axis=1).astype(BF)
    r = _dot(m_ref[...], x)
    a_ref[0] = r[:n1].astype(BF)
    a_ref[1] = r[n1:].astype(BF)


def _f2_kernel(a_ref, m2_ref, mc_ref, o_ref):
    slabs = a_ref.shape[1]
    n2 = a_ref.shape[2]
    for t in range(slabs):
        a = jnp.concatenate([a_ref[0, t], a_ref[1, t]], axis=0)
        y = _dot(m2_ref[t], a)
        yr = y[:n2].astype(BF)
        yi = y[n2:].astype(BF)
        for g in range(FOURIER_GROUPS):
            sl = slice(g * LANES, (g + 1) * LANES)
            yy = jnp.concatenate([yr[:, sl], yi[:, sl]], axis=1)
            o_ref[:, t, sl] = _dot(yy, mc_ref[...])


def _fourier(u_f, batch, seq):
    n2 = FFT_N2
    n1 = seq // n2
    m1, m2, mc = _fourier_tables(n1, n2)
    cols = n2 * FOURIER_DIM
    pos = min(F1_POS, n2)
    a = pl.pallas_call(
        _f1_kernel,
        grid=(batch, n2 // pos),
        in_specs=[pl.BlockSpec((None, n1, pos, FOURIER_DIM), lambda b, j: (b, 0, j, 0)),
                  pl.BlockSpec((2 * n1, n1), lambda b, j: (0, 0))],
        out_specs=pl.BlockSpec((None, 2, n1, pos * FOURIER_DIM), lambda b, j: (b, 0, 0, j)),
        out_shape=jax.ShapeDtypeStruct((batch, 2, n1, cols), BF),
        compiler_params=_cparams("parallel", "parallel"),
        name="fourier1",
    )(u_f.reshape(batch, n1, n2, FOURIER_DIM), m1)
    kb = min(F2_SLABS, n1)
    out = pl.pallas_call(
        _f2_kernel,
        grid=(batch, n1 // kb),
        in_specs=[pl.BlockSpec((None, 2, kb, n2, FOURIER_DIM), lambda b, j: (b, 0, j, 0, 0)),
                  pl.BlockSpec((kb, 2 * n2, 2 * n2), lambda b, j: (j, 0, 0)),
                  pl.BlockSpec((2 * FOURIER_GROUP_DIM, FOURIER_GROUP_DIM), lambda b, j: (0, 0))],
        out_specs=pl.BlockSpec((None, n2, kb, FOURIER_DIM), lambda b, j: (b, 0, j, 0)),
        out_shape=jax.ShapeDtypeStruct((batch, n2, n1, FOURIER_DIM), F32),
        compiler_params=_cparams("parallel", "parallel"),
        name="fourier2",
    )(a.reshape(batch, 2, n1, n2, FOURIER_DIM), m2, mc)
    return out.reshape(batch * seq, FOURIER_DIM)


def _na_group_start(r0, n_rows):
    rs0 = min(max(r0 - NA_ROW_WIN // 2, 0), n_rows - NA_ROW_WIN)
    return min(rs0, n_rows - NA_GROUP_WIN)


def _na_bias_table(rpb, n_rows):
    g = NA_GROUP_ROWS
    assert n_rows >= 2 * NA_GROUP_WIN and n_rows % g == 0 and (NA_ROW_WIN // 2) % g == 0
    cols = np.arange(GRID_W)
    cs = np.clip(cols - NA_COL_WIN // 2, 0, GRID_W - NA_COL_WIN)
    kc = np.arange(GRID_W)
    in_win = (kc[None, :] >= cs[:, None]) & (kc[None, :] < cs[:, None] + NA_COL_WIN)
    dc = kc[None, :] - cols[:, None] + (NA_COL_WIN - 1)
    n_dc = 2 * NA_COL_WIN - 1
    pick = ((dc.reshape(-1)[None, :] == np.arange(n_dc)[:, None]) & in_win.reshape(-1)[None, :])
    picked = jnp.dot(rpb.astype(F32).reshape(-1, n_dc), jnp.asarray(pick.astype(np.float32)),
                     precision=lax.Precision.HIGHEST)
    picked = picked.reshape(NA_HEADS, 2 * NA_ROW_WIN - 1, GRID_W, GRID_W)
    full_t = jnp.where(in_win[None, None], picked * LOG2_E, MASK_NEG).transpose(0, 1, 3, 2)
    masked = jnp.full((NA_HEADS, GRID_W, GRID_W), MASK_NEG, F32)
    half = NA_ROW_WIN // 2
    group_rows = list(range(0, half, g)) + [half] + list(range(n_rows - half, n_rows, g))
    tabs = []
    for r0 in group_rows:
        start = _na_group_start(r0, n_rows)
        key_rows = []
        for kr in range(NA_GROUP_WIN):
            blocks = []
            for rr in range(g):
                r = r0 + rr
                rs = min(max(r - half, 0), n_rows - NA_ROW_WIN)
                in_rows = rs <= start + kr < rs + NA_ROW_WIN
                blocks.append(full_t[:, start + kr - r + NA_ROW_WIN - 1] if in_rows else masked)
            key_rows.append(jnp.concatenate(blocks, axis=2))
        tabs.append(jnp.concatenate(key_rows, axis=1))
    return jnp.stack(tabs, axis=0)


def _na_kernel(q3_ref, k_ref, v3_ref, tab_ref, o_ref, *, n_rows):
    i = pl.program_id(2)
    g = NA_GROUP_ROWS
    slabs_per_group = g * GRID_W // LANES
    groups = q3_ref.shape[0] // slabs_per_group
    win = NA_GROUP_WIN * GRID_W
    tokens = g * GRID_W
    half = NA_ROW_WIN // 2
    n_top = half // g
    chan = lax.broadcasted_iota(I32, (2 * NA_HEAD_DIM, tokens), 0)
    first_head = chan < NA_HEAD_DIM

    def scores(jj):
        r0 = (i * groups + jj) * g
        start = jnp.minimum(jnp.clip(r0 - half, 0, n_rows - NA_ROW_WIN), n_rows - NA_GROUP_WIN)
        cfg = jnp.where(r0 < half, r0 // g,
                        jnp.where(r0 >= n_rows - half, n_top + 1 + (r0 - (n_rows - half)) // g, n_top))
        kw = k_ref[pl.ds(pl.multiple_of(start * GRID_W, LANES), win), :]
        qt = jnp.concatenate([q3_ref[jj * slabs_per_group + t] for t in range(slabs_per_group)], axis=1)
        zero = jnp.zeros_like(qt)
        q_bd = jnp.concatenate([jnp.where(first_head, qt, zero), jnp.where(first_head, zero, qt)], axis=1)
        s = _dot(kw, q_bd) + jnp.concatenate([tab_ref[cfg, 0], tab_ref[cfg, 1]], axis=1)
        return s, start

    ones_rows = jnp.ones((NA_SUM_ROWS, win), BF)

    def outputs(jj, p, start):
        slab = start // 2
        vw = jnp.concatenate([v3_ref[slab + t] for t in range(NA_GROUP_WIN // 2)] , axis=1)
        o_l = _dot(jnp.concatenate([vw, ones_rows], axis=0), p)
        o_t = o_l[:2 * NA_HEAD_DIM] / o_l[2 * NA_HEAD_DIM:2 * NA_HEAD_DIM + 1]
        o = jnp.where(first_head, o_t[:, :tokens], o_t[:, tokens:])
        o_ref[jj * tokens:(jj + 1) * tokens, :] = jnp.transpose(o).astype(BF)

    pending = None
    for jj in range(groups):
        s, start = scores(jj)
        if pending is not None:
            outputs(*pending)
        p = jnp.exp2(s - jnp.max(s, axis=0, keepdims=True))
        pending = (jj, p.astype(BF), start)
    outputs(*pending)


def _natten(q3, k, v3, rpb, batch, seq):
    assert 2 * GRID_W == LANES and 2 * NA_HEAD_DIM == LANES and NA_GROUP_ROWS % 2 == 0
    n_rows = seq // GRID_W
    slabs = n_rows // 2
    step_slabs = min(NA_GROUPS_PER_STEP * NA_GROUP_ROWS // 2, slabs)
    steps = slabs // step_slabs
    hp = NA_HEADS // 2
    tab = _na_bias_table(rpb, n_rows)
    return pl.pallas_call(
        functools.partial(_na_kernel, n_rows=n_rows),
        grid=(batch, hp, steps),
        in_specs=[pl.BlockSpec((step_slabs, LANES, LANES), lambda b, p, i: (b * steps + i, p, 0)),
                  pl.BlockSpec((seq, LANES), lambda b, p, i: (b, p)),
                  pl.BlockSpec((slabs, LANES, LANES), lambda b, p, i: (b, p, 0)),
                  pl.BlockSpec((tab.shape[0], 2) + tab.shape[2:], lambda b, p, i: (0, p, 0, 0))],
        out_specs=pl.BlockSpec((step_slabs * LANES, LANES), lambda b, p, i: (b * steps + i, p)),
        out_shape=jax.ShapeDtypeStruct((batch * seq, NA_DIM), BF),
        compiler_params=_cparams("parallel", "parallel", "parallel"),
        name="natten",
    )(q3, k, v3, tab)


def _merge_kernel(x_ref, fo_ref, na_ref, mo_ref, wg_ref, bg_ref, wf_ref, wn_ref, wm_ref, wo_ref,
                  g_ref, b_ref, ht_ref):
    d = x_ref.shape[1]
    x = x_ref[...]
    xb = x.astype(BF)
    merged = None
    for j, (br_ref, w_ref) in enumerate(((fo_ref, wf_ref), (na_ref, wn_ref), (mo_ref, wm_ref))):
        z = _dot(xb, wg_ref[:, j * d:(j + 1) * d]) + bg_ref[:, j * d:(j + 1) * d]
        gate = 1.0 / (1.0 + jnp.exp(-z))
        term = gate * _dot(br_ref[...].astype(BF), w_ref[...])
        merged = term if merged is None else merged + term
    mix = _dot(merged.astype(BF), wo_ref[...])
    _store_tile_rows(ht_ref, _layer_norm(ALPHA * x + mix, g_ref[...], b_ref[...]))


def _merge(x2d, fo, na, mo, w_gate, b_gate, w_fo, w_na, w_mo, w_out, ln_g, ln_b):
    t, d = x2d.shape
    assert d == SUBLANES * LANES
    tm = TM_MERGE
    tok = lambda i: (i, 0)
    full = lambda a: pl.BlockSpec(a.shape, lambda i: (0, 0))
    return pl.pallas_call(
        _merge_kernel,
        grid=(t // tm,),
        in_specs=[pl.BlockSpec((tm, d), tok)] + [pl.BlockSpec((tm, 512), tok)] * 3
                 + [full(a) for a in (w_gate, b_gate, w_fo, w_na, w_mo, w_out, ln_g, ln_b)],
        out_specs=pl.BlockSpec((tm * SUBLANES, LANES), tok),
        out_shape=jax.ShapeDtypeStruct((t * SUBLANES, LANES), F32),
        compiler_params=_cparams("parallel"),
        name="merge",
    )(x2d, fo, na, mo, w_gate, b_gate, w_fo, w_na, w_mo, w_out, ln_g, ln_b)


def _router_kernel(ht_ref, whi_ref, wlo_ref, b_ref, oi_ref, of_ref, cnt_ref, carry_ref):
    tm = ht_ref.shape[0] // SUBLANES

    @pl.when(pl.program_id(0) == 0)
    def _():
        carry_ref[...] = jnp.zeros_like(carry_ref)

    h = _load_tile_rows(ht_ref)
    hh = h.astype(BF)
    hl = (h - hh.astype(F32)).astype(BF)
    logits = (_dot_nt(whi_ref[...], hh) + _dot_nt(wlo_ref[...], hh) + _dot_nt(whi_ref[...], hl)
              + b_ref[...])
    gl = logits[:N_GROUPS, :]
    g_row = lax.broadcasted_iota(I32, (N_GROUPS, tm), 0)
    gmax = jnp.max(gl, axis=0, keepdims=True)
    g_idx = jnp.min(jnp.where(gl == gmax, g_row, N_GROUPS), axis=0, keepdims=True)
    p_group = 1.0 / jnp.sum(jnp.exp(gl - gmax), axis=0, keepdims=True)
    e_row = lax.broadcasted_iota(I32, (N_EXPERTS, tm), 0)
    el = jnp.where((e_row >> 3) == g_idx, logits[N_GROUPS:N_GROUPS + N_EXPERTS, :], MASK_NEG)
    v1 = jnp.max(el, axis=0, keepdims=True)
    i1 = jnp.min(jnp.where(el == v1, e_row, N_EXPERTS), axis=0, keepdims=True)
    el2 = jnp.where(e_row == i1, MASK_NEG, el)
    v2 = jnp.max(el2, axis=0, keepdims=True)
    i2 = jnp.min(jnp.where(el2 == v2, e_row, N_EXPERTS), axis=0, keepdims=True)
    tt = jnp.exp(v2 - v1)
    w1 = p_group / (1.0 + tt)
    w2 = p_group * tt / (1.0 + tt)
    sel1 = e_row == i1
    sel2 = e_row == i2
    onehot = jnp.where(sel1 | sel2, 1.0, 0.0)
    t_in = lax.broadcasted_iota(I32, (tm, tm), 0)
    t_out = lax.broadcasted_iota(I32, (tm, tm), 1)
    tri = jnp.where(t_in < t_out, 1.0, 0.0).astype(BF)
    prefix = _dot(onehot.astype(BF), tri) + carry_ref[...]
    r1 = jnp.sum(jnp.where(sel1, prefix, 0.0), axis=0, keepdims=True)
    r2 = jnp.sum(jnp.where(sel2, prefix, 0.0), axis=0, keepdims=True)
    carry_ref[...] += jnp.sum(onehot, axis=1, keepdims=True)
    cnt_ref[...] = carry_ref[...]
    sub = pl.program_id(0) % ROUTE_STEPS_PER_PLANE
    for f, vals in enumerate((i1, i2, r1.astype(I32), r2.astype(I32))):
        for c in range(tm // LANES):
            oi_ref[f, pl.ds(sub * (tm // LANES) + c, 1), :] = vals[:, c * LANES:(c + 1) * LANES]
    w_row = lax.broadcasted_iota(I32, (LANES, tm), 0)
    w_rows = jnp.where(w_row == 0, w1, jnp.where(w_row == 1, w2, 0.0))
    of_ref[...] = jnp.transpose(w_rows)[:, :8]


def _router(ht, w_rg, b_rg, w_re, b_re):
    t = ht.shape[0] // SUBLANES
    d = w_rg.shape[0]
    tm = TM_ROUTE
    pad = ROUTE_ROWS - N_GROUPS - N_EXPERTS
    w = jnp.concatenate([w_rg.astype(F32), w_re.astype(F32), jnp.zeros((d, pad), F32)], axis=1).T
    b = jnp.concatenate([b_rg.astype(F32), b_re.astype(F32), jnp.zeros((pad,), F32)])[:, None]
    w_hi = w.astype(BF)
    w_lo = (w - w_hi.astype(F32)).astype(BF)
    tok = lambda i: (i, 0)
    const = lambda i: (0, 0)
    return pl.pallas_call(
        _router_kernel,
        grid=(t // tm,),
        in_specs=[pl.BlockSpec((tm * SUBLANES, LANES), tok),
                  pl.BlockSpec((ROUTE_ROWS, d), const),
                  pl.BlockSpec((ROUTE_ROWS, d), const),
                  pl.BlockSpec((ROUTE_ROWS, 1), const)],
        out_specs=[pl.BlockSpec((4, SUBLANES, LANES), lambda i: (0, i // ROUTE_STEPS_PER_PLANE, 0)),
                   pl.BlockSpec((tm, 8), tok),
                   pl.BlockSpec((N_EXPERTS, 1), const)],
        out_shape=[jax.ShapeDtypeStruct((4, t // LANES, LANES), I32), jax.ShapeDtypeStruct((t, 8), F32),
                   jax.ShapeDtypeStruct((N_EXPERTS, 1), F32)],
        scratch_shapes=[pltpu.VMEM((N_EXPERTS, 1), F32)],
        compiler_params=_cparams("arbitrary"),
        name="router",
    )(ht, w_hi, w_lo, b)


def _row_groups(n_rows, copies_of_row, start):
    def group(g, c):
        copies = []
        for u in range(ROW_GROUP):
            copies.extend(copies_of_row(g * ROW_GROUP + u))
        for cp in copies:
            if start:
                cp.start()
            else:
                cp.wait()
        return c

    lax.fori_loop(0, n_rows // ROW_GROUP, group, 0)


def _dests_kernel(ps_ref, idx_ref, d_ref):
    for a in range(2):
        e = idx_ref[a]
        d = idx_ref[2 + a]
        for k in range(N_EXPERTS):
            d = d + jnp.where(e == k, ps_ref[k], 0)
        d_ref[a] = d


def _dests(idx, pad_start):
    planes = idx.shape[1]
    out = pl.pallas_call(
        _dests_kernel,
        grid_spec=pltpu.PrefetchScalarGridSpec(
            num_scalar_prefetch=1,
            grid=(1,),
            in_specs=[pl.BlockSpec(idx.shape, lambda i, ps: (0, 0, 0))],
            out_specs=pl.BlockSpec((2, planes, LANES), lambda i, ps: (0, 0, 0)),
        ),
        out_shape=jax.ShapeDtypeStruct((2, planes, LANES), I32),
        compiler_params=_cparams("arbitrary"),
        name="dests",
    )(pad_start, idx)
    return out[0].reshape(-1), out[1].reshape(-1)


def _pack_words(val):
    words = []
    for c in range(PACK_ROWS):
        hi = pltpu.bitcast(val[:, c * LANES:(c + 1) * LANES].astype(BF).astype(F32), U32)
        lo = pltpu.bitcast(val[:, (c + PACK_ROWS) * LANES:(c + PACK_ROWS + 1) * LANES].astype(BF).astype(F32), U32)
        words.append(hi | (lo >> 16))
    return words


def _unpack_words(words):
    his = [pltpu.bitcast(w & jnp.uint32(0xFFFF0000), F32) for w in words]
    los = [pltpu.bitcast(w << 16, F32) for w in words]
    return jnp.concatenate(his + los, axis=1)


def _load_packed(ref):
    n = ref.shape[0] // PACK_ROWS
    return [ref[pl.ds(c, n, stride=PACK_ROWS), :] for c in range(PACK_ROWS)]


def _store_packed(ref, words):
    n = ref.shape[0] // PACK_ROWS
    for c, w in enumerate(words):
        ref[pl.ds(c, n, stride=PACK_ROWS), :] = w


def _packed_row_copy(src_ref, src_row, dst_ref, dst_row, sem):
    src = pl.multiple_of(src_row * PACK_ROWS, PACK_ROWS)
    dst = pl.multiple_of(dst_row * PACK_ROWS, PACK_ROWS)
    return pltpu.make_async_copy(src_ref.at[pl.ds(src, PACK_ROWS)], dst_ref.at[pl.ds(dst, PACK_ROWS)], sem)


def _dispatch_kernel(ps_ref, pe_ref, d1_ref, d2_ref, ht_ref, xg_ref, pk, zbuf, sem, zsem):
    i = pl.program_id(0)
    tm = d1_ref.shape[0]
    blk = MOE_BLOCK * PACK_ROWS
    n_blocks = xg_ref.shape[0] // blk

    @pl.when(i == 0)
    def _():
        zbuf[...] = jnp.zeros_like(zbuf)

        def zero_copy(block):
            return pltpu.make_async_copy(zbuf, xg_ref.at[pl.ds(pl.multiple_of(block * blk, blk), blk)], zsem)

        def seg_issue(e, c):
            @pl.when(pe_ref[e] > ps_ref[e])
            def _():
                zero_copy(pe_ref[e] // MOE_BLOCK - 1).start()
            return c

        def seg_drain(e, c):
            @pl.when(pe_ref[e] > ps_ref[e])
            def _():
                zero_copy(pe_ref[e] // MOE_BLOCK - 1).wait()
            return c

        def tail_issue(b, c):
            zero_copy(b).start()
            return c

        def tail_drain(b, c):
            zero_copy(b).wait()
            return c

        first_tail = pe_ref[N_EXPERTS - 1] // MOE_BLOCK
        lax.fori_loop(0, N_EXPERTS, seg_issue, 0)
        lax.fori_loop(first_tail, n_blocks, tail_issue, 0)
        lax.fori_loop(0, N_EXPERTS, seg_drain, 0)
        lax.fori_loop(first_tail, n_blocks, tail_drain, 0)

    _store_packed(pk, _pack_words(_load_tile_rows(ht_ref)))
    _row_groups(tm, lambda t: [_packed_row_copy(pk, t, xg_ref, d1_ref[t], sem),
                               _packed_row_copy(pk, t, xg_ref, d2_ref[t], sem)], True)
    _row_groups(tm, lambda t: [_packed_row_copy(pk, 0, xg_ref, 0, sem)] * 2, False)


def _dispatch(ht, d1, d2, pad_start, pad_end, n_pad):
    t = d1.shape[0]
    tm = TM_DISPATCH
    smem = lambda: pl.BlockSpec((tm,), lambda i, ps, pe: (i,), memory_space=pltpu.SMEM)
    return pl.pallas_call(
        _dispatch_kernel,
        grid_spec=pltpu.PrefetchScalarGridSpec(
            num_scalar_prefetch=2,
            grid=(t // tm,),
            in_specs=[smem(), smem(), pl.BlockSpec((tm * SUBLANES, LANES), lambda i, ps, pe: (i, 0))],
            out_specs=pl.BlockSpec(memory_space=pl.ANY),
            scratch_shapes=[pltpu.VMEM((tm * PACK_ROWS, LANES), U32), pltpu.VMEM((MOE_BLOCK * PACK_ROWS, LANES), U32),
                            pltpu.SemaphoreType.DMA(()), pltpu.SemaphoreType.DMA(())],
        ),
        out_shape=jax.ShapeDtypeStruct((n_pad * PACK_ROWS, LANES), U32),
        compiler_params=_cparams("arbitrary"),
        name="dispatch",
    )(pad_start, pad_end, d1, d2, ht)


def _expert_kernel(be_ref, nx_ref, nv_ref, nu_ref, x_ref, wg_hbm, wu_hbm, wd_hbm, y_ref,
                   wg_f, wu_f, wd_f, wg_s, wu_s, wd_s, sems):
    j = pl.program_id(0)
    stage = ((wg_hbm, wg_f, wg_s), (wu_hbm, wu_f, wu_s), (wd_hbm, wd_f, wd_s))

    def weight_copies(e):
        return [pltpu.make_async_copy(hbm.at[e], buf, sems.at[k]) for k, (hbm, buf, _) in enumerate(stage)]

    @pl.when(j < nu_ref[0])
    def _():
        e = be_ref[j]

        @pl.when(j == 0)
        def _():
            for cp in weight_copies(e):
                cp.start()

        @pl.when((j == 0) | (e != be_ref[jnp.maximum(j - 1, 0)]))
        def _():
            for cp, (_, buf, dst) in zip(weight_copies(e), stage):
                cp.wait()
                dst[...] = buf[...].astype(BF)
            nxt = nx_ref[e]

            @pl.when(nxt >= 0)
            def _():
                for cp in weight_copies(nxt):
                    cp.start()

        part_rows = x_ref.shape[0] // EXPERT_PARTS
        rows = part_rows // EXPERT_SUBBLOCKS

        def part(x_part, y_part):
            def up(h):
                xb = _unpack_words(_load_packed(x_part.at[pl.ds(h * rows, rows)])).astype(BF)
                a = _dot(xb, wg_s[...])
                u = _dot(xb, wu_s[...])
                return ((a / (1.0 + jnp.exp(-a))) * u).astype(BF)

            mids = [up(h) for h in range(EXPERT_SUBBLOCKS)]
            for h, mid in enumerate(mids):
                _store_packed(y_part.at[pl.ds(h * rows, rows)], _pack_words(_dot(mid, wd_s[...])))

        for q in range(EXPERT_PARTS):
            x_part = x_ref.at[pl.ds(q * part_rows, part_rows)]
            y_part = y_ref.at[pl.ds(q * part_rows, part_rows)]
            has_tokens = nv_ref[j] > q * (MOE_BLOCK // EXPERT_PARTS)
            pl.when(has_tokens)(functools.partial(part, x_part, y_part))

            @pl.when(jnp.logical_not(has_tokens))
            def _():
                y_part[...] = jnp.zeros_like(y_part)

    @pl.when(j >= nu_ref[0])
    def _():
        y_ref[...] = jnp.zeros_like(y_ref)


def _experts(xg, blk_expert, next_expert, blk_valid, n_used, w_eg, w_eu, w_ed):
    blk = MOE_BLOCK * PACK_ROWS
    nb = xg.shape[0] // blk
    d, de = w_eg.shape[1], w_eg.shape[2]
    rows = lambda j, be, nx, nv, nu: (jnp.minimum(j, nu[0] - 1), 0)
    hbm = pl.BlockSpec(memory_space=pl.ANY)
    return pl.pallas_call(
        _expert_kernel,
        grid_spec=pltpu.PrefetchScalarGridSpec(
            num_scalar_prefetch=4,
            grid=(nb,),
            in_specs=[pl.BlockSpec((blk, LANES), rows), hbm, hbm, hbm],
            out_specs=pl.BlockSpec((blk, LANES), lambda j, be, nx, nv, nu: (j, 0)),
            scratch_shapes=[pltpu.VMEM((d, de), F32), pltpu.VMEM((d, de), F32), pltpu.VMEM((de, d), F32),
                            pltpu.VMEM((d, de), BF), pltpu.VMEM((d, de), BF), pltpu.VMEM((de, d), BF),
                            pltpu.SemaphoreType.DMA((3,))],
        ),
        out_shape=jax.ShapeDtypeStruct(xg.shape, U32),
        compiler_params=_cparams("arbitrary"),
        name="experts",
    )(blk_expert, next_expert, blk_valid, n_used, xg, w_eg, w_eu, w_ed)


def _combine_kernel(d1_ref, d2_ref, d1_next_ref, d2_next_ref, wt_ref, ht_ref, yb_ref, g_ref, b_ref, o_ref,
                    buf, sems):
    i = pl.program_id(0)
    n = pl.num_programs(0)
    tm = d1_ref.shape[0]

    def gather_tile(da_ref, db_ref, slot, start):
        def copies(t):
            if not start:
                return [_packed_row_copy(yb_ref, 0, buf.at[slot, a], 0, sems.at[slot]) for a in range(2)]
            return [_packed_row_copy(yb_ref, da_ref[t], buf.at[slot, 0], t, sems.at[slot]),
                    _packed_row_copy(yb_ref, db_ref[t], buf.at[slot, 1], t, sems.at[slot])]
        _row_groups(tm, copies, start)

    def gathered(slot, a):
        return _unpack_words(_load_packed(buf.at[slot, a]))

    slot = i % 2

    @pl.when(i == 0)
    def _():
        gather_tile(d1_ref, d2_ref, 0, True)

    @pl.when(i + 1 < n)
    def _():
        gather_tile(d1_next_ref, d2_next_ref, 1 - slot, True)

    gather_tile(d1_ref, d2_ref, slot, False)
    wt = wt_ref[...]
    ffn = wt[:, 0:1] * gathered(slot, 0) + wt[:, 1:2] * gathered(slot, 1)
    o_ref[...] = _layer_norm(ALPHA * _load_tile_rows(ht_ref) + ffn, g_ref[...], b_ref[...])


def _combine(ht, yb, d1, d2, wts, ln_g, ln_b):
    t = d1.shape[0]
    d = SUBLANES * LANES
    tm = TM_COMBINE
    n = t // tm
    tok = lambda i: (i, 0)
    const = lambda i: (0, 0)
    cur = lambda: pl.BlockSpec((tm,), lambda i: (i,), memory_space=pltpu.SMEM)
    nxt = lambda: pl.BlockSpec((tm,), lambda i: (jnp.minimum(i + 1, n - 1),), memory_space=pltpu.SMEM)
    return pl.pallas_call(
        _combine_kernel,
        grid=(n,),
        in_specs=[cur(), cur(), nxt(), nxt(),
                  pl.BlockSpec((tm, 8), tok),
                  pl.BlockSpec((tm * SUBLANES, LANES), tok),
                  pl.BlockSpec(memory_space=pl.ANY),
                  pl.BlockSpec((1, d), const), pl.BlockSpec((1, d), const)],
        out_specs=pl.BlockSpec((tm, d), tok),
        out_shape=jax.ShapeDtypeStruct((t, d), F32),
        scratch_shapes=[pltpu.VMEM((2, 2, tm * PACK_ROWS, LANES), U32), pltpu.SemaphoreType.DMA((2,))],
        compiler_params=_cparams("arbitrary"),
        name="combine",
    )(d1, d2, d1, d2, wts, ht, yb, ln_g, ln_b)


def _token_mixing(x2d, mem2d, batch, seq, m_len, w_in, w_gate, b_gate, w_mem_kv, rpb, w_fo, w_na, w_mo, w_out,
                  ln_g, ln_b):
    k_mem, v_mem = _memkv(mem2d, w_mem_kv.astype(BF), m_len)
    u_f, k, q3, v3, mo = _proj(x2d, w_in, k_mem, v_mem, seq, m_len)
    fo = _fourier(u_f, batch, seq)
    na = _natten(q3, k, v3, rpb, batch, seq)
    return _merge(x2d, fo, na, mo, w_gate.astype(BF), b_gate.astype(F32)[None, :], w_fo.astype(BF),
                  w_na.astype(BF), w_mo.astype(BF), w_out.astype(BF), ln_g.astype(F32)[None, :],
                  ln_b.astype(F32)[None, :])


def _moe(ht, w_rg, b_rg, w_re, b_re, w_eg, w_eu, w_ed, ln_g, ln_b):
    t = ht.shape[0] // SUBLANES
    idx, wts, cnt = _router(ht, w_rg, b_rg, w_re, b_re)
    counts = cnt[:, 0].astype(I32)
    padded = (counts + MOE_BLOCK - 1) // MOE_BLOCK * MOE_BLOCK
    pad_end = jnp.cumsum(padded).astype(I32)
    pad_start = pad_end - padded
    n_pad = 2 * t + N_EXPERTS * MOE_BLOCK
    nb = n_pad // MOE_BLOCK
    blk_start = jnp.arange(nb, dtype=I32) * MOE_BLOCK
    blk_expert = jnp.minimum(jnp.sum((pad_end[None, :] <= blk_start[:, None]).astype(I32), axis=1), N_EXPERTS - 1)
    n_used = pad_end[-1:] // MOE_BLOCK
    seg_left = counts[None, :] - (blk_start[:, None] - pad_start[None, :])
    own = blk_expert[:, None] == jnp.arange(N_EXPERTS, dtype=I32)[None, :]
    blk_valid = jnp.clip(jnp.sum(jnp.where(own, seg_left, 0), axis=1), 0, MOE_BLOCK).astype(I32)
    ids = jnp.arange(N_EXPERTS, dtype=I32)
    later_used = (padded[None, :] > 0) & (ids[None, :] > ids[:, None])
    next_expert = jnp.min(jnp.where(later_used, ids[None, :], N_EXPERTS), axis=1)
    next_expert = jnp.where(next_expert == N_EXPERTS, -1, next_expert).astype(I32)
    d1, d2 = _dests(idx, pad_start)
    xg = _dispatch(ht, d1, d2, pad_start, pad_end, n_pad)
    yb = _experts(xg, blk_expert, next_expert, blk_valid, n_used, w_eg, w_eu, w_ed)
    return _combine(ht, yb, d1, d2, wts, ln_g.astype(F32)[None, :], ln_b.astype(F32)[None, :])


def kernel(x, mem, w_in, w_gate, b_gate, w_mem_kv, rpb, w_fourier_o, w_na_o, w_mem_o, w_out, ln1_g, ln1_b,
           w_router_group, b_router_group, w_router_expert, b_router_expert, w_exp_gate, w_exp_up,
           w_exp_down, ln2_g, ln2_b):
    batch, seq, d = x.shape
    m_len = mem.shape[1]
    x2d = x.reshape(batch * seq, d)
    mem2d = mem.reshape(batch * m_len, d)
    for l in range(w_in.shape[0]):
        ht = _token_mixing(x2d, mem2d, batch, seq, m_len, w_in[l], w_gate[l], b_gate[l], w_mem_kv[l], rpb[l],
                           w_fourier_o[l], w_na_o[l], w_mem_o[l], w_out[l], ln1_g[l], ln1_b[l])
        x2d = _moe(ht, w_router_group[l], b_router_group[l], w_router_expert[l], b_router_expert[l],
                   w_exp_gate[l], w_exp_up[l], w_exp_down[l], ln2_g[l], ln2_b[l])
    return x2d.reshape(batch, seq, d)
```

```python
import functools

import numpy as np
import jax
import jax.numpy as jnp
from jax import lax
from jax.experimental import pallas as pl
from jax.experimental.pallas import tpu as pltpu

BF = jnp.bfloat16
F32 = jnp.float32
I32 = jnp.int32
U32 = jnp.uint32

GRID_W = 64
MEM_HEADS = 4
MEM_HEAD_DIM = 128
MEM_DIM = MEM_HEADS * MEM_HEAD_DIM
FOURIER_GROUPS = 4
FOURIER_GROUP_DIM = 128
FOURIER_DIM = FOURIER_GROUPS * FOURIER_GROUP_DIM
NA_HEADS = 8
NA_HEAD_DIM = 64
NA_DIM = NA_HEADS * NA_HEAD_DIM
NA_ROW_WIN = 8
NA_COL_WIN = 16
N_GROUPS = 8
EXPERTS_PER_GROUP = 8
N_EXPERTS = N_GROUPS * EXPERTS_PER_GROUP
DEPTH = 1
ALPHA = (2.0 * DEPTH) ** 0.25
LN_EPS = 1e-5
NA_SCALE = NA_HEAD_DIM ** -0.5
LOG2_E = 1.4426950408889634
MEM_SCALE = MEM_HEAD_DIM ** -0.5
MASK_NEG = -1e30

LANES = 128
SUBLANES = 8
FFT_N2 = 128
TM_PROJ = 512
TM_MERGE = 512
TM_ROUTE = 512
TM_DISPATCH = 1024
TM_COMBINE = 256
NA_GROUP_ROWS = 4
NA_GROUP_WIN = NA_GROUP_ROWS + NA_ROW_WIN
NA_GROUPS_PER_STEP = 8
NA_SUM_ROWS = 16
F1_POS = 16
F2_SLABS = 8
MOE_BLOCK = 512
EXPERT_PARTS = 2
PACK_ROWS = 4
EXPERT_SUBBLOCKS = 2
ROUTE_ROWS = 128
ROUTE_STEPS_PER_PLANE = SUBLANES * LANES // TM_ROUTE
DMA_QUEUES = 2
ROW_GROUP = 8
VMEM_LIMIT = 56 * 1024 * 1024


def _cparams(*sem):
    return pltpu.CompilerParams(dimension_semantics=sem, vmem_limit_bytes=VMEM_LIMIT)


def _dot(a, b):
    return jnp.dot(a, b, preferred_element_type=F32)


def _dot_nt(a, b):
    return lax.dot_general(a, b, (((1,), (1,)), ((), ())), preferred_element_type=F32)


def _layer_norm(y, g, b):
    mu = jnp.mean(y, axis=-1, keepdims=True)
    yc = y - mu
    var = jnp.mean(yc * yc, axis=-1, keepdims=True)
    return yc * lax.rsqrt(var + LN_EPS) * g + b


def _store_tile_rows(ref, val):
    n = val.shape[0]
    for s in range(SUBLANES):
        ref[pl.ds(s, n, stride=SUBLANES), :] = val[:, s * LANES:(s + 1) * LANES]


def _load_tile_rows(ref):
    n = ref.shape[0] // SUBLANES
    return jnp.concatenate([ref[pl.ds(s, n, stride=SUBLANES), :] for s in range(SUBLANES)], axis=1)


def _memkv_kernel(mem_ref, w_ref, k_ref, v_ref):
    kv = _dot(mem_ref[...].astype(BF), w_ref[...])
    k_ref[...] = kv[:, :MEM_DIM].astype(BF)
    v_ref[...] = kv[:, MEM_DIM:].astype(BF)


def _memkv(mem2d, w_kv, m_len):
    rows, d = mem2d.shape
    return pl.pallas_call(
        _memkv_kernel,
        grid=(rows // m_len,),
        in_specs=[pl.BlockSpec((m_len, d), lambda i: (i, 0)),
                  pl.BlockSpec((d, 2 * MEM_DIM), lambda i: (0, 0))],
        out_specs=[pl.BlockSpec((m_len, MEM_DIM), lambda i: (i, 0))] * 2,
        out_shape=[jax.ShapeDtypeStruct((rows, MEM_DIM), BF)] * 2,
        compiler_params=_cparams("parallel"),
        name="memkv",
    )(mem2d, w_kv)


def _proj_kernel(x_ref, w_ref, wt_ref, km_ref, vm_ref, uf_ref, k_ref, q3_ref, v3_ref, mo_ref):
    xb = x_ref[...].astype(BF)

    def seg(j):
        return _dot(xb, w_ref[:, j * 512:(j + 1) * 512])

    uf_ref[...] = seg(0)
    k_ref[...] = seg(1).astype(BF)
    q_t = _dot_nt(wt_ref[:NA_DIM, :], xb) * (NA_SCALE * LOG2_E)
    v_t = _dot_nt(wt_ref[NA_DIM:, :], xb)
    for s in range(q3_ref.shape[0]):
        q3_ref[s] = q_t[:, s * LANES:(s + 1) * LANES].astype(BF)
        v3_ref[s] = v_t[:, s * LANES:(s + 1) * LANES].astype(BF)
    qm = seg(2).astype(BF)
    for h in range(MEM_HEADS):
        sl = slice(h * MEM_HEAD_DIM, (h + 1) * MEM_HEAD_DIM)
        s = _dot_nt(qm[:, sl], km_ref[:, sl]) * MEM_SCALE
        m = jnp.max(s, axis=-1, keepdims=True)
        p = jnp.exp(s - m)
        l = jnp.sum(p, axis=-1, keepdims=True)
        o = _dot(p.astype(BF), vm_ref[:, sl])
        mo_ref[:, sl] = (o / l).astype(BF)


def _proj(x2d, w_in, k_mem, v_mem, seq, m_len):
    t, d = x2d.shape
    tm = TM_PROJ
    tiles_per_batch = seq // tm
    w_tok = jnp.concatenate([w_in[:, 0:512], w_in[:, 1024:1536], w_in[:, 2048:2560]], axis=1).astype(BF)
    w_chan = jnp.concatenate([w_in[:, 512:1024], w_in[:, 1536:2048]], axis=1).T.astype(BF)
    tok = lambda i: (i, 0)
    slab = lambda i: (i, 0, 0)
    memb = lambda i: (i // tiles_per_batch, 0)
    const = lambda i: (0, 0)
    tok_out = pl.BlockSpec((tm, 512), tok)
    slab_out = pl.BlockSpec((tm // LANES, NA_DIM, LANES), slab)
    slab_shape = jax.ShapeDtypeStruct((t // LANES, NA_DIM, LANES), BF)
    return pl.pallas_call(
        _proj_kernel,
        grid=(t // tm,),
        in_specs=[pl.BlockSpec((tm, d), tok),
                  pl.BlockSpec(w_tok.shape, const),
                  pl.BlockSpec(w_chan.shape, const),
                  pl.BlockSpec((m_len, MEM_DIM), memb),
                  pl.BlockSpec((m_len, MEM_DIM), memb)],
        out_specs=[tok_out, tok_out, slab_out, slab_out, tok_out],
        out_shape=[jax.ShapeDtypeStruct((t, 512), F32), jax.ShapeDtypeStruct((t, 512), BF)] + [slab_shape] * 2
                  + [jax.ShapeDtypeStruct((t, 512), BF)],
        compiler_params=_cparams("parallel"),
        name="proj",
    )(x2d, w_tok, w_chan, k_mem, v_mem)


def _fourier_tables(n1, n2):
    n = n1 * n2
    k1 = np.arange(n1)
    ang1 = 2.0 * np.pi * ((k1[:, None] * k1[None, :]) % n1) / n1
    norm = 1.0 / np.sqrt(float(n) * FOURIER_GROUP_DIM)
    m1 = np.concatenate([np.cos(ang1), -np.sin(ang1)], axis=0) * norm
    kk = k1[:, None, None] + n1 * np.arange(n2)[None, :, None]
    nn = np.arange(n2)[None, None, :]
    ang2 = 2.0 * np.pi * ((kk * nn) % n) / n
    c2, s2 = np.cos(ang2), np.sin(ang2)
    m2 = np.concatenate([np.concatenate([c2, s2], axis=2),
                         np.concatenate([-s2, c2], axis=2)], axis=1)
    c = np.arange(FOURIER_GROUP_DIM)
    angc = 2.0 * np.pi * ((c[:, None] * c[None, :]) % FOURIER_GROUP_DIM) / FOURIER_GROUP_DIM
    mc = np.concatenate([np.cos(angc), np.sin(angc)], axis=0)
    as_bf = lambda a: jnp.asarray(a.astype(np.float32)).astype(BF)
    return as_bf(m1), as_bf(m2), as_bf(mc)


def _f1_kernel(u_ref, m_ref, a_ref):
    n1, pos, _ = u_ref.shape
    x = jnp.concatenate([u_ref[:, p, :] for p in range(pos)], axis=1).astype(BF)
    r = _dot(m_ref[...], x)
    a_ref[0] = r[:n1].astype(BF)
    a_ref[1] = r[n1:].astype(BF)


def _f2_kernel(a_ref, m2_ref, mc_ref, o_ref):
    slabs = a_ref.shape[1]
    n2 = a_ref.shape[2]
    for t in range(slabs):
        a = jnp.concatenate([a_ref[0, t], a_ref[1, t]], axis=0)
        y = _dot(m2_ref[t], a)
        yr = y[:n2].astype(BF)
        yi = y[n2:].astype(BF)
        for g in range(FOURIER_GROUPS):
            sl = slice(g * LANES, (g + 1) * LANES)
            yy = jnp.concatenate([yr[:, sl], yi[:, sl]], axis=1)
            o_ref[:, t, sl] = _dot(yy, mc_ref[...])


def _fourier(u_f, batch, seq):
    n2 = FFT_N2
    n1 = seq // n2
    m1, m2, mc = _fourier_tables(n1, n2)
    cols = n2 * FOURIER_DIM
    pos = min(F1_POS, n2)
    a = pl.pallas_call(
        _f1_kernel,
        grid=(batch, n2 // pos),
        in_specs=[pl.BlockSpec((None, n1, pos, FOURIER_DIM), lambda b, j: (b, 0, j, 0)),
                  pl.BlockSpec((2 * n1, n1), lambda b, j: (0, 0))],
        out_specs=pl.BlockSpec((None, 2, n1, pos * FOURIER_DIM), lambda b, j: (b, 0, 0, j)),
        out_shape=jax.ShapeDtypeStruct((batch, 2, n1, cols), BF),
        compiler_params=_cparams("parallel", "parallel"),
        name="fourier1",
    )(u_f.reshape(batch, n1, n2, FOURIER_DIM), m1)
    kb = min(F2_SLABS, n1)
    out = pl.pallas_call(
        _f2_kernel,
        grid=(batch, n1 // kb),
        in_specs=[pl.BlockSpec((None, 2, kb, n2, FOURIER_DIM), lambda b, j: (b, 0, j, 0, 0)),
                  pl.BlockSpec((kb, 2 * n2, 2 * n2), lambda b, j: (j, 0, 0)),
                  pl.BlockSpec((2 * FOURIER_GROUP_DIM, FOURIER_GROUP_DIM), lambda b, j: (0, 0))],
        out_specs=pl.BlockSpec((None, n2, kb, FOURIER_DIM), lambda b, j: (b, 0, j, 0)),
        out_shape=jax.ShapeDtypeStruct((batch, n2, n1, FOURIER_DIM), F32),
        compiler_params=_cparams("parallel", "parallel"),
        name="fourier2",
    )(a.reshape(batch, 2, n1, n2, FOURIER_DIM), m2, mc)
    return out.reshape(batch * seq, FOURIER_DIM)


def _na_group_start(r0, n_rows):
    rs0 = min(max(r0 - NA_ROW_WIN // 2, 0), n_rows - NA_ROW_WIN)
    return min(rs0, n_rows - NA_GROUP_WIN)


def _na_bias_table(rpb, n_rows):
    g = NA_GROUP_ROWS
    assert n_rows >= 2 * NA_GROUP_WIN and n_rows % g == 0 and (NA_ROW_WIN // 2) % g == 0
    cols = np.arange(GRID_W)
    cs = np.clip(cols - NA_COL_WIN // 2, 0, GRID_W - NA_COL_WIN)
    kc = np.arange(GRID_W)
    in_win = (kc[None, :] >= cs[:, None]) & (kc[None, :] < cs[:, None] + NA_COL_WIN)
    dc = kc[None, :] - cols[:, None] + (NA_COL_WIN - 1)
    n_dc = 2 * NA_COL_WIN - 1
    pick = ((dc.reshape(-1)[None, :] == np.arange(n_dc)[:, None]) & in_win.reshape(-1)[None, :])
    picked = jnp.dot(rpb.astype(F32).reshape(-1, n_dc), jnp.asarray(pick.astype(np.float32)),
                     precision=lax.Precision.HIGHEST)
    picked = picked.reshape(NA_HEADS, 2 * NA_ROW_WIN - 1, GRID_W, GRID_W)
    full_t = jnp.where(in_win[None, None], picked * LOG2_E, MASK_NEG).transpose(0, 1, 3, 2)
    masked = jnp.full((NA_HEADS, GRID_W, GRID_W), MASK_NEG, F32)
    half = NA_ROW_WIN // 2
    group_rows = list(range(0, half, g)) + [half] + list(range(n_rows - half, n_rows, g))
    tabs = []
    for r0 in group_rows:
        start = _na_group_start(r0, n_rows)
        key_rows = []
        for kr in range(NA_GROUP_WIN):
            blocks = []
            for rr in range(g):
                r = r0 + rr
                rs = min(max(r - half, 0), n_rows - NA_ROW_WIN)
                in_rows = rs <= start + kr < rs + NA_ROW_WIN
                blocks.append(full_t[:, start + kr - r + NA_ROW_WIN - 1] if in_rows else masked)
            key_rows.append(jnp.concatenate(blocks, axis=2))
        tabs.append(jnp.concatenate(key_rows, axis=1))
    return jnp.stack(tabs, axis=0)


def _na_kernel(q3_ref, k_ref, v3_ref, tab_ref, o_ref, *, n_rows):
    i = pl.program_id(2)
    g = NA_GROUP_ROWS
    slabs_per_group = g * GRID_W // LANES
    groups = q3_ref.shape[0] // slabs_per_group
    win = NA_GROUP_WIN * GRID_W
    tokens = g * GRID_W
    half = NA_ROW_WIN // 2
    n_top = half // g
    chan = lax.broadcasted_iota(I32, (2 * NA_HEAD_DIM, tokens), 0)
    first_head = chan < NA_HEAD_DIM

    def scores(jj):
        r0 = (i * groups + jj) * g
        start = jnp.minimum(jnp.clip(r0 - half, 0, n_rows - NA_ROW_WIN), n_rows - NA_GROUP_WIN)
        cfg = jnp.where(r0 < half, r0 // g,
                        jnp.where(r0 >= n_rows - half, n_top + 1 + (r0 - (n_rows - half)) // g, n_top))
        kw = k_ref[pl.ds(pl.multiple_of(start * GRID_W, LANES), win), :]
        qt = jnp.concatenate([q3_ref[jj * slabs_per_group + t] for t in range(slabs_per_group)], axis=1)
        zero = jnp.zeros_like(qt)
        q_bd = jnp.concatenate([jnp.where(first_head, qt, zero), jnp.where(first_head, zero, qt)], axis=1)
        s = _dot(kw, q_bd) + jnp.concatenate([tab_ref[cfg, 0], tab_ref[cfg, 1]], axis=1)
        return s, start

    ones_rows = jnp.ones((NA_SUM_ROWS, win), BF)

    def outputs(jj, p, start):
        slab = start // 2
        vw = jnp.concatenate([v3_ref[slab + t] for t in range(NA_GROUP_WIN // 2)] , axis=1)
        o_l = _dot(jnp.concatenate([vw, ones_rows], axis=0), p)
        o_t = o_l[:2 * NA_HEAD_DIM] / o_l[2 * NA_HEAD_DIM:2 * NA_HEAD_DIM + 1]
        o = jnp.where(first_head, o_t[:, :tokens], o_t[:, tokens:])
        o_ref[jj * tokens:(jj + 1) * tokens, :] = jnp.transpose(o).astype(BF)

    pending = None
    for jj in range(groups):
        s, start = scores(jj)
        if pending is not None:
            outputs(*pending)
        p = jnp.exp2(s - jnp.max(s, axis=0, keepdims=True))
        pending = (jj, p.astype(BF), start)
    outputs(*pending)


def _natten(q3, k, v3, rpb, batch, seq):
    assert 2 * GRID_W == LANES and 2 * NA_HEAD_DIM == LANES and NA_GROUP_ROWS % 2 == 0
    n_rows = seq // GRID_W
    slabs = n_rows // 2
    step_slabs = min(NA_GROUPS_PER_STEP * NA_GROUP_ROWS // 2, slabs)
    steps = slabs // step_slabs
    hp = NA_HEADS // 2
    tab = _na_bias_table(rpb, n_rows)
    return pl.pallas_call(
        functools.partial(_na_kernel, n_rows=n_rows),
        grid=(batch, hp, steps),
        in_specs=[pl.BlockSpec((step_slabs, LANES, LANES), lambda b, p, i: (b * steps + i, p, 0)),
                  pl.BlockSpec((seq, LANES), lambda b, p, i: (b, p)),
                  pl.BlockSpec((slabs, LANES, LANES), lambda b, p, i: (b, p, 0)),
                  pl.BlockSpec((tab.shape[0], 2) + tab.shape[2:], lambda b, p, i: (0, p, 0, 0))],
        out_specs=pl.BlockSpec((step_slabs * LANES, LANES), lambda b, p, i: (b * steps + i, p)),
        out_shape=jax.ShapeDtypeStruct((batch * seq, NA_DIM), BF),
        compiler_params=_cparams("parallel", "parallel", "parallel"),
        name="natten",
    )(q3, k, v3, tab)


def _merge_kernel(x_ref, fo_ref, na_ref, mo_ref, wg_ref, bg_ref, wf_ref, wn_ref, wm_ref, wo_ref,
                  g_ref, b_ref, ht_ref):
    d = x_ref.shape[1]
    x = x_ref[...]
    xb = x.astype(BF)
    merged = None
    for j, (br_ref, w_ref) in enumerate(((fo_ref, wf_ref), (na_ref, wn_ref), (mo_ref, wm_ref))):
        z = _dot(xb, wg_ref[:, j * d:(j + 1) * d]) + bg_ref[:, j * d:(j + 1) * d]
        gate = 1.0 / (1.0 + jnp.exp(-z))
        term = gate * _dot(br_ref[...].astype(BF), w_ref[...])
        merged = term if merged is None else merged + term
    mix = _dot(merged.astype(BF), wo_ref[...])
    _store_tile_rows(ht_ref, _layer_norm(ALPHA * x + mix, g_ref[...], b_ref[...]))


def _merge(x2d, fo, na, mo, w_gate, b_gate, w_fo, w_na, w_mo, w_out, ln_g, ln_b):
    t, d = x2d.shape
    assert d == SUBLANES * LANES
    tm = TM_MERGE
    tok = lambda i: (i, 0)
    full = lambda a: pl.BlockSpec(a.shape, lambda i: (0, 0))
    return pl.pallas_call(
        _merge_kernel,
        grid=(t // tm,),
        in_specs=[pl.BlockSpec((tm, d), tok)] + [pl.BlockSpec((tm, 512), tok)] * 3
                 + [full(a) for a in (w_gate, b_gate, w_fo, w_na, w_mo, w_out, ln_g, ln_b)],
        out_specs=pl.BlockSpec((tm * SUBLANES, LANES), tok),
        out_shape=jax.ShapeDtypeStruct((t * SUBLANES, LANES), F32),
        compiler_params=_cparams("parallel"),
        name="merge",
    )(x2d, fo, na, mo, w_gate, b_gate, w_fo, w_na, w_mo, w_out, ln_g, ln_b)


def _router_kernel(ht_ref, whi_ref, wlo_ref, b_ref, oi_ref, of_ref, cnt_ref, carry_ref):
    tm = ht_ref.shape[0] // SUBLANES

    @pl.when(pl.program_id(0) == 0)
    def _():
        carry_ref[...] = jnp.zeros_like(carry_ref)

    h = _load_tile_rows(ht_ref)
    hh = h.astype(BF)
    hl = (h - hh.astype(F32)).astype(BF)
    logits = (_dot_nt(whi_ref[...], hh) + _dot_nt(wlo_ref[...], hh) + _dot_nt(whi_ref[...], hl)
              + b_ref[...])
    gl = logits[:N_GROUPS, :]
    g_row = lax.broadcasted_iota(I32, (N_GROUPS, tm), 0)
    gmax = jnp.max(gl, axis=0, keepdims=True)
    g_idx = jnp.min(jnp.where(gl == gmax, g_row, N_GROUPS), axis=0, keepdims=True)
    p_group = 1.0 / jnp.sum(jnp.exp(gl - gmax), axis=0, keepdims=True)
    e_row = lax.broadcasted_iota(I32, (N_EXPERTS, tm), 0)
    el = jnp.where((e_row >> 3) == g_idx, logits[N_GROUPS:N_GROUPS + N_EXPERTS, :], MASK_NEG)
    v1 = jnp.max(el, axis=0, keepdims=True)
    i1 = jnp.min(jnp.where(el == v1, e_row, N_EXPERTS), axis=0, keepdims=True)
    el2 = jnp.where(e_row == i1, MASK_NEG, el)
    v2 = jnp.max(el2, axis=0, keepdims=True)
    i2 = jnp.min(jnp.where(el2 == v2, e_row, N_EXPERTS), axis=0, keepdims=True)
    tt = jnp.exp(v2 - v1)
    w1 = p_group / (1.0 + tt)
    w2 = p_group * tt / (1.0 + tt)
    sel1 = e_row == i1
    sel2 = e_row == i2
    onehot = jnp.where(sel1 | sel2, 1.0, 0.0)
    t_in = lax.broadcasted_iota(I32, (tm, tm), 0)
    t_out = lax.broadcasted_iota(I32, (tm, tm), 1)
    tri = jnp.where(t_in < t_out, 1.0, 0.0).astype(BF)
    prefix = _dot(onehot.astype(BF), tri) + carry_ref[...]
    r1 = jnp.sum(jnp.where(sel1, prefix, 0.0), axis=0, keepdims=True)
    r2 = jnp.sum(jnp.where(sel2, prefix, 0.0), axis=0, keepdims=True)
    carry_ref[...] += jnp.sum(onehot, axis=1, keepdims=True)
    cnt_ref[...] = carry_ref[...]
    sub = pl.program_id(0) % ROUTE_STEPS_PER_PLANE
    for f, vals in enumerate((i1, i2, r1.astype(I32), r2.astype(I32))):
        for c in range(tm // LANES):
            oi_ref[f, pl.ds(sub * (tm // LANES) + c, 1), :] = vals[:, c * LANES:(c + 1) * LANES]
    w_row = lax.broadcasted_iota(I32, (LANES, tm), 0)
    w_rows = jnp.where(w_row == 0, w1, jnp.where(w_row == 1, w2, 0.0))
    of_ref[...] = jnp.transpose(w_rows)[:, :8]


def _router(ht, w_rg, b_rg, w_re, b_re):
    t = ht.shape[0] // SUBLANES
    d = w_rg.shape[0]
    tm = TM_ROUTE
    pad = ROUTE_ROWS - N_GROUPS - N_EXPERTS
    w = jnp.concatenate([w_rg.astype(F32), w_re.astype(F32), jnp.zeros((d, pad), F32)], axis=1).T
    b = jnp.concatenate([b_rg.astype(F32), b_re.astype(F32), jnp.zeros((pad,), F32)])[:, None]
    w_hi = w.astype(BF)
    w_lo = (w - w_hi.astype(F32)).astype(BF)
    tok = lambda i: (i, 0)
    const = lambda i: (0, 0)
    return pl.pallas_call(
        _router_kernel,
        grid=(t // tm,),
        in_specs=[pl.BlockSpec((tm * SUBLANES, LANES), tok),
                  pl.BlockSpec((ROUTE_ROWS, d), const),
                  pl.BlockSpec((ROUTE_ROWS, d), const),
                  pl.BlockSpec((ROUTE_ROWS, 1), const)],
        out_specs=[pl.BlockSpec((4, SUBLANES, LANES), lambda i: (0, i // ROUTE_STEPS_PER_PLANE, 0)),
                   pl.BlockSpec((tm, 8), tok),
                   pl.BlockSpec((N_EXPERTS, 1), const)],
        out_shape=[jax.ShapeDtypeStruct((4, t // LANES, LANES), I32), jax.ShapeDtypeStruct((t, 8), F32),
                   jax.ShapeDtypeStruct((N_EXPERTS, 1), F32)],
        scratch_shapes=[pltpu.VMEM((N_EXPERTS, 1), F32)],
        compiler_params=_cparams("arbitrary"),
        name="router",
    )(ht, w_hi, w_lo, b)


def _row_groups(n_rows, copies_of_row, start):
    def group(g, c):
        copies = []
        for u in range(ROW_GROUP):
            copies.extend(enumerate(copies_of_row(g * ROW_GROUP + u)))
        for k, (src, dst, sem) in copies:
            if start:
                pltpu.async_copy(src, dst, sem, priority=k % DMA_QUEUES)
            else:
                pltpu.make_async_copy(src, dst, sem).wait()
        return c

    lax.fori_loop(0, n_rows // ROW_GROUP, group, 0)


def _dests_kernel(ps_ref, idx_ref, d_ref):
    for a in range(2):
        e = idx_ref[a]
        d = idx_ref[2 + a]
        for k in range(N_EXPERTS):
            d = d + jnp.where(e == k, ps_ref[k], 0)
        d_ref[a] = d


def _dests(idx, pad_start):
    planes = idx.shape[1]
    out = pl.pallas_call(
        _dests_kernel,
        grid_spec=pltpu.PrefetchScalarGridSpec(
            num_scalar_prefetch=1,
            grid=(1,),
            in_specs=[pl.BlockSpec(idx.shape, lambda i, ps: (0, 0, 0))],
            out_specs=pl.BlockSpec((2, planes, LANES), lambda i, ps: (0, 0, 0)),
        ),
        out_shape=jax.ShapeDtypeStruct((2, planes, LANES), I32),
        compiler_params=_cparams("arbitrary"),
        name="dests",
    )(pad_start, idx)
    return out[0].reshape(-1), out[1].reshape(-1)


def _pack_words(val):
    words = []
    for c in range(PACK_ROWS):
        hi = pltpu.bitcast(val[:, c * LANES:(c + 1) * LANES].astype(BF).astype(F32), U32)
        lo = pltpu.bitcast(val[:, (c + PACK_ROWS) * LANES:(c + PACK_ROWS + 1) * LANES].astype(BF).astype(F32), U32)
        words.append(hi | (lo >> 16))
    return words


def _unpack_words(words):
    his = [pltpu.bitcast(w & jnp.uint32(0xFFFF0000), F32) for w in words]
    los = [pltpu.bitcast(w << 16, F32) for w in words]
    return jnp.concatenate(his + los, axis=1)


def _load_packed(ref):
    n = ref.shape[0] // PACK_ROWS
    return [ref[pl.ds(c, n, stride=PACK_ROWS), :] for c in range(PACK_ROWS)]


def _store_packed(ref, words):
    n = ref.shape[0] // PACK_ROWS
    for c, w in enumerate(words):
        ref[pl.ds(c, n, stride=PACK_ROWS), :] = w


def _packed_row_copy(src_ref, src_row, dst_ref, dst_row, sem):
    src = pl.multiple_of(src_row * PACK_ROWS, PACK_ROWS)
    dst = pl.multiple_of(dst_row * PACK_ROWS, PACK_ROWS)
    return src_ref.at[pl.ds(src, PACK_ROWS)], dst_ref.at[pl.ds(dst, PACK_ROWS)], sem


def _dispatch_kernel(ps_ref, pe_ref, d1_ref, d2_ref, ht_ref, xg_ref, pk, zbuf, sem, zsem):
    i = pl.program_id(0)
    tm = d1_ref.shape[0]
    blk = MOE_BLOCK * PACK_ROWS
    n_blocks = xg_ref.shape[0] // blk

    @pl.when(i == 0)
    def _():
        zbuf[...] = jnp.zeros_like(zbuf)

        def zero_copy(block):
            return pltpu.make_async_copy(zbuf, xg_ref.at[pl.ds(pl.multiple_of(block * blk, blk), blk)], zsem)

        def seg_issue(e, c):
            @pl.when(pe_ref[e] > ps_ref[e])
            def _():
                zero_copy(pe_ref[e] // MOE_BLOCK - 1).start()
            return c

        def seg_drain(e, c):
            @pl.when(pe_ref[e] > ps_ref[e])
            def _():
                zero_copy(pe_ref[e] // MOE_BLOCK - 1).wait()
            return c

        def tail_issue(b, c):
            zero_copy(b).start()
            return c

        def tail_drain(b, c):
            zero_copy(b).wait()
            return c

        first_tail = pe_ref[N_EXPERTS - 1] // MOE_BLOCK
        lax.fori_loop(0, N_EXPERTS, seg_issue, 0)
        lax.fori_loop(first_tail, n_blocks, tail_issue, 0)
        lax.fori_loop(0, N_EXPERTS, seg_drain, 0)
        lax.fori_loop(first_tail, n_blocks, tail_drain, 0)

    _store_packed(pk, _pack_words(_load_tile_rows(ht_ref)))
    _row_groups(tm, lambda t: [_packed_row_copy(pk, t, xg_ref, d1_ref[t], sem),
                               _packed_row_copy(pk, t, xg_ref, d2_ref[t], sem)], True)
    _row_groups(tm, lambda t: [_packed_row_copy(pk, 0, xg_ref, 0, sem)] * 2, False)


def _dispatch(ht, d1, d2, pad_start, pad_end, n_pad):
    t = d1.shape[0]
    tm = TM_DISPATCH
    smem = lambda: pl.BlockSpec((tm,), lambda i, ps, pe: (i,), memory_space=pltpu.SMEM)
    return pl.pallas_call(
        _dispatch_kernel,
        grid_spec=pltpu.PrefetchScalarGridSpec(
            num_scalar_prefetch=2,
            grid=(t // tm,),
            in_specs=[smem(), smem(), pl.BlockSpec((tm * SUBLANES, LANES), lambda i, ps, pe: (i, 0))],
            out_specs=pl.BlockSpec(memory_space=pl.ANY),
            scratch_shapes=[pltpu.VMEM((tm * PACK_ROWS, LANES), U32), pltpu.VMEM((MOE_BLOCK * PACK_ROWS, LANES), U32),
                            pltpu.SemaphoreType.DMA(()), pltpu.SemaphoreType.DMA(())],
        ),
        out_shape=jax.ShapeDtypeStruct((n_pad * PACK_ROWS, LANES), U32),
        compiler_params=_cparams("arbitrary"),
        name="dispatch",
    )(pad_start, pad_end, d1, d2, ht)


def _expert_kernel(be_ref, nx_ref, nv_ref, nu_ref, x_ref, wg_hbm, wu_hbm, wd_hbm, y_ref,
                   wg_f, wu_f, wd_f, wg_s, wu_s, wd_s, sems):
    j = pl.program_id(0)
    stage = ((wg_hbm, wg_f, wg_s), (wu_hbm, wu_f, wu_s), (wd_hbm, wd_f, wd_s))

    def weight_copies(e):
        return [pltpu.make_async_copy(hbm.at[e], buf, sems.at[k]) for k, (hbm, buf, _) in enumerate(stage)]

    @pl.when(j < nu_ref[0])
    def _():
        e = be_ref[j]

        @pl.when(j == 0)
        def _():
            for cp in weight_copies(e):
                cp.start()

        @pl.when((j == 0) | (e != be_ref[jnp.maximum(j - 1, 0)]))
        def _():
            for cp, (_, buf, dst) in zip(weight_copies(e), stage):
                cp.wait()
                dst[...] = buf[...].astype(BF)
            nxt = nx_ref[e]

            @pl.when(nxt >= 0)
            def _():
                for cp in weight_copies(nxt):
                    cp.start()

        part_rows = x_ref.shape[0] // EXPERT_PARTS
        rows = part_rows // EXPERT_SUBBLOCKS

        def part(x_part, y_part):
            def up(h):
                xb = _unpack_words(_load_packed(x_part.at[pl.ds(h * rows, rows)])).astype(BF)
                a = _dot(xb, wg_s[...])
                u = _dot(xb, wu_s[...])
                return ((a / (1.0 + jnp.exp(-a))) * u).astype(BF)

            mids = [up(h) for h in range(EXPERT_SUBBLOCKS)]
            for h, mid in enumerate(mids):
                _store_packed(y_part.at[pl.ds(h * rows, rows)], _pack_words(_dot(mid, wd_s[...])))

        for q in range(EXPERT_PARTS):
            x_part = x_ref.at[pl.ds(q * part_rows, part_rows)]
            y_part = y_ref.at[pl.ds(q * part_rows, part_rows)]
            has_tokens = nv_ref[j] > q * (MOE_BLOCK // EXPERT_PARTS)
            pl.when(has_tokens)(functools.partial(part, x_part, y_part))

            @pl.when(jnp.logical_not(has_tokens))
            def _():
                y_part[...] = jnp.zeros_like(y_part)

    @pl.when(j >= nu_ref[0])
    def _():
        y_ref[...] = jnp.zeros_like(y_ref)


def _experts(xg, blk_expert, next_expert, blk_valid, n_used, w_eg, w_eu, w_ed):
    blk = MOE_BLOCK * PACK_ROWS
    nb = xg.shape[0] // blk
    d, de = w_eg.shape[1], w_eg.shape[2]
    rows = lambda j, be, nx, nv, nu: (jnp.minimum(j, nu[0] - 1), 0)
    hbm = pl.BlockSpec(memory_space=pl.ANY)
    return pl.pallas_call(
        _expert_kernel,
        grid_spec=pltpu.PrefetchScalarGridSpec(
            num_scalar_prefetch=4,
            grid=(nb,),
            in_specs=[pl.BlockSpec((blk, LANES), rows), hbm, hbm, hbm],
            out_specs=pl.BlockSpec((blk, LANES), lambda j, be, nx, nv, nu: (j, 0)),
            scratch_shapes=[pltpu.VMEM((d, de), F32), pltpu.VMEM((d, de), F32), pltpu.VMEM((de, d), F32),
                            pltpu.VMEM((d, de), BF), pltpu.VMEM((d, de), BF), pltpu.VMEM((de, d), BF),
                            pltpu.SemaphoreType.DMA((3,))],
        ),
        out_shape=jax.ShapeDtypeStruct(xg.shape, U32),
        compiler_params=_cparams("arbitrary"),
        name="experts",
    )(blk_expert, next_expert, blk_valid, n_used, xg, w_eg, w_eu, w_ed)


def _combine_kernel(d1_ref, d2_ref, d1_next_ref, d2_next_ref, wt_ref, ht_ref, yb_ref, g_ref, b_ref, o_ref,
                    buf, sems):
    i = pl.program_id(0)
    n = pl.num_programs(0)
    tm = d1_ref.shape[0]

    def gather_tile(da_ref, db_ref, slot, start):
        def copies(t):
            if not start:
                return [_packed_row_copy(yb_ref, 0, buf.at[slot, a], 0, sems.at[slot]) for a in range(2)]
            return [_packed_row_copy(yb_ref, da_ref[t], buf.at[slot, 0], t, sems.at[slot]),
                    _packed_row_copy(yb_ref, db_ref[t], buf.at[slot, 1], t, sems.at[slot])]
        _row_groups(tm, copies, start)

    def gathered(slot, a):
        return _unpack_words(_load_packed(buf.at[slot, a]))

    slot = i % 2

    @pl.when(i == 0)
    def _():
        gather_tile(d1_ref, d2_ref, 0, True)

    @pl.when(i + 1 < n)
    def _():
        gather_tile(d1_next_ref, d2_next_ref, 1 - slot, True)

    gather_tile(d1_ref, d2_ref, slot, False)
    wt = wt_ref[...]
    ffn = wt[:, 0:1] * gathered(slot, 0) + wt[:, 1:2] * gathered(slot, 1)
    o_ref[...] = _layer_norm(ALPHA * _load_tile_rows(ht_ref) + ffn, g_ref[...], b_ref[...])


def _combine(ht, yb, d1, d2, wts, ln_g, ln_b):
    t = d1.shape[0]
    d = SUBLANES * LANES
    tm = TM_COMBINE
    n = t // tm
    tok = lambda i: (i, 0)
    const = lambda i: (0, 0)
    cur = lambda: pl.BlockSpec((tm,), lambda i: (i,), memory_space=pltpu.SMEM)
    nxt = lambda: pl.BlockSpec((tm,), lambda i: (jnp.minimum(i + 1, n - 1),), memory_space=pltpu.SMEM)
    return pl.pallas_call(
        _combine_kernel,
        grid=(n,),
        in_specs=[cur(), cur(), nxt(), nxt(),
                  pl.BlockSpec((tm, 8), tok),
                  pl.BlockSpec((tm * SUBLANES, LANES), tok),
                  pl.BlockSpec(memory_space=pl.ANY),
                  pl.BlockSpec((1, d), const), pl.BlockSpec((1, d), const)],
        out_specs=pl.BlockSpec((tm, d), tok),
        out_shape=jax.ShapeDtypeStruct((t, d), F32),
        scratch_shapes=[pltpu.VMEM((2, 2, tm * PACK_ROWS, LANES), U32), pltpu.SemaphoreType.DMA((2,))],
        compiler_params=_cparams("arbitrary"),
        name="combine",
    )(d1, d2, d1, d2, wts, ht, yb, ln_g, ln_b)


def _token_mixing(x2d, mem2d, batch, seq, m_len, w_in, w_gate, b_gate, w_mem_kv, rpb, w_fo, w_na, w_mo, w_out,
                  ln_g, ln_b):
    k_mem, v_mem = _memkv(mem2d, w_mem_kv.astype(BF), m_len)
    u_f, k, q3, v3, mo = _proj(x2d, w_in, k_mem, v_mem, seq, m_len)
    fo = _fourier(u_f, batch, seq)
    na = _natten(q3, k, v3, rpb, batch, seq)
    return _merge(x2d, fo, na, mo, w_gate.astype(BF), b_gate.astype(F32)[None, :], w_fo.astype(BF),
                  w_na.astype(BF), w_mo.astype(BF), w_out.astype(BF), ln_g.astype(F32)[None, :],
                  ln_b.astype(F32)[None, :])


def _moe(ht, w_rg, b_rg, w_re, b_re, w_eg, w_eu, w_ed, ln_g, ln_b):
    t = ht.shape[0] // SUBLANES
    idx, wts, cnt = _router(ht, w_rg, b_rg, w_re, b_re)
    counts = cnt[:, 0].astype(I32)
    padded = (counts + MOE_BLOCK - 1) // MOE_BLOCK * MOE_BLOCK
    pad_end = jnp.cumsum(padded).astype(I32)
    pad_start = pad_end - padded
    n_pad = 2 * t + N_EXPERTS * MOE_BLOCK
    nb = n_pad // MOE_BLOCK
    blk_start = jnp.arange(nb, dtype=I32) * MOE_BLOCK
    blk_expert = jnp.minimum(jnp.sum((pad_end[None, :] <= blk_start[:, None]).astype(I32), axis=1), N_EXPERTS - 1)
    n_used = pad_end[-1:] // MOE_BLOCK
    seg_left = counts[None, :] - (blk_start[:, None] - pad_start[None, :])
    own = blk_expert[:, None] == jnp.arange(N_EXPERTS, dtype=I32)[None, :]
    blk_valid = jnp.clip(jnp.sum(jnp.where(own, seg_left, 0), axis=1), 0, MOE_BLOCK).astype(I32)
    ids = jnp.arange(N_EXPERTS, dtype=I32)
    later_used = (padded[None, :] > 0) & (ids[None, :] > ids[:, None])
    next_expert = jnp.min(jnp.where(later_used, ids[None, :], N_EXPERTS), axis=1)
    next_expert = jnp.where(next_expert == N_EXPERTS, -1, next_expert).astype(I32)
    d1, d2 = _dests(idx, pad_start)
    xg = _dispatch(ht, d1, d2, pad_start, pad_end, n_pad)
    yb = _experts(xg, blk_expert, next_expert, blk_valid, n_used, w_eg, w_eu, w_ed)
    return _combine(ht, yb, d1, d2, wts, ln_g.astype(F32)[None, :], ln_b.astype(F32)[None, :])


def kernel(x, mem, w_in, w_gate, b_gate, w_mem_kv, rpb, w_fourier_o, w_na_o, w_mem_o, w_out, ln1_g, ln1_b,
           w_router_group, b_router_group, w_router_expert, b_router_expert, w_exp_gate, w_exp_up,
           w_exp_down, ln2_g, ln2_b):
    batch, seq, d = x.shape
    m_len = mem.shape[1]
    x2d = x.reshape(batch * seq, d)
    mem2d = mem.reshape(batch * m_len, d)
    for l in range(w_in.shape[0]):
        ht = _token_mixing(x2d, mem2d, batch, seq, m_len, w_in[l], w_gate[l], b_gate[l], w_mem_kv[l], rpb[l],
                           w_fourier_o[l], w_na_o[l], w_mem_o[l], w_out[l], ln1_g[l], ln1_b[l])
        x2d = _moe(ht, w_router_group[l], b_router_group[l], w_router_expert[l], b_router_expert[l],
                   w_exp_gate[l], w_exp_up[l], w_exp_down[l], ln2_g[l], ln2_b[l])
    return x2d.reshape(batch, seq, d)
```

```python
import functools

import numpy as np
import jax
import jax.numpy as jnp
from jax import lax
from jax.experimental import pallas as pl
from jax.experimental.pallas import tpu as pltpu

BF = jnp.bfloat16
F32 = jnp.float32
I32 = jnp.int32
U32 = jnp.uint32

GRID_W = 64
MEM_HEADS = 4
MEM_HEAD_DIM = 128
MEM_DIM = MEM_HEADS * MEM_HEAD_DIM
FOURIER_GROUPS = 4
FOURIER_GROUP_DIM = 128
FOURIER_DIM = FOURIER_GROUPS * FOURIER_GROUP_DIM
NA_HEADS = 8
NA_HEAD_DIM = 64
NA_DIM = NA_HEADS * NA_HEAD_DIM
NA_ROW_WIN = 8
NA_COL_WIN = 16
N_GROUPS = 8
EXPERTS_PER_GROUP = 8
N_EXPERTS = N_GROUPS * EXPERTS_PER_GROUP
DEPTH = 1
ALPHA = (2.0 * DEPTH) ** 0.25
LN_EPS = 1e-5
NA_SCALE = NA_HEAD_DIM ** -0.5
LOG2_E = 1.4426950408889634
MEM_SCALE = MEM_HEAD_DIM ** -0.5
MASK_NEG = -1e30

LANES = 128
SUBLANES = 8
FFT_N2 = 128
TM_PROJ = 512
TM_MERGE = 512
TM_ROUTE = 512
TM_DISPATCH = 1024
TM_COMBINE = 256
NA_GROUP_ROWS = 4
NA_GROUP_WIN = NA_GROUP_ROWS + NA_ROW_WIN
NA_GROUPS_PER_STEP = 8
NA_SUM_ROWS = 16
F1_POS = 16
F2_SLABS = 8
MOE_BLOCK = 512
EXPERT_PARTS = 2
PACK_ROWS = 4
EXPERT_SUBBLOCKS = 2
ROUTE_ROWS = 128
ROUTE_STEPS_PER_PLANE = SUBLANES * LANES // TM_ROUTE
DMA_QUEUES = 2
ROW_GROUP = 16
VMEM_LIMIT = 56 * 1024 * 1024


def _cparams(*sem):
    return pltpu.CompilerParams(dimension_semantics=sem, vmem_limit_bytes=VMEM_LIMIT)


def _dot(a, b):
    return jnp.dot(a, b, preferred_element_type=F32)


def _dot_nt(a, b):
    return lax.dot_general(a, b, (((1,), (1,)), ((), ())), preferred_element_type=F32)


def _layer_norm(y, g, b):
    mu = jnp.mean(y, axis=-1, keepdims=True)
    yc = y - mu
    var = jnp.mean(yc * yc, axis=-1, keepdims=True)
    return yc * lax.rsqrt(var + LN_EPS) * g + b


def _store_tile_rows(ref, val):
    n = val.shape[0]
    for s in range(SUBLANES):
        ref[pl.ds(s, n, stride=SUBLANES), :] = val[:, s * LANES:(s + 1) * LANES]


def _load_tile_rows(ref):
    n = ref.shape[0] // SUBLANES
    return jnp.concatenate([ref[pl.ds(s, n, stride=SUBLANES), :] for s in range(SUBLANES)], axis=1)


def _memkv_kernel(mem_ref, w_ref, k_ref, v_ref):
    kv = _dot(mem_ref[...].astype(BF), w_ref[...])
    k_ref[...] = kv[:, :MEM_DIM].astype(BF)
    v_ref[...] = kv[:, MEM_DIM:].astype(BF)


def _memkv(mem2d, w_kv, m_len):
    rows, d = mem2d.shape
    return pl.pallas_call(
        _memkv_kernel,
        grid=(rows // m_len,),
        in_specs=[pl.BlockSpec((m_len, d), lambda i: (i, 0)),
                  pl.BlockSpec((d, 2 * MEM_DIM), lambda i: (0, 0))],
        out_specs=[pl.BlockSpec((m_len, MEM_DIM), lambda i: (i, 0))] * 2,
        out_shape=[jax.ShapeDtypeStruct((rows, MEM_DIM), BF)] * 2,
        compiler_params=_cparams("parallel"),
        name="memkv",
    )(mem2d, w_kv)


def _proj_kernel(x_ref, w_ref, wt_ref, km_ref, vm_ref, uf_ref, k_ref, q3_ref, v3_ref, mo_ref):
    xb = x_ref[...].astype(BF)

    def seg(j):
        return _dot(xb, w_ref[:, j * 512:(j + 1) * 512])

    uf_ref[...] = seg(0)
    k_ref[...] = seg(1).astype(BF)
    q_t = _dot_nt(wt_ref[:NA_DIM, :], xb) * (NA_SCALE * LOG2_E)
    v_t = _dot_nt(wt_ref[NA_DIM:, :], xb)
    for s in range(q3_ref.shape[0]):
        q3_ref[s] = q_t[:, s * LANES:(s + 1) * LANES].astype(BF)
        v3_ref[s] = v_t[:, s * LANES:(s + 1) * LANES].astype(BF)
    qm = seg(2).astype(BF)
    for h in range(MEM_HEADS):
        sl = slice(h * MEM_HEAD_DIM, (h + 1) * MEM_HEAD_DIM)
        s = _dot_nt(qm[:, sl], km_ref[:, sl]) * MEM_SCALE
        m = jnp.max(s, axis=-1, keepdims=True)
        p = jnp.exp(s - m)
        l = jnp.sum(p, axis=-1, keepdims=True)
        o = _dot(p.astype(BF), vm_ref[:, sl])
        mo_ref[:, sl] = (o / l).astype(BF)


def _proj(x2d, w_in, k_mem, v_mem, seq, m_len):
    t, d = x2d.shape
    tm = TM_PROJ
    tiles_per_batch = seq // tm
    w_tok = jnp.concatenate([w_in[:, 0:512], w_in[:, 1024:1536], w_in[:, 2048:2560]], axis=1).astype(BF)
    w_chan = jnp.concatenate([w_in[:, 512:1024], w_in[:, 1536:2048]], axis=1).T.astype(BF)
    tok = lambda i: (i, 0)
    slab = lambda i: (i, 0, 0)
    memb = lambda i: (i // tiles_per_batch, 0)
    const = lambda i: (0, 0)
    tok_out = pl.BlockSpec((tm, 512), tok)
    slab_out = pl.BlockSpec((tm // LANES, NA_DIM, LANES), slab)
    slab_shape = jax.ShapeDtypeStruct((t // LANES, NA_DIM, LANES), BF)
    return pl.pallas_call(
        _proj_kernel,
        grid=(t // tm,),
        in_specs=[pl.BlockSpec((tm, d), tok),
                  pl.BlockSpec(w_tok.shape, const),
                  pl.BlockSpec(w_chan.shape, const),
                  pl.BlockSpec((m_len, MEM_DIM), memb),
                  pl.BlockSpec((m_len, MEM_DIM), memb)],
        out_specs=[tok_out, tok_out, slab_out, slab_out, tok_out],
        out_shape=[jax.ShapeDtypeStruct((t, 512), F32), jax.ShapeDtypeStruct((t, 512), BF)] + [slab_shape] * 2
                  + [jax.ShapeDtypeStruct((t, 512), BF)],
        compiler_params=_cparams("parallel"),
        name="proj",
    )(x2d, w_tok, w_chan, k_mem, v_mem)


def _fourier_tables(n1, n2):
    n = n1 * n2
    k1 = np.arange(n1)
    ang1 = 2.0 * np.pi * ((k1[:, None] * k1[None, :]) % n1) / n1
    norm = 1.0 / np.sqrt(float(n) * FOURIER_GROUP_DIM)
    m1 = np.concatenate([np.cos(ang1), -np.sin(ang1)], axis=0) * norm
    kk = k1[:, None, None] + n1 * np.arange(n2)[None, :, None]
    nn = np.arange(n2)[None, None, :]
    ang2 = 2.0 * np.pi * ((kk * nn) % n) / n
    c2, s2 = np.cos(ang2), np.sin(ang2)
    m2 = np.concatenate([np.concatenate([c2, s2], axis=2),
                         np.concatenate([-s2, c2], axis=2)], axis=1)
    c = np.arange(FOURIER_GROUP_DIM)
    angc = 2.0 * np.pi * ((c[:, None] * c[None, :]) % FOURIER_GROUP_DIM) / FOURIER_GROUP_DIM
    mc = np.concatenate([np.cos(angc), np.sin(angc)], axis=0)
    as_bf = lambda a: jnp.asarray(a.astype(np.float32)).astype(BF)
    return as_bf(m1), as_bf(m2), as_bf(mc)


def _f1_kernel(u_ref, m_ref, a_ref):
    n1, pos, _ = u_ref.shape
    x = jnp.concatenate([u_ref[:, p, :] for p in range(pos)], axis=1).astype(BF)
    r = _dot(m_ref[...], x)
    a_ref[0] = r[:n1].astype(BF)
    a_ref[1] = r[n1:].astype(BF)


def _f2_kernel(a_ref, m2_ref, mc_ref, o_ref):
    slabs = a_ref.shape[1]
    n2 = a_ref.shape[2]
    for t in range(slabs):
        a = jnp.concatenate([a_ref[0, t], a_ref[1, t]], axis=0)
        y = _dot(m2_ref[t], a)
        yr = y[:n2].astype(BF)
        yi = y[n2:].astype(BF)
        yy = jnp.concatenate([jnp.concatenate([yr[:, g * LANES:(g + 1) * LANES], yi[:, g * LANES:(g + 1) * LANES]],
                                              axis=1) for g in range(FOURIER_GROUPS)], axis=0)
        z = _dot(yy, mc_ref[...])
        for g in range(FOURIER_GROUPS):
            o_ref[:, t, g * LANES:(g + 1) * LANES] = z[g * n2:(g + 1) * n2]


def _fourier(u_f, batch, seq):
    n2 = FFT_N2
    n1 = seq // n2
    m1, m2, mc = _fourier_tables(n1, n2)
    cols = n2 * FOURIER_DIM
    pos = min(F1_POS, n2)
    a = pl.pallas_call(
        _f1_kernel,
        grid=(batch, n2 // pos),
        in_specs=[pl.BlockSpec((None, n1, pos, FOURIER_DIM), lambda b, j: (b, 0, j, 0)),
                  pl.BlockSpec((2 * n1, n1), lambda b, j: (0, 0))],
        out_specs=pl.BlockSpec((None, 2, n1, pos * FOURIER_DIM), lambda b, j: (b, 0, 0, j)),
        out_shape=jax.ShapeDtypeStruct((batch, 2, n1, cols), BF),
        compiler_params=_cparams("parallel", "parallel"),
        name="fourier1",
    )(u_f.reshape(batch, n1, n2, FOURIER_DIM), m1)
    kb = min(F2_SLABS, n1)
    out = pl.pallas_call(
        _f2_kernel,
        grid=(batch, n1 // kb),
        in_specs=[pl.BlockSpec((None, 2, kb, n2, FOURIER_DIM), lambda b, j: (b, 0, j, 0, 0)),
                  pl.BlockSpec((kb, 2 * n2, 2 * n2), lambda b, j: (j, 0, 0)),
                  pl.BlockSpec((2 * FOURIER_GROUP_DIM, FOURIER_GROUP_DIM), lambda b, j: (0, 0))],
        out_specs=pl.BlockSpec((None, n2, kb, FOURIER_DIM), lambda b, j: (b, 0, j, 0)),
        out_shape=jax.ShapeDtypeStruct((batch, n2, n1, FOURIER_DIM), F32),
        compiler_params=_cparams("parallel", "parallel"),
        name="fourier2",
    )(a.reshape(batch, 2, n1, n2, FOURIER_DIM), m2, mc)
    return out.reshape(batch * seq, FOURIER_DIM)


def _na_group_start(r0, n_rows):
    rs0 = min(max(r0 - NA_ROW_WIN // 2, 0), n_rows - NA_ROW_WIN)
    return min(rs0, n_rows - NA_GROUP_WIN)


def _na_bias_table(rpb, n_rows):
    g = NA_GROUP_ROWS
    assert n_rows >= 2 * NA_GROUP_WIN and n_rows % g == 0 and (NA_ROW_WIN // 2) % g == 0
    cols = np.arange(GRID_W)
    cs = np.clip(cols - NA_COL_WIN // 2, 0, GRID_W - NA_COL_WIN)
    kc = np.arange(GRID_W)
    in_win = (kc[None, :] >= cs[:, None]) & (kc[None, :] < cs[:, None] + NA_COL_WIN)
    dc = kc[None, :] - cols[:, None] + (NA_COL_WIN - 1)
    n_dc = 2 * NA_COL_WIN - 1
    pick = ((dc.reshape(-1)[None, :] == np.arange(n_dc)[:, None]) & in_win.reshape(-1)[None, :])
    picked = jnp.dot(rpb.astype(F32).reshape(-1, n_dc), jnp.asarray(pick.astype(np.float32)),
                     precision=lax.Precision.HIGHEST)
    picked = picked.reshape(NA_HEADS, 2 * NA_ROW_WIN - 1, GRID_W, GRID_W)
    full_t = jnp.where(in_win[None, None], picked * LOG2_E, MASK_NEG).transpose(0, 1, 3, 2)
    masked = jnp.full((NA_HEADS, GRID_W, GRID_W), MASK_NEG, F32)
    half = NA_ROW_WIN // 2
    group_rows = list(range(0, half, g)) + [half] + list(range(n_rows - half, n_rows, g))
    tabs = []
    for r0 in group_rows:
        start = _na_group_start(r0, n_rows)
        key_rows = []
        for kr in range(NA_GROUP_WIN):
            blocks = []
            for rr in range(g):
                r = r0 + rr
                rs = min(max(r - half, 0), n_rows - NA_ROW_WIN)
                in_rows = rs <= start + kr < rs + NA_ROW_WIN
                blocks.append(full_t[:, start + kr - r + NA_ROW_WIN - 1] if in_rows else masked)
            key_rows.append(jnp.concatenate(blocks, axis=2))
        tabs.append(jnp.concatenate(key_rows, axis=1))
    return jnp.stack(tabs, axis=0)


def _na_kernel(q3_ref, k_ref, v3_ref, tab_ref, o_ref, *, n_rows):
    i = pl.program_id(2)
    g = NA_GROUP_ROWS
    slabs_per_group = g * GRID_W // LANES
    groups = q3_ref.shape[0] // slabs_per_group
    win = NA_GROUP_WIN * GRID_W
    tokens = g * GRID_W
    half = NA_ROW_WIN // 2
    n_top = half // g
    chan = lax.broadcasted_iota(I32, (2 * NA_HEAD_DIM, tokens), 0)
    first_head = chan < NA_HEAD_DIM

    def scores(jj):
        r0 = (i * groups + jj) * g
        start = jnp.minimum(jnp.clip(r0 - half, 0, n_rows - NA_ROW_WIN), n_rows - NA_GROUP_WIN)
        cfg = jnp.where(r0 < half, r0 // g,
                        jnp.where(r0 >= n_rows - half, n_top + 1 + (r0 - (n_rows - half)) // g, n_top))
        kw = k_ref[pl.ds(pl.multiple_of(start * GRID_W, LANES), win), :]
        qt = jnp.concatenate([q3_ref[jj * slabs_per_group + t] for t in range(slabs_per_group)], axis=1)
        zero = jnp.zeros_like(qt)
        q_bd = jnp.concatenate([jnp.where(first_head, qt, zero), jnp.where(first_head, zero, qt)], axis=1)
        s = _dot(kw, q_bd) + jnp.concatenate([tab_ref[cfg, 0], tab_ref[cfg, 1]], axis=1)
        return s, start

    ones_rows = jnp.ones((NA_SUM_ROWS, win), BF)

    def outputs(jj, p, start):
        slab = start // 2
        vw = jnp.concatenate([v3_ref[slab + t] for t in range(NA_GROUP_WIN // 2)] , axis=1)
        o_l = _dot(jnp.concatenate([vw, ones_rows], axis=0), p)
        o_t = o_l[:2 * NA_HEAD_DIM] / o_l[2 * NA_HEAD_DIM:2 * NA_HEAD_DIM + 1]
        o = jnp.where(first_head, o_t[:, :tokens], o_t[:, tokens:])
        o_ref[jj * tokens:(jj + 1) * tokens, :] = jnp.transpose(o).astype(BF)

    pending = None
    for jj in range(groups):
        s, start = scores(jj)
        if pending is not None:
            outputs(*pending)
        p = jnp.exp2(s - jnp.max(s, axis=0, keepdims=True))
        pending = (jj, p.astype(BF), start)
    outputs(*pending)


def _natten(q3, k, v3, rpb, batch, seq):
    assert 2 * GRID_W == LANES and 2 * NA_HEAD_DIM == LANES and NA_GROUP_ROWS % 2 == 0
    n_rows = seq // GRID_W
    slabs = n_rows // 2
    step_slabs = min(NA_GROUPS_PER_STEP * NA_GROUP_ROWS // 2, slabs)
    steps = slabs // step_slabs
    hp = NA_HEADS // 2
    tab = _na_bias_table(rpb, n_rows)
    return pl.pallas_call(
        functools.partial(_na_kernel, n_rows=n_rows),
        grid=(batch, hp, steps),
        in_specs=[pl.BlockSpec((step_slabs, LANES, LANES), lambda b, p, i: (b * steps + i, p, 0)),
                  pl.BlockSpec((seq, LANES), lambda b, p, i: (b, p)),
                  pl.BlockSpec((slabs, LANES, LANES), lambda b, p, i: (b, p, 0)),
                  pl.BlockSpec((tab.shape[0], 2) + tab.shape[2:], lambda b, p, i: (0, p, 0, 0))],
        out_specs=pl.BlockSpec((step_slabs * LANES, LANES), lambda b, p, i: (b * steps + i, p)),
        out_shape=jax.ShapeDtypeStruct((batch * seq, NA_DIM), BF),
        compiler_params=_cparams("parallel", "parallel", "parallel"),
        name="natten",
    )(q3, k, v3, tab)


def _merge_kernel(x_ref, fo_ref, na_ref, mo_ref, wg_ref, bg_ref, wf_ref, wn_ref, wm_ref, wo_ref,
                  g_ref, b_ref, ht_ref):
    d = x_ref.shape[1]
    x = x_ref[...]
    xb = x.astype(BF)
    merged = None
    for j, (br_ref, w_ref) in enumerate(((fo_ref, wf_ref), (na_ref, wn_ref), (mo_ref, wm_ref))):
        z = _dot(xb, wg_ref[:, j * d:(j + 1) * d]) + bg_ref[:, j * d:(j + 1) * d]
        gate = 1.0 / (1.0 + jnp.exp(-z))
        term = gate * _dot(br_ref[...].astype(BF), w_ref[...])
        merged = term if merged is None else merged + term
    mix = _dot(merged.astype(BF), wo_ref[...])
    _store_tile_rows(ht_ref, _layer_norm(ALPHA * x + mix, g_ref[...], b_ref[...]))


def _merge(x2d, fo, na, mo, w_gate, b_gate, w_fo, w_na, w_mo, w_out, ln_g, ln_b):
    t, d = x2d.shape
    assert d == SUBLANES * LANES
    tm = TM_MERGE
    tok = lambda i: (i, 0)
    full = lambda a: pl.BlockSpec(a.shape, lambda i: (0, 0))
    return pl.pallas_call(
        _merge_kernel,
        grid=(t // tm,),
        in_specs=[pl.BlockSpec((tm, d), tok)] + [pl.BlockSpec((tm, 512), tok)] * 3
                 + [full(a) for a in (w_gate, b_gate, w_fo, w_na, w_mo, w_out, ln_g, ln_b)],
        out_specs=pl.BlockSpec((tm * SUBLANES, LANES), tok),
        out_shape=jax.ShapeDtypeStruct((t * SUBLANES, LANES), F32),
        compiler_params=_cparams("parallel"),
        name="merge",
    )(x2d, fo, na, mo, w_gate, b_gate, w_fo, w_na, w_mo, w_out, ln_g, ln_b)


def _router_kernel(ht_ref, whi_ref, wlo_ref, b_ref, oi_ref, of_ref, cnt_ref, carry_ref):
    tm = ht_ref.shape[0] // SUBLANES

    @pl.when(pl.program_id(0) == 0)
    def _():
        carry_ref[...] = jnp.zeros_like(carry_ref)

    h = _load_tile_rows(ht_ref)
    hh = h.astype(BF)
    hl = (h - hh.astype(F32)).astype(BF)
    logits = (_dot_nt(whi_ref[...], hh) + _dot_nt(wlo_ref[...], hh) + _dot_nt(whi_ref[...], hl)
              + b_ref[...])
    gl = logits[:N_GROUPS, :]
    g_row = lax.broadcasted_iota(I32, (N_GROUPS, tm), 0)
    gmax = jnp.max(gl, axis=0, keepdims=True)
    g_idx = jnp.min(jnp.where(gl == gmax, g_row, N_GROUPS), axis=0, keepdims=True)
    p_group = 1.0 / jnp.sum(jnp.exp(gl - gmax), axis=0, keepdims=True)
    e_row = lax.broadcasted_iota(I32, (N_EXPERTS, tm), 0)
    el = jnp.where((e_row >> 3) == g_idx, logits[N_GROUPS:N_GROUPS + N_EXPERTS, :], MASK_NEG)
    v1 = jnp.max(el, axis=0, keepdims=True)
    i1 = jnp.min(jnp.where(el == v1, e_row, N_EXPERTS), axis=0, keepdims=True)
    el2 = jnp.where(e_row == i1, MASK_NEG, el)
    v2 = jnp.max(el2, axis=0, keepdims=True)
    i2 = jnp.min(jnp.where(el2 == v2, e_row, N_EXPERTS), axis=0, keepdims=True)
    tt = jnp.exp(v2 - v1)
    w1 = p_group / (1.0 + tt)
    w2 = p_group * tt / (1.0 + tt)
    sel1 = e_row == i1
    sel2 = e_row == i2
    onehot = jnp.where(sel1 | sel2, 1.0, 0.0)
    t_in = lax.broadcasted_iota(I32, (tm, tm), 0)
    t_out = lax.broadcasted_iota(I32, (tm, tm), 1)
    tri = jnp.where(t_in < t_out, 1.0, 0.0).astype(BF)
    prefix = _dot(onehot.astype(BF), tri) + carry_ref[...]
    r1 = jnp.sum(jnp.where(sel1, prefix, 0.0), axis=0, keepdims=True)
    r2 = jnp.sum(jnp.where(sel2, prefix, 0.0), axis=0, keepdims=True)
    carry_ref[...] += jnp.sum(onehot, axis=1, keepdims=True)
    cnt_ref[...] = carry_ref[...]
    sub = pl.program_id(0) % ROUTE_STEPS_PER_PLANE
    for f, vals in enumerate((i1, i2, r1.astype(I32), r2.astype(I32))):
        for c in range(tm // LANES):
            oi_ref[f, pl.ds(sub * (tm // LANES) + c, 1), :] = vals[:, c * LANES:(c + 1) * LANES]
    w_row = lax.broadcasted_iota(I32, (LANES, tm), 0)
    w_rows = jnp.where(w_row == 0, w1, jnp.where(w_row == 1, w2, 0.0))
    of_ref[...] = jnp.transpose(w_rows)[:, :8]


def _router(ht, w_rg, b_rg, w_re, b_re):
    t = ht.shape[0] // SUBLANES
    d = w_rg.shape[0]
    tm = TM_ROUTE
    pad = ROUTE_ROWS - N_GROUPS - N_EXPERTS
    w = jnp.concatenate([w_rg.astype(F32), w_re.astype(F32), jnp.zeros((d, pad), F32)], axis=1).T
    b = jnp.concatenate([b_rg.astype(F32), b_re.astype(F32), jnp.zeros((pad,), F32)])[:, None]
    w_hi = w.astype(BF)
    w_lo = (w - w_hi.astype(F32)).astype(BF)
    tok = lambda i: (i, 0)
    const = lambda i: (0, 0)
    return pl.pallas_call(
        _router_kernel,
        grid=(t // tm,),
        in_specs=[pl.BlockSpec((tm * SUBLANES, LANES), tok),
                  pl.BlockSpec((ROUTE_ROWS, d), const),
                  pl.BlockSpec((ROUTE_ROWS, d), const),
                  pl.BlockSpec((ROUTE_ROWS, 1), const)],
        out_specs=[pl.BlockSpec((4, SUBLANES, LANES), lambda i: (0, i // ROUTE_STEPS_PER_PLANE, 0)),
                   pl.BlockSpec((tm, 8), tok),
                   pl.BlockSpec((N_EXPERTS, 1), const)],
        out_shape=[jax.ShapeDtypeStruct((4, t // LANES, LANES), I32), jax.ShapeDtypeStruct((t, 8), F32),
                   jax.ShapeDtypeStruct((N_EXPERTS, 1), F32)],
        scratch_shapes=[pltpu.VMEM((N_EXPERTS, 1), F32)],
        compiler_params=_cparams("arbitrary"),
        name="router",
    )(ht, w_hi, w_lo, b)


def _row_groups(n_rows, copies_of_row, start):
    def group(g, c):
        copies = []
        for u in range(ROW_GROUP):
            copies.extend(enumerate(copies_of_row(g * ROW_GROUP + u)))
        for k, (src, dst, sem) in copies:
            if start:
                pltpu.async_copy(src, dst, sem, priority=k % DMA_QUEUES)
            else:
                pltpu.make_async_copy(src, dst, sem).wait()
        return c

    lax.fori_loop(0, n_rows // ROW_GROUP, group, 0)


def _dests_kernel(ps_ref, idx_ref, d_ref):
    for a in range(2):
        e = idx_ref[a]
        d = idx_ref[2 + a]
        for k in range(N_EXPERTS):
            d = d + jnp.where(e == k, ps_ref[k], 0)
        d_ref[a] = d


def _dests(idx, pad_start):
    planes = idx.shape[1]
    out = pl.pallas_call(
        _dests_kernel,
        grid_spec=pltpu.PrefetchScalarGridSpec(
            num_scalar_prefetch=1,
            grid=(1,),
            in_specs=[pl.BlockSpec(idx.shape, lambda i, ps: (0, 0, 0))],
            out_specs=pl.BlockSpec((2, planes, LANES), lambda i, ps: (0, 0, 0)),
        ),
        out_shape=jax.ShapeDtypeStruct((2, planes, LANES), I32),
        compiler_params=_cparams("arbitrary"),
        name="dests",
    )(pad_start, idx)
    return out[0].reshape(-1), out[1].reshape(-1)


def _pack_words(val):
    words = []
    for c in range(PACK_ROWS):
        hi = pltpu.bitcast(val[:, c * LANES:(c + 1) * LANES].astype(BF).astype(F32), U32)
        lo = pltpu.bitcast(val[:, (c + PACK_ROWS) * LANES:(c + PACK_ROWS + 1) * LANES].astype(BF).astype(F32), U32)
        words.append(hi | (lo >> 16))
    return words


def _unpack_words(words):
    his = [pltpu.bitcast(w & jnp.uint32(0xFFFF0000), F32) for w in words]
    los = [pltpu.bitcast(w << 16, F32) for w in words]
    return jnp.concatenate(his + los, axis=1)


def _load_packed(ref):
    n = ref.shape[0] // PACK_ROWS
    return [ref[pl.ds(c, n, stride=PACK_ROWS), :] for c in range(PACK_ROWS)]


def _store_packed(ref, words):
    n = ref.shape[0] // PACK_ROWS
    for c, w in enumerate(words):
        ref[pl.ds(c, n, stride=PACK_ROWS), :] = w


def _packed_row_copy(src_ref, src_row, dst_ref, dst_row, sem):
    src = pl.multiple_of(src_row * PACK_ROWS, PACK_ROWS)
    dst = pl.multiple_of(dst_row * PACK_ROWS, PACK_ROWS)
    return src_ref.at[pl.ds(src, PACK_ROWS)], dst_ref.at[pl.ds(dst, PACK_ROWS)], sem


def _dispatch_kernel(ps_ref, pe_ref, d1_ref, d2_ref, ht_ref, xg_ref, pk, zbuf, sem, zsem):
    i = pl.program_id(0)
    tm = d1_ref.shape[0]
    blk = MOE_BLOCK * PACK_ROWS
    n_blocks = xg_ref.shape[0] // blk

    @pl.when(i == 0)
    def _():
        zbuf[...] = jnp.zeros_like(zbuf)

        def zero_copy(block):
            return pltpu.make_async_copy(zbuf, xg_ref.at[pl.ds(pl.multiple_of(block * blk, blk), blk)], zsem)

        def seg_issue(e, c):
            @pl.when(pe_ref[e] > ps_ref[e])
            def _():
                zero_copy(pe_ref[e] // MOE_BLOCK - 1).start()
            return c

        def seg_drain(e, c):
            @pl.when(pe_ref[e] > ps_ref[e])
            def _():
                zero_copy(pe_ref[e] // MOE_BLOCK - 1).wait()
            return c

        def tail_issue(b, c):
            zero_copy(b).start()
            return c

        def tail_drain(b, c):
            zero_copy(b).wait()
            return c

        first_tail = pe_ref[N_EXPERTS - 1] // MOE_BLOCK
        lax.fori_loop(0, N_EXPERTS, seg_issue, 0)
        lax.fori_loop(first_tail, n_blocks, tail_issue, 0)
        lax.fori_loop(0, N_EXPERTS, seg_drain, 0)
        lax.fori_loop(first_tail, n_blocks, tail_drain, 0)

    _store_packed(pk, _pack_words(_load_tile_rows(ht_ref)))
    _row_groups(tm, lambda t: [_packed_row_copy(pk, t, xg_ref, d1_ref[t], sem),
                               _packed_row_copy(pk, t, xg_ref, d2_ref[t], sem)], True)
    _row_groups(tm, lambda t: [_packed_row_copy(pk, 0, xg_ref, 0, sem)] * 2, False)


def _dispatch(ht, d1, d2, pad_start, pad_end, n_pad):
    t = d1.shape[0]
    tm = TM_DISPATCH
    smem = lambda: pl.BlockSpec((tm,), lambda i, ps, pe: (i,), memory_space=pltpu.SMEM)
    return pl.pallas_call(
        _dispatch_kernel,
        grid_spec=pltpu.PrefetchScalarGridSpec(
            num_scalar_prefetch=2,
            grid=(t // tm,),
            in_specs=[smem(), smem(), pl.BlockSpec((tm * SUBLANES, LANES), lambda i, ps, pe: (i, 0))],
            out_specs=pl.BlockSpec(memory_space=pl.ANY),
            scratch_shapes=[pltpu.VMEM((tm * PACK_ROWS, LANES), U32), pltpu.VMEM((MOE_BLOCK * PACK_ROWS, LANES), U32),
                            pltpu.SemaphoreType.DMA(()), pltpu.SemaphoreType.DMA(())],
        ),
        out_shape=jax.ShapeDtypeStruct((n_pad * PACK_ROWS, LANES), U32),
        compiler_params=_cparams("arbitrary"),
        name="dispatch",
    )(pad_start, pad_end, d1, d2, ht)


def _expert_kernel(be_ref, nx_ref, nv_ref, nu_ref, x_ref, wg_hbm, wu_hbm, wd_hbm, y_ref,
                   wg_f, wu_f, wd_f, wg_s, wu_s, wd_s, sems):
    j = pl.program_id(0)
    stage = ((wg_hbm, wg_f, wg_s), (wu_hbm, wu_f, wu_s), (wd_hbm, wd_f, wd_s))

    def weight_copies(e):
        return [pltpu.make_async_copy(hbm.at[e], buf, sems.at[k]) for k, (hbm, buf, _) in enumerate(stage)]

    @pl.when(j < nu_ref[0])
    def _():
        e = be_ref[j]

        @pl.when(j == 0)
        def _():
            for cp in weight_copies(e):
                cp.start()

        @pl.when((j == 0) | (e != be_ref[jnp.maximum(j - 1, 0)]))
        def _():
            for cp, (_, buf, dst) in zip(weight_copies(e), stage):
                cp.wait()
                dst[...] = buf[...].astype(BF)
            nxt = nx_ref[e]

            @pl.when(nxt >= 0)
            def _():
                for cp in weight_copies(nxt):
                    cp.start()

        part_rows = x_ref.shape[0] // EXPERT_PARTS
        rows = part_rows // EXPERT_SUBBLOCKS

        def part(x_part, y_part):
            def up(h):
                xb = _unpack_words(_load_packed(x_part.at[pl.ds(h * rows, rows)])).astype(BF)
                a = _dot(xb, wg_s[...])
                u = _dot(xb, wu_s[...])
                return ((a / (1.0 + jnp.exp(-a))) * u).astype(BF)

            mids = [up(h) for h in range(EXPERT_SUBBLOCKS)]
            for h, mid in enumerate(mids):
                _store_packed(y_part.at[pl.ds(h * rows, rows)], _pack_words(_dot(mid, wd_s[...])))

        for q in range(EXPERT_PARTS):
            x_part = x_ref.at[pl.ds(q * part_rows, part_rows)]
            y_part = y_ref.at[pl.ds(q * part_rows, part_rows)]
            has_tokens = nv_ref[j] > q * (MOE_BLOCK // EXPERT_PARTS)
            pl.when(has_tokens)(functools.partial(part, x_part, y_part))

            @pl.when(jnp.logical_not(has_tokens))
            def _():
                y_part[...] = jnp.zeros_like(y_part)

    @pl.when(j >= nu_ref[0])
    def _():
        y_ref[...] = jnp.zeros_like(y_ref)


def _experts(xg, blk_expert, next_expert, blk_valid, n_used, w_eg, w_eu, w_ed):
    blk = MOE_BLOCK * PACK_ROWS
    nb = xg.shape[0] // blk
    d, de = w_eg.shape[1], w_eg.shape[2]
    rows = lambda j, be, nx, nv, nu: (jnp.minimum(j, nu[0] - 1), 0)
    hbm = pl.BlockSpec(memory_space=pl.ANY)
    return pl.pallas_call(
        _expert_kernel,
        grid_spec=pltpu.PrefetchScalarGridSpec(
            num_scalar_prefetch=4,
            grid=(nb,),
            in_specs=[pl.BlockSpec((blk, LANES), rows), hbm, hbm, hbm],
            out_specs=pl.BlockSpec((blk, LANES), lambda j, be, nx, nv, nu: (j, 0)),
            scratch_shapes=[pltpu.VMEM((d, de), F32), pltpu.VMEM((d, de), F32), pltpu.VMEM((de, d), F32),
                            pltpu.VMEM((d, de), BF), pltpu.VMEM((d, de), BF), pltpu.VMEM((de, d), BF),
                            pltpu.SemaphoreType.DMA((3,))],
        ),
        out_shape=jax.ShapeDtypeStruct(xg.shape, U32),
        compiler_params=_cparams("arbitrary"),
        name="experts",
    )(blk_expert, next_expert, blk_valid, n_used, xg, w_eg, w_eu, w_ed)


def _combine_kernel(d1_ref, d2_ref, d1_next_ref, d2_next_ref, wt_ref, ht_ref, yb_ref, g_ref, b_ref, o_ref,
                    buf, sems):
    i = pl.program_id(0)
    n = pl.num_programs(0)
    tm = d1_ref.shape[0]

    def gather_tile(da_ref, db_ref, slot, start):
        def copies(t):
            if not start:
                return [_packed_row_copy(yb_ref, 0, buf.at[slot, a], 0, sems.at[slot]) for a in range(2)]
            return [_packed_row_copy(yb_ref, da_ref[t], buf.at[slot, 0], t, sems.at[slot]),
                    _packed_row_copy(yb_ref, db_ref[t], buf.at[slot, 1], t, sems.at[slot])]
        _row_groups(tm, copies, start)

    def gathered(slot, a):
        return _unpack_words(_load_packed(buf.at[slot, a]))

    slot = i % 2

    @pl.when(i == 0)
    def _():
        gather_tile(d1_ref, d2_ref, 0, True)

    @pl.when(i + 1 < n)
    def _():
        gather_tile(d1_next_ref, d2_next_ref, 1 - slot, True)

    gather_tile(d1_ref, d2_ref, slot, False)
    wt = wt_ref[...]
    ffn = wt[:, 0:1] * gathered(slot, 0) + wt[:, 1:2] * gathered(slot, 1)
    o_ref[...] = _layer_norm(ALPHA * _load_tile_rows(ht_ref) + ffn, g_ref[...], b_ref[...])


def _combine(ht, yb, d1, d2, wts, ln_g, ln_b):
    t = d1.shape[0]
    d = SUBLANES * LANES
    tm = TM_COMBINE
    n = t // tm
    tok = lambda i: (i, 0)
    const = lambda i: (0, 0)
    cur = lambda: pl.BlockSpec((tm,), lambda i: (i,), memory_space=pltpu.SMEM)
    nxt = lambda: pl.BlockSpec((tm,), lambda i: (jnp.minimum(i + 1, n - 1),), memory_space=pltpu.SMEM)
    return pl.pallas_call(
        _combine_kernel,
        grid=(n,),
        in_specs=[cur(), cur(), nxt(), nxt(),
                  pl.BlockSpec((tm, 8), tok),
                  pl.BlockSpec((tm * SUBLANES, LANES), tok),
                  pl.BlockSpec(memory_space=pl.ANY),
                  pl.BlockSpec((1, d), const), pl.BlockSpec((1, d), const)],
        out_specs=pl.BlockSpec((tm, d), tok),
        out_shape=jax.ShapeDtypeStruct((t, d), F32),
        scratch_shapes=[pltpu.VMEM((2, 2, tm * PACK_ROWS, LANES), U32), pltpu.SemaphoreType.DMA((2,))],
        compiler_params=_cparams("arbitrary"),
        name="combine",
    )(d1, d2, d1, d2, wts, ht, yb, ln_g, ln_b)


def _token_mixing(x2d, mem2d, batch, seq, m_len, w_in, w_gate, b_gate, w_mem_kv, rpb, w_fo, w_na, w_mo, w_out,
                  ln_g, ln_b):
    k_mem, v_mem = _memkv(mem2d, w_mem_kv.astype(BF), m_len)
    u_f, k, q3, v3, mo = _proj(x2d, w_in, k_mem, v_mem, seq, m_len)
    fo = _fourier(u_f, batch, seq)
    na = _natten(q3, k, v3, rpb, batch, seq)
    return _merge(x2d, fo, na, mo, w_gate.astype(BF), b_gate.astype(F32)[None, :], w_fo.astype(BF),
                  w_na.astype(BF), w_mo.astype(BF), w_out.astype(BF), ln_g.astype(F32)[None, :],
                  ln_b.astype(F32)[None, :])


def _moe(ht, w_rg, b_rg, w_re, b_re, w_eg, w_eu, w_ed, ln_g, ln_b):
    t = ht.shape[0] // SUBLANES
    idx, wts, cnt = _router(ht, w_rg, b_rg, w_re, b_re)
    counts = cnt[:, 0].astype(I32)
    padded = (counts + MOE_BLOCK - 1) // MOE_BLOCK * MOE_BLOCK
    pad_end = jnp.cumsum(padded).astype(I32)
    pad_start = pad_end - padded
    n_pad = 2 * t + N_EXPERTS * MOE_BLOCK
    nb = n_pad // MOE_BLOCK
    blk_start = jnp.arange(nb, dtype=I32) * MOE_BLOCK
    blk_expert = jnp.minimum(jnp.sum((pad_end[None, :] <= blk_start[:, None]).astype(I32), axis=1), N_EXPERTS - 1)
    n_used = pad_end[-1:] // MOE_BLOCK
    seg_left = counts[None, :] - (blk_start[:, None] - pad_start[None, :])
    own = blk_expert[:, None] == jnp.arange(N_EXPERTS, dtype=I32)[None, :]
    blk_valid = jnp.clip(jnp.sum(jnp.where(own, seg_left, 0), axis=1), 0, MOE_BLOCK).astype(I32)
    ids = jnp.arange(N_EXPERTS, dtype=I32)
    later_used = (padded[None, :] > 0) & (ids[None, :] > ids[:, None])
    next_expert = jnp.min(jnp.where(later_used, ids[None, :], N_EXPERTS), axis=1)
    next_expert = jnp.where(next_expert == N_EXPERTS, -1, next_expert).astype(I32)
    d1, d2 = _dests(idx, pad_start)
    xg = _dispatch(ht, d1, d2, pad_start, pad_end, n_pad)
    yb = _experts(xg, blk_expert, next_expert, blk_valid, n_used, w_eg, w_eu, w_ed)
    return _combine(ht, yb, d1, d2, wts, ln_g.astype(F32)[None, :], ln_b.astype(F32)[None, :])


def kernel(x, mem, w_in, w_gate, b_gate, w_mem_kv, rpb, w_fourier_o, w_na_o, w_mem_o, w_out, ln1_g, ln1_b,
           w_router_group, b_router_group, w_router_expert, b_router_expert, w_exp_gate, w_exp_up,
           w_exp_down, ln2_g, ln2_b):
    batch, seq, d = x.shape
    m_len = mem.shape[1]
    x2d = x.reshape(batch * seq, d)
    mem2d = mem.reshape(batch * m_len, d)
    for l in range(w_in.shape[0]):
        ht = _token_mixing(x2d, mem2d, batch, seq, m_len, w_in[l], w_gate[l], b_gate[l], w_mem_kv[l], rpb[l],
                           w_fourier_o[l], w_na_o[l], w_mem_o[l], w_out[l], ln1_g[l], ln1_b[l])
        x2d = _moe(ht, w_router_group[l], b_router_group[l], w_router_expert[l], b_router_expert[l],
                   w_exp_gate[l], w_exp_up[l], w_exp_down[l], ln2_g[l], ln2_b[l])
    return x2d.reshape(batch, seq, d)
```

```python
import functools

import numpy as np
import jax
import jax.numpy as jnp
from jax import lax
from jax.experimental import pallas as pl
from jax.experimental.pallas import tpu as pltpu

BF = jnp.bfloat16
F32 = jnp.float32
I32 = jnp.int32
U32 = jnp.uint32

GRID_W = 64
MEM_HEADS = 4
MEM_HEAD_DIM = 128
MEM_DIM = MEM_HEADS * MEM_HEAD_DIM
FOURIER_GROUPS = 4
FOURIER_GROUP_DIM = 128
FOURIER_DIM = FOURIER_GROUPS * FOURIER_GROUP_DIM
NA_HEADS = 8
NA_HEAD_DIM = 64
NA_DIM = NA_HEADS * NA_HEAD_DIM
NA_ROW_WIN = 8
NA_COL_WIN = 16
N_GROUPS = 8
EXPERTS_PER_GROUP = 8
N_EXPERTS = N_GROUPS * EXPERTS_PER_GROUP
DEPTH = 1
ALPHA = (2.0 * DEPTH) ** 0.25
LN_EPS = 1e-5
NA_SCALE = NA_HEAD_DIM ** -0.5
LOG2_E = 1.4426950408889634
MEM_SCALE = MEM_HEAD_DIM ** -0.5
MASK_NEG = -1e30

LANES = 128
SUBLANES = 8
FFT_N2 = 128
TM_PROJ = 512
TM_MERGE = 512
MERGE_SUBTILES = 2
TM_ROUTE = 1024
TM_DISPATCH = 1024
TM_COMBINE = 256
NA_GROUP_ROWS = 4
NA_GROUP_WIN = NA_GROUP_ROWS + NA_ROW_WIN
NA_GROUPS_PER_STEP = 8
NA_SUM_ROWS = 16
F1_POS = 16
F2_SLABS = 8
MOE_BLOCK = 512
EXPERT_PARTS = 2
PACK_ROWS = 4
EXPERT_SUBBLOCKS = 2
ROUTE_ROWS = 128
ROUTE_STEPS_PER_PLANE = SUBLANES * LANES // TM_ROUTE
DMA_QUEUES = 2
ROW_GROUP = 16
VMEM_LIMIT = 56 * 1024 * 1024


def _cparams(*sem):
    return pltpu.CompilerParams(dimension_semantics=sem, vmem_limit_bytes=VMEM_LIMIT)


def _dot(a, b):
    return jnp.dot(a, b, preferred_element_type=F32)


def _dot_nt(a, b):
    return lax.dot_general(a, b, (((1,), (1,)), ((), ())), preferred_element_type=F32)


def _layer_norm(y, g, b):
    mu = jnp.mean(y, axis=-1, keepdims=True)
    yc = y - mu
    var = jnp.mean(yc * yc, axis=-1, keepdims=True)
    return yc * lax.rsqrt(var + LN_EPS) * g + b


def _store_tile_rows(ref, val):
    n = val.shape[0]
    for s in range(SUBLANES):
        ref[pl.ds(s, n, stride=SUBLANES), :] = val[:, s * LANES:(s + 1) * LANES]


def _load_tile_rows(ref):
    n = ref.shape[0] // SUBLANES
    return jnp.concatenate([ref[pl.ds(s, n, stride=SUBLANES), :] for s in range(SUBLANES)], axis=1)


def _memkv_kernel(mem_ref, w_ref, k_ref, v_ref):
    kv = _dot(mem_ref[...].astype(BF), w_ref[...])
    k_ref[...] = kv[:, :MEM_DIM].astype(BF)
    v_ref[...] = kv[:, MEM_DIM:].astype(BF)


def _memkv(mem2d, w_kv, m_len):
    rows, d = mem2d.shape
    return pl.pallas_call(
        _memkv_kernel,
        grid=(rows // m_len,),
        in_specs=[pl.BlockSpec((m_len, d), lambda i: (i, 0)),
                  pl.BlockSpec((d, 2 * MEM_DIM), lambda i: (0, 0))],
        out_specs=[pl.BlockSpec((m_len, MEM_DIM), lambda i: (i, 0))] * 2,
        out_shape=[jax.ShapeDtypeStruct((rows, MEM_DIM), BF)] * 2,
        compiler_params=_cparams("parallel"),
        name="memkv",
    )(mem2d, w_kv)


def _proj_kernel(x_ref, w_ref, wt_ref, km_ref, vm_ref, uf_ref, k_ref, q3_ref, v3_ref, mo_ref):
    xb = x_ref[...].astype(BF)

    def seg(j):
        return _dot(xb, w_ref[:, j * 512:(j + 1) * 512])

    uf_ref[...] = seg(0)
    k_ref[...] = seg(1).astype(BF)
    q_t = _dot_nt(wt_ref[:NA_DIM, :], xb) * (NA_SCALE * LOG2_E)
    v_t = _dot_nt(wt_ref[NA_DIM:, :], xb)
    for s in range(q3_ref.shape[0]):
        q3_ref[s] = q_t[:, s * LANES:(s + 1) * LANES].astype(BF)
        v3_ref[s] = v_t[:, s * LANES:(s + 1) * LANES].astype(BF)
    qm = seg(2).astype(BF)
    for h in range(MEM_HEADS):
        sl = slice(h * MEM_HEAD_DIM, (h + 1) * MEM_HEAD_DIM)
        s = _dot_nt(qm[:, sl], km_ref[:, sl]) * MEM_SCALE
        m = jnp.max(s, axis=-1, keepdims=True)
        p = jnp.exp(s - m)
        l = jnp.sum(p, axis=-1, keepdims=True)
        o = _dot(p.astype(BF), vm_ref[:, sl])
        mo_ref[:, sl] = (o / l).astype(BF)


def _proj(x2d, w_in, k_mem, v_mem, seq, m_len):
    t, d = x2d.shape
    tm = TM_PROJ
    tiles_per_batch = seq // tm
    w_tok = jnp.concatenate([w_in[:, 0:512], w_in[:, 1024:1536], w_in[:, 2048:2560]], axis=1).astype(BF)
    w_chan = jnp.concatenate([w_in[:, 512:1024], w_in[:, 1536:2048]], axis=1).T.astype(BF)
    tok = lambda i: (i, 0)
    slab = lambda i: (i, 0, 0)
    memb = lambda i: (i // tiles_per_batch, 0)
    const = lambda i: (0, 0)
    tok_out = pl.BlockSpec((tm, 512), tok)
    slab_out = pl.BlockSpec((tm // LANES, NA_DIM, LANES), slab)
    slab_shape = jax.ShapeDtypeStruct((t // LANES, NA_DIM, LANES), BF)
    return pl.pallas_call(
        _proj_kernel,
        grid=(t // tm,),
        in_specs=[pl.BlockSpec((tm, d), tok),
                  pl.BlockSpec(w_tok.shape, const),
                  pl.BlockSpec(w_chan.shape, const),
                  pl.BlockSpec((m_len, MEM_DIM), memb),
                  pl.BlockSpec((m_len, MEM_DIM), memb)],
        out_specs=[tok_out, tok_out, slab_out, slab_out, tok_out],
        out_shape=[jax.ShapeDtypeStruct((t, 512), F32), jax.ShapeDtypeStruct((t, 512), BF)] + [slab_shape] * 2
                  + [jax.ShapeDtypeStruct((t, 512), BF)],
        compiler_params=_cparams("parallel"),
        name="proj",
    )(x2d, w_tok, w_chan, k_mem, v_mem)


def _fourier_tables(n1, n2):
    n = n1 * n2
    k1 = np.arange(n1)
    ang1 = 2.0 * np.pi * ((k1[:, None] * k1[None, :]) % n1) / n1
    norm = 1.0 / np.sqrt(float(n) * FOURIER_GROUP_DIM)
    m1 = np.concatenate([np.cos(ang1), -np.sin(ang1)], axis=0) * norm
    kk = k1[:, None, None] + n1 * np.arange(n2)[None, :, None]
    nn = np.arange(n2)[None, None, :]
    ang2 = 2.0 * np.pi * ((kk * nn) % n) / n
    c2, s2 = np.cos(ang2), np.sin(ang2)
    m2 = np.concatenate([np.concatenate([c2, s2], axis=2),
                         np.concatenate([-s2, c2], axis=2)], axis=1)
    c = np.arange(FOURIER_GROUP_DIM)
    angc = 2.0 * np.pi * ((c[:, None] * c[None, :]) % FOURIER_GROUP_DIM) / FOURIER_GROUP_DIM
    mc = np.concatenate([np.cos(angc), np.sin(angc)], axis=0)
    as_bf = lambda a: jnp.asarray(a.astype(np.float32)).astype(BF)
    return as_bf(m1), as_bf(m2), as_bf(mc)


def _f1_kernel(u_ref, m_ref, a_ref):
    n1, pos, _ = u_ref.shape
    x = jnp.concatenate([u_ref[:, p, :] for p in range(pos)], axis=1).astype(BF)
    r = _dot(m_ref[...], x)
    a_ref[0] = r[:n1].astype(BF)
    a_ref[1] = r[n1:].astype(BF)


def _f2_kernel(a_ref, m2_ref, mc_ref, o_ref):
    slabs = a_ref.shape[1]
    n2 = a_ref.shape[2]
    for t in range(slabs):
        a = jnp.concatenate([a_ref[0, t], a_ref[1, t]], axis=0)
        y = _dot(m2_ref[t], a)
        yr = y[:n2].astype(BF)
        yi = y[n2:].astype(BF)
        yy = jnp.concatenate([jnp.concatenate([yr[:, g * LANES:(g + 1) * LANES], yi[:, g * LANES:(g + 1) * LANES]],
                                              axis=1) for g in range(FOURIER_GROUPS)], axis=0)
        z = _dot(yy, mc_ref[...])
        for g in range(FOURIER_GROUPS):
            o_ref[:, t, g * LANES:(g + 1) * LANES] = z[g * n2:(g + 1) * n2]


def _fourier(u_f, batch, seq):
    n2 = FFT_N2
    n1 = seq // n2
    m1, m2, mc = _fourier_tables(n1, n2)
    cols = n2 * FOURIER_DIM
    pos = min(F1_POS, n2)
    a = pl.pallas_call(
        _f1_kernel,
        grid=(batch, n2 // pos),
        in_specs=[pl.BlockSpec((None, n1, pos, FOURIER_DIM), lambda b, j: (b, 0, j, 0)),
                  pl.BlockSpec((2 * n1, n1), lambda b, j: (0, 0))],
        out_specs=pl.BlockSpec((None, 2, n1, pos * FOURIER_DIM), lambda b, j: (b, 0, 0, j)),
        out_shape=jax.ShapeDtypeStruct((batch, 2, n1, cols), BF),
        compiler_params=_cparams("parallel", "parallel"),
        name="fourier1",
    )(u_f.reshape(batch, n1, n2, FOURIER_DIM), m1)
    kb = min(F2_SLABS, n1)
    out = pl.pallas_call(
        _f2_kernel,
        grid=(batch, n1 // kb),
        in_specs=[pl.BlockSpec((None, 2, kb, n2, FOURIER_DIM), lambda b, j: (b, 0, j, 0, 0)),
                  pl.BlockSpec((kb, 2 * n2, 2 * n2), lambda b, j: (j, 0, 0)),
                  pl.BlockSpec((2 * FOURIER_GROUP_DIM, FOURIER_GROUP_DIM), lambda b, j: (0, 0))],
        out_specs=pl.BlockSpec((None, n2, kb, FOURIER_DIM), lambda b, j: (b, 0, j, 0)),
        out_shape=jax.ShapeDtypeStruct((batch, n2, n1, FOURIER_DIM), F32),
        compiler_params=_cparams("parallel", "parallel"),
        name="fourier2",
    )(a.reshape(batch, 2, n1, n2, FOURIER_DIM), m2, mc)
    return out.reshape(batch * seq, FOURIER_DIM)


def _na_group_start(r0, n_rows):
    rs0 = min(max(r0 - NA_ROW_WIN // 2, 0), n_rows - NA_ROW_WIN)
    return min(rs0, n_rows - NA_GROUP_WIN)


def _na_bias_table(rpb, n_rows):
    g = NA_GROUP_ROWS
    assert n_rows >= 2 * NA_GROUP_WIN and n_rows % g == 0 and (NA_ROW_WIN // 2) % g == 0
    cols = np.arange(GRID_W)
    cs = np.clip(cols - NA_COL_WIN // 2, 0, GRID_W - NA_COL_WIN)
    kc = np.arange(GRID_W)
    in_win = (kc[None, :] >= cs[:, None]) & (kc[None, :] < cs[:, None] + NA_COL_WIN)
    dc = kc[None, :] - cols[:, None] + (NA_COL_WIN - 1)
    n_dc = 2 * NA_COL_WIN - 1
    pick = ((dc.reshape(-1)[None, :] == np.arange(n_dc)[:, None]) & in_win.reshape(-1)[None, :])
    picked = jnp.dot(rpb.astype(F32).reshape(-1, n_dc), jnp.asarray(pick.astype(np.float32)),
                     precision=lax.Precision.HIGHEST)
    picked = picked.reshape(NA_HEADS, 2 * NA_ROW_WIN - 1, GRID_W, GRID_W)
    full_t = jnp.where(in_win[None, None], picked * LOG2_E, MASK_NEG).transpose(0, 1, 3, 2)
    masked = jnp.full((NA_HEADS, GRID_W, GRID_W), MASK_NEG, F32)
    half = NA_ROW_WIN // 2
    group_rows = list(range(0, half, g)) + [half] + list(range(n_rows - half, n_rows, g))
    tabs = []
    for r0 in group_rows:
        start = _na_group_start(r0, n_rows)
        key_rows = []
        for kr in range(NA_GROUP_WIN):
            blocks = []
            for rr in range(g):
                r = r0 + rr
                rs = min(max(r - half, 0), n_rows - NA_ROW_WIN)
                in_rows = rs <= start + kr < rs + NA_ROW_WIN
                blocks.append(full_t[:, start + kr - r + NA_ROW_WIN - 1] if in_rows else masked)
            key_rows.append(jnp.concatenate(blocks, axis=2))
        tabs.append(jnp.concatenate(key_rows, axis=1))
    return jnp.stack(tabs, axis=0)


def _na_kernel(q3_ref, k_ref, v3_ref, tab_ref, o_ref, *, n_rows):
    i = pl.program_id(2)
    g = NA_GROUP_ROWS
    slabs_per_group = g * GRID_W // LANES
    groups = q3_ref.shape[0] // slabs_per_group
    win = NA_GROUP_WIN * GRID_W
    tokens = g * GRID_W
    half = NA_ROW_WIN // 2
    n_top = half // g
    chan = lax.broadcasted_iota(I32, (2 * NA_HEAD_DIM, tokens), 0)
    first_head = chan < NA_HEAD_DIM

    def scores(jj):
        r0 = (i * groups + jj) * g
        start = jnp.minimum(jnp.clip(r0 - half, 0, n_rows - NA_ROW_WIN), n_rows - NA_GROUP_WIN)
        cfg = jnp.where(r0 < half, r0 // g,
                        jnp.where(r0 >= n_rows - half, n_top + 1 + (r0 - (n_rows - half)) // g, n_top))
        kw = k_ref[pl.ds(pl.multiple_of(start * GRID_W, LANES), win), :]
        qt = jnp.concatenate([q3_ref[jj * slabs_per_group + t] for t in range(slabs_per_group)], axis=1)
        zero = jnp.zeros_like(qt)
        q_bd = jnp.concatenate([jnp.where(first_head, qt, zero), jnp.where(first_head, zero, qt)], axis=1)
        s = _dot(kw, q_bd) + jnp.concatenate([tab_ref[cfg, 0], tab_ref[cfg, 1]], axis=1)
        return s, start

    ones_rows = jnp.ones((NA_SUM_ROWS, win), BF)

    def outputs(jj, p, start):
        slab = start // 2
        vw = jnp.concatenate([v3_ref[slab + t] for t in range(NA_GROUP_WIN // 2)] , axis=1)
        o_l = _dot(jnp.concatenate([vw, ones_rows], axis=0), p)
        o_t = o_l[:2 * NA_HEAD_DIM] / o_l[2 * NA_HEAD_DIM:2 * NA_HEAD_DIM + 1]
        o = jnp.where(first_head, o_t[:, :tokens], o_t[:, tokens:])
        o_ref[jj * tokens:(jj + 1) * tokens, :] = jnp.transpose(o).astype(BF)

    pending = None
    for jj in range(groups):
        s, start = scores(jj)
        if pending is not None:
            outputs(*pending)
        p = jnp.exp2(s - jnp.max(s, axis=0, keepdims=True))
        pending = (jj, p.astype(BF), start)
    outputs(*pending)


def _natten(q3, k, v3, rpb, batch, seq):
    assert 2 * GRID_W == LANES and 2 * NA_HEAD_DIM == LANES and NA_GROUP_ROWS % 2 == 0
    n_rows = seq // GRID_W
    slabs = n_rows // 2
    step_slabs = min(NA_GROUPS_PER_STEP * NA_GROUP_ROWS // 2, slabs)
    steps = slabs // step_slabs
    hp = NA_HEADS // 2
    tab = _na_bias_table(rpb, n_rows)
    return pl.pallas_call(
        functools.partial(_na_kernel, n_rows=n_rows),
        grid=(batch, hp, steps),
        in_specs=[pl.BlockSpec((step_slabs, LANES, LANES), lambda b, p, i: (b * steps + i, p, 0)),
                  pl.BlockSpec((seq, LANES), lambda b, p, i: (b, p)),
                  pl.BlockSpec((slabs, LANES, LANES), lambda b, p, i: (b, p, 0)),
                  pl.BlockSpec((tab.shape[0], 2) + tab.shape[2:], lambda b, p, i: (0, p, 0, 0))],
        out_specs=pl.BlockSpec((step_slabs * LANES, LANES), lambda b, p, i: (b * steps + i, p)),
        out_shape=jax.ShapeDtypeStruct((batch * seq, NA_DIM), BF),
        compiler_params=_cparams("parallel", "parallel", "parallel"),
        name="natten",
    )(q3, k, v3, tab)


def _merge_kernel(x_ref, fo_ref, na_ref, mo_ref, wg_ref, bg_ref, wf_ref, wn_ref, wm_ref, wo_ref,
                  g_ref, b_ref, ht_ref):
    d = x_ref.shape[1]
    rows = x_ref.shape[0] // MERGE_SUBTILES

    def pre_norm(h):
        sl = pl.ds(h * rows, rows)
        x = x_ref[sl, :]
        xb = x.astype(BF)
        merged = None
        for j, (br_ref, w_ref) in enumerate(((fo_ref, wf_ref), (na_ref, wn_ref), (mo_ref, wm_ref))):
            z = _dot(xb, wg_ref[:, j * d:(j + 1) * d]) + bg_ref[:, j * d:(j + 1) * d]
            gate = 1.0 / (1.0 + jnp.exp(-z))
            term = gate * _dot(br_ref[sl, :].astype(BF), w_ref[...])
            merged = term if merged is None else merged + term
        return ALPHA * x + _dot(merged.astype(BF), wo_ref[...])

    ys = [pre_norm(h) for h in range(MERGE_SUBTILES)]
    for h, y in enumerate(ys):
        _store_tile_rows(ht_ref.at[pl.ds(h * rows * SUBLANES, rows * SUBLANES)],
                         _layer_norm(y, g_ref[...], b_ref[...]))


def _merge(x2d, fo, na, mo, w_gate, b_gate, w_fo, w_na, w_mo, w_out, ln_g, ln_b):
    t, d = x2d.shape
    assert d == SUBLANES * LANES
    tm = TM_MERGE
    tok = lambda i: (i, 0)
    full = lambda a: pl.BlockSpec(a.shape, lambda i: (0, 0))
    return pl.pallas_call(
        _merge_kernel,
        grid=(t // tm,),
        in_specs=[pl.BlockSpec((tm, d), tok)] + [pl.BlockSpec((tm, 512), tok)] * 3
                 + [full(a) for a in (w_gate, b_gate, w_fo, w_na, w_mo, w_out, ln_g, ln_b)],
        out_specs=pl.BlockSpec((tm * SUBLANES, LANES), tok),
        out_shape=jax.ShapeDtypeStruct((t * SUBLANES, LANES), F32),
        compiler_params=_cparams("parallel"),
        name="merge",
    )(x2d, fo, na, mo, w_gate, b_gate, w_fo, w_na, w_mo, w_out, ln_g, ln_b)


def _router_kernel(ht_ref, whi_ref, wlo_ref, b_ref, oi_ref, of_ref, cnt_ref, carry_ref):
    tm = ht_ref.shape[0] // SUBLANES

    @pl.when(pl.program_id(0) == 0)
    def _():
        carry_ref[...] = jnp.zeros_like(carry_ref)

    h = _load_tile_rows(ht_ref)
    hh = h.astype(BF)
    hl = (h - hh.astype(F32)).astype(BF)
    logits = (_dot_nt(whi_ref[...], hh) + _dot_nt(wlo_ref[...], hh) + _dot_nt(whi_ref[...], hl)
              + b_ref[...])
    gl = logits[:N_GROUPS, :]
    g_row = lax.broadcasted_iota(I32, (N_GROUPS, tm), 0)
    gmax = jnp.max(gl, axis=0, keepdims=True)
    g_idx = jnp.min(jnp.where(gl == gmax, g_row, N_GROUPS), axis=0, keepdims=True)
    p_group = 1.0 / jnp.sum(jnp.exp(gl - gmax), axis=0, keepdims=True)
    e_row = lax.broadcasted_iota(I32, (N_EXPERTS, tm), 0)
    el = jnp.where((e_row >> 3) == g_idx, logits[N_GROUPS:N_GROUPS + N_EXPERTS, :], MASK_NEG)
    v1 = jnp.max(el, axis=0, keepdims=True)
    i1 = jnp.min(jnp.where(el == v1, e_row, N_EXPERTS), axis=0, keepdims=True)
    el2 = jnp.where(e_row == i1, MASK_NEG, el)
    v2 = jnp.max(el2, axis=0, keepdims=True)
    i2 = jnp.min(jnp.where(el2 == v2, e_row, N_EXPERTS), axis=0, keepdims=True)
    tt = jnp.exp(v2 - v1)
    w1 = p_group / (1.0 + tt)
    w2 = p_group * tt / (1.0 + tt)
    sel1 = e_row == i1
    sel2 = e_row == i2
    onehot = jnp.where(sel1 | sel2, 1.0, 0.0)
    t_in = lax.broadcasted_iota(I32, (tm, tm), 0)
    t_out = lax.broadcasted_iota(I32, (tm, tm), 1)
    tri = jnp.where(t_in < t_out, 1.0, 0.0).astype(BF)
    prefix = _dot(onehot.astype(BF), tri) + carry_ref[...]
    r1 = jnp.sum(jnp.where(sel1, prefix, 0.0), axis=0, keepdims=True)
    r2 = jnp.sum(jnp.where(sel2, prefix, 0.0), axis=0, keepdims=True)
    carry_ref[...] += jnp.sum(onehot, axis=1, keepdims=True)
    cnt_ref[...] = carry_ref[...]
    sub = pl.program_id(0) % ROUTE_STEPS_PER_PLANE
    for f, vals in enumerate((i1, i2, r1.astype(I32), r2.astype(I32))):
        for c in range(tm // LANES):
            oi_ref[f, pl.ds(sub * (tm // LANES) + c, 1), :] = vals[:, c * LANES:(c + 1) * LANES]
    w_row = lax.broadcasted_iota(I32, (LANES, tm), 0)
    w_rows = jnp.where(w_row == 0, w1, jnp.where(w_row == 1, w2, 0.0))
    of_ref[...] = jnp.transpose(w_rows)[:, :8]


def _router(ht, w_rg, b_rg, w_re, b_re):
    t = ht.shape[0] // SUBLANES
    d = w_rg.shape[0]
    tm = TM_ROUTE
    pad = ROUTE_ROWS - N_GROUPS - N_EXPERTS
    w = jnp.concatenate([w_rg.astype(F32), w_re.astype(F32), jnp.zeros((d, pad), F32)], axis=1).T
    b = jnp.concatenate([b_rg.astype(F32), b_re.astype(F32), jnp.zeros((pad,), F32)])[:, None]
    w_hi = w.astype(BF)
    w_lo = (w - w_hi.astype(F32)).astype(BF)
    tok = lambda i: (i, 0)
    const = lambda i: (0, 0)
    return pl.pallas_call(
        _router_kernel,
        grid=(t // tm,),
        in_specs=[pl.BlockSpec((tm * SUBLANES, LANES), tok),
                  pl.BlockSpec((ROUTE_ROWS, d), const),
                  pl.BlockSpec((ROUTE_ROWS, d), const),
                  pl.BlockSpec((ROUTE_ROWS, 1), const)],
        out_specs=[pl.BlockSpec((4, SUBLANES, LANES), lambda i: (0, i // ROUTE_STEPS_PER_PLANE, 0)),
                   pl.BlockSpec((tm, 8), tok),
                   pl.BlockSpec((N_EXPERTS, 1), const)],
        out_shape=[jax.ShapeDtypeStruct((4, t // LANES, LANES), I32), jax.ShapeDtypeStruct((t, 8), F32),
                   jax.ShapeDtypeStruct((N_EXPERTS, 1), F32)],
        scratch_shapes=[pltpu.VMEM((N_EXPERTS, 1), F32)],
        compiler_params=_cparams("arbitrary"),
        name="router",
    )(ht, w_hi, w_lo, b)


def _row_groups(n_rows, copies_of_row, start):
    def group(g, c):
        copies = []
        for u in range(ROW_GROUP):
            copies.extend(copies_of_row(g * ROW_GROUP + u))
        for k, (src, dst, sem) in enumerate(copies):
            if start:
                pltpu.async_copy(src, dst, sem, priority=k % DMA_QUEUES)
            else:
                pltpu.make_async_copy(src, dst, sem).wait()
        return c

    lax.fori_loop(0, n_rows // ROW_GROUP, group, 0)


def _dests_kernel(ps_ref, idx_ref, d_ref):
    for a in range(2):
        e = idx_ref[a]
        d = idx_ref[2 + a]
        for k in range(N_EXPERTS):
            d = d + jnp.where(e == k, ps_ref[k], 0)
        d_ref[a] = d


def _dests(idx, pad_start):
    planes = idx.shape[1]
    out = pl.pallas_call(
        _dests_kernel,
        grid_spec=pltpu.PrefetchScalarGridSpec(
            num_scalar_prefetch=1,
            grid=(1,),
            in_specs=[pl.BlockSpec(idx.shape, lambda i, ps: (0, 0, 0))],
            out_specs=pl.BlockSpec((2, planes, LANES), lambda i, ps: (0, 0, 0)),
        ),
        out_shape=jax.ShapeDtypeStruct((2, planes, LANES), I32),
        compiler_params=_cparams("arbitrary"),
        name="dests",
    )(pad_start, idx)
    return out[0].reshape(-1), out[1].reshape(-1)


def _pack_words(val):
    words = []
    for c in range(PACK_ROWS):
        hi = pltpu.bitcast(val[:, c * LANES:(c + 1) * LANES].astype(BF).astype(F32), U32)
        lo = pltpu.bitcast(val[:, (c + PACK_ROWS) * LANES:(c + PACK_ROWS + 1) * LANES].astype(BF).astype(F32), U32)
        words.append(hi | (lo >> 16))
    return words


def _unpack_words(words):
    his = [pltpu.bitcast(w & jnp.uint32(0xFFFF0000), F32) for w in words]
    los = [pltpu.bitcast(w << 16, F32) for w in words]
    return jnp.concatenate(his + los, axis=1)


def _load_packed(ref):
    n = ref.shape[0] // PACK_ROWS
    return [ref[pl.ds(c, n, stride=PACK_ROWS), :] for c in range(PACK_ROWS)]


def _store_packed(ref, words):
    n = ref.shape[0] // PACK_ROWS
    for c, w in enumerate(words):
        ref[pl.ds(c, n, stride=PACK_ROWS), :] = w


def _packed_row_copy(src_ref, src_row, dst_ref, dst_row, sem):
    src = pl.multiple_of(src_row * PACK_ROWS, PACK_ROWS)
    dst = pl.multiple_of(dst_row * PACK_ROWS, PACK_ROWS)
    return src_ref.at[pl.ds(src, PACK_ROWS)], dst_ref.at[pl.ds(dst, PACK_ROWS)], sem


def _dispatch_kernel(ps_ref, pe_ref, d1_ref, d2_ref, ht_ref, xg_ref, pk, zbuf, sem, zsem):
    i = pl.program_id(0)
    tm = d1_ref.shape[0]
    blk = MOE_BLOCK * PACK_ROWS
    n_blocks = xg_ref.shape[0] // blk

    @pl.when(i == 0)
    def _():
        zbuf[...] = jnp.zeros_like(zbuf)

        def zero_copy(block):
            return pltpu.make_async_copy(zbuf, xg_ref.at[pl.ds(pl.multiple_of(block * blk, blk), blk)], zsem)

        def seg_issue(e, c):
            @pl.when(pe_ref[e] > ps_ref[e])
            def _():
                zero_copy(pe_ref[e] // MOE_BLOCK - 1).start()
            return c

        def seg_drain(e, c):
            @pl.when(pe_ref[e] > ps_ref[e])
            def _():
                zero_copy(pe_ref[e] // MOE_BLOCK - 1).wait()
            return c

        def tail_issue(b, c):
            zero_copy(b).start()
            return c

        def tail_drain(b, c):
            zero_copy(b).wait()
            return c

        first_tail = pe_ref[N_EXPERTS - 1] // MOE_BLOCK
        lax.fori_loop(0, N_EXPERTS, seg_issue, 0)
        lax.fori_loop(first_tail, n_blocks, tail_issue, 0)
        lax.fori_loop(0, N_EXPERTS, seg_drain, 0)
        lax.fori_loop(first_tail, n_blocks, tail_drain, 0)

    _store_packed(pk, _pack_words(_load_tile_rows(ht_ref)))
    _row_groups(tm, lambda t: [_packed_row_copy(pk, t, xg_ref, d1_ref[t], sem),
                               _packed_row_copy(pk, t, xg_ref, d2_ref[t], sem)], True)
    _row_groups(tm, lambda t: [_packed_row_copy(pk, 0, xg_ref, 0, sem)] * 2, False)


def _dispatch(ht, d1, d2, pad_start, pad_end, n_pad):
    t = d1.shape[0]
    tm = TM_DISPATCH
    smem = lambda: pl.BlockSpec((tm,), lambda i, ps, pe: (i,), memory_space=pltpu.SMEM)
    return pl.pallas_call(
        _dispatch_kernel,
        grid_spec=pltpu.PrefetchScalarGridSpec(
            num_scalar_prefetch=2,
            grid=(t // tm,),
            in_specs=[smem(), smem(), pl.BlockSpec((tm * SUBLANES, LANES), lambda i, ps, pe: (i, 0))],
            out_specs=pl.BlockSpec(memory_space=pl.ANY),
            scratch_shapes=[pltpu.VMEM((tm * PACK_ROWS, LANES), U32), pltpu.VMEM((MOE_BLOCK * PACK_ROWS, LANES), U32),
                            pltpu.SemaphoreType.DMA(()), pltpu.SemaphoreType.DMA(())],
        ),
        out_shape=jax.ShapeDtypeStruct((n_pad * PACK_ROWS, LANES), U32),
        compiler_params=_cparams("arbitrary"),
        name="dispatch",
    )(pad_start, pad_end, d1, d2, ht)


def _expert_kernel(be_ref, nx_ref, nv_ref, nu_ref, x_ref, wg_hbm, wu_hbm, wd_hbm, y_ref,
                   wg_f, wu_f, wd_f, wg_s, wu_s, wd_s, sems):
    j = pl.program_id(0)
    stage = ((wg_hbm, wg_f, wg_s), (wu_hbm, wu_f, wu_s), (wd_hbm, wd_f, wd_s))

    def weight_copies(e):
        return [pltpu.make_async_copy(hbm.at[e], buf, sems.at[k]) for k, (hbm, buf, _) in enumerate(stage)]

    @pl.when(j < nu_ref[0])
    def _():
        e = be_ref[j]

        @pl.when(j == 0)
        def _():
            for cp in weight_copies(e):
                cp.start()

        @pl.when((j == 0) | (e != be_ref[jnp.maximum(j - 1, 0)]))
        def _():
            for cp, (_, buf, dst) in zip(weight_copies(e), stage):
                cp.wait()
                dst[...] = buf[...].astype(BF)
            nxt = nx_ref[e]

            @pl.when(nxt >= 0)
            def _():
                for cp in weight_copies(nxt):
                    cp.start()

        part_rows = x_ref.shape[0] // EXPERT_PARTS
        rows = part_rows // EXPERT_SUBBLOCKS

        def part(x_part, y_part):
            def up(h):
                xb = _unpack_words(_load_packed(x_part.at[pl.ds(h * rows, rows)])).astype(BF)
                a = _dot(xb, wg_s[...])
                u = _dot(xb, wu_s[...])
                return ((a / (1.0 + jnp.exp(-a))) * u).astype(BF)

            mids = [up(h) for h in range(EXPERT_SUBBLOCKS)]
            for h, mid in enumerate(mids):
                _store_packed(y_part.at[pl.ds(h * rows, rows)], _pack_words(_dot(mid, wd_s[...])))

        for q in range(EXPERT_PARTS):
            x_part = x_ref.at[pl.ds(q * part_rows, part_rows)]
            y_part = y_ref.at[pl.ds(q * part_rows, part_rows)]
            has_tokens = nv_ref[j] > q * (MOE_BLOCK // EXPERT_PARTS)
            pl.when(has_tokens)(functools.partial(part, x_part, y_part))

            @pl.when(jnp.logical_not(has_tokens))
            def _():
                y_part[...] = jnp.zeros_like(y_part)

    @pl.when(j >= nu_ref[0])
    def _():
        y_ref[...] = jnp.zeros_like(y_ref)


def _experts(xg, blk_expert, next_expert, blk_valid, n_used, w_eg, w_eu, w_ed):
    blk = MOE_BLOCK * PACK_ROWS
    nb = xg.shape[0] // blk
    d, de = w_eg.shape[1], w_eg.shape[2]
    rows = lambda j, be, nx, nv, nu: (jnp.minimum(j, nu[0] - 1), 0)
    hbm = pl.BlockSpec(memory_space=pl.ANY)
    return pl.pallas_call(
        _expert_kernel,
        grid_spec=pltpu.PrefetchScalarGridSpec(
            num_scalar_prefetch=4,
            grid=(nb,),
            in_specs=[pl.BlockSpec((blk, LANES), rows), hbm, hbm, hbm],
            out_specs=pl.BlockSpec((blk, LANES), lambda j, be, nx, nv, nu: (j, 0)),
            scratch_shapes=[pltpu.VMEM((d, de), F32), pltpu.VMEM((d, de), F32), pltpu.VMEM((de, d), F32),
                            pltpu.VMEM((d, de), BF), pltpu.VMEM((d, de), BF), pltpu.VMEM((de, d), BF),
                            pltpu.SemaphoreType.DMA((3,))],
        ),
        out_shape=jax.ShapeDtypeStruct(xg.shape, U32),
        compiler_params=_cparams("arbitrary"),
        name="experts",
    )(blk_expert, next_expert, blk_valid, n_used, xg, w_eg, w_eu, w_ed)


def _combine_kernel(d1_ref, d2_ref, d1_next_ref, d2_next_ref, wt_ref, ht_ref, yb_ref, g_ref, b_ref, o_ref,
                    buf, sems):
    i = pl.program_id(0)
    n = pl.num_programs(0)
    tm = d1_ref.shape[0]

    def gather_tile(da_ref, db_ref, slot, start):
        def copies(t):
            if not start:
                return [_packed_row_copy(yb_ref, 0, buf.at[slot, a], 0, sems.at[slot]) for a in range(2)]
            return [_packed_row_copy(yb_ref, da_ref[t], buf.at[slot, 0], t, sems.at[slot]),
                    _packed_row_copy(yb_ref, db_ref[t], buf.at[slot, 1], t, sems.at[slot])]
        _row_groups(tm, copies, start)

    def gathered(slot, a):
        return _unpack_words(_load_packed(buf.at[slot, a]))

    slot = i % 2

    @pl.when(i == 0)
    def _():
        gather_tile(d1_ref, d2_ref, 0, True)

    @pl.when(i + 1 < n)
    def _():
        gather_tile(d1_next_ref, d2_next_ref, 1 - slot, True)

    gather_tile(d1_ref, d2_ref, slot, False)
    wt = wt_ref[...]
    ffn = wt[:, 0:1] * gathered(slot, 0) + wt[:, 1:2] * gathered(slot, 1)
    o_ref[...] = _layer_norm(ALPHA * _load_tile_rows(ht_ref) + ffn, g_ref[...], b_ref[...])


def _combine(ht, yb, d1, d2, wts, ln_g, ln_b):
    t = d1.shape[0]
    d = SUBLANES * LANES
    tm = TM_COMBINE
    n = t // tm
    tok = lambda i: (i, 0)
    const = lambda i: (0, 0)
    cur = lambda: pl.BlockSpec((tm,), lambda i: (i,), memory_space=pltpu.SMEM)
    nxt = lambda: pl.BlockSpec((tm,), lambda i: (jnp.minimum(i + 1, n - 1),), memory_space=pltpu.SMEM)
    return pl.pallas_call(
        _combine_kernel,
        grid=(n,),
        in_specs=[cur(), cur(), nxt(), nxt(),
                  pl.BlockSpec((tm, 8), tok),
                  pl.BlockSpec((tm * SUBLANES, LANES), tok),
                  pl.BlockSpec(memory_space=pl.ANY),
                  pl.BlockSpec((1, d), const), pl.BlockSpec((1, d), const)],
        out_specs=pl.BlockSpec((tm, d), tok),
        out_shape=jax.ShapeDtypeStruct((t, d), F32),
        scratch_shapes=[pltpu.VMEM((2, 2, tm * PACK_ROWS, LANES), U32), pltpu.SemaphoreType.DMA((2,))],
        compiler_params=_cparams("arbitrary"),
        name="combine",
    )(d1, d2, d1, d2, wts, ht, yb, ln_g, ln_b)


def _token_mixing(x2d, mem2d, batch, seq, m_len, w_in, w_gate, b_gate, w_mem_kv, rpb, w_fo, w_na, w_mo, w_out,
                  ln_g, ln_b):
    k_mem, v_mem = _memkv(mem2d, w_mem_kv.astype(BF), m_len)
    u_f, k, q3, v3, mo = _proj(x2d, w_in, k_mem, v_mem, seq, m_len)
    fo = _fourier(u_f, batch, seq)
    na = _natten(q3, k, v3, rpb, batch, seq)
    return _merge(x2d, fo, na, mo, w_gate.astype(BF), b_gate.astype(F32)[None, :], w_fo.astype(BF),
                  w_na.astype(BF), w_mo.astype(BF), w_out.astype(BF), ln_g.astype(F32)[None, :],
                  ln_b.astype(F32)[None, :])


def _moe(ht, w_rg, b_rg, w_re, b_re, w_eg, w_eu, w_ed, ln_g, ln_b):
    t = ht.shape[0] // SUBLANES
    idx, wts, cnt = _router(ht, w_rg, b_rg, w_re, b_re)
    counts = cnt[:, 0].astype(I32)
    padded = (counts + MOE_BLOCK - 1) // MOE_BLOCK * MOE_BLOCK
    pad_end = jnp.cumsum(padded).astype(I32)
    pad_start = pad_end - padded
    n_pad = 2 * t + N_EXPERTS * MOE_BLOCK
    nb = n_pad // MOE_BLOCK
    blk_start = jnp.arange(nb, dtype=I32) * MOE_BLOCK
    blk_expert = jnp.minimum(jnp.sum((pad_end[None, :] <= blk_start[:, None]).astype(I32), axis=1), N_EXPERTS - 1)
    n_used = pad_end[-1:] // MOE_BLOCK
    seg_left = counts[None, :] - (blk_start[:, None] - pad_start[None, :])
    own = blk_expert[:, None] == jnp.arange(N_EXPERTS, dtype=I32)[None, :]
    blk_valid = jnp.clip(jnp.sum(jnp.where(own, seg_left, 0), axis=1), 0, MOE_BLOCK).astype(I32)
    ids = jnp.arange(N_EXPERTS, dtype=I32)
    later_used = (padded[None, :] > 0) & (ids[None, :] > ids[:, None])
    next_expert = jnp.min(jnp.where(later_used, ids[None, :], N_EXPERTS), axis=1)
    next_expert = jnp.where(next_expert == N_EXPERTS, -1, next_expert).astype(I32)
    d1, d2 = _dests(idx, pad_start)
    xg = _dispatch(ht, d1, d2, pad_start, pad_end, n_pad)
    yb = _experts(xg, blk_expert, next_expert, blk_valid, n_used, w_eg, w_eu, w_ed)
    return _combine(ht, yb, d1, d2, wts, ln_g.astype(F32)[None, :], ln_b.astype(F32)[None, :])


def kernel(x, mem, w_in, w_gate, b_gate, w_mem_kv, rpb, w_fourier_o, w_na_o, w_mem_o, w_out, ln1_g, ln1_b,
           w_router_group, b_router_group, w_router_expert, b_router_expert, w_exp_gate, w_exp_up,
           w_exp_down, ln2_g, ln2_b):
    batch, seq, d = x.shape
    m_len = mem.shape[1]
    x2d = x.reshape(batch * seq, d)
    mem2d = mem.reshape(batch * m_len, d)
    for l in range(w_in.shape[0]):
        ht = _token_mixing(x2d, mem2d, batch, seq, m_len, w_in[l], w_gate[l], b_gate[l], w_mem_kv[l], rpb[l],
                           w_fourier_o[l], w_na_o[l], w_mem_o[l], w_out[l], ln1_g[l], ln1_b[l])
        x2d = _moe(ht, w_router_group[l], b_router_group[l], w_router_expert[l], b_router_expert[l],
                   w_exp_gate[l], w_exp_up[l], w_exp_down[l], ln2_g[l], ln2_b[l])
    return x2d.reshape(batch, seq, d)
```

```python
import functools

import numpy as np
import jax
import jax.numpy as jnp
from jax import lax
from jax.experimental import pallas as pl
from jax.experimental.pallas import tpu as pltpu

BF = jnp.bfloat16
F32 = jnp.float32
I32 = jnp.int32
U32 = jnp.uint32

GRID_W = 64
MEM_HEADS = 4
MEM_HEAD_DIM = 128
MEM_DIM = MEM_HEADS * MEM_HEAD_DIM
FOURIER_GROUPS = 4
FOURIER_GROUP_DIM = 128
FOURIER_DIM = FOURIER_GROUPS * FOURIER_GROUP_DIM
NA_HEADS = 8
NA_HEAD_DIM = 64
NA_DIM = NA_HEADS * NA_HEAD_DIM
NA_ROW_WIN = 8
NA_COL_WIN = 16
N_GROUPS = 8
EXPERTS_PER_GROUP = 8
N_EXPERTS = N_GROUPS * EXPERTS_PER_GROUP
DEPTH = 1
ALPHA = (2.0 * DEPTH) ** 0.25
LN_EPS = 1e-5
NA_SCALE = NA_HEAD_DIM ** -0.5
LOG2_E = 1.4426950408889634
MEM_SCALE = MEM_HEAD_DIM ** -0.5
MASK_NEG = -1e30

LANES = 128
SUBLANES = 8
FFT_N2 = 128
TM_PROJ = 512
TM_MERGE = 512
MERGE_SUBTILES = 2
TM_ROUTE = 1024
TM_DISPATCH = 1024
TM_COMBINE = 256
NA_GROUP_ROWS = 4
NA_GROUP_WIN = NA_GROUP_ROWS + NA_ROW_WIN
NA_GROUPS_PER_STEP = 16
NA_SUM_ROWS = 16
F1_POS = 16
F2_SLABS = 8
MOE_BLOCK = 512
EXPERT_PARTS = 2
PACK_ROWS = 4
EXPERT_SUBBLOCKS = 2
ROUTE_ROWS = 128
ROUTE_STEPS_PER_PLANE = SUBLANES * LANES // TM_ROUTE
DMA_QUEUES = 2
ROW_GROUP = 16
VMEM_LIMIT = 56 * 1024 * 1024


def _cparams(*sem):
    return pltpu.CompilerParams(dimension_semantics=sem, vmem_limit_bytes=VMEM_LIMIT)


def _dot(a, b):
    return jnp.dot(a, b, preferred_element_type=F32)


def _dot_nt(a, b):
    return lax.dot_general(a, b, (((1,), (1,)), ((), ())), preferred_element_type=F32)


def _layer_norm(y, g, b):
    mu = jnp.mean(y, axis=-1, keepdims=True)
    yc = y - mu
    var = jnp.mean(yc * yc, axis=-1, keepdims=True)
    return yc * lax.rsqrt(var + LN_EPS) * g + b


def _store_tile_rows(ref, val):
    n = val.shape[0]
    for s in range(SUBLANES):
        ref[pl.ds(s, n, stride=SUBLANES), :] = val[:, s * LANES:(s + 1) * LANES]


def _load_tile_rows(ref):
    n = ref.shape[0] // SUBLANES
    return jnp.concatenate([ref[pl.ds(s, n, stride=SUBLANES), :] for s in range(SUBLANES)], axis=1)


def _memkv_kernel(mem_ref, w_ref, k_ref, v_ref):
    kv = _dot(mem_ref[...].astype(BF), w_ref[...])
    k_ref[...] = kv[:, :MEM_DIM].astype(BF)
    v_ref[...] = kv[:, MEM_DIM:].astype(BF)


def _memkv(mem2d, w_kv, m_len):
    rows, d = mem2d.shape
    return pl.pallas_call(
        _memkv_kernel,
        grid=(rows // m_len,),
        in_specs=[pl.BlockSpec((m_len, d), lambda i: (i, 0)),
                  pl.BlockSpec((d, 2 * MEM_DIM), lambda i: (0, 0))],
        out_specs=[pl.BlockSpec((m_len, MEM_DIM), lambda i: (i, 0))] * 2,
        out_shape=[jax.ShapeDtypeStruct((rows, MEM_DIM), BF)] * 2,
        compiler_params=_cparams("parallel"),
        name="memkv",
    )(mem2d, w_kv)


def _proj_kernel(x_ref, w_ref, wt_ref, km_ref, vm_ref, uf_ref, k_ref, q3_ref, v3_ref, mo_ref):
    xb = x_ref[...].astype(BF)

    def seg(j):
        return _dot(xb, w_ref[:, j * 512:(j + 1) * 512])

    uf_ref[...] = seg(0)
    k_ref[...] = seg(1).astype(BF)
    q_t = _dot_nt(wt_ref[:NA_DIM, :], xb) * (NA_SCALE * LOG2_E)
    v_t = _dot_nt(wt_ref[NA_DIM:, :], xb)
    for s in range(q3_ref.shape[0]):
        q3_ref[s] = q_t[:, s * LANES:(s + 1) * LANES].astype(BF)
        v3_ref[s] = v_t[:, s * LANES:(s + 1) * LANES].astype(BF)
    qm = seg(2).astype(BF)
    for h in range(MEM_HEADS):
        sl = slice(h * MEM_HEAD_DIM, (h + 1) * MEM_HEAD_DIM)
        s = _dot_nt(qm[:, sl], km_ref[:, sl]) * MEM_SCALE
        m = jnp.max(s, axis=-1, keepdims=True)
        p = jnp.exp(s - m)
        l = jnp.sum(p, axis=-1, keepdims=True)
        o = _dot(p.astype(BF), vm_ref[:, sl])
        mo_ref[:, sl] = (o / l).astype(BF)


def _proj(x2d, w_in, k_mem, v_mem, seq, m_len):
    t, d = x2d.shape
    tm = TM_PROJ
    tiles_per_batch = seq // tm
    w_tok = jnp.concatenate([w_in[:, 0:512], w_in[:, 1024:1536], w_in[:, 2048:2560]], axis=1).astype(BF)
    w_chan = jnp.concatenate([w_in[:, 512:1024], w_in[:, 1536:2048]], axis=1).T.astype(BF)
    tok = lambda i: (i, 0)
    slab = lambda i: (i, 0, 0)
    memb = lambda i: (i // tiles_per_batch, 0)
    const = lambda i: (0, 0)
    tok_out = pl.BlockSpec((tm, 512), tok)
    slab_out = pl.BlockSpec((tm // LANES, NA_DIM, LANES), slab)
    slab_shape = jax.ShapeDtypeStruct((t // LANES, NA_DIM, LANES), BF)
    return pl.pallas_call(
        _proj_kernel,
        grid=(t // tm,),
        in_specs=[pl.BlockSpec((tm, d), tok),
                  pl.BlockSpec(w_tok.shape, const),
                  pl.BlockSpec(w_chan.shape, const),
                  pl.BlockSpec((m_len, MEM_DIM), memb),
                  pl.BlockSpec((m_len, MEM_DIM), memb)],
        out_specs=[tok_out, tok_out, slab_out, slab_out, tok_out],
        out_shape=[jax.ShapeDtypeStruct((t, 512), F32), jax.ShapeDtypeStruct((t, 512), BF)] + [slab_shape] * 2
                  + [jax.ShapeDtypeStruct((t, 512), BF)],
        compiler_params=_cparams("parallel"),
        name="proj",
    )(x2d, w_tok, w_chan, k_mem, v_mem)


def _fourier_tables(n1, n2):
    n = n1 * n2
    k1 = np.arange(n1)
    ang1 = 2.0 * np.pi * ((k1[:, None] * k1[None, :]) % n1) / n1
    norm = 1.0 / np.sqrt(float(n) * FOURIER_GROUP_DIM)
    m1 = np.concatenate([np.cos(ang1), -np.sin(ang1)], axis=0) * norm
    kk = k1[:, None, None] + n1 * np.arange(n2)[None, :, None]
    nn = np.arange(n2)[None, None, :]
    ang2 = 2.0 * np.pi * ((kk * nn) % n) / n
    c2, s2 = np.cos(ang2), np.sin(ang2)
    m2 = np.concatenate([np.concatenate([c2, s2], axis=2),
                         np.concatenate([-s2, c2], axis=2)], axis=1)
    c = np.arange(FOURIER_GROUP_DIM)
    angc = 2.0 * np.pi * ((c[:, None] * c[None, :]) % FOURIER_GROUP_DIM) / FOURIER_GROUP_DIM
    mc = np.concatenate([np.cos(angc), np.sin(angc)], axis=0)
    as_bf = lambda a: jnp.asarray(a.astype(np.float32)).astype(BF)
    return as_bf(m1), as_bf(m2), as_bf(mc)


def _f1_kernel(u_ref, m_ref, a_ref):
    n1, pos, _ = u_ref.shape
    x = jnp.concatenate([u_ref[:, p, :] for p in range(pos)], axis=1).astype(BF)
    r = _dot(m_ref[...], x)
    a_ref[0] = r[:n1].astype(BF)
    a_ref[1] = r[n1:].astype(BF)


def _f2_kernel(a_ref, m2_ref, mc_ref, o_ref):
    slabs = a_ref.shape[1]
    n2 = a_ref.shape[2]
    for t in range(slabs):
        a = jnp.concatenate([a_ref[0, t], a_ref[1, t]], axis=0)
        y = _dot(m2_ref[t], a)
        yr = y[:n2].astype(BF)
        yi = y[n2:].astype(BF)
        yy = jnp.concatenate([jnp.concatenate([yr[:, g * LANES:(g + 1) * LANES], yi[:, g * LANES:(g + 1) * LANES]],
                                              axis=1) for g in range(FOURIER_GROUPS)], axis=0)
        z = _dot(yy, mc_ref[...])
        for g in range(FOURIER_GROUPS):
            o_ref[:, t, g * LANES:(g + 1) * LANES] = z[g * n2:(g + 1) * n2]


def _fourier(u_f, batch, seq):
    n2 = FFT_N2
    n1 = seq // n2
    m1, m2, mc = _fourier_tables(n1, n2)
    cols = n2 * FOURIER_DIM
    pos = min(F1_POS, n2)
    a = pl.pallas_call(
        _f1_kernel,
        grid=(batch, n2 // pos),
        in_specs=[pl.BlockSpec((None, n1, pos, FOURIER_DIM), lambda b, j: (b, 0, j, 0)),
                  pl.BlockSpec((2 * n1, n1), lambda b, j: (0, 0))],
        out_specs=pl.BlockSpec((None, 2, n1, pos * FOURIER_DIM), lambda b, j: (b, 0, 0, j)),
        out_shape=jax.ShapeDtypeStruct((batch, 2, n1, cols), BF),
        compiler_params=_cparams("parallel", "parallel"),
        name="fourier1",
    )(u_f.reshape(batch, n1, n2, FOURIER_DIM), m1)
    kb = min(F2_SLABS, n1)
    out = pl.pallas_call(
        _f2_kernel,
        grid=(batch, n1 // kb),
        in_specs=[pl.BlockSpec((None, 2, kb, n2, FOURIER_DIM), lambda b, j: (b, 0, j, 0, 0)),
                  pl.BlockSpec((kb, 2 * n2, 2 * n2), lambda b, j: (j, 0, 0)),
                  pl.BlockSpec((2 * FOURIER_GROUP_DIM, FOURIER_GROUP_DIM), lambda b, j: (0, 0))],
        out_specs=pl.BlockSpec((None, n2, kb, FOURIER_DIM), lambda b, j: (b, 0, j, 0)),
        out_shape=jax.ShapeDtypeStruct((batch, n2, n1, FOURIER_DIM), F32),
        compiler_params=_cparams("parallel", "parallel"),
        name="fourier2",
    )(a.reshape(batch, 2, n1, n2, FOURIER_DIM), m2, mc)
    return out.reshape(batch * seq, FOURIER_DIM)


def _na_group_start(r0, n_rows):
    rs0 = min(max(r0 - NA_ROW_WIN // 2, 0), n_rows - NA_ROW_WIN)
    return min(rs0, n_rows - NA_GROUP_WIN)


def _na_bias_table(rpb, n_rows):
    g = NA_GROUP_ROWS
    assert n_rows >= 2 * NA_GROUP_WIN and n_rows % g == 0 and (NA_ROW_WIN // 2) % g == 0
    cols = np.arange(GRID_W)
    cs = np.clip(cols - NA_COL_WIN // 2, 0, GRID_W - NA_COL_WIN)
    kc = np.arange(GRID_W)
    in_win = (kc[None, :] >= cs[:, None]) & (kc[None, :] < cs[:, None] + NA_COL_WIN)
    dc = kc[None, :] - cols[:, None] + (NA_COL_WIN - 1)
    n_dc = 2 * NA_COL_WIN - 1
    pick = ((dc.reshape(-1)[None, :] == np.arange(n_dc)[:, None]) & in_win.reshape(-1)[None, :])
    picked = jnp.dot(rpb.astype(F32).reshape(-1, n_dc), jnp.asarray(pick.astype(np.float32)),
                     precision=lax.Precision.HIGHEST)
    picked = picked.reshape(NA_HEADS, 2 * NA_ROW_WIN - 1, GRID_W, GRID_W)
    full_t = jnp.where(in_win[None, None], picked * LOG2_E, MASK_NEG).transpose(0, 1, 3, 2)
    masked = jnp.full((NA_HEADS, GRID_W, GRID_W), MASK_NEG, F32)
    half = NA_ROW_WIN // 2
    group_rows = list(range(0, half, g)) + [half] + list(range(n_rows - half, n_rows, g))
    tabs = []
    for r0 in group_rows:
        start = _na_group_start(r0, n_rows)
        key_rows = []
        for kr in range(NA_GROUP_WIN):
            blocks = []
            for rr in range(g):
                r = r0 + rr
                rs = min(max(r - half, 0), n_rows - NA_ROW_WIN)
                in_rows = rs <= start + kr < rs + NA_ROW_WIN
                blocks.append(full_t[:, start + kr - r + NA_ROW_WIN - 1] if in_rows else masked)
            key_rows.append(jnp.concatenate(blocks, axis=2))
        tabs.append(jnp.concatenate(key_rows, axis=1))
    return jnp.stack(tabs, axis=0)


def _na_kernel(q3_ref, k_ref, v3_ref, tab_ref, o_ref, *, n_rows):
    i = pl.program_id(2)
    g = NA_GROUP_ROWS
    slabs_per_group = g * GRID_W // LANES
    groups = q3_ref.shape[0] // slabs_per_group
    win = NA_GROUP_WIN * GRID_W
    tokens = g * GRID_W
    half = NA_ROW_WIN // 2
    n_top = half // g
    chan = lax.broadcasted_iota(I32, (2 * NA_HEAD_DIM, tokens), 0)
    first_head = chan < NA_HEAD_DIM

    def scores(jj):
        r0 = (i * groups + jj) * g
        start = jnp.minimum(jnp.clip(r0 - half, 0, n_rows - NA_ROW_WIN), n_rows - NA_GROUP_WIN)
        cfg = jnp.where(r0 < half, r0 // g,
                        jnp.where(r0 >= n_rows - half, n_top + 1 + (r0 - (n_rows - half)) // g, n_top))
        kw = k_ref[pl.ds(pl.multiple_of(start * GRID_W, LANES), win), :]
        qt = jnp.concatenate([q3_ref[jj * slabs_per_group + t] for t in range(slabs_per_group)], axis=1)
        zero = jnp.zeros_like(qt)
        q_bd = jnp.concatenate([jnp.where(first_head, qt, zero), jnp.where(first_head, zero, qt)], axis=1)
        s = _dot(kw, q_bd) + jnp.concatenate([tab_ref[cfg, 0], tab_ref[cfg, 1]], axis=1)
        return s, start

    ones_rows = jnp.ones((NA_SUM_ROWS, win), BF)

    def outputs(jj, p, start):
        slab = start // 2
        vw = jnp.concatenate([v3_ref[slab + t] for t in range(NA_GROUP_WIN // 2)] , axis=1)
        o_l = _dot(jnp.concatenate([vw, ones_rows], axis=0), p)
        o_t = o_l[:2 * NA_HEAD_DIM] / o_l[2 * NA_HEAD_DIM:2 * NA_HEAD_DIM + 1]
        o = jnp.where(first_head, o_t[:, :tokens], o_t[:, tokens:])
        o_ref[jj * tokens:(jj + 1) * tokens, :] = jnp.transpose(o).astype(BF)

    pending = None
    for jj in range(groups):
        s, start = scores(jj)
        if pending is not None:
            outputs(*pending)
        p = jnp.exp2(s - jnp.max(s, axis=0, keepdims=True))
        pending = (jj, p.astype(BF), start)
    outputs(*pending)


def _natten(q3, k, v3, rpb, batch, seq):
    assert 2 * GRID_W == LANES and 2 * NA_HEAD_DIM == LANES and NA_GROUP_ROWS % 2 == 0
    n_rows = seq // GRID_W
    slabs = n_rows // 2
    step_slabs = min(NA_GROUPS_PER_STEP * NA_GROUP_ROWS // 2, slabs)
    steps = slabs // step_slabs
    hp = NA_HEADS // 2
    tab = _na_bias_table(rpb, n_rows)
    return pl.pallas_call(
        functools.partial(_na_kernel, n_rows=n_rows),
        grid=(batch, hp, steps),
        in_specs=[pl.BlockSpec((step_slabs, LANES, LANES), lambda b, p, i: (b * steps + i, p, 0)),
                  pl.BlockSpec((seq, LANES), lambda b, p, i: (b, p)),
                  pl.BlockSpec((slabs, LANES, LANES), lambda b, p, i: (b, p, 0)),
                  pl.BlockSpec((tab.shape[0], 2) + tab.shape[2:], lambda b, p, i: (0, p, 0, 0))],
        out_specs=pl.BlockSpec((step_slabs * LANES, LANES), lambda b, p, i: (b * steps + i, p)),
        out_shape=jax.ShapeDtypeStruct((batch * seq, NA_DIM), BF),
        compiler_params=_cparams("parallel", "parallel", "parallel"),
        name="natten",
    )(q3, k, v3, tab)


def _merge_kernel(x_ref, fo_ref, na_ref, mo_ref, wg_ref, bg_ref, wf_ref, wn_ref, wm_ref, wo_ref,
                  g_ref, b_ref, ht_ref):
    d = x_ref.shape[1]
    rows = x_ref.shape[0] // MERGE_SUBTILES

    def pre_norm(h):
        sl = pl.ds(h * rows, rows)
        x = x_ref[sl, :]
        xb = x.astype(BF)
        merged = None
        for j, (br_ref, w_ref) in enumerate(((fo_ref, wf_ref), (na_ref, wn_ref), (mo_ref, wm_ref))):
            z = _dot(xb, wg_ref[:, j * d:(j + 1) * d]) + bg_ref[:, j * d:(j + 1) * d]
            gate = 1.0 / (1.0 + jnp.exp(-z))
            term = gate * _dot(br_ref[sl, :].astype(BF), w_ref[...])
            merged = term if merged is None else merged + term
        return ALPHA * x + _dot(merged.astype(BF), wo_ref[...])

    ys = [pre_norm(h) for h in range(MERGE_SUBTILES)]
    for h, y in enumerate(ys):
        _store_tile_rows(ht_ref.at[pl.ds(h * rows * SUBLANES, rows * SUBLANES)],
                         _layer_norm(y, g_ref[...], b_ref[...]))


def _merge(x2d, fo, na, mo, w_gate, b_gate, w_fo, w_na, w_mo, w_out, ln_g, ln_b):
    t, d = x2d.shape
    assert d == SUBLANES * LANES
    tm = TM_MERGE
    tok = lambda i: (i, 0)
    full = lambda a: pl.BlockSpec(a.shape, lambda i: (0, 0))
    return pl.pallas_call(
        _merge_kernel,
        grid=(t // tm,),
        in_specs=[pl.BlockSpec((tm, d), tok)] + [pl.BlockSpec((tm, 512), tok)] * 3
                 + [full(a) for a in (w_gate, b_gate, w_fo, w_na, w_mo, w_out, ln_g, ln_b)],
        out_specs=pl.BlockSpec((tm * SUBLANES, LANES), tok),
        out_shape=jax.ShapeDtypeStruct((t * SUBLANES, LANES), F32),
        compiler_params=_cparams("parallel"),
        name="merge",
    )(x2d, fo, na, mo, w_gate, b_gate, w_fo, w_na, w_mo, w_out, ln_g, ln_b)


def _router_kernel(ht_ref, whi_ref, wlo_ref, b_ref, oi_ref, of_ref, cnt_ref, carry_ref):
    tm = ht_ref.shape[0] // SUBLANES

    @pl.when(pl.program_id(0) == 0)
    def _():
        carry_ref[...] = jnp.zeros_like(carry_ref)

    h = _load_tile_rows(ht_ref)
    hh = h.astype(BF)
    hl = (h - hh.astype(F32)).astype(BF)
    logits = (_dot_nt(whi_ref[...], hh) + _dot_nt(wlo_ref[...], hh) + _dot_nt(whi_ref[...], hl)
              + b_ref[...])
    gl = logits[:N_GROUPS, :]
    g_row = lax.broadcasted_iota(I32, (N_GROUPS, tm), 0)
    gmax = jnp.max(gl, axis=0, keepdims=True)
    g_idx = jnp.min(jnp.where(gl == gmax, g_row, N_GROUPS), axis=0, keepdims=True)
    p_group = 1.0 / jnp.sum(jnp.exp(gl - gmax), axis=0, keepdims=True)
    e_row = lax.broadcasted_iota(I32, (N_EXPERTS, tm), 0)
    el = jnp.where((e_row >> 3) == g_idx, logits[N_GROUPS:N_GROUPS + N_EXPERTS, :], MASK_NEG)
    v1 = jnp.max(el, axis=0, keepdims=True)
    i1 = jnp.min(jnp.where(el == v1, e_row, N_EXPERTS), axis=0, keepdims=True)
    el2 = jnp.where(e_row == i1, MASK_NEG, el)
    v2 = jnp.max(el2, axis=0, keepdims=True)
    i2 = jnp.min(jnp.where(el2 == v2, e_row, N_EXPERTS), axis=0, keepdims=True)
    tt = jnp.exp(v2 - v1)
    w1 = p_group / (1.0 + tt)
    w2 = p_group * tt / (1.0 + tt)
    sel1 = e_row == i1
    sel2 = e_row == i2
    onehot = jnp.where(sel1 | sel2, 1.0, 0.0)
    t_in = lax.broadcasted_iota(I32, (tm, tm), 0)
    t_out = lax.broadcasted_iota(I32, (tm, tm), 1)
    tri = jnp.where(t_in < t_out, 1.0, 0.0).astype(BF)
    prefix = _dot(onehot.astype(BF), tri) + carry_ref[...]
    r1 = jnp.sum(jnp.where(sel1, prefix, 0.0), axis=0, keepdims=True)
    r2 = jnp.sum(jnp.where(sel2, prefix, 0.0), axis=0, keepdims=True)
    carry_ref[...] += jnp.sum(onehot, axis=1, keepdims=True)
    cnt_ref[...] = carry_ref[...]
    sub = pl.program_id(0) % ROUTE_STEPS_PER_PLANE
    for f, vals in enumerate((i1, i2, r1.astype(I32), r2.astype(I32))):
        for c in range(tm // LANES):
            oi_ref[f, pl.ds(sub * (tm // LANES) + c, 1), :] = vals[:, c * LANES:(c + 1) * LANES]
    w_row = lax.broadcasted_iota(I32, (LANES, tm), 0)
    w_rows = jnp.where(w_row == 0, w1, jnp.where(w_row == 1, w2, 0.0))
    of_ref[...] = jnp.transpose(w_rows)[:, :8]


def _router(ht, w_rg, b_rg, w_re, b_re):
    t = ht.shape[0] // SUBLANES
    d = w_rg.shape[0]
    tm = TM_ROUTE
    pad = ROUTE_ROWS - N_GROUPS - N_EXPERTS
    w = jnp.concatenate([w_rg.astype(F32), w_re.astype(F32), jnp.zeros((d, pad), F32)], axis=1).T
    b = jnp.concatenate([b_rg.astype(F32), b_re.astype(F32), jnp.zeros((pad,), F32)])[:, None]
    w_hi = w.astype(BF)
    w_lo = (w - w_hi.astype(F32)).astype(BF)
    tok = lambda i: (i, 0)
    const = lambda i: (0, 0)
    return pl.pallas_call(
        _router_kernel,
        grid=(t // tm,),
        in_specs=[pl.BlockSpec((tm * SUBLANES, LANES), tok),
                  pl.BlockSpec((ROUTE_ROWS, d), const),
                  pl.BlockSpec((ROUTE_ROWS, d), const),
                  pl.BlockSpec((ROUTE_ROWS, 1), const)],
        out_specs=[pl.BlockSpec((4, SUBLANES, LANES), lambda i: (0, i // ROUTE_STEPS_PER_PLANE, 0)),
                   pl.BlockSpec((tm, 8), tok),
                   pl.BlockSpec((N_EXPERTS, 1), const)],
        out_shape=[jax.ShapeDtypeStruct((4, t // LANES, LANES), I32), jax.ShapeDtypeStruct((t, 8), F32),
                   jax.ShapeDtypeStruct((N_EXPERTS, 1), F32)],
        scratch_shapes=[pltpu.VMEM((N_EXPERTS, 1), F32)],
        compiler_params=_cparams("arbitrary"),
        name="router",
    )(ht, w_hi, w_lo, b)


def _row_groups(n_rows, copies_of_row, start):
    def group(g, c):
        copies = []
        for u in range(ROW_GROUP):
            copies.extend(copies_of_row(g * ROW_GROUP + u))
        for k, (src, dst, sem) in enumerate(copies):
            if start:
                pltpu.async_copy(src, dst, sem, priority=k % DMA_QUEUES)
            else:
                pltpu.make_async_copy(src, dst, sem).wait()
        return c

    lax.fori_loop(0, n_rows // ROW_GROUP, group, 0)


def _dests_kernel(ps_ref, idx_ref, d_ref):
    for a in range(2):
        e = idx_ref[a]
        d = idx_ref[2 + a]
        for k in range(N_EXPERTS):
            d = d + jnp.where(e == k, ps_ref[k], 0)
        d_ref[a] = d


def _dests(idx, pad_start):
    planes = idx.shape[1]
    out = pl.pallas_call(
        _dests_kernel,
        grid_spec=pltpu.PrefetchScalarGridSpec(
            num_scalar_prefetch=1,
            grid=(1,),
            in_specs=[pl.BlockSpec(idx.shape, lambda i, ps: (0, 0, 0))],
            out_specs=pl.BlockSpec((2, planes, LANES), lambda i, ps: (0, 0, 0)),
        ),
        out_shape=jax.ShapeDtypeStruct((2, planes, LANES), I32),
        compiler_params=_cparams("arbitrary"),
        name="dests",
    )(pad_start, idx)
    return out[0].reshape(-1), out[1].reshape(-1)


def _pack_words(val):
    words = []
    for c in range(PACK_ROWS):
        hi = pltpu.bitcast(val[:, c * LANES:(c + 1) * LANES].astype(BF).astype(F32), U32)
        lo = pltpu.bitcast(val[:, (c + PACK_ROWS) * LANES:(c + PACK_ROWS + 1) * LANES].astype(BF).astype(F32), U32)
        words.append(hi | (lo >> 16))
    return words


def _unpack_words(words):
    his = [pltpu.bitcast(w & jnp.uint32(0xFFFF0000), F32) for w in words]
    los = [pltpu.bitcast(w << 16, F32) for w in words]
    return jnp.concatenate(his + los, axis=1)


def _load_packed(ref):
    n = ref.shape[0] // PACK_ROWS
    return [ref[pl.ds(c, n, stride=PACK_ROWS), :] for c in range(PACK_ROWS)]


def _store_packed(ref, words):
    n = ref.shape[0] // PACK_ROWS
    for c, w in enumerate(words):
        ref[pl.ds(c, n, stride=PACK_ROWS), :] = w


def _packed_row_copy(src_ref, src_row, dst_ref, dst_row, sem):
    src = pl.multiple_of(src_row * PACK_ROWS, PACK_ROWS)
    dst = pl.multiple_of(dst_row * PACK_ROWS, PACK_ROWS)
    return src_ref.at[pl.ds(src, PACK_ROWS)], dst_ref.at[pl.ds(dst, PACK_ROWS)], sem


def _dispatch_kernel(ps_ref, pe_ref, d1_ref, d2_ref, ht_ref, xg_ref, pk, zbuf, sem, zsem):
    i = pl.program_id(0)
    tm = d1_ref.shape[0]
    blk = MOE_BLOCK * PACK_ROWS
    n_blocks = xg_ref.shape[0] // blk

    @pl.when(i == 0)
    def _():
        zbuf[...] = jnp.zeros_like(zbuf)

        def zero_copy(block):
            return pltpu.make_async_copy(zbuf, xg_ref.at[pl.ds(pl.multiple_of(block * blk, blk), blk)], zsem)

        def seg_issue(e, c):
            @pl.when(pe_ref[e] > ps_ref[e])
            def _():
                zero_copy(pe_ref[e] // MOE_BLOCK - 1).start()
            return c

        def seg_drain(e, c):
            @pl.when(pe_ref[e] > ps_ref[e])
            def _():
                zero_copy(pe_ref[e] // MOE_BLOCK - 1).wait()
            return c

        def tail_issue(b, c):
            zero_copy(b).start()
            return c

        def tail_drain(b, c):
            zero_copy(b).wait()
            return c

        first_tail = pe_ref[N_EXPERTS - 1] // MOE_BLOCK
        lax.fori_loop(0, N_EXPERTS, seg_issue, 0)
        lax.fori_loop(first_tail, n_blocks, tail_issue, 0)
        lax.fori_loop(0, N_EXPERTS, seg_drain, 0)
        lax.fori_loop(first_tail, n_blocks, tail_drain, 0)

    _store_packed(pk, _pack_words(_load_tile_rows(ht_ref)))
    _row_groups(tm, lambda t: [_packed_row_copy(pk, t, xg_ref, d1_ref[t], sem),
                               _packed_row_copy(pk, t, xg_ref, d2_ref[t], sem)], True)
    _row_groups(tm, lambda t: [_packed_row_copy(pk, 0, xg_ref, 0, sem)] * 2, False)


def _dispatch(ht, d1, d2, pad_start, pad_end, n_pad):
    t = d1.shape[0]
    tm = TM_DISPATCH
    smem = lambda: pl.BlockSpec((tm,), lambda i, ps, pe: (i,), memory_space=pltpu.SMEM)
    return pl.pallas_call(
        _dispatch_kernel,
        grid_spec=pltpu.PrefetchScalarGridSpec(
            num_scalar_prefetch=2,
            grid=(t // tm,),
            in_specs=[smem(), smem(), pl.BlockSpec((tm * SUBLANES, LANES), lambda i, ps, pe: (i, 0))],
            out_specs=pl.BlockSpec(memory_space=pl.ANY),
            scratch_shapes=[pltpu.VMEM((tm * PACK_ROWS, LANES), U32), pltpu.VMEM((MOE_BLOCK * PACK_ROWS, LANES), U32),
                            pltpu.SemaphoreType.DMA(()), pltpu.SemaphoreType.DMA(())],
        ),
        out_shape=jax.ShapeDtypeStruct((n_pad * PACK_ROWS, LANES), U32),
        compiler_params=_cparams("arbitrary"),
        name="dispatch",
    )(pad_start, pad_end, d1, d2, ht)


def _expert_kernel(be_ref, nx_ref, nv_ref, nu_ref, x_ref, wg_hbm, wu_hbm, wd_hbm, y_ref,
                   wg_f, wu_f, wd_f, wg_s, wu_s, wd_s, sems):
    j = pl.program_id(0)
    stage = ((wg_hbm, wg_f, wg_s), (wu_hbm, wu_f, wu_s), (wd_hbm, wd_f, wd_s))

    def weight_copies(e):
        return [pltpu.make_async_copy(hbm.at[e], buf, sems.at[k]) for k, (hbm, buf, _) in enumerate(stage)]

    @pl.when(j < nu_ref[0])
    def _():
        e = be_ref[j]

        @pl.when(j == 0)
        def _():
            for cp in weight_copies(e):
                cp.start()

        @pl.when((j == 0) | (e != be_ref[jnp.maximum(j - 1, 0)]))
        def _():
            for cp, (_, buf, dst) in zip(weight_copies(e), stage):
                cp.wait()
                dst[...] = buf[...].astype(BF)
            nxt = nx_ref[e]

            @pl.when(nxt >= 0)
            def _():
                for cp in weight_copies(nxt):
                    cp.start()

        part_rows = x_ref.shape[0] // EXPERT_PARTS
        rows = part_rows // EXPERT_SUBBLOCKS

        def part(x_part, y_part):
            def up(h):
                xb = _unpack_words(_load_packed(x_part.at[pl.ds(h * rows, rows)])).astype(BF)
                a = _dot(xb, wg_s[...])
                u = _dot(xb, wu_s[...])
                return ((a / (1.0 + jnp.exp(-a))) * u).astype(BF)

            mids = [up(h) for h in range(EXPERT_SUBBLOCKS)]
            for h, mid in enumerate(mids):
                _store_packed(y_part.at[pl.ds(h * rows, rows)], _pack_words(_dot(mid, wd_s[...])))

        for q in range(EXPERT_PARTS):
            x_part = x_ref.at[pl.ds(q * part_rows, part_rows)]
            y_part = y_ref.at[pl.ds(q * part_rows, part_rows)]
            has_tokens = nv_ref[j] > q * (MOE_BLOCK // EXPERT_PARTS)
            pl.when(has_tokens)(functools.partial(part, x_part, y_part))

            @pl.when(jnp.logical_not(has_tokens))
            def _():
                y_part[...] = jnp.zeros_like(y_part)

    @pl.when(j >= nu_ref[0])
    def _():
        y_ref[...] = jnp.zeros_like(y_ref)


def _experts(xg, blk_expert, next_expert, blk_valid, n_used, w_eg, w_eu, w_ed):
    blk = MOE_BLOCK * PACK_ROWS
    nb = xg.shape[0] // blk
    d, de = w_eg.shape[1], w_eg.shape[2]
    rows = lambda j, be, nx, nv, nu: (jnp.minimum(j, nu[0] - 1), 0)
    hbm = pl.BlockSpec(memory_space=pl.ANY)
    return pl.pallas_call(
        _expert_kernel,
        grid_spec=pltpu.PrefetchScalarGridSpec(
            num_scalar_prefetch=4,
            grid=(nb,),
            in_specs=[pl.BlockSpec((blk, LANES), rows), hbm, hbm, hbm],
            out_specs=pl.BlockSpec((blk, LANES), lambda j, be, nx, nv, nu: (j, 0)),
            scratch_shapes=[pltpu.VMEM((d, de), F32), pltpu.VMEM((d, de), F32), pltpu.VMEM((de, d), F32),
                            pltpu.VMEM((d, de), BF), pltpu.VMEM((d, de), BF), pltpu.VMEM((de, d), BF),
                            pltpu.SemaphoreType.DMA((3,))],
        ),
        out_shape=jax.ShapeDtypeStruct(xg.shape, U32),
        compiler_params=_cparams("arbitrary"),
        name="experts",
    )(blk_expert, next_expert, blk_valid, n_used, xg, w_eg, w_eu, w_ed)


def _combine_kernel(d1_ref, d2_ref, d1_next_ref, d2_next_ref, wt_ref, ht_ref, yb_ref, g_ref, b_ref, o_ref,
                    buf, sems):
    i = pl.program_id(0)
    n = pl.num_programs(0)
    tm = d1_ref.shape[0]

    def gather_tile(da_ref, db_ref, slot, start):
        def copies(t):
            if not start:
                return [_packed_row_copy(yb_ref, 0, buf.at[slot, a], 0, sems.at[slot]) for a in range(2)]
            return [_packed_row_copy(yb_ref, da_ref[t], buf.at[slot, 0], t, sems.at[slot]),
                    _packed_row_copy(yb_ref, db_ref[t], buf.at[slot, 1], t, sems.at[slot])]
        _row_groups(tm, copies, start)

    def gathered(slot, a):
        return _unpack_words(_load_packed(buf.at[slot, a]))

    slot = i % 2

    @pl.when(i == 0)
    def _():
        gather_tile(d1_ref, d2_ref, 0, True)

    @pl.when(i + 1 < n)
    def _():
        gather_tile(d1_next_ref, d2_next_ref, 1 - slot, True)

    gather_tile(d1_ref, d2_ref, slot, False)
    wt = wt_ref[...]
    ffn = wt[:, 0:1] * gathered(slot, 0) + wt[:, 1:2] * gathered(slot, 1)
    o_ref[...] = _layer_norm(ALPHA * _load_tile_rows(ht_ref) + ffn, g_ref[...], b_ref[...])


def _combine(ht, yb, d1, d2, wts, ln_g, ln_b):
    t = d1.shape[0]
    d = SUBLANES * LANES
    tm = TM_COMBINE
    n = t // tm
    tok = lambda i: (i, 0)
    const = lambda i: (0, 0)
    cur = lambda: pl.BlockSpec((tm,), lambda i: (i,), memory_space=pltpu.SMEM)
    nxt = lambda: pl.BlockSpec((tm,), lambda i: (jnp.minimum(i + 1, n - 1),), memory_space=pltpu.SMEM)
    return pl.pallas_call(
        _combine_kernel,
        grid=(n,),
        in_specs=[cur(), cur(), nxt(), nxt(),
                  pl.BlockSpec((tm, 8), tok),
                  pl.BlockSpec((tm * SUBLANES, LANES), tok),
                  pl.BlockSpec(memory_space=pl.ANY),
                  pl.BlockSpec((1, d), const), pl.BlockSpec((1, d), const)],
        out_specs=pl.BlockSpec((tm, d), tok),
        out_shape=jax.ShapeDtypeStruct((t, d), F32),
        scratch_shapes=[pltpu.VMEM((2, 2, tm * PACK_ROWS, LANES), U32), pltpu.SemaphoreType.DMA((2,))],
        compiler_params=_cparams("arbitrary"),
        name="combine",
    )(d1, d2, d1, d2, wts, ht, yb, ln_g, ln_b)


def _token_mixing(x2d, mem2d, batch, seq, m_len, w_in, w_gate, b_gate, w_mem_kv, rpb, w_fo, w_na, w_mo, w_out,
                  ln_g, ln_b):
    k_mem, v_mem = _memkv(mem2d, w_mem_kv.astype(BF), m_len)
    u_f, k, q3, v3, mo = _proj(x2d, w_in, k_mem, v_mem, seq, m_len)
    fo = _fourier(u_f, batch, seq)
    na = _natten(q3, k, v3, rpb, batch, seq)
    return _merge(x2d, fo, na, mo, w_gate.astype(BF), b_gate.astype(F32)[None, :], w_fo.astype(BF),
                  w_na.astype(BF), w_mo.astype(BF), w_out.astype(BF), ln_g.astype(F32)[None, :],
                  ln_b.astype(F32)[None, :])


def _moe(ht, w_rg, b_rg, w_re, b_re, w_eg, w_eu, w_ed, ln_g, ln_b):
    t = ht.shape[0] // SUBLANES
    idx, wts, cnt = _router(ht, w_rg, b_rg, w_re, b_re)
    counts = cnt[:, 0].astype(I32)
    padded = (counts + MOE_BLOCK - 1) // MOE_BLOCK * MOE_BLOCK
    pad_end = jnp.cumsum(padded).astype(I32)
    pad_start = pad_end - padded
    n_pad = 2 * t + N_EXPERTS * MOE_BLOCK
    nb = n_pad // MOE_BLOCK
    blk_start = jnp.arange(nb, dtype=I32) * MOE_BLOCK
    blk_expert = jnp.minimum(jnp.sum((pad_end[None, :] <= blk_start[:, None]).astype(I32), axis=1), N_EXPERTS - 1)
    n_used = pad_end[-1:] // MOE_BLOCK
    seg_left = counts[None, :] - (blk_start[:, None] - pad_start[None, :])
    own = blk_expert[:, None] == jnp.arange(N_EXPERTS, dtype=I32)[None, :]
    blk_valid = jnp.clip(jnp.sum(jnp.where(own, seg_left, 0), axis=1), 0, MOE_BLOCK).astype(I32)
    ids = jnp.arange(N_EXPERTS, dtype=I32)
    later_used = (padded[None, :] > 0) & (ids[None, :] > ids[:, None])
    next_expert = jnp.min(jnp.where(later_used, ids[None, :], N_EXPERTS), axis=1)
    next_expert = jnp.where(next_expert == N_EXPERTS, -1, next_expert).astype(I32)
    d1, d2 = _dests(idx, pad_start)
    xg = _dispatch(ht, d1, d2, pad_start, pad_end, n_pad)
    yb = _experts(xg, blk_expert, next_expert, blk_valid, n_used, w_eg, w_eu, w_ed)
    return _combine(ht, yb, d1, d2, wts, ln_g.astype(F32)[None, :], ln_b.astype(F32)[None, :])


def kernel(x, mem, w_in, w_gate, b_gate, w_mem_kv, rpb, w_fourier_o, w_na_o, w_mem_o, w_out, ln1_g, ln1_b,
           w_router_group, b_router_group, w_router_expert, b_router_expert, w_exp_gate, w_exp_up,
           w_exp_down, ln2_g, ln2_b):
    batch, seq, d = x.shape
    m_len = mem.shape[1]
    x2d = x.reshape(batch * seq, d)
    mem2d = mem.reshape(batch * m_len, d)
    for l in range(w_in.shape[0]):
        ht = _token_mixing(x2d, mem2d, batch, seq, m_len, w_in[l], w_gate[l], b_gate[l], w_mem_kv[l], rpb[l],
                           w_fourier_o[l], w_na_o[l], w_mem_o[l], w_out[l], ln1_g[l], ln1_b[l])
        x2d = _moe(ht, w_router_group[l], b_router_group[l], w_router_expert[l], b_router_expert[l],
                   w_exp_gate[l], w_exp_up[l], w_exp_down[l], ln2_g[l], ln2_b[l])
    return x2d.reshape(batch, seq, d)
```

```python
import functools

import numpy as np
import jax
import jax.numpy as jnp
from jax import lax
from jax.experimental import pallas as pl
from jax.experimental.pallas import tpu as pltpu

BF = jnp.bfloat16
F32 = jnp.float32
I32 = jnp.int32
U32 = jnp.uint32

GRID_W = 64
MEM_HEADS = 4
MEM_HEAD_DIM = 128
MEM_DIM = MEM_HEADS * MEM_HEAD_DIM
FOURIER_GROUPS = 4
FOURIER_GROUP_DIM = 128
FOURIER_DIM = FOURIER_GROUPS * FOURIER_GROUP_DIM
NA_HEADS = 8
NA_HEAD_DIM = 64
NA_DIM = NA_HEADS * NA_HEAD_DIM
NA_ROW_WIN = 8
NA_COL_WIN = 16
N_GROUPS = 8
EXPERTS_PER_GROUP = 8
N_EXPERTS = N_GROUPS * EXPERTS_PER_GROUP
DEPTH = 1
ALPHA = (2.0 * DEPTH) ** 0.25
LN_EPS = 1e-5
NA_SCALE = NA_HEAD_DIM ** -0.5
LOG2_E = 1.4426950408889634
MEM_SCALE = MEM_HEAD_DIM ** -0.5
MASK_NEG = -1e30

LANES = 128
SUBLANES = 8
FFT_N2 = 128
TM_PROJ = 512
TM_MERGE = 512
MERGE_SUBTILES = 2
TM_ROUTE = 1024
TM_DISPATCH = 1024
TM_COMBINE = 256
NA_GROUP_ROWS = 4
NA_GROUP_WIN = NA_GROUP_ROWS + NA_ROW_WIN
NA_GROUPS_PER_STEP = 16
NA_SUM_ROWS = 16
F1_POS = 16
F2_SLABS = 8
MOE_BLOCK = 512
EXPERT_PARTS = 2
PACK_ROWS = 4
EXPERT_SUBBLOCKS = 2
ROUTE_ROWS = 128
ROUTE_STEPS_PER_PLANE = SUBLANES * LANES // TM_ROUTE
DMA_QUEUES = 2
ROW_GROUP = 16
VMEM_LIMIT = 56 * 1024 * 1024


def _cparams(*sem):
    return pltpu.CompilerParams(dimension_semantics=sem, vmem_limit_bytes=VMEM_LIMIT)


def _dot(a, b):
    return jnp.dot(a, b, preferred_element_type=F32)


def _dot_nt(a, b):
    return lax.dot_general(a, b, (((1,), (1,)), ((), ())), preferred_element_type=F32)


def _layer_norm(y, g, b):
    mu = jnp.mean(y, axis=-1, keepdims=True)
    yc = y - mu
    var = jnp.mean(yc * yc, axis=-1, keepdims=True)
    return yc * lax.rsqrt(var + LN_EPS) * g + b


def _store_tile_rows(ref, val):
    n = val.shape[0]
    for s in range(SUBLANES):
        ref[pl.ds(s, n, stride=SUBLANES), :] = val[:, s * LANES:(s + 1) * LANES]


def _load_tile_rows(ref):
    n = ref.shape[0] // SUBLANES
    return jnp.concatenate([ref[pl.ds(s, n, stride=SUBLANES), :] for s in range(SUBLANES)], axis=1)


def _memkv_kernel(mem_ref, w_ref, k_ref, v_ref):
    kv = _dot(mem_ref[...].astype(BF), w_ref[...])
    k_ref[...] = kv[:, :MEM_DIM].astype(BF)
    v_ref[...] = kv[:, MEM_DIM:].astype(BF)


def _memkv(mem2d, w_kv, m_len):
    rows, d = mem2d.shape
    return pl.pallas_call(
        _memkv_kernel,
        grid=(rows // m_len,),
        in_specs=[pl.BlockSpec((m_len, d), lambda i: (i, 0)),
                  pl.BlockSpec((d, 2 * MEM_DIM), lambda i: (0, 0))],
        out_specs=[pl.BlockSpec((m_len, MEM_DIM), lambda i: (i, 0))] * 2,
        out_shape=[jax.ShapeDtypeStruct((rows, MEM_DIM), BF)] * 2,
        compiler_params=_cparams("parallel"),
        name="memkv",
    )(mem2d, w_kv)


def _proj_kernel(x_ref, w_ref, wt_ref, km_ref, vm_ref, uf_ref, k_ref, q3_ref, v3_ref, mo_ref):
    xb = x_ref[...].astype(BF)

    def seg(j):
        return _dot(xb, w_ref[:, j * 512:(j + 1) * 512])

    uf_ref[...] = seg(0)
    k_ref[...] = seg(1).astype(BF)
    q_t = _dot_nt(wt_ref[:NA_DIM, :], xb) * (NA_SCALE * LOG2_E)
    v_t = _dot_nt(wt_ref[NA_DIM:, :], xb)
    for s in range(q3_ref.shape[0]):
        q3_ref[s] = q_t[:, s * LANES:(s + 1) * LANES].astype(BF)
        v3_ref[s] = v_t[:, s * LANES:(s + 1) * LANES].astype(BF)
    qm = seg(2).astype(BF)
    for h in range(MEM_HEADS):
        sl = slice(h * MEM_HEAD_DIM, (h + 1) * MEM_HEAD_DIM)
        s = _dot_nt(qm[:, sl], km_ref[:, sl]) * MEM_SCALE
        m = jnp.max(s, axis=-1, keepdims=True)
        p = jnp.exp(s - m)
        l = jnp.sum(p, axis=-1, keepdims=True)
        o = _dot(p.astype(BF), vm_ref[:, sl])
        mo_ref[:, sl] = (o / l).astype(BF)


def _proj(x2d, w_in, k_mem, v_mem, seq, m_len):
    t, d = x2d.shape
    tm = TM_PROJ
    tiles_per_batch = seq // tm
    w_tok = jnp.concatenate([w_in[:, 0:512], w_in[:, 1024:1536], w_in[:, 2048:2560]], axis=1).astype(BF)
    w_chan = jnp.concatenate([w_in[:, 512:1024], w_in[:, 1536:2048]], axis=1).T.astype(BF)
    tok = lambda i: (i, 0)
    slab = lambda i: (i, 0, 0)
    memb = lambda i: (i // tiles_per_batch, 0)
    const = lambda i: (0, 0)
    tok_out = pl.BlockSpec((tm, 512), tok)
    slab_out = pl.BlockSpec((tm // LANES, NA_DIM, LANES), slab)
    slab_shape = jax.ShapeDtypeStruct((t // LANES, NA_DIM, LANES), BF)
    return pl.pallas_call(
        _proj_kernel,
        grid=(t // tm,),
        in_specs=[pl.BlockSpec((tm, d), tok),
                  pl.BlockSpec(w_tok.shape, const),
                  pl.BlockSpec(w_chan.shape, const),
                  pl.BlockSpec((m_len, MEM_DIM), memb),
                  pl.BlockSpec((m_len, MEM_DIM), memb)],
        out_specs=[tok_out, tok_out, slab_out, slab_out, tok_out],
        out_shape=[jax.ShapeDtypeStruct((t, 512), F32), jax.ShapeDtypeStruct((t, 512), BF)] + [slab_shape] * 2
                  + [jax.ShapeDtypeStruct((t, 512), BF)],
        compiler_params=_cparams("parallel"),
        name="proj",
    )(x2d, w_tok, w_chan, k_mem, v_mem)


def _fourier_tables(n1, n2):
    n = n1 * n2
    k1 = np.arange(n1)
    ang1 = 2.0 * np.pi * ((k1[:, None] * k1[None, :]) % n1) / n1
    norm = 1.0 / np.sqrt(float(n) * FOURIER_GROUP_DIM)
    m1 = np.concatenate([np.cos(ang1), -np.sin(ang1)], axis=0) * norm
    kk = k1[:, None, None] + n1 * np.arange(n2)[None, :, None]
    nn = np.arange(n2)[None, None, :]
    ang2 = 2.0 * np.pi * ((kk * nn) % n) / n
    c2, s2 = np.cos(ang2), np.sin(ang2)
    m2 = np.concatenate([np.concatenate([c2, s2], axis=2),
                         np.concatenate([-s2, c2], axis=2)], axis=1)
    c = np.arange(FOURIER_GROUP_DIM)
    angc = 2.0 * np.pi * ((c[:, None] * c[None, :]) % FOURIER_GROUP_DIM) / FOURIER_GROUP_DIM
    mc = np.concatenate([np.cos(angc), np.sin(angc)], axis=0)
    as_bf = lambda a: jnp.asarray(a.astype(np.float32)).astype(BF)
    return as_bf(m1), as_bf(m2), as_bf(mc)


def _f1_kernel(u_ref, m_ref, a_ref):
    n1, pos, _ = u_ref.shape
    x = jnp.concatenate([u_ref[:, p, :] for p in range(pos)], axis=1).astype(BF)
    r = _dot(m_ref[...], x)
    a_ref[0] = r[:n1].astype(BF)
    a_ref[1] = r[n1:].astype(BF)


def _f2_kernel(a_ref, m2_ref, mc_ref, o_ref):
    slabs = a_ref.shape[1]
    n2 = a_ref.shape[2]
    for t in range(slabs):
        a = jnp.concatenate([a_ref[0, t], a_ref[1, t]], axis=0)
        y = _dot(m2_ref[t], a)
        yr = y[:n2].astype(BF)
        yi = y[n2:].astype(BF)
        yy = jnp.concatenate([jnp.concatenate([yr[:, g * LANES:(g + 1) * LANES], yi[:, g * LANES:(g + 1) * LANES]],
                                              axis=1) for g in range(FOURIER_GROUPS)], axis=0)
        z = _dot(yy, mc_ref[...])
        for g in range(FOURIER_GROUPS):
            o_ref[:, t, g * LANES:(g + 1) * LANES] = z[g * n2:(g + 1) * n2]


def _fourier(u_f, batch, seq):
    n2 = FFT_N2
    n1 = seq // n2
    m1, m2, mc = _fourier_tables(n1, n2)
    cols = n2 * FOURIER_DIM
    pos = min(F1_POS, n2)
    a = pl.pallas_call(
        _f1_kernel,
        grid=(batch, n2 // pos),
        in_specs=[pl.BlockSpec((None, n1, pos, FOURIER_DIM), lambda b, j: (b, 0, j, 0)),
                  pl.BlockSpec((2 * n1, n1), lambda b, j: (0, 0))],
        out_specs=pl.BlockSpec((None, 2, n1, pos * FOURIER_DIM), lambda b, j: (b, 0, 0, j)),
        out_shape=jax.ShapeDtypeStruct((batch, 2, n1, cols), BF),
        compiler_params=_cparams("parallel", "parallel"),
        name="fourier1",
    )(u_f.reshape(batch, n1, n2, FOURIER_DIM), m1)
    kb = min(F2_SLABS, n1)
    out = pl.pallas_call(
        _f2_kernel,
        grid=(n1 // kb, batch),
        in_specs=[pl.BlockSpec((None, 2, kb, n2, FOURIER_DIM), lambda j, b: (b, 0, j, 0, 0)),
                  pl.BlockSpec((kb, 2 * n2, 2 * n2), lambda j, b: (j, 0, 0)),
                  pl.BlockSpec((2 * FOURIER_GROUP_DIM, FOURIER_GROUP_DIM), lambda j, b: (0, 0))],
        out_specs=pl.BlockSpec((None, n2, kb, FOURIER_DIM), lambda j, b: (b, 0, j, 0)),
        out_shape=jax.ShapeDtypeStruct((batch, n2, n1, FOURIER_DIM), F32),
        compiler_params=_cparams("parallel", "parallel"),
        name="fourier2",
    )(a.reshape(batch, 2, n1, n2, FOURIER_DIM), m2, mc)
    return out.reshape(batch * seq, FOURIER_DIM)


def _na_group_start(r0, n_rows):
    rs0 = min(max(r0 - NA_ROW_WIN // 2, 0), n_rows - NA_ROW_WIN)
    return min(rs0, n_rows - NA_GROUP_WIN)


def _na_bias_table(rpb, n_rows):
    g = NA_GROUP_ROWS
    assert n_rows >= 2 * NA_GROUP_WIN and n_rows % g == 0 and (NA_ROW_WIN // 2) % g == 0
    cols = np.arange(GRID_W)
    cs = np.clip(cols - NA_COL_WIN // 2, 0, GRID_W - NA_COL_WIN)
    kc = np.arange(GRID_W)
    in_win = (kc[None, :] >= cs[:, None]) & (kc[None, :] < cs[:, None] + NA_COL_WIN)
    dc = kc[None, :] - cols[:, None] + (NA_COL_WIN - 1)
    n_dc = 2 * NA_COL_WIN - 1
    pick = ((dc.reshape(-1)[None, :] == np.arange(n_dc)[:, None]) & in_win.reshape(-1)[None, :])
    picked = jnp.dot(rpb.astype(F32).reshape(-1, n_dc), jnp.asarray(pick.astype(np.float32)),
                     precision=lax.Precision.HIGHEST)
    picked = picked.reshape(NA_HEADS, 2 * NA_ROW_WIN - 1, GRID_W, GRID_W)
    full_t = jnp.where(in_win[None, None], picked * LOG2_E, MASK_NEG).transpose(0, 1, 3, 2)
    masked = jnp.full((NA_HEADS, GRID_W, GRID_W), MASK_NEG, F32)
    half = NA_ROW_WIN // 2
    group_rows = list(range(0, half, g)) + [half] + list(range(n_rows - half, n_rows, g))
    tabs = []
    for r0 in group_rows:
        start = _na_group_start(r0, n_rows)
        key_rows = []
        for kr in range(NA_GROUP_WIN):
            blocks = []
            for rr in range(g):
                r = r0 + rr
                rs = min(max(r - half, 0), n_rows - NA_ROW_WIN)
                in_rows = rs <= start + kr < rs + NA_ROW_WIN
                blocks.append(full_t[:, start + kr - r + NA_ROW_WIN - 1] if in_rows else masked)
            key_rows.append(jnp.concatenate(blocks, axis=2))
        tabs.append(jnp.concatenate(key_rows, axis=1))
    return jnp.stack(tabs, axis=0)


def _na_kernel(q3_ref, k_ref, v3_ref, tab_ref, o_ref, *, n_rows):
    i = pl.program_id(2)
    g = NA_GROUP_ROWS
    slabs_per_group = g * GRID_W // LANES
    groups = q3_ref.shape[0] // slabs_per_group
    win = NA_GROUP_WIN * GRID_W
    tokens = g * GRID_W
    half = NA_ROW_WIN // 2
    n_top = half // g
    chan = lax.broadcasted_iota(I32, (2 * NA_HEAD_DIM, tokens), 0)
    first_head = chan < NA_HEAD_DIM

    def scores(jj):
        r0 = (i * groups + jj) * g
        start = jnp.minimum(jnp.clip(r0 - half, 0, n_rows - NA_ROW_WIN), n_rows - NA_GROUP_WIN)
        cfg = jnp.where(r0 < half, r0 // g,
                        jnp.where(r0 >= n_rows - half, n_top + 1 + (r0 - (n_rows - half)) // g, n_top))
        kw = k_ref[pl.ds(pl.multiple_of(start * GRID_W, LANES), win), :]
        qt = jnp.concatenate([q3_ref[jj * slabs_per_group + t] for t in range(slabs_per_group)], axis=1)
        zero = jnp.zeros_like(qt)
        q_bd = jnp.concatenate([jnp.where(first_head, qt, zero), jnp.where(first_head, zero, qt)], axis=1)
        s = _dot(kw, q_bd) + jnp.concatenate([tab_ref[cfg, 0], tab_ref[cfg, 1]], axis=1)
        return s, start

    ones_rows = jnp.ones((NA_SUM_ROWS, win), BF)

    def outputs(jj, p, start):
        slab = start // 2
        vw = jnp.concatenate([v3_ref[slab + t] for t in range(NA_GROUP_WIN // 2)] , axis=1)
        o_l = _dot(jnp.concatenate([vw, ones_rows], axis=0), p)
        o_t = o_l[:2 * NA_HEAD_DIM] / o_l[2 * NA_HEAD_DIM:2 * NA_HEAD_DIM + 1]
        o = jnp.where(first_head, o_t[:, :tokens], o_t[:, tokens:])
        o_ref[jj * tokens:(jj + 1) * tokens, :] = jnp.transpose(o).astype(BF)

    pending = None
    for jj in range(groups):
        s, start = scores(jj)
        if pending is not None:
            outputs(*pending)
        p = jnp.exp2(s - jnp.max(s, axis=0, keepdims=True))
        pending = (jj, p.astype(BF), start)
    outputs(*pending)


def _natten(q3, k, v3, rpb, batch, seq):
    assert 2 * GRID_W == LANES and 2 * NA_HEAD_DIM == LANES and NA_GROUP_ROWS % 2 == 0
    n_rows = seq // GRID_W
    slabs = n_rows // 2
    step_slabs = min(NA_GROUPS_PER_STEP * NA_GROUP_ROWS // 2, slabs)
    steps = slabs // step_slabs
    hp = NA_HEADS // 2
    tab = _na_bias_table(rpb, n_rows)
    return pl.pallas_call(
        functools.partial(_na_kernel, n_rows=n_rows),
        grid=(hp, batch, steps),
        in_specs=[pl.BlockSpec((step_slabs, LANES, LANES), lambda p, b, i: (b * steps + i, p, 0)),
                  pl.BlockSpec((seq, LANES), lambda p, b, i: (b, p)),
                  pl.BlockSpec((slabs, LANES, LANES), lambda p, b, i: (b, p, 0)),
                  pl.BlockSpec((tab.shape[0], 2) + tab.shape[2:], lambda p, b, i: (0, p, 0, 0))],
        out_specs=pl.BlockSpec((step_slabs * LANES, LANES), lambda p, b, i: (b * steps + i, p)),
        out_shape=jax.ShapeDtypeStruct((batch * seq, NA_DIM), BF),
        compiler_params=_cparams("parallel", "parallel", "parallel"),
        name="natten",
    )(q3, k, v3, tab)


def _merge_kernel(x_ref, fo_ref, na_ref, mo_ref, wg_ref, bg_ref, wf_ref, wn_ref, wm_ref, wo_ref,
                  g_ref, b_ref, ht_ref):
    d = x_ref.shape[1]
    rows = x_ref.shape[0] // MERGE_SUBTILES

    def pre_norm(h):
        sl = pl.ds(h * rows, rows)
        x = x_ref[sl, :]
        xb = x.astype(BF)
        merged = None
        for j, (br_ref, w_ref) in enumerate(((fo_ref, wf_ref), (na_ref, wn_ref), (mo_ref, wm_ref))):
            z = _dot(xb, wg_ref[:, j * d:(j + 1) * d]) + bg_ref[:, j * d:(j + 1) * d]
            gate = 1.0 / (1.0 + jnp.exp(-z))
            term = gate * _dot(br_ref[sl, :].astype(BF), w_ref[...])
            merged = term if merged is None else merged + term
        return ALPHA * x + _dot(merged.astype(BF), wo_ref[...])

    ys = [pre_norm(h) for h in range(MERGE_SUBTILES)]
    for h, y in enumerate(ys):
        _store_tile_rows(ht_ref.at[pl.ds(h * rows * SUBLANES, rows * SUBLANES)],
                         _layer_norm(y, g_ref[...], b_ref[...]))


def _merge(x2d, fo, na, mo, w_gate, b_gate, w_fo, w_na, w_mo, w_out, ln_g, ln_b):
    t, d = x2d.shape
    assert d == SUBLANES * LANES
    tm = TM_MERGE
    tok = lambda i: (i, 0)
    full = lambda a: pl.BlockSpec(a.shape, lambda i: (0, 0))
    return pl.pallas_call(
        _merge_kernel,
        grid=(t // tm,),
        in_specs=[pl.BlockSpec((tm, d), tok)] + [pl.BlockSpec((tm, 512), tok)] * 3
                 + [full(a) for a in (w_gate, b_gate, w_fo, w_na, w_mo, w_out, ln_g, ln_b)],
        out_specs=pl.BlockSpec((tm * SUBLANES, LANES), tok),
        out_shape=jax.ShapeDtypeStruct((t * SUBLANES, LANES), F32),
        compiler_params=_cparams("parallel"),
        name="merge",
    )(x2d, fo, na, mo, w_gate, b_gate, w_fo, w_na, w_mo, w_out, ln_g, ln_b)


def _router_kernel(ht_ref, whi_ref, wlo_ref, b_ref, oi_ref, of_ref, cnt_ref, carry_ref):
    tm = ht_ref.shape[0] // SUBLANES

    @pl.when(pl.program_id(0) == 0)
    def _():
        carry_ref[...] = jnp.zeros_like(carry_ref)

    h = _load_tile_rows(ht_ref)
    hh = h.astype(BF)
    hl = (h - hh.astype(F32)).astype(BF)
    logits = (_dot_nt(whi_ref[...], hh) + _dot_nt(wlo_ref[...], hh) + _dot_nt(whi_ref[...], hl)
              + b_ref[...])
    gl = logits[:N_GROUPS, :]
    g_row = lax.broadcasted_iota(I32, (N_GROUPS, tm), 0)
    gmax = jnp.max(gl, axis=0, keepdims=True)
    g_idx = jnp.min(jnp.where(gl == gmax, g_row, N_GROUPS), axis=0, keepdims=True)
    p_group = 1.0 / jnp.sum(jnp.exp(gl - gmax), axis=0, keepdims=True)
    e_row = lax.broadcasted_iota(I32, (N_EXPERTS, tm), 0)
    el = jnp.where((e_row >> 3) == g_idx, logits[N_GROUPS:N_GROUPS + N_EXPERTS, :], MASK_NEG)
    v1 = jnp.max(el, axis=0, keepdims=True)
    i1 = jnp.min(jnp.where(el == v1, e_row, N_EXPERTS), axis=0, keepdims=True)
    el2 = jnp.where(e_row == i1, MASK_NEG, el)
    v2 = jnp.max(el2, axis=0, keepdims=True)
    i2 = jnp.min(jnp.where(el2 == v2, e_row, N_EXPERTS), axis=0, keepdims=True)
    tt = jnp.exp(v2 - v1)
    w1 = p_group / (1.0 + tt)
    w2 = p_group * tt / (1.0 + tt)
    sel1 = e_row == i1
    sel2 = e_row == i2
    onehot = jnp.where(sel1 | sel2, 1.0, 0.0)
    t_in = lax.broadcasted_iota(I32, (tm, tm), 0)
    t_out = lax.broadcasted_iota(I32, (tm, tm), 1)
    tri = jnp.where(t_in < t_out, 1.0, 0.0).astype(BF)
    prefix = _dot(onehot.astype(BF), tri) + carry_ref[...]
    r1 = jnp.sum(jnp.where(sel1, prefix, 0.0), axis=0, keepdims=True)
    r2 = jnp.sum(jnp.where(sel2, prefix, 0.0), axis=0, keepdims=True)
    carry_ref[...] += jnp.sum(onehot, axis=1, keepdims=True)
    cnt_ref[...] = carry_ref[...]
    sub = pl.program_id(0) % ROUTE_STEPS_PER_PLANE
    for f, vals in enumerate((i1, i2, r1.astype(I32), r2.astype(I32))):
        for c in range(tm // LANES):
            oi_ref[f, pl.ds(sub * (tm // LANES) + c, 1), :] = vals[:, c * LANES:(c + 1) * LANES]
    w_row = lax.broadcasted_iota(I32, (LANES, tm), 0)
    w_rows = jnp.where(w_row == 0, w1, jnp.where(w_row == 1, w2, 0.0))
    of_ref[...] = jnp.transpose(w_rows)[:, :8]


def _router(ht, w_rg, b_rg, w_re, b_re):
    t = ht.shape[0] // SUBLANES
    d = w_rg.shape[0]
    tm = TM_ROUTE
    pad = ROUTE_ROWS - N_GROUPS - N_EXPERTS
    w = jnp.concatenate([w_rg.astype(F32), w_re.astype(F32), jnp.zeros((d, pad), F32)], axis=1).T
    b = jnp.concatenate([b_rg.astype(F32), b_re.astype(F32), jnp.zeros((pad,), F32)])[:, None]
    w_hi = w.astype(BF)
    w_lo = (w - w_hi.astype(F32)).astype(BF)
    tok = lambda i: (i, 0)
    const = lambda i: (0, 0)
    return pl.pallas_call(
        _router_kernel,
        grid=(t // tm,),
        in_specs=[pl.BlockSpec((tm * SUBLANES, LANES), tok),
                  pl.BlockSpec((ROUTE_ROWS, d), const),
                  pl.BlockSpec((ROUTE_ROWS, d), const),
                  pl.BlockSpec((ROUTE_ROWS, 1), const)],
        out_specs=[pl.BlockSpec((4, SUBLANES, LANES), lambda i: (0, i // ROUTE_STEPS_PER_PLANE, 0)),
                   pl.BlockSpec((tm, 8), tok),
                   pl.BlockSpec((N_EXPERTS, 1), const)],
        out_shape=[jax.ShapeDtypeStruct((4, t // LANES, LANES), I32), jax.ShapeDtypeStruct((t, 8), F32),
                   jax.ShapeDtypeStruct((N_EXPERTS, 1), F32)],
        scratch_shapes=[pltpu.VMEM((N_EXPERTS, 1), F32)],
        compiler_params=_cparams("arbitrary"),
        name="router",
    )(ht, w_hi, w_lo, b)


def _row_groups(n_rows, copies_of_row, start):
    def group(g, c):
        copies = []
        for u in range(ROW_GROUP):
            copies.extend(copies_of_row(g * ROW_GROUP + u))
        for k, (src, dst, sem) in enumerate(copies):
            if start:
                pltpu.async_copy(src, dst, sem, priority=k % DMA_QUEUES)
            else:
                pltpu.make_async_copy(src, dst, sem).wait()
        return c

    lax.fori_loop(0, n_rows // ROW_GROUP, group, 0)


def _dests_kernel(ps_ref, idx_ref, d_ref):
    for a in range(2):
        e = idx_ref[a]
        d = idx_ref[2 + a]
        for k in range(N_EXPERTS):
            d = d + jnp.where(e == k, ps_ref[k], 0)
        d_ref[a] = d


def _dests(idx, pad_start):
    planes = idx.shape[1]
    out = pl.pallas_call(
        _dests_kernel,
        grid_spec=pltpu.PrefetchScalarGridSpec(
            num_scalar_prefetch=1,
            grid=(1,),
            in_specs=[pl.BlockSpec(idx.shape, lambda i, ps: (0, 0, 0))],
            out_specs=pl.BlockSpec((2, planes, LANES), lambda i, ps: (0, 0, 0)),
        ),
        out_shape=jax.ShapeDtypeStruct((2, planes, LANES), I32),
        compiler_params=_cparams("arbitrary"),
        name="dests",
    )(pad_start, idx)
    return out[0].reshape(-1), out[1].reshape(-1)


def _pack_words(val):
    words = []
    for c in range(PACK_ROWS):
        hi = pltpu.bitcast(val[:, c * LANES:(c + 1) * LANES].astype(BF).astype(F32), U32)
        lo = pltpu.bitcast(val[:, (c + PACK_ROWS) * LANES:(c + PACK_ROWS + 1) * LANES].astype(BF).astype(F32), U32)
        words.append(hi | (lo >> 16))
    return words


def _unpack_words(words):
    his = [pltpu.bitcast(w & jnp.uint32(0xFFFF0000), F32) for w in words]
    los = [pltpu.bitcast(w << 16, F32) for w in words]
    return jnp.concatenate(his + los, axis=1)


def _load_packed(ref):
    n = ref.shape[0] // PACK_ROWS
    return [ref[pl.ds(c, n, stride=PACK_ROWS), :] for c in range(PACK_ROWS)]


def _store_packed(ref, words):
    n = ref.shape[0] // PACK_ROWS
    for c, w in enumerate(words):
        ref[pl.ds(c, n, stride=PACK_ROWS), :] = w


def _packed_row_copy(src_ref, src_row, dst_ref, dst_row, sem):
    src = pl.multiple_of(src_row * PACK_ROWS, PACK_ROWS)
    dst = pl.multiple_of(dst_row * PACK_ROWS, PACK_ROWS)
    return src_ref.at[pl.ds(src, PACK_ROWS)], dst_ref.at[pl.ds(dst, PACK_ROWS)], sem


def _dispatch_kernel(ps_ref, pe_ref, d1_ref, d2_ref, ht_ref, xg_ref, pk, zbuf, sem, zsem):
    i = pl.program_id(0)
    tm = d1_ref.shape[0]
    blk = MOE_BLOCK * PACK_ROWS
    n_blocks = xg_ref.shape[0] // blk

    @pl.when(i == 0)
    def _():
        zbuf[...] = jnp.zeros_like(zbuf)

        def zero_copy(block):
            return pltpu.make_async_copy(zbuf, xg_ref.at[pl.ds(pl.multiple_of(block * blk, blk), blk)], zsem)

        def seg_issue(e, c):
            @pl.when(pe_ref[e] > ps_ref[e])
            def _():
                zero_copy(pe_ref[e] // MOE_BLOCK - 1).start()
            return c

        def seg_drain(e, c):
            @pl.when(pe_ref[e] > ps_ref[e])
            def _():
                zero_copy(pe_ref[e] // MOE_BLOCK - 1).wait()
            return c

        def tail_issue(b, c):
            zero_copy(b).start()
            return c

        def tail_drain(b, c):
            zero_copy(b).wait()
            return c

        first_tail = pe_ref[N_EXPERTS - 1] // MOE_BLOCK
        lax.fori_loop(0, N_EXPERTS, seg_issue, 0)
        lax.fori_loop(first_tail, n_blocks, tail_issue, 0)
        lax.fori_loop(0, N_EXPERTS, seg_drain, 0)
        lax.fori_loop(first_tail, n_blocks, tail_drain, 0)

    _store_packed(pk, _pack_words(_load_tile_rows(ht_ref)))
    _row_groups(tm, lambda t: [_packed_row_copy(pk, t, xg_ref, d1_ref[t], sem),
                               _packed_row_copy(pk, t, xg_ref, d2_ref[t], sem)], True)
    _row_groups(tm, lambda t: [_packed_row_copy(pk, 0, xg_ref, 0, sem)] * 2, False)


def _dispatch(ht, d1, d2, pad_start, pad_end, n_pad):
    t = d1.shape[0]
    tm = TM_DISPATCH
    smem = lambda: pl.BlockSpec((tm,), lambda i, ps, pe: (i,), memory_space=pltpu.SMEM)
    return pl.pallas_call(
        _dispatch_kernel,
        grid_spec=pltpu.PrefetchScalarGridSpec(
            num_scalar_prefetch=2,
            grid=(t // tm,),
            in_specs=[smem(), smem(), pl.BlockSpec((tm * SUBLANES, LANES), lambda i, ps, pe: (i, 0))],
            out_specs=pl.BlockSpec(memory_space=pl.ANY),
            scratch_shapes=[pltpu.VMEM((tm * PACK_ROWS, LANES), U32), pltpu.VMEM((MOE_BLOCK * PACK_ROWS, LANES), U32),
                            pltpu.SemaphoreType.DMA(()), pltpu.SemaphoreType.DMA(())],
        ),
        out_shape=jax.ShapeDtypeStruct((n_pad * PACK_ROWS, LANES), U32),
        compiler_params=_cparams("arbitrary"),
        name="dispatch",
    )(pad_start, pad_end, d1, d2, ht)


def _expert_kernel(be_ref, nx_ref, nv_ref, nu_ref, x_ref, wg_hbm, wu_hbm, wd_hbm, y_ref,
                   wg_f, wu_f, wd_f, wg_s, wu_s, wd_s, sems):
    j = pl.program_id(0)
    stage = ((wg_hbm, wg_f, wg_s), (wu_hbm, wu_f, wu_s), (wd_hbm, wd_f, wd_s))

    def weight_copies(e):
        return [pltpu.make_async_copy(hbm.at[e], buf, sems.at[k]) for k, (hbm, buf, _) in enumerate(stage)]

    @pl.when(j < nu_ref[0])
    def _():
        e = be_ref[j]

        @pl.when(j == 0)
        def _():
            for cp in weight_copies(e):
                cp.start()

        @pl.when((j == 0) | (e != be_ref[jnp.maximum(j - 1, 0)]))
        def _():
            for cp, (_, buf, dst) in zip(weight_copies(e), stage):
                cp.wait()
                dst[...] = buf[...].astype(BF)
            nxt = nx_ref[e]

            @pl.when(nxt >= 0)
            def _():
                for cp in weight_copies(nxt):
                    cp.start()

        part_rows = x_ref.shape[0] // EXPERT_PARTS
        rows = part_rows // EXPERT_SUBBLOCKS

        def part(x_part, y_part):
            def up(h):
                xb = _unpack_words(_load_packed(x_part.at[pl.ds(h * rows, rows)])).astype(BF)
                a = _dot(xb, wg_s[...])
                u = _dot(xb, wu_s[...])
                return ((a / (1.0 + jnp.exp(-a))) * u).astype(BF)

            mids = [up(h) for h in range(EXPERT_SUBBLOCKS)]
            for h, mid in enumerate(mids):
                _store_packed(y_part.at[pl.ds(h * rows, rows)], _pack_words(_dot(mid, wd_s[...])))

        for q in range(EXPERT_PARTS):
            x_part = x_ref.at[pl.ds(q * part_rows, part_rows)]
            y_part = y_ref.at[pl.ds(q * part_rows, part_rows)]
            has_tokens = nv_ref[j] > q * (MOE_BLOCK // EXPERT_PARTS)
            pl.when(has_tokens)(functools.partial(part, x_part, y_part))

            @pl.when(jnp.logical_not(has_tokens))
            def _():
                y_part[...] = jnp.zeros_like(y_part)

    @pl.when(j >= nu_ref[0])
    def _():
        y_ref[...] = jnp.zeros_like(y_ref)


def _experts(xg, blk_expert, next_expert, blk_valid, n_used, w_eg, w_eu, w_ed):
    blk = MOE_BLOCK * PACK_ROWS
    nb = xg.shape[0] // blk
    d, de = w_eg.shape[1], w_eg.shape[2]
    rows = lambda j, be, nx, nv, nu: (jnp.minimum(j, nu[0] - 1), 0)
    hbm = pl.BlockSpec(memory_space=pl.ANY)
    return pl.pallas_call(
        _expert_kernel,
        grid_spec=pltpu.PrefetchScalarGridSpec(
            num_scalar_prefetch=4,
            grid=(nb,),
            in_specs=[pl.BlockSpec((blk, LANES), rows), hbm, hbm, hbm],
            out_specs=pl.BlockSpec((blk, LANES), lambda j, be, nx, nv, nu: (j, 0)),
            scratch_shapes=[pltpu.VMEM((d, de), F32), pltpu.VMEM((d, de), F32), pltpu.VMEM((de, d), F32),
                            pltpu.VMEM((d, de), BF), pltpu.VMEM((d, de), BF), pltpu.VMEM((de, d), BF),
                            pltpu.SemaphoreType.DMA((3,))],
        ),
        out_shape=jax.ShapeDtypeStruct(xg.shape, U32),
        compiler_params=_cparams("arbitrary"),
        name="experts",
    )(blk_expert, next_expert, blk_valid, n_used, xg, w_eg, w_eu, w_ed)


def _combine_kernel(d1_ref, d2_ref, d1_next_ref, d2_next_ref, wt_ref, ht_ref, yb_ref, g_ref, b_ref, o_ref,
                    buf, sems):
    i = pl.program_id(0)
    n = pl.num_programs(0)
    tm = d1_ref.shape[0]

    def gather_tile(da_ref, db_ref, slot, start):
        def copies(t):
            if not start:
                return [_packed_row_copy(yb_ref, 0, buf.at[slot, a], 0, sems.at[slot]) for a in range(2)]
            return [_packed_row_copy(yb_ref, da_ref[t], buf.at[slot, 0], t, sems.at[slot]),
                    _packed_row_copy(yb_ref, db_ref[t], buf.at[slot, 1], t, sems.at[slot])]
        _row_groups(tm, copies, start)

    def gathered(slot, a):
        return _unpack_words(_load_packed(buf.at[slot, a]))

    slot = i % 2

    @pl.when(i == 0)
    def _():
        gather_tile(d1_ref, d2_ref, 0, True)

    @pl.when(i + 1 < n)
    def _():
        gather_tile(d1_next_ref, d2_next_ref, 1 - slot, True)

    gather_tile(d1_ref, d2_ref, slot, False)
    wt = wt_ref[...]
    ffn = wt[:, 0:1] * gathered(slot, 0) + wt[:, 1:2] * gathered(slot, 1)
    o_ref[...] = _layer_norm(ALPHA * _load_tile_rows(ht_ref) + ffn, g_ref[...], b_ref[...])


def _combine(ht, yb, d1, d2, wts, ln_g, ln_b):
    t = d1.shape[0]
    d = SUBLANES * LANES
    tm = TM_COMBINE
    n = t // tm
    tok = lambda i: (i, 0)
    const = lambda i: (0, 0)
    cur = lambda: pl.BlockSpec((tm,), lambda i: (i,), memory_space=pltpu.SMEM)
    nxt = lambda: pl.BlockSpec((tm,), lambda i: (jnp.minimum(i + 1, n - 1),), memory_space=pltpu.SMEM)
    return pl.pallas_call(
        _combine_kernel,
        grid=(n,),
        in_specs=[cur(), cur(), nxt(), nxt(),
                  pl.BlockSpec((tm, 8), tok),
                  pl.BlockSpec((tm * SUBLANES, LANES), tok),
                  pl.BlockSpec(memory_space=pl.ANY),
                  pl.BlockSpec((1, d), const), pl.BlockSpec((1, d), const)],
        out_specs=pl.BlockSpec((tm, d), tok),
        out_shape=jax.ShapeDtypeStruct((t, d), F32),
        scratch_shapes=[pltpu.VMEM((2, 2, tm * PACK_ROWS, LANES), U32), pltpu.SemaphoreType.DMA((2,))],
        compiler_params=_cparams("arbitrary"),
        name="combine",
    )(d1, d2, d1, d2, wts, ht, yb, ln_g, ln_b)


def _token_mixing(x2d, mem2d, batch, seq, m_len, w_in, w_gate, b_gate, w_mem_kv, rpb, w_fo, w_na, w_mo, w_out,
                  ln_g, ln_b):
    k_mem, v_mem = _memkv(mem2d, w_mem_kv.astype(BF), m_len)
    u_f, k, q3, v3, mo = _proj(x2d, w_in, k_mem, v_mem, seq, m_len)
    fo = _fourier(u_f, batch, seq)
    na = _natten(q3, k, v3, rpb, batch, seq)
    return _merge(x2d, fo, na, mo, w_gate.astype(BF), b_gate.astype(F32)[None, :], w_fo.astype(BF),
                  w_na.astype(BF), w_mo.astype(BF), w_out.astype(BF), ln_g.astype(F32)[None, :],
                  ln_b.astype(F32)[None, :])


def _moe(ht, w_rg, b_rg, w_re, b_re, w_eg, w_eu, w_ed, ln_g, ln_b):
    t = ht.shape[0] // SUBLANES
    idx, wts, cnt = _router(ht, w_rg, b_rg, w_re, b_re)
    counts = cnt[:, 0].astype(I32)
    padded = (counts + MOE_BLOCK - 1) // MOE_BLOCK * MOE_BLOCK
    pad_end = jnp.cumsum(padded).astype(I32)
    pad_start = pad_end - padded
    n_pad = 2 * t + N_EXPERTS * MOE_BLOCK
    nb = n_pad // MOE_BLOCK
    blk_start = jnp.arange(nb, dtype=I32) * MOE_BLOCK
    blk_expert = jnp.minimum(jnp.sum((pad_end[None, :] <= blk_start[:, None]).astype(I32), axis=1), N_EXPERTS - 1)
    n_used = pad_end[-1:] // MOE_BLOCK
    seg_left = counts[None, :] - (blk_start[:, None] - pad_start[None, :])
    own = blk_expert[:, None] == jnp.arange(N_EXPERTS, dtype=I32)[None, :]
    blk_valid = jnp.clip(jnp.sum(jnp.where(own, seg_left, 0), axis=1), 0, MOE_BLOCK).astype(I32)
    ids = jnp.arange(N_EXPERTS, dtype=I32)
    later_used = (padded[None, :] > 0) & (ids[None, :] > ids[:, None])
    next_expert = jnp.min(jnp.where(later_used, ids[None, :], N_EXPERTS), axis=1)
    next_expert = jnp.where(next_expert == N_EXPERTS, -1, next_expert).astype(I32)
    d1, d2 = _dests(idx, pad_start)
    xg = _dispatch(ht, d1, d2, pad_start, pad_end, n_pad)
    yb = _experts(xg, blk_expert, next_expert, blk_valid, n_used, w_eg, w_eu, w_ed)
    return _combine(ht, yb, d1, d2, wts, ln_g.astype(F32)[None, :], ln_b.astype(F32)[None, :])


def kernel(x, mem, w_in, w_gate, b_gate, w_mem_kv, rpb, w_fourier_o, w_na_o, w_mem_o, w_out, ln1_g, ln1_b,
           w_router_group, b_router_group, w_router_expert, b_router_expert, w_exp_gate, w_exp_up,
           w_exp_down, ln2_g, ln2_b):
    batch, seq, d = x.shape
    m_len = mem.shape[1]
    x2d = x.reshape(batch * seq, d)
    mem2d = mem.reshape(batch * m_len, d)
    for l in range(w_in.shape[0]):
        ht = _token_mixing(x2d, mem2d, batch, seq, m_len, w_in[l], w_gate[l], b_gate[l], w_mem_kv[l], rpb[l],
                           w_fourier_o[l], w_na_o[l], w_mem_o[l], w_out[l], ln1_g[l], ln1_b[l])
        x2d = _moe(ht, w_router_group[l], b_router_group[l], w_router_expert[l], b_router_expert[l],
                   w_exp_gate[l], w_exp_up[l], w_exp_down[l], ln2_g[l], ln2_b[l])
    return x2d.reshape(batch, seq, d)
```

```python
import functools

import numpy as np
import jax
import jax.numpy as jnp
from jax import lax
from jax.experimental import pallas as pl
from jax.experimental.pallas import tpu as pltpu

BF = jnp.bfloat16
F32 = jnp.float32
I32 = jnp.int32
U32 = jnp.uint32

GRID_W = 64
MEM_HEADS = 4
MEM_HEAD_DIM = 128
MEM_DIM = MEM_HEADS * MEM_HEAD_DIM
FOURIER_GROUPS = 4
FOURIER_GROUP_DIM = 128
FOURIER_DIM = FOURIER_GROUPS * FOURIER_GROUP_DIM
NA_HEADS = 8
NA_HEAD_DIM = 64
NA_DIM = NA_HEADS * NA_HEAD_DIM
NA_ROW_WIN = 8
NA_COL_WIN = 16
N_GROUPS = 8
EXPERTS_PER_GROUP = 8
N_EXPERTS = N_GROUPS * EXPERTS_PER_GROUP
DEPTH = 1
ALPHA = (2.0 * DEPTH) ** 0.25
LN_EPS = 1e-5
NA_SCALE = NA_HEAD_DIM ** -0.5
LOG2_E = 1.4426950408889634
MEM_SCALE = MEM_HEAD_DIM ** -0.5
MASK_NEG = -1e30

LANES = 128
SUBLANES = 8
FFT_N2 = 128
TM_PROJ = 512
TM_MERGE = 512
MERGE_SUBTILES = 2
TM_ROUTE = 1024
TM_DISPATCH = 1024
TM_COMBINE = 256
NA_GROUP_ROWS = 4
NA_GROUP_WIN = NA_GROUP_ROWS + NA_ROW_WIN
NA_GROUPS_PER_STEP = 16
NA_SUM_ROWS = 16
F1_POS = 16
F2_SLABS = 8
MOE_BLOCK = 512
EXPERT_PARTS = 2
PACK_ROWS = 4
EXPERT_SUBBLOCKS = 2
ROUTE_ROWS = 128
ROUTE_STEPS_PER_PLANE = SUBLANES * LANES // TM_ROUTE
DMA_QUEUES = 2
ROW_GROUP = 16
VMEM_LIMIT = 56 * 1024 * 1024


def _cparams(*sem):
    return pltpu.CompilerParams(dimension_semantics=sem, vmem_limit_bytes=VMEM_LIMIT)


def _dot(a, b):
    return jnp.dot(a, b, preferred_element_type=F32)


def _dot_nt(a, b):
    return lax.dot_general(a, b, (((1,), (1,)), ((), ())), preferred_element_type=F32)


def _layer_norm(y, g, b):
    mu = jnp.mean(y, axis=-1, keepdims=True)
    yc = y - mu
    var = jnp.mean(yc * yc, axis=-1, keepdims=True)
    return yc * lax.rsqrt(var + LN_EPS) * g + b


def _store_tile_rows(ref, val):
    n = val.shape[0]
    for s in range(SUBLANES):
        ref[pl.ds(s, n, stride=SUBLANES), :] = val[:, s * LANES:(s + 1) * LANES]


def _load_tile_rows(ref):
    n = ref.shape[0] // SUBLANES
    return jnp.concatenate([ref[pl.ds(s, n, stride=SUBLANES), :] for s in range(SUBLANES)], axis=1)


def _memkv_kernel(mem_ref, w_ref, k_ref, v_ref):
    kv = _dot(mem_ref[...].astype(BF), w_ref[...])
    k_ref[...] = kv[:, :MEM_DIM].astype(BF)
    v_ref[...] = kv[:, MEM_DIM:].astype(BF)


def _memkv(mem2d, w_kv, m_len):
    rows, d = mem2d.shape
    return pl.pallas_call(
        _memkv_kernel,
        grid=(rows // m_len,),
        in_specs=[pl.BlockSpec((m_len, d), lambda i: (i, 0)),
                  pl.BlockSpec((d, 2 * MEM_DIM), lambda i: (0, 0))],
        out_specs=[pl.BlockSpec((m_len, MEM_DIM), lambda i: (i, 0))] * 2,
        out_shape=[jax.ShapeDtypeStruct((rows, MEM_DIM), BF)] * 2,
        compiler_params=_cparams("parallel"),
        name="memkv",
    )(mem2d, w_kv)


def _proj_kernel(x_ref, w_ref, wt_ref, km_ref, vm_ref, uf_ref, k_ref, q3_ref, v3_ref, mo_ref):
    xb = x_ref[...].astype(BF)

    def seg(j):
        return _dot(xb, w_ref[:, j * 512:(j + 1) * 512])

    uf_ref[...] = seg(0)
    k_ref[...] = seg(1).astype(BF)
    q_t = _dot_nt(wt_ref[:NA_DIM, :], xb) * (NA_SCALE * LOG2_E)
    v_t = _dot_nt(wt_ref[NA_DIM:, :], xb)
    for s in range(q3_ref.shape[0]):
        q3_ref[s] = q_t[:, s * LANES:(s + 1) * LANES].astype(BF)
        v3_ref[s] = v_t[:, s * LANES:(s + 1) * LANES].astype(BF)
    qm = seg(2).astype(BF)
    for h in range(MEM_HEADS):
        sl = slice(h * MEM_HEAD_DIM, (h + 1) * MEM_HEAD_DIM)
        s = _dot_nt(qm[:, sl], km_ref[:, sl]) * MEM_SCALE
        m = jnp.max(s, axis=-1, keepdims=True)
        p = jnp.exp(s - m)
        l = jnp.sum(p, axis=-1, keepdims=True)
        o = _dot(p.astype(BF), vm_ref[:, sl])
        mo_ref[:, sl] = (o / l).astype(BF)


def _proj(x2d, w_in, k_mem, v_mem, seq, m_len):
    t, d = x2d.shape
    tm = TM_PROJ
    tiles_per_batch = seq // tm
    w_tok = jnp.concatenate([w_in[:, 0:512], w_in[:, 1024:1536], w_in[:, 2048:2560]], axis=1).astype(BF)
    w_chan = jnp.concatenate([w_in[:, 512:1024], w_in[:, 1536:2048]], axis=1).T.astype(BF)
    tok = lambda i: (i, 0)
    slab = lambda i: (i, 0, 0)
    memb = lambda i: (i // tiles_per_batch, 0)
    const = lambda i: (0, 0)
    tok_out = pl.BlockSpec((tm, 512), tok)
    slab_out = pl.BlockSpec((tm // LANES, NA_DIM, LANES), slab)
    slab_shape = jax.ShapeDtypeStruct((t // LANES, NA_DIM, LANES), BF)
    return pl.pallas_call(
        _proj_kernel,
        grid=(t // tm,),
        in_specs=[pl.BlockSpec((tm, d), tok),
                  pl.BlockSpec(w_tok.shape, const),
                  pl.BlockSpec(w_chan.shape, const),
                  pl.BlockSpec((m_len, MEM_DIM), memb),
                  pl.BlockSpec((m_len, MEM_DIM), memb)],
        out_specs=[tok_out, tok_out, slab_out, slab_out, tok_out],
        out_shape=[jax.ShapeDtypeStruct((t, 512), F32), jax.ShapeDtypeStruct((t, 512), BF)] + [slab_shape] * 2
                  + [jax.ShapeDtypeStruct((t, 512), BF)],
        compiler_params=_cparams("parallel"),
        name="proj",
    )(x2d, w_tok, w_chan, k_mem, v_mem)


def _fourier_tables(n1, n2):
    n = n1 * n2
    k1 = np.arange(n1)
    ang1 = 2.0 * np.pi * ((k1[:, None] * k1[None, :]) % n1) / n1
    norm = 1.0 / np.sqrt(float(n) * FOURIER_GROUP_DIM)
    m1 = np.concatenate([np.cos(ang1), -np.sin(ang1)], axis=0) * norm
    kk = k1[:, None, None] + n1 * np.arange(n2)[None, :, None]
    nn = np.arange(n2)[None, None, :]
    ang2 = 2.0 * np.pi * ((kk * nn) % n) / n
    c2, s2 = np.cos(ang2), np.sin(ang2)
    m2 = np.concatenate([np.concatenate([c2, s2], axis=2),
                         np.concatenate([-s2, c2], axis=2)], axis=1)
    c = np.arange(FOURIER_GROUP_DIM)
    angc = 2.0 * np.pi * ((c[:, None] * c[None, :]) % FOURIER_GROUP_DIM) / FOURIER_GROUP_DIM
    mc = np.concatenate([np.cos(angc), np.sin(angc)], axis=0)
    as_bf = lambda a: jnp.asarray(a.astype(np.float32)).astype(BF)
    return as_bf(m1), as_bf(m2), as_bf(mc)


def _f1_kernel(u_ref, m_ref, a_ref):
    n1, pos, _ = u_ref.shape
    x = jnp.concatenate([u_ref[:, p, :] for p in range(pos)], axis=1).astype(BF)
    r = _dot(m_ref[...], x)
    a_ref[0] = r[:n1].astype(BF)
    a_ref[1] = r[n1:].astype(BF)


def _f2_kernel(a_ref, m2_ref, mc_ref, o_ref):
    slabs = a_ref.shape[1]
    n2 = a_ref.shape[2]
    for t in range(slabs):
        a = jnp.concatenate([a_ref[0, t], a_ref[1, t]], axis=0)
        y = _dot(m2_ref[t], a)
        yr = y[:n2].astype(BF)
        yi = y[n2:].astype(BF)
        yy = jnp.concatenate([jnp.concatenate([yr[:, g * LANES:(g + 1) * LANES], yi[:, g * LANES:(g + 1) * LANES]],
                                              axis=1) for g in range(FOURIER_GROUPS)], axis=0)
        z = _dot(yy, mc_ref[...])
        for g in range(FOURIER_GROUPS):
            o_ref[:, t, g * LANES:(g + 1) * LANES] = z[g * n2:(g + 1) * n2]


def _fourier(u_f, batch, seq):
    n2 = FFT_N2
    n1 = seq // n2
    m1, m2, mc = _fourier_tables(n1, n2)
    cols = n2 * FOURIER_DIM
    pos = min(F1_POS, n2)
    a = pl.pallas_call(
        _f1_kernel,
        grid=(batch, n2 // pos),
        in_specs=[pl.BlockSpec((None, n1, pos, FOURIER_DIM), lambda b, j: (b, 0, j, 0)),
                  pl.BlockSpec((2 * n1, n1), lambda b, j: (0, 0))],
        out_specs=pl.BlockSpec((None, 2, n1, pos * FOURIER_DIM), lambda b, j: (b, 0, 0, j)),
        out_shape=jax.ShapeDtypeStruct((batch, 2, n1, cols), BF),
        compiler_params=_cparams("parallel", "parallel"),
        name="fourier1",
    )(u_f.reshape(batch, n1, n2, FOURIER_DIM), m1)
    kb = min(F2_SLABS, n1)
    out = pl.pallas_call(
        _f2_kernel,
        grid=(n1 // kb, batch),
        in_specs=[pl.BlockSpec((None, 2, kb, n2, FOURIER_DIM), lambda j, b: (b, 0, j, 0, 0)),
                  pl.BlockSpec((kb, 2 * n2, 2 * n2), lambda j, b: (j, 0, 0)),
                  pl.BlockSpec((2 * FOURIER_GROUP_DIM, FOURIER_GROUP_DIM), lambda j, b: (0, 0))],
        out_specs=pl.BlockSpec((None, n2, kb, FOURIER_DIM), lambda j, b: (b, 0, j, 0)),
        out_shape=jax.ShapeDtypeStruct((batch, n2, n1, FOURIER_DIM), F32),
        compiler_params=_cparams("parallel", "parallel"),
        name="fourier2",
    )(a.reshape(batch, 2, n1, n2, FOURIER_DIM), m2, mc)
    return out.reshape(batch * seq, FOURIER_DIM)


def _na_group_start(r0, n_rows):
    rs0 = min(max(r0 - NA_ROW_WIN // 2, 0), n_rows - NA_ROW_WIN)
    return min(rs0, n_rows - NA_GROUP_WIN)


def _na_bias_table(rpb, n_rows):
    g = NA_GROUP_ROWS
    assert n_rows >= 2 * NA_GROUP_WIN and n_rows % g == 0 and (NA_ROW_WIN // 2) % g == 0
    cols = np.arange(GRID_W)
    cs = np.clip(cols - NA_COL_WIN // 2, 0, GRID_W - NA_COL_WIN)
    kc = np.arange(GRID_W)
    in_win = (kc[None, :] >= cs[:, None]) & (kc[None, :] < cs[:, None] + NA_COL_WIN)
    dc = kc[None, :] - cols[:, None] + (NA_COL_WIN - 1)
    n_dc = 2 * NA_COL_WIN - 1
    pick = ((dc.reshape(-1)[None, :] == np.arange(n_dc)[:, None]) & in_win.reshape(-1)[None, :])
    picked = jnp.dot(rpb.astype(F32).reshape(-1, n_dc), jnp.asarray(pick.astype(np.float32)),
                     precision=lax.Precision.HIGHEST)
    picked = picked.reshape(NA_HEADS, 2 * NA_ROW_WIN - 1, GRID_W, GRID_W)
    full_t = jnp.where(in_win[None, None], picked * LOG2_E, MASK_NEG).transpose(0, 1, 3, 2)
    masked = jnp.full((NA_HEADS, GRID_W, GRID_W), MASK_NEG, F32)
    half = NA_ROW_WIN // 2
    group_rows = list(range(0, half, g)) + [half] + list(range(n_rows - half, n_rows, g))
    tabs = []
    for r0 in group_rows:
        start = _na_group_start(r0, n_rows)
        key_rows = []
        for kr in range(NA_GROUP_WIN):
            blocks = []
            for rr in range(g):
                r = r0 + rr
                rs = min(max(r - half, 0), n_rows - NA_ROW_WIN)
                in_rows = rs <= start + kr < rs + NA_ROW_WIN
                blocks.append(full_t[:, start + kr - r + NA_ROW_WIN - 1] if in_rows else masked)
            key_rows.append(jnp.concatenate(blocks, axis=2))
        tabs.append(jnp.concatenate(key_rows, axis=1))
    return jnp.stack(tabs, axis=0)


def _na_kernel(q3_ref, k_ref, v3_ref, tab_ref, o_ref, *, n_rows):
    i = pl.program_id(2)
    g = NA_GROUP_ROWS
    slabs_per_group = g * GRID_W // LANES
    groups = q3_ref.shape[0] // slabs_per_group
    win = NA_GROUP_WIN * GRID_W
    tokens = g * GRID_W
    half = NA_ROW_WIN // 2
    n_top = half // g
    chan = lax.broadcasted_iota(I32, (2 * NA_HEAD_DIM, tokens), 0)
    first_head = chan < NA_HEAD_DIM

    def scores(jj):
        r0 = (i * groups + jj) * g
        start = jnp.minimum(jnp.clip(r0 - half, 0, n_rows - NA_ROW_WIN), n_rows - NA_GROUP_WIN)
        cfg = jnp.where(r0 < half, r0 // g,
                        jnp.where(r0 >= n_rows - half, n_top + 1 + (r0 - (n_rows - half)) // g, n_top))
        kw = k_ref[pl.ds(pl.multiple_of(start * GRID_W, LANES), win), :]
        qt = jnp.concatenate([q3_ref[jj * slabs_per_group + t] for t in range(slabs_per_group)], axis=1)
        zero = jnp.zeros_like(qt)
        q_bd = jnp.concatenate([jnp.where(first_head, qt, zero), jnp.where(first_head, zero, qt)], axis=1)
        s = _dot(kw, q_bd) + jnp.concatenate([tab_ref[cfg, 0], tab_ref[cfg, 1]], axis=1)
        return s, start

    ones_rows = jnp.ones((NA_SUM_ROWS, win), BF)

    def outputs(jj, p, start):
        slab = start // 2
        vw = jnp.concatenate([v3_ref[slab + t] for t in range(NA_GROUP_WIN // 2)] , axis=1)
        o_l = _dot(jnp.concatenate([vw, ones_rows], axis=0), p)
        o_t = o_l[:2 * NA_HEAD_DIM] / o_l[2 * NA_HEAD_DIM:2 * NA_HEAD_DIM + 1]
        o = jnp.where(first_head, o_t[:, :tokens], o_t[:, tokens:])
        o_ref[jj * tokens:(jj + 1) * tokens, :] = jnp.transpose(o).astype(BF)

    pending = None
    for jj in range(groups):
        s, start = scores(jj)
        if pending is not None:
            outputs(*pending)
        p = jnp.exp2(s - jnp.max(s, axis=0, keepdims=True))
        pending = (jj, p.astype(BF), start)
    outputs(*pending)


def _natten(q3, k, v3, rpb, batch, seq):
    assert 2 * GRID_W == LANES and 2 * NA_HEAD_DIM == LANES and NA_GROUP_ROWS % 2 == 0
    n_rows = seq // GRID_W
    slabs = n_rows // 2
    step_slabs = min(NA_GROUPS_PER_STEP * NA_GROUP_ROWS // 2, slabs)
    steps = slabs // step_slabs
    hp = NA_HEADS // 2
    tab = _na_bias_table(rpb, n_rows)
    return pl.pallas_call(
        functools.partial(_na_kernel, n_rows=n_rows),
        grid=(hp, batch, steps),
        in_specs=[pl.BlockSpec((step_slabs, LANES, LANES), lambda p, b, i: (b * steps + i, p, 0)),
                  pl.BlockSpec((seq, LANES), lambda p, b, i: (b, p)),
                  pl.BlockSpec((slabs, LANES, LANES), lambda p, b, i: (b, p, 0)),
                  pl.BlockSpec((tab.shape[0], 2) + tab.shape[2:], lambda p, b, i: (0, p, 0, 0))],
        out_specs=pl.BlockSpec((step_slabs * LANES, LANES), lambda p, b, i: (b * steps + i, p)),
        out_shape=jax.ShapeDtypeStruct((batch * seq, NA_DIM), BF),
        compiler_params=_cparams("parallel", "parallel", "parallel"),
        name="natten",
    )(q3, k, v3, tab)


def _merge_kernel(x_ref, fo_ref, na_ref, mo_ref, wg_ref, bg_ref, wf_ref, wn_ref, wm_ref, wo_ref,
                  g_ref, b_ref, ht_ref):
    d = x_ref.shape[1]
    rows = x_ref.shape[0] // MERGE_SUBTILES

    def pre_norm(h):
        sl = pl.ds(h * rows, rows)
        x = x_ref[sl, :]
        xb = x.astype(BF)
        merged = None
        for j, (br_ref, w_ref) in enumerate(((fo_ref, wf_ref), (na_ref, wn_ref), (mo_ref, wm_ref))):
            z = _dot(xb, wg_ref[:, j * d:(j + 1) * d]) + bg_ref[:, j * d:(j + 1) * d]
            gate = 1.0 / (1.0 + jnp.exp(-z))
            term = gate * _dot(br_ref[sl, :].astype(BF), w_ref[...])
            merged = term if merged is None else merged + term
        return ALPHA * x + _dot(merged.astype(BF), wo_ref[...])

    ys = [pre_norm(h) for h in range(MERGE_SUBTILES)]
    for h, y in enumerate(ys):
        _store_tile_rows(ht_ref.at[pl.ds(h * rows * SUBLANES, rows * SUBLANES)],
                         _layer_norm(y, g_ref[...], b_ref[...]))


def _merge(x2d, fo, na, mo, w_gate, b_gate, w_fo, w_na, w_mo, w_out, ln_g, ln_b):
    t, d = x2d.shape
    assert d == SUBLANES * LANES
    tm = TM_MERGE
    tok = lambda i: (i, 0)
    full = lambda a: pl.BlockSpec(a.shape, lambda i: (0, 0))
    return pl.pallas_call(
        _merge_kernel,
        grid=(t // tm,),
        in_specs=[pl.BlockSpec((tm, d), tok)] + [pl.BlockSpec((tm, 512), tok)] * 3
                 + [full(a) for a in (w_gate, b_gate, w_fo, w_na, w_mo, w_out, ln_g, ln_b)],
        out_specs=pl.BlockSpec((tm * SUBLANES, LANES), tok),
        out_shape=jax.ShapeDtypeStruct((t * SUBLANES, LANES), F32),
        compiler_params=_cparams("parallel"),
        name="merge",
    )(x2d, fo, na, mo, w_gate, b_gate, w_fo, w_na, w_mo, w_out, ln_g, ln_b)


def _router_kernel(ht_ref, whi_ref, wlo_ref, b_ref, oi_ref, of_ref, cnt_ref, carry_ref):
    tm = ht_ref.shape[0] // SUBLANES

    @pl.when(pl.program_id(0) == 0)
    def _():
        carry_ref[...] = jnp.zeros_like(carry_ref)

    h = _load_tile_rows(ht_ref)
    hh = h.astype(BF)
    hl = (h - hh.astype(F32)).astype(BF)
    logits = (_dot_nt(whi_ref[...], hh) + _dot_nt(wlo_ref[...], hh) + _dot_nt(whi_ref[...], hl)
              + b_ref[...])
    gl = logits[:N_GROUPS, :]
    g_row = lax.broadcasted_iota(I32, (N_GROUPS, tm), 0)
    gmax = jnp.max(gl, axis=0, keepdims=True)
    g_idx = jnp.min(jnp.where(gl == gmax, g_row, N_GROUPS), axis=0, keepdims=True)
    p_group = 1.0 / jnp.sum(jnp.exp(gl - gmax), axis=0, keepdims=True)
    e_row = lax.broadcasted_iota(I32, (N_EXPERTS, tm), 0)
    el = jnp.where((e_row >> 3) == g_idx, logits[N_GROUPS:N_GROUPS + N_EXPERTS, :], MASK_NEG)
    v1 = jnp.max(el, axis=0, keepdims=True)
    i1 = jnp.min(jnp.where(el == v1, e_row, N_EXPERTS), axis=0, keepdims=True)
    el2 = jnp.where(e_row == i1, MASK_NEG, el)
    v2 = jnp.max(el2, axis=0, keepdims=True)
    i2 = jnp.min(jnp.where(el2 == v2, e_row, N_EXPERTS), axis=0, keepdims=True)
    tt = jnp.exp(v2 - v1)
    w1 = p_group / (1.0 + tt)
    w2 = p_group * tt / (1.0 + tt)
    sel1 = e_row == i1
    sel2 = e_row == i2
    onehot = jnp.where(sel1 | sel2, 1.0, 0.0)
    t_in = lax.broadcasted_iota(I32, (tm, tm), 0)
    t_out = lax.broadcasted_iota(I32, (tm, tm), 1)
    tri = jnp.where(t_in < t_out, 1.0, 0.0).astype(BF)
    prefix = _dot(onehot.astype(BF), tri) + carry_ref[...]
    r1 = jnp.sum(jnp.where(sel1, prefix, 0.0), axis=0, keepdims=True)
    r2 = jnp.sum(jnp.where(sel2, prefix, 0.0), axis=0, keepdims=True)
    carry_ref[...] += jnp.sum(onehot, axis=1, keepdims=True)
    cnt_ref[...] = carry_ref[...]
    sub = pl.program_id(0) % ROUTE_STEPS_PER_PLANE
    for f, vals in enumerate((i1, i2, r1.astype(I32), r2.astype(I32))):
        for c in range(tm // LANES):
            oi_ref[f, pl.ds(sub * (tm // LANES) + c, 1), :] = vals[:, c * LANES:(c + 1) * LANES]
    w_row = lax.broadcasted_iota(I32, (LANES, tm), 0)
    w_rows = jnp.where(w_row == 0, w1, jnp.where(w_row == 1, w2, 0.0))
    of_ref[...] = jnp.transpose(w_rows)[:, :8]


def _router(ht, w_rg, b_rg, w_re, b_re):
    t = ht.shape[0] // SUBLANES
    d = w_rg.shape[0]
    tm = TM_ROUTE
    pad = ROUTE_ROWS - N_GROUPS - N_EXPERTS
    w = jnp.concatenate([w_rg.astype(F32), w_re.astype(F32), jnp.zeros((d, pad), F32)], axis=1).T
    b = jnp.concatenate([b_rg.astype(F32), b_re.astype(F32), jnp.zeros((pad,), F32)])[:, None]
    w_hi = w.astype(BF)
    w_lo = (w - w_hi.astype(F32)).astype(BF)
    tok = lambda i: (i, 0)
    const = lambda i: (0, 0)
    return pl.pallas_call(
        _router_kernel,
        grid=(t // tm,),
        in_specs=[pl.BlockSpec((tm * SUBLANES, LANES), tok),
                  pl.BlockSpec((ROUTE_ROWS, d), const),
                  pl.BlockSpec((ROUTE_ROWS, d), const),
                  pl.BlockSpec((ROUTE_ROWS, 1), const)],
        out_specs=[pl.BlockSpec((4, SUBLANES, LANES), lambda i: (0, i // ROUTE_STEPS_PER_PLANE, 0)),
                   pl.BlockSpec((tm, 8), tok),
                   pl.BlockSpec((N_EXPERTS, 1), const)],
        out_shape=[jax.ShapeDtypeStruct((4, t // LANES, LANES), I32), jax.ShapeDtypeStruct((t, 8), F32),
                   jax.ShapeDtypeStruct((N_EXPERTS, 1), F32)],
        scratch_shapes=[pltpu.VMEM((N_EXPERTS, 1), F32)],
        compiler_params=_cparams("arbitrary"),
        name="router",
    )(ht, w_hi, w_lo, b)


def _row_groups(n_rows, copies_of_row, start):
    def group(g, c):
        copies = []
        for u in range(ROW_GROUP):
            copies.extend(copies_of_row(g * ROW_GROUP + u))
        for k, (src, dst, sem) in enumerate(copies):
            if start:
                pltpu.async_copy(src, dst, sem, priority=k % DMA_QUEUES)
            else:
                pltpu.make_async_copy(src, dst, sem).wait()
        return c

    lax.fori_loop(0, n_rows // ROW_GROUP, group, 0)


def _dests_kernel(ps_ref, idx_ref, d_ref):
    for a in range(2):
        e = idx_ref[a]
        d = idx_ref[2 + a]
        for k in range(N_EXPERTS):
            d = d + jnp.where(e == k, ps_ref[k], 0)
        d_ref[a] = d


def _dests(idx, pad_start):
    planes = idx.shape[1]
    out = pl.pallas_call(
        _dests_kernel,
        grid_spec=pltpu.PrefetchScalarGridSpec(
            num_scalar_prefetch=1,
            grid=(1,),
            in_specs=[pl.BlockSpec(idx.shape, lambda i, ps: (0, 0, 0))],
            out_specs=pl.BlockSpec((2, planes, LANES), lambda i, ps: (0, 0, 0)),
        ),
        out_shape=jax.ShapeDtypeStruct((2, planes, LANES), I32),
        compiler_params=_cparams("arbitrary"),
        name="dests",
    )(pad_start, idx)
    return out[0].reshape(-1), out[1].reshape(-1)


def _pack_words(val):
    words = []
    for c in range(PACK_ROWS):
        hi = pltpu.bitcast(val[:, c * LANES:(c + 1) * LANES].astype(BF).astype(F32), U32)
        lo = pltpu.bitcast(val[:, (c + PACK_ROWS) * LANES:(c + PACK_ROWS + 1) * LANES].astype(BF).astype(F32), U32)
        words.append(hi | (lo >> 16))
    return words


def _unpack_words(words):
    his = [pltpu.bitcast(w & jnp.uint32(0xFFFF0000), F32) for w in words]
    los = [pltpu.bitcast(w << 16, F32) for w in words]
    return jnp.concatenate(his + los, axis=1)


def _load_packed(ref):
    n = ref.shape[0] // PACK_ROWS
    return [ref[pl.ds(c, n, stride=PACK_ROWS), :] for c in range(PACK_ROWS)]


def _store_packed(ref, words):
    n = ref.shape[0] // PACK_ROWS
    for c, w in enumerate(words):
        ref[pl.ds(c, n, stride=PACK_ROWS), :] = w


def _packed_row_copy(src_ref, src_row, dst_ref, dst_row, sem):
    src = pl.multiple_of(src_row * PACK_ROWS, PACK_ROWS)
    dst = pl.multiple_of(dst_row * PACK_ROWS, PACK_ROWS)
    return src_ref.at[pl.ds(src, PACK_ROWS)], dst_ref.at[pl.ds(dst, PACK_ROWS)], sem


def _dispatch_kernel(ps_ref, pe_ref, d1_ref, d2_ref, ht_ref, xg_ref, pk, zbuf, sem, zsem):
    i = pl.program_id(0)
    tm = d1_ref.shape[0]
    blk = MOE_BLOCK * PACK_ROWS
    n_blocks = xg_ref.shape[0] // blk

    @pl.when(i == 0)
    def _():
        zbuf[...] = jnp.zeros_like(zbuf)

        def zero_copy(block):
            return pltpu.make_async_copy(zbuf, xg_ref.at[pl.ds(pl.multiple_of(block * blk, blk), blk)], zsem)

        def seg_issue(e, c):
            @pl.when(pe_ref[e] > ps_ref[e])
            def _():
                zero_copy(pe_ref[e] // MOE_BLOCK - 1).start()
            return c

        def seg_drain(e, c):
            @pl.when(pe_ref[e] > ps_ref[e])
            def _():
                zero_copy(pe_ref[e] // MOE_BLOCK - 1).wait()
            return c

        def tail_issue(b, c):
            zero_copy(b).start()
            return c

        def tail_drain(b, c):
            zero_copy(b).wait()
            return c

        first_tail = pe_ref[N_EXPERTS - 1] // MOE_BLOCK
        lax.fori_loop(0, N_EXPERTS, seg_issue, 0)
        lax.fori_loop(first_tail, n_blocks, tail_issue, 0)
        lax.fori_loop(0, N_EXPERTS, seg_drain, 0)
        lax.fori_loop(first_tail, n_blocks, tail_drain, 0)

    _store_packed(pk, _pack_words(_load_tile_rows(ht_ref)))
    _row_groups(tm, lambda t: [_packed_row_copy(pk, t, xg_ref, d1_ref[t], sem),
                               _packed_row_copy(pk, t, xg_ref, d2_ref[t], sem)], True)
    _row_groups(tm, lambda t: [_packed_row_copy(pk, 0, xg_ref, 0, sem)] * 2, False)


def _dispatch(ht, d1, d2, pad_start, pad_end, n_pad):
    t = d1.shape[0]
    tm = TM_DISPATCH
    smem = lambda: pl.BlockSpec((tm,), lambda i, ps, pe: (i,), memory_space=pltpu.SMEM)
    return pl.pallas_call(
        _dispatch_kernel,
        grid_spec=pltpu.PrefetchScalarGridSpec(
            num_scalar_prefetch=2,
            grid=(t // tm,),
            in_specs=[smem(), smem(), pl.BlockSpec((tm * SUBLANES, LANES), lambda i, ps, pe: (i, 0))],
            out_specs=pl.BlockSpec(memory_space=pl.ANY),
            scratch_shapes=[pltpu.VMEM((tm * PACK_ROWS, LANES), U32), pltpu.VMEM((MOE_BLOCK * PACK_ROWS, LANES), U32),
                            pltpu.SemaphoreType.DMA(()), pltpu.SemaphoreType.DMA(())],
        ),
        out_shape=jax.ShapeDtypeStruct((n_pad * PACK_ROWS, LANES), U32),
        compiler_params=_cparams("arbitrary"),
        name="dispatch",
    )(pad_start, pad_end, d1, d2, ht)


def _expert_kernel(be_ref, nx_ref, nv_ref, nu_ref, x_ref, wg_hbm, wu_hbm, wd_hbm, y_ref,
                   wg_f, wu_f, wd_f, wg_s, wu_s, wd_s, sems):
    j = pl.program_id(0)
    stage = ((wg_hbm, wg_f, wg_s), (wu_hbm, wu_f, wu_s), (wd_hbm, wd_f, wd_s))

    def weight_copies(e):
        return [pltpu.make_async_copy(hbm.at[e], buf, sems.at[k]) for k, (hbm, buf, _) in enumerate(stage)]

    @pl.when(j < nu_ref[0])
    def _():
        e = be_ref[j]

        @pl.when(j == 0)
        def _():
            for cp in weight_copies(e):
                cp.start()

        @pl.when((j == 0) | (e != be_ref[jnp.maximum(j - 1, 0)]))
        def _():
            for cp, (_, buf, dst) in zip(weight_copies(e), stage):
                cp.wait()
                dst[...] = buf[...].astype(BF)
            nxt = nx_ref[e]

            @pl.when(nxt >= 0)
            def _():
                for cp in weight_copies(nxt):
                    cp.start()

        part_rows = x_ref.shape[0] // EXPERT_PARTS

        def run(sub_blocks):
            def up(x_sub):
                xb = _unpack_words(_load_packed(x_sub)).astype(BF)
                a = _dot(xb, wg_s[...])
                u = _dot(xb, wu_s[...])
                return ((a / (1.0 + jnp.exp(-a))) * u).astype(BF)

            mids = [up(x_sub) for x_sub, _ in sub_blocks]
            for (_, y_sub), mid in zip(sub_blocks, mids):
                _store_packed(y_sub, _pack_words(_dot(mid, wd_s[...])))

        def views(ref, start, rows, n):
            return [ref.at[pl.ds(start + h * rows, rows)] for h in range(n)]

        both = nv_ref[j] > MOE_BLOCK // EXPERT_PARTS

        @pl.when(both)
        def _():
            run(list(zip(views(x_ref, 0, part_rows, EXPERT_PARTS), views(y_ref, 0, part_rows, EXPERT_PARTS))))

        @pl.when(jnp.logical_not(both))
        def _():
            rows = part_rows // EXPERT_SUBBLOCKS
            run(list(zip(views(x_ref, 0, rows, EXPERT_SUBBLOCKS), views(y_ref, 0, rows, EXPERT_SUBBLOCKS))))
            tail = y_ref.at[pl.ds(part_rows, part_rows)]
            tail[...] = jnp.zeros_like(tail)

    @pl.when(j >= nu_ref[0])
    def _():
        y_ref[...] = jnp.zeros_like(y_ref)


def _experts(xg, blk_expert, next_expert, blk_valid, n_used, w_eg, w_eu, w_ed):
    blk = MOE_BLOCK * PACK_ROWS
    nb = xg.shape[0] // blk
    d, de = w_eg.shape[1], w_eg.shape[2]
    rows = lambda j, be, nx, nv, nu: (jnp.minimum(j, nu[0] - 1), 0)
    hbm = pl.BlockSpec(memory_space=pl.ANY)
    return pl.pallas_call(
        _expert_kernel,
        grid_spec=pltpu.PrefetchScalarGridSpec(
            num_scalar_prefetch=4,
            grid=(nb,),
            in_specs=[pl.BlockSpec((blk, LANES), rows), hbm, hbm, hbm],
            out_specs=pl.BlockSpec((blk, LANES), lambda j, be, nx, nv, nu: (j, 0)),
            scratch_shapes=[pltpu.VMEM((d, de), F32), pltpu.VMEM((d, de), F32), pltpu.VMEM((de, d), F32),
                            pltpu.VMEM((d, de), BF), pltpu.VMEM((d, de), BF), pltpu.VMEM((de, d), BF),
                            pltpu.SemaphoreType.DMA((3,))],
        ),
        out_shape=jax.ShapeDtypeStruct(xg.shape, U32),
        compiler_params=_cparams("arbitrary"),
        name="experts",
    )(blk_expert, next_expert, blk_valid, n_used, xg, w_eg, w_eu, w_ed)


def _combine_kernel(d1_ref, d2_ref, d1_next_ref, d2_next_ref, wt_ref, ht_ref, yb_ref, g_ref, b_ref, o_ref,
                    buf, sems):
    i = pl.program_id(0)
    n = pl.num_programs(0)
    tm = d1_ref.shape[0]

    def gather_tile(da_ref, db_ref, slot, start):
        def copies(t):
            if not start:
                return [_packed_row_copy(yb_ref, 0, buf.at[slot, a], 0, sems.at[slot]) for a in range(2)]
            return [_packed_row_copy(yb_ref, da_ref[t], buf.at[slot, 0], t, sems.at[slot]),
                    _packed_row_copy(yb_ref, db_ref[t], buf.at[slot, 1], t, sems.at[slot])]
        _row_groups(tm, copies, start)

    def gathered(slot, a):
        return _unpack_words(_load_packed(buf.at[slot, a]))

    slot = i % 2

    @pl.when(i == 0)
    def _():
        gather_tile(d1_ref, d2_ref, 0, True)

    @pl.when(i + 1 < n)
    def _():
        gather_tile(d1_next_ref, d2_next_ref, 1 - slot, True)

    gather_tile(d1_ref, d2_ref, slot, False)
    wt = wt_ref[...]
    ffn = wt[:, 0:1] * gathered(slot, 0) + wt[:, 1:2] * gathered(slot, 1)
    o_ref[...] = _layer_norm(ALPHA * _load_tile_rows(ht_ref) + ffn, g_ref[...], b_ref[...])


def _combine(ht, yb, d1, d2, wts, ln_g, ln_b):
    t = d1.shape[0]
    d = SUBLANES * LANES
    tm = TM_COMBINE
    n = t // tm
    tok = lambda i: (i, 0)
    const = lambda i: (0, 0)
    cur = lambda: pl.BlockSpec((tm,), lambda i: (i,), memory_space=pltpu.SMEM)
    nxt = lambda: pl.BlockSpec((tm,), lambda i: (jnp.minimum(i + 1, n - 1),), memory_space=pltpu.SMEM)
    return pl.pallas_call(
        _combine_kernel,
        grid=(n,),
        in_specs=[cur(), cur(), nxt(), nxt(),
                  pl.BlockSpec((tm, 8), tok),
                  pl.BlockSpec((tm * SUBLANES, LANES), tok),
                  pl.BlockSpec(memory_space=pl.ANY),
                  pl.BlockSpec((1, d), const), pl.BlockSpec((1, d), const)],
        out_specs=pl.BlockSpec((tm, d), tok),
        out_shape=jax.ShapeDtypeStruct((t, d), F32),
        scratch_shapes=[pltpu.VMEM((2, 2, tm * PACK_ROWS, LANES), U32), pltpu.SemaphoreType.DMA((2,))],
        compiler_params=_cparams("arbitrary"),
        name="combine",
    )(d1, d2, d1, d2, wts, ht, yb, ln_g, ln_b)


def _token_mixing(x2d, mem2d, batch, seq, m_len, w_in, w_gate, b_gate, w_mem_kv, rpb, w_fo, w_na, w_mo, w_out,
                  ln_g, ln_b):
    k_mem, v_mem = _memkv(mem2d, w_mem_kv.astype(BF), m_len)
    u_f, k, q3, v3, mo = _proj(x2d, w_in, k_mem, v_mem, seq, m_len)
    fo = _fourier(u_f, batch, seq)
    na = _natten(q3, k, v3, rpb, batch, seq)
    return _merge(x2d, fo, na, mo, w_gate.astype(BF), b_gate.astype(F32)[None, :], w_fo.astype(BF),
                  w_na.astype(BF), w_mo.astype(BF), w_out.astype(BF), ln_g.astype(F32)[None, :],
                  ln_b.astype(F32)[None, :])


def _moe(ht, w_rg, b_rg, w_re, b_re, w_eg, w_eu, w_ed, ln_g, ln_b):
    t = ht.shape[0] // SUBLANES
    idx, wts, cnt = _router(ht, w_rg, b_rg, w_re, b_re)
    counts = cnt[:, 0].astype(I32)
    padded = (counts + MOE_BLOCK - 1) // MOE_BLOCK * MOE_BLOCK
    pad_end = jnp.cumsum(padded).astype(I32)
    pad_start = pad_end - padded
    n_pad = 2 * t + N_EXPERTS * MOE_BLOCK
    nb = n_pad // MOE_BLOCK
    blk_start = jnp.arange(nb, dtype=I32) * MOE_BLOCK
    blk_expert = jnp.minimum(jnp.sum((pad_end[None, :] <= blk_start[:, None]).astype(I32), axis=1), N_EXPERTS - 1)
    n_used = pad_end[-1:] // MOE_BLOCK
    seg_left = counts[None, :] - (blk_start[:, None] - pad_start[None, :])
    own = blk_expert[:, None] == jnp.arange(N_EXPERTS, dtype=I32)[None, :]
    blk_valid = jnp.clip(jnp.sum(jnp.where(own, seg_left, 0), axis=1), 0, MOE_BLOCK).astype(I32)
    ids = jnp.arange(N_EXPERTS, dtype=I32)
    later_used = (padded[None, :] > 0) & (ids[None, :] > ids[:, None])
    next_expert = jnp.min(jnp.where(later_used, ids[None, :], N_EXPERTS), axis=1)
    next_expert = jnp.where(next_expert == N_EXPERTS, -1, next_expert).astype(I32)
    d1, d2 = _dests(idx, pad_start)
    xg = _dispatch(ht, d1, d2, pad_start, pad_end, n_pad)
    yb = _experts(xg, blk_expert, next_expert, blk_valid, n_used, w_eg, w_eu, w_ed)
    return _combine(ht, yb, d1, d2, wts, ln_g.astype(F32)[None, :], ln_b.astype(F32)[None, :])


def kernel(x, mem, w_in, w_gate, b_gate, w_mem_kv, rpb, w_fourier_o, w_na_o, w_mem_o, w_out, ln1_g, ln1_b,
           w_router_group, b_router_group, w_router_expert, b_router_expert, w_exp_gate, w_exp_up,
           w_exp_down, ln2_g, ln2_b):
    batch, seq, d = x.shape
    m_len = mem.shape[1]
    x2d = x.reshape(batch * seq, d)
    mem2d = mem.reshape(batch * m_len, d)
    for l in range(w_in.shape[0]):
        ht = _token_mixing(x2d, mem2d, batch, seq, m_len, w_in[l], w_gate[l], b_gate[l], w_mem_kv[l], rpb[l],
                           w_fourier_o[l], w_na_o[l], w_mem_o[l], w_out[l], ln1_g[l], ln1_b[l])
        x2d = _moe(ht, w_router_group[l], b_router_group[l], w_router_expert[l], b_router_expert[l],
                   w_exp_gate[l], w_exp_up[l], w_exp_down[l], ln2_g[l], ln2_b[l])
    return x2d.reshape(batch, seq, d)
```

```python
import functools

import numpy as np
import jax
import jax.numpy as jnp
from jax import lax
from jax.experimental import pallas as pl
from jax.experimental.pallas import tpu as pltpu

BF = jnp.bfloat16
F32 = jnp.float32
I32 = jnp.int32
U32 = jnp.uint32

GRID_W = 64
MEM_HEADS = 4
MEM_HEAD_DIM = 128
MEM_DIM = MEM_HEADS * MEM_HEAD_DIM
FOURIER_GROUPS = 4
FOURIER_GROUP_DIM = 128
FOURIER_DIM = FOURIER_GROUPS * FOURIER_GROUP_DIM
NA_HEADS = 8
NA_HEAD_DIM = 64
NA_DIM = NA_HEADS * NA_HEAD_DIM
NA_ROW_WIN = 8
NA_COL_WIN = 16
N_GROUPS = 8
EXPERTS_PER_GROUP = 8
N_EXPERTS = N_GROUPS * EXPERTS_PER_GROUP
DEPTH = 1
ALPHA = (2.0 * DEPTH) ** 0.25
LN_EPS = 1e-5
NA_SCALE = NA_HEAD_DIM ** -0.5
LOG2_E = 1.4426950408889634
MEM_SCALE = MEM_HEAD_DIM ** -0.5
MASK_NEG = -1e30

LANES = 128
SUBLANES = 8
FFT_N2 = 128
TM_PROJ = 512
TM_MERGE = 512
MERGE_SUBTILES = 2
TM_ROUTE = 1024
TM_DISPATCH = 1024
TM_COMBINE = 256
NA_GROUP_ROWS = 4
NA_GROUP_WIN = NA_GROUP_ROWS + NA_ROW_WIN
NA_GROUPS_PER_STEP = 16
NA_SUM_ROWS = 16
F1_POS = 16
F2_SLABS = 8
MOE_BLOCK = 512
EXPERT_PARTS = 2
PACK_ROWS = 4
EXPERT_SUBBLOCKS = 2
ROUTE_ROWS = 128
ROUTE_STEPS_PER_PLANE = SUBLANES * LANES // TM_ROUTE
DMA_QUEUES = 2
ROW_GROUP = 16
VMEM_LIMIT = 56 * 1024 * 1024


def _cparams(*sem):
    return pltpu.CompilerParams(dimension_semantics=sem, vmem_limit_bytes=VMEM_LIMIT)


def _dot(a, b):
    return jnp.dot(a, b, preferred_element_type=F32)


def _dot_nt(a, b):
    return lax.dot_general(a, b, (((1,), (1,)), ((), ())), preferred_element_type=F32)


def _layer_norm(y, g, b):
    mu = jnp.mean(y, axis=-1, keepdims=True)
    yc = y - mu
    var = jnp.mean(yc * yc, axis=-1, keepdims=True)
    return yc * lax.rsqrt(var + LN_EPS) * g + b


def _store_tile_rows(ref, val):
    n = val.shape[0]
    for s in range(SUBLANES):
        ref[pl.ds(s, n, stride=SUBLANES), :] = val[:, s * LANES:(s + 1) * LANES]


def _load_tile_rows(ref):
    n = ref.shape[0] // SUBLANES
    return jnp.concatenate([ref[pl.ds(s, n, stride=SUBLANES), :] for s in range(SUBLANES)], axis=1)


def _memkv_kernel(mem_ref, w_ref, k_ref, v_ref):
    kv = _dot(mem_ref[...].astype(BF), w_ref[...])
    k_ref[...] = kv[:, :MEM_DIM].astype(BF)
    v_ref[...] = kv[:, MEM_DIM:].astype(BF)


def _memkv(mem2d, w_kv, m_len):
    rows, d = mem2d.shape
    return pl.pallas_call(
        _memkv_kernel,
        grid=(rows // m_len,),
        in_specs=[pl.BlockSpec((m_len, d), lambda i: (i, 0)),
                  pl.BlockSpec((d, 2 * MEM_DIM), lambda i: (0, 0))],
        out_specs=[pl.BlockSpec((m_len, MEM_DIM), lambda i: (i, 0))] * 2,
        out_shape=[jax.ShapeDtypeStruct((rows, MEM_DIM), BF)] * 2,
        compiler_params=_cparams("parallel"),
        name="memkv",
    )(mem2d, w_kv)


def _proj_kernel(x_ref, w_ref, wt_ref, km_ref, vm_ref, uf_ref, k_ref, q3_ref, v3_ref, mo_ref):
    xb = x_ref[...].astype(BF)

    def seg(j):
        return _dot(xb, w_ref[:, j * 512:(j + 1) * 512])

    heads = [slice(h * MEM_HEAD_DIM, (h + 1) * MEM_HEAD_DIM) for h in range(MEM_HEADS)]
    qm = seg(2).astype(BF)
    scores = [_dot_nt(qm[:, sl], km_ref[:, sl]) * MEM_SCALE for sl in heads]
    uf_ref[...] = seg(0)
    k_ref[...] = seg(1).astype(BF)
    probs = [jnp.exp(s - jnp.max(s, axis=-1, keepdims=True)) for s in scores]
    q_t = _dot_nt(wt_ref[:NA_DIM, :], xb) * (NA_SCALE * LOG2_E)
    v_t = _dot_nt(wt_ref[NA_DIM:, :], xb)
    for s in range(q3_ref.shape[0]):
        q3_ref[s] = q_t[:, s * LANES:(s + 1) * LANES].astype(BF)
        v3_ref[s] = v_t[:, s * LANES:(s + 1) * LANES].astype(BF)
    for sl, p in zip(heads, probs):
        l = jnp.sum(p, axis=-1, keepdims=True)
        mo_ref[:, sl] = (_dot(p.astype(BF), vm_ref[:, sl]) / l).astype(BF)


def _proj(x2d, w_in, k_mem, v_mem, seq, m_len):
    t, d = x2d.shape
    tm = TM_PROJ
    tiles_per_batch = seq // tm
    w_tok = jnp.concatenate([w_in[:, 0:512], w_in[:, 1024:1536], w_in[:, 2048:2560]], axis=1).astype(BF)
    w_chan = jnp.concatenate([w_in[:, 512:1024], w_in[:, 1536:2048]], axis=1).T.astype(BF)
    tok = lambda i: (i, 0)
    slab = lambda i: (i, 0, 0)
    memb = lambda i: (i // tiles_per_batch, 0)
    const = lambda i: (0, 0)
    tok_out = pl.BlockSpec((tm, 512), tok)
    slab_out = pl.BlockSpec((tm // LANES, NA_DIM, LANES), slab)
    slab_shape = jax.ShapeDtypeStruct((t // LANES, NA_DIM, LANES), BF)
    return pl.pallas_call(
        _proj_kernel,
        grid=(t // tm,),
        in_specs=[pl.BlockSpec((tm, d), tok),
                  pl.BlockSpec(w_tok.shape, const),
                  pl.BlockSpec(w_chan.shape, const),
                  pl.BlockSpec((m_len, MEM_DIM), memb),
                  pl.BlockSpec((m_len, MEM_DIM), memb)],
        out_specs=[tok_out, tok_out, slab_out, slab_out, tok_out],
        out_shape=[jax.ShapeDtypeStruct((t, 512), F32), jax.ShapeDtypeStruct((t, 512), BF)] + [slab_shape] * 2
                  + [jax.ShapeDtypeStruct((t, 512), BF)],
        compiler_params=_cparams("parallel"),
        name="proj",
    )(x2d, w_tok, w_chan, k_mem, v_mem)


def _fourier_tables(n1, n2):
    n = n1 * n2
    k1 = np.arange(n1)
    ang1 = 2.0 * np.pi * ((k1[:, None] * k1[None, :]) % n1) / n1
    norm = 1.0 / np.sqrt(float(n) * FOURIER_GROUP_DIM)
    m1 = np.concatenate([np.cos(ang1), -np.sin(ang1)], axis=0) * norm
    kk = k1[:, None, None] + n1 * np.arange(n2)[None, :, None]
    nn = np.arange(n2)[None, None, :]
    ang2 = 2.0 * np.pi * ((kk * nn) % n) / n
    c2, s2 = np.cos(ang2), np.sin(ang2)
    m2 = np.concatenate([np.concatenate([c2, s2], axis=2),
                         np.concatenate([-s2, c2], axis=2)], axis=1)
    c = np.arange(FOURIER_GROUP_DIM)
    angc = 2.0 * np.pi * ((c[:, None] * c[None, :]) % FOURIER_GROUP_DIM) / FOURIER_GROUP_DIM
    mc = np.concatenate([np.cos(angc), np.sin(angc)], axis=0)
    as_bf = lambda a: jnp.asarray(a.astype(np.float32)).astype(BF)
    return as_bf(m1), as_bf(m2), as_bf(mc)


def _f1_kernel(u_ref, m_ref, a_ref):
    n1, pos, _ = u_ref.shape
    x = jnp.concatenate([u_ref[:, p, :] for p in range(pos)], axis=1).astype(BF)
    r = _dot(m_ref[...], x)
    a_ref[0] = r[:n1].astype(BF)
    a_ref[1] = r[n1:].astype(BF)


def _f2_kernel(a_ref, m2_ref, mc_ref, o_ref):
    slabs = a_ref.shape[1]
    n2 = a_ref.shape[2]
    for t in range(slabs):
        a = jnp.concatenate([a_ref[0, t], a_ref[1, t]], axis=0)
        y = _dot(m2_ref[t], a)
        yr = y[:n2].astype(BF)
        yi = y[n2:].astype(BF)
        yy = jnp.concatenate([jnp.concatenate([yr[:, g * LANES:(g + 1) * LANES], yi[:, g * LANES:(g + 1) * LANES]],
                                              axis=1) for g in range(FOURIER_GROUPS)], axis=0)
        z = _dot(yy, mc_ref[...])
        for g in range(FOURIER_GROUPS):
            o_ref[:, t, g * LANES:(g + 1) * LANES] = z[g * n2:(g + 1) * n2]


def _fourier(u_f, batch, seq):
    n2 = FFT_N2
    n1 = seq // n2
    m1, m2, mc = _fourier_tables(n1, n2)
    cols = n2 * FOURIER_DIM
    pos = min(F1_POS, n2)
    a = pl.pallas_call(
        _f1_kernel,
        grid=(batch, n2 // pos),
        in_specs=[pl.BlockSpec((None, n1, pos, FOURIER_DIM), lambda b, j: (b, 0, j, 0)),
                  pl.BlockSpec((2 * n1, n1), lambda b, j: (0, 0))],
        out_specs=pl.BlockSpec((None, 2, n1, pos * FOURIER_DIM), lambda b, j: (b, 0, 0, j)),
        out_shape=jax.ShapeDtypeStruct((batch, 2, n1, cols), BF),
        compiler_params=_cparams("parallel", "parallel"),
        name="fourier1",
    )(u_f.reshape(batch, n1, n2, FOURIER_DIM), m1)
    kb = min(F2_SLABS, n1)
    out = pl.pallas_call(
        _f2_kernel,
        grid=(n1 // kb, batch),
        in_specs=[pl.BlockSpec((None, 2, kb, n2, FOURIER_DIM), lambda j, b: (b, 0, j, 0, 0)),
                  pl.BlockSpec((kb, 2 * n2, 2 * n2), lambda j, b: (j, 0, 0)),
                  pl.BlockSpec((2 * FOURIER_GROUP_DIM, FOURIER_GROUP_DIM), lambda j, b: (0, 0))],
        out_specs=pl.BlockSpec((None, n2, kb, FOURIER_DIM), lambda j, b: (b, 0, j, 0)),
        out_shape=jax.ShapeDtypeStruct((batch, n2, n1, FOURIER_DIM), F32),
        compiler_params=_cparams("parallel", "parallel"),
        name="fourier2",
    )(a.reshape(batch, 2, n1, n2, FOURIER_DIM), m2, mc)
    return out.reshape(batch * seq, FOURIER_DIM)


def _na_group_start(r0, n_rows):
    rs0 = min(max(r0 - NA_ROW_WIN // 2, 0), n_rows - NA_ROW_WIN)
    return min(rs0, n_rows - NA_GROUP_WIN)


def _na_bias_table(rpb, n_rows):
    g = NA_GROUP_ROWS
    assert n_rows >= 2 * NA_GROUP_WIN and n_rows % g == 0 and (NA_ROW_WIN // 2) % g == 0
    cols = np.arange(GRID_W)
    cs = np.clip(cols - NA_COL_WIN // 2, 0, GRID_W - NA_COL_WIN)
    kc = np.arange(GRID_W)
    in_win = (kc[None, :] >= cs[:, None]) & (kc[None, :] < cs[:, None] + NA_COL_WIN)
    dc = kc[None, :] - cols[:, None] + (NA_COL_WIN - 1)
    n_dc = 2 * NA_COL_WIN - 1
    pick = ((dc.reshape(-1)[None, :] == np.arange(n_dc)[:, None]) & in_win.reshape(-1)[None, :])
    picked = jnp.dot(rpb.astype(F32).reshape(-1, n_dc), jnp.asarray(pick.astype(np.float32)),
                     precision=lax.Precision.HIGHEST)
    picked = picked.reshape(NA_HEADS, 2 * NA_ROW_WIN - 1, GRID_W, GRID_W)
    full_t = jnp.where(in_win[None, None], picked * LOG2_E, MASK_NEG).transpose(0, 1, 3, 2)
    masked = jnp.full((NA_HEADS, GRID_W, GRID_W), MASK_NEG, F32)
    half = NA_ROW_WIN // 2
    group_rows = list(range(0, half, g)) + [half] + list(range(n_rows - half, n_rows, g))
    tabs = []
    for r0 in group_rows:
        start = _na_group_start(r0, n_rows)
        key_rows = []
        for kr in range(NA_GROUP_WIN):
            blocks = []
            for rr in range(g):
                r = r0 + rr
                rs = min(max(r - half, 0), n_rows - NA_ROW_WIN)
                in_rows = rs <= start + kr < rs + NA_ROW_WIN
                blocks.append(full_t[:, start + kr - r + NA_ROW_WIN - 1] if in_rows else masked)
            key_rows.append(jnp.concatenate(blocks, axis=2))
        tabs.append(jnp.concatenate(key_rows, axis=1))
    return jnp.stack(tabs, axis=0)


def _na_kernel(q3_ref, k_ref, v3_ref, tab_ref, o_ref, *, n_rows):
    i = pl.program_id(2)
    g = NA_GROUP_ROWS
    slabs_per_group = g * GRID_W // LANES
    groups = q3_ref.shape[0] // slabs_per_group
    win = NA_GROUP_WIN * GRID_W
    tokens = g * GRID_W
    half = NA_ROW_WIN // 2
    n_top = half // g
    chan = lax.broadcasted_iota(I32, (2 * NA_HEAD_DIM, tokens), 0)
    first_head = chan < NA_HEAD_DIM

    def scores(jj):
        r0 = (i * groups + jj) * g
        start = jnp.minimum(jnp.clip(r0 - half, 0, n_rows - NA_ROW_WIN), n_rows - NA_GROUP_WIN)
        cfg = jnp.where(r0 < half, r0 // g,
                        jnp.where(r0 >= n_rows - half, n_top + 1 + (r0 - (n_rows - half)) // g, n_top))
        kw = k_ref[pl.ds(pl.multiple_of(start * GRID_W, LANES), win), :]
        qt = jnp.concatenate([q3_ref[jj * slabs_per_group + t] for t in range(slabs_per_group)], axis=1)
        zero = jnp.zeros_like(qt)
        q_bd = jnp.concatenate([jnp.where(first_head, qt, zero), jnp.where(first_head, zero, qt)], axis=1)
        s = _dot(kw, q_bd) + jnp.concatenate([tab_ref[cfg, 0], tab_ref[cfg, 1]], axis=1)
        return s, start

    ones_rows = jnp.ones((NA_SUM_ROWS, win), BF)

    def outputs(jj, p, start):
        slab = start // 2
        vw = jnp.concatenate([v3_ref[slab + t] for t in range(NA_GROUP_WIN // 2)] , axis=1)
        o_l = _dot(jnp.concatenate([vw, ones_rows], axis=0), p)
        o_t = o_l[:2 * NA_HEAD_DIM] / o_l[2 * NA_HEAD_DIM:2 * NA_HEAD_DIM + 1]
        o = jnp.where(first_head, o_t[:, :tokens], o_t[:, tokens:])
        o_ref[jj * tokens:(jj + 1) * tokens, :] = jnp.transpose(o).astype(BF)

    pending = None
    for jj in range(groups):
        s, start = scores(jj)
        if pending is not None:
            outputs(*pending)
        p = jnp.exp2(s - jnp.max(s, axis=0, keepdims=True))
        pending = (jj, p.astype(BF), start)
    outputs(*pending)


def _natten(q3, k, v3, rpb, batch, seq):
    assert 2 * GRID_W == LANES and 2 * NA_HEAD_DIM == LANES and NA_GROUP_ROWS % 2 == 0
    n_rows = seq // GRID_W
    slabs = n_rows // 2
    step_slabs = min(NA_GROUPS_PER_STEP * NA_GROUP_ROWS // 2, slabs)
    steps = slabs // step_slabs
    hp = NA_HEADS // 2
    tab = _na_bias_table(rpb, n_rows)
    return pl.pallas_call(
        functools.partial(_na_kernel, n_rows=n_rows),
        grid=(hp, batch, steps),
        in_specs=[pl.BlockSpec((step_slabs, LANES, LANES), lambda p, b, i: (b * steps + i, p, 0)),
                  pl.BlockSpec((seq, LANES), lambda p, b, i: (b, p)),
                  pl.BlockSpec((slabs, LANES, LANES), lambda p, b, i: (b, p, 0)),
                  pl.BlockSpec((tab.shape[0], 2) + tab.shape[2:], lambda p, b, i: (0, p, 0, 0))],
        out_specs=pl.BlockSpec((step_slabs * LANES, LANES), lambda p, b, i: (b * steps + i, p)),
        out_shape=jax.ShapeDtypeStruct((batch * seq, NA_DIM), BF),
        compiler_params=_cparams("parallel", "parallel", "parallel"),
        name="natten",
    )(q3, k, v3, tab)


def _merge_kernel(x_ref, fo_ref, na_ref, mo_ref, wg_ref, bg_ref, wf_ref, wn_ref, wm_ref, wo_ref,
                  g_ref, b_ref, ht_ref):
    d = x_ref.shape[1]
    rows = x_ref.shape[0] // MERGE_SUBTILES

    def pre_norm(h):
        sl = pl.ds(h * rows, rows)
        x = x_ref[sl, :]
        xb = x.astype(BF)
        merged = None
        for j, (br_ref, w_ref) in enumerate(((fo_ref, wf_ref), (na_ref, wn_ref), (mo_ref, wm_ref))):
            z = _dot(xb, wg_ref[:, j * d:(j + 1) * d]) + bg_ref[:, j * d:(j + 1) * d]
            gate = 1.0 / (1.0 + jnp.exp(-z))
            term = gate * _dot(br_ref[sl, :].astype(BF), w_ref[...])
            merged = term if merged is None else merged + term
        return ALPHA * x + _dot(merged.astype(BF), wo_ref[...])

    ys = [pre_norm(h) for h in range(MERGE_SUBTILES)]
    for h, y in enumerate(ys):
        _store_tile_rows(ht_ref.at[pl.ds(h * rows * SUBLANES, rows * SUBLANES)],
                         _layer_norm(y, g_ref[...], b_ref[...]))


def _merge(x2d, fo, na, mo, w_gate, b_gate, w_fo, w_na, w_mo, w_out, ln_g, ln_b):
    t, d = x2d.shape
    assert d == SUBLANES * LANES
    tm = TM_MERGE
    tok = lambda i: (i, 0)
    full = lambda a: pl.BlockSpec(a.shape, lambda i: (0, 0))
    return pl.pallas_call(
        _merge_kernel,
        grid=(t // tm,),
        in_specs=[pl.BlockSpec((tm, d), tok)] + [pl.BlockSpec((tm, 512), tok)] * 3
                 + [full(a) for a in (w_gate, b_gate, w_fo, w_na, w_mo, w_out, ln_g, ln_b)],
        out_specs=pl.BlockSpec((tm * SUBLANES, LANES), tok),
        out_shape=jax.ShapeDtypeStruct((t * SUBLANES, LANES), F32),
        compiler_params=_cparams("parallel"),
        name="merge",
    )(x2d, fo, na, mo, w_gate, b_gate, w_fo, w_na, w_mo, w_out, ln_g, ln_b)


def _router_kernel(ht_ref, whi_ref, wlo_ref, b_ref, oi_ref, of_ref, cnt_ref, carry_ref):
    tm = ht_ref.shape[0] // SUBLANES

    @pl.when(pl.program_id(0) == 0)
    def _():
        carry_ref[...] = jnp.zeros_like(carry_ref)

    h = _load_tile_rows(ht_ref)
    hh = h.astype(BF)
    hl = (h - hh.astype(F32)).astype(BF)
    logits = (_dot_nt(whi_ref[...], hh) + _dot_nt(wlo_ref[...], hh) + _dot_nt(whi_ref[...], hl)
              + b_ref[...])
    gl = logits[:N_GROUPS, :]
    g_row = lax.broadcasted_iota(I32, (N_GROUPS, tm), 0)
    gmax = jnp.max(gl, axis=0, keepdims=True)
    g_idx = jnp.min(jnp.where(gl == gmax, g_row, N_GROUPS), axis=0, keepdims=True)
    p_group = 1.0 / jnp.sum(jnp.exp(gl - gmax), axis=0, keepdims=True)
    e_row = lax.broadcasted_iota(I32, (N_EXPERTS, tm), 0)
    el = jnp.where((e_row >> 3) == g_idx, logits[N_GROUPS:N_GROUPS + N_EXPERTS, :], MASK_NEG)
    v1 = jnp.max(el, axis=0, keepdims=True)
    i1 = jnp.min(jnp.where(el == v1, e_row, N_EXPERTS), axis=0, keepdims=True)
    el2 = jnp.where(e_row == i1, MASK_NEG, el)
    v2 = jnp.max(el2, axis=0, keepdims=True)
    i2 = jnp.min(jnp.where(el2 == v2, e_row, N_EXPERTS), axis=0, keepdims=True)
    tt = jnp.exp(v2 - v1)
    w1 = p_group / (1.0 + tt)
    w2 = p_group * tt / (1.0 + tt)
    sel1 = e_row == i1
    sel2 = e_row == i2
    onehot = jnp.where(sel1 | sel2, 1.0, 0.0)
    t_in = lax.broadcasted_iota(I32, (tm, tm), 0)
    t_out = lax.broadcasted_iota(I32, (tm, tm), 1)
    tri = jnp.where(t_in < t_out, 1.0, 0.0).astype(BF)
    prefix = _dot(onehot.astype(BF), tri) + carry_ref[...]
    r1 = jnp.sum(jnp.where(sel1, prefix, 0.0), axis=0, keepdims=True)
    r2 = jnp.sum(jnp.where(sel2, prefix, 0.0), axis=0, keepdims=True)
    carry_ref[...] += jnp.sum(onehot, axis=1, keepdims=True)
    cnt_ref[...] = carry_ref[...]
    sub = pl.program_id(0) % ROUTE_STEPS_PER_PLANE
    for f, vals in enumerate((i1, i2, r1.astype(I32), r2.astype(I32))):
        for c in range(tm // LANES):
            oi_ref[f, pl.ds(sub * (tm // LANES) + c, 1), :] = vals[:, c * LANES:(c + 1) * LANES]
    w_row = lax.broadcasted_iota(I32, (LANES, tm), 0)
    w_rows = jnp.where(w_row == 0, w1, jnp.where(w_row == 1, w2, 0.0))
    of_ref[...] = jnp.transpose(w_rows)[:, :8]


def _router(ht, w_rg, b_rg, w_re, b_re):
    t = ht.shape[0] // SUBLANES
    d = w_rg.shape[0]
    tm = TM_ROUTE
    pad = ROUTE_ROWS - N_GROUPS - N_EXPERTS
    w = jnp.concatenate([w_rg.astype(F32), w_re.astype(F32), jnp.zeros((d, pad), F32)], axis=1).T
    b = jnp.concatenate([b_rg.astype(F32), b_re.astype(F32), jnp.zeros((pad,), F32)])[:, None]
    w_hi = w.astype(BF)
    w_lo = (w - w_hi.astype(F32)).astype(BF)
    tok = lambda i: (i, 0)
    const = lambda i: (0, 0)
    return pl.pallas_call(
        _router_kernel,
        grid=(t // tm,),
        in_specs=[pl.BlockSpec((tm * SUBLANES, LANES), tok),
                  pl.BlockSpec((ROUTE_ROWS, d), const),
                  pl.BlockSpec((ROUTE_ROWS, d), const),
                  pl.BlockSpec((ROUTE_ROWS, 1), const)],
        out_specs=[pl.BlockSpec((4, SUBLANES, LANES), lambda i: (0, i // ROUTE_STEPS_PER_PLANE, 0)),
                   pl.BlockSpec((tm, 8), tok),
                   pl.BlockSpec((N_EXPERTS, 1), const)],
        out_shape=[jax.ShapeDtypeStruct((4, t // LANES, LANES), I32), jax.ShapeDtypeStruct((t, 8), F32),
                   jax.ShapeDtypeStruct((N_EXPERTS, 1), F32)],
        scratch_shapes=[pltpu.VMEM((N_EXPERTS, 1), F32)],
        compiler_params=_cparams("arbitrary"),
        name="router",
    )(ht, w_hi, w_lo, b)


def _row_groups(n_rows, copies_of_row, start):
    def group(g, c):
        copies = []
        for u in range(ROW_GROUP):
            copies.extend(copies_of_row(g * ROW_GROUP + u))
        for k, (src, dst, sem) in enumerate(copies):
            if start:
                pltpu.async_copy(src, dst, sem, priority=k % DMA_QUEUES)
            else:
                pltpu.make_async_copy(src, dst, sem).wait()
        return c

    lax.fori_loop(0, n_rows // ROW_GROUP, group, 0)


def _dests_kernel(ps_ref, idx_ref, d_ref):
    for a in range(2):
        e = idx_ref[a]
        d = idx_ref[2 + a]
        for k in range(N_EXPERTS):
            d = d + jnp.where(e == k, ps_ref[k], 0)
        d_ref[a] = d


def _dests(idx, pad_start):
    planes = idx.shape[1]
    out = pl.pallas_call(
        _dests_kernel,
        grid_spec=pltpu.PrefetchScalarGridSpec(
            num_scalar_prefetch=1,
            grid=(1,),
            in_specs=[pl.BlockSpec(idx.shape, lambda i, ps: (0, 0, 0))],
            out_specs=pl.BlockSpec((2, planes, LANES), lambda i, ps: (0, 0, 0)),
        ),
        out_shape=jax.ShapeDtypeStruct((2, planes, LANES), I32),
        compiler_params=_cparams("arbitrary"),
        name="dests",
    )(pad_start, idx)
    return out[0].reshape(-1), out[1].reshape(-1)


def _pack_words(val):
    words = []
    for c in range(PACK_ROWS):
        hi = pltpu.bitcast(val[:, c * LANES:(c + 1) * LANES].astype(BF).astype(F32), U32)
        lo = pltpu.bitcast(val[:, (c + PACK_ROWS) * LANES:(c + PACK_ROWS + 1) * LANES].astype(BF).astype(F32), U32)
        words.append(hi | (lo >> 16))
    return words


def _unpack_words(words):
    his = [pltpu.bitcast(w & jnp.uint32(0xFFFF0000), F32) for w in words]
    los = [pltpu.bitcast(w << 16, F32) for w in words]
    return jnp.concatenate(his + los, axis=1)


def _load_packed(ref):
    n = ref.shape[0] // PACK_ROWS
    return [ref[pl.ds(c, n, stride=PACK_ROWS), :] for c in range(PACK_ROWS)]


def _store_packed(ref, words):
    n = ref.shape[0] // PACK_ROWS
    for c, w in enumerate(words):
        ref[pl.ds(c, n, stride=PACK_ROWS), :] = w


def _packed_row_copy(src_ref, src_row, dst_ref, dst_row, sem):
    src = pl.multiple_of(src_row * PACK_ROWS, PACK_ROWS)
    dst = pl.multiple_of(dst_row * PACK_ROWS, PACK_ROWS)
    return src_ref.at[pl.ds(src, PACK_ROWS)], dst_ref.at[pl.ds(dst, PACK_ROWS)], sem


def _dispatch_kernel(ps_ref, pe_ref, d1_ref, d2_ref, ht_ref, xg_ref, pk, zbuf, sem, zsem):
    i = pl.program_id(0)
    tm = d1_ref.shape[0]
    blk = MOE_BLOCK * PACK_ROWS
    n_blocks = xg_ref.shape[0] // blk

    @pl.when(i == 0)
    def _():
        zbuf[...] = jnp.zeros_like(zbuf)

        def zero_copy(block):
            return pltpu.make_async_copy(zbuf, xg_ref.at[pl.ds(pl.multiple_of(block * blk, blk), blk)], zsem)

        def seg_issue(e, c):
            @pl.when(pe_ref[e] > ps_ref[e])
            def _():
                zero_copy(pe_ref[e] // MOE_BLOCK - 1).start()
            return c

        def seg_drain(e, c):
            @pl.when(pe_ref[e] > ps_ref[e])
            def _():
                zero_copy(pe_ref[e] // MOE_BLOCK - 1).wait()
            return c

        def tail_issue(b, c):
            zero_copy(b).start()
            return c

        def tail_drain(b, c):
            zero_copy(b).wait()
            return c

        first_tail = pe_ref[N_EXPERTS - 1] // MOE_BLOCK
        lax.fori_loop(0, N_EXPERTS, seg_issue, 0)
        lax.fori_loop(first_tail, n_blocks, tail_issue, 0)
        lax.fori_loop(0, N_EXPERTS, seg_drain, 0)
        lax.fori_loop(first_tail, n_blocks, tail_drain, 0)

    _store_packed(pk, _pack_words(_load_tile_rows(ht_ref)))
    _row_groups(tm, lambda t: [_packed_row_copy(pk, t, xg_ref, d1_ref[t], sem),
                               _packed_row_copy(pk, t, xg_ref, d2_ref[t], sem)], True)
    _row_groups(tm, lambda t: [_packed_row_copy(pk, 0, xg_ref, 0, sem)] * 2, False)


def _dispatch(ht, d1, d2, pad_start, pad_end, n_pad):
    t = d1.shape[0]
    tm = TM_DISPATCH
    smem = lambda: pl.BlockSpec((tm,), lambda i, ps, pe: (i,), memory_space=pltpu.SMEM)
    return pl.pallas_call(
        _dispatch_kernel,
        grid_spec=pltpu.PrefetchScalarGridSpec(
            num_scalar_prefetch=2,
            grid=(t // tm,),
            in_specs=[smem(), smem(), pl.BlockSpec((tm * SUBLANES, LANES), lambda i, ps, pe: (i, 0))],
            out_specs=pl.BlockSpec(memory_space=pl.ANY),
            scratch_shapes=[pltpu.VMEM((tm * PACK_ROWS, LANES), U32), pltpu.VMEM((MOE_BLOCK * PACK_ROWS, LANES), U32),
                            pltpu.SemaphoreType.DMA(()), pltpu.SemaphoreType.DMA(())],
        ),
        out_shape=jax.ShapeDtypeStruct((n_pad * PACK_ROWS, LANES), U32),
        compiler_params=_cparams("arbitrary"),
        name="dispatch",
    )(pad_start, pad_end, d1, d2, ht)


def _expert_kernel(be_ref, nx_ref, nv_ref, nu_ref, x_ref, wg_hbm, wu_hbm, wd_hbm, y_ref,
                   wg_f, wu_f, wd_f, wg_s, wu_s, wd_s, sems):
    j = pl.program_id(0)
    stage = ((wg_hbm, wg_f, wg_s), (wu_hbm, wu_f, wu_s), (wd_hbm, wd_f, wd_s))

    def weight_copies(e):
        return [pltpu.make_async_copy(hbm.at[e], buf, sems.at[k]) for k, (hbm, buf, _) in enumerate(stage)]

    @pl.when(j < nu_ref[0])
    def _():
        e = be_ref[j]

        @pl.when(j == 0)
        def _():
            for cp in weight_copies(e):
                cp.start()

        @pl.when((j == 0) | (e != be_ref[jnp.maximum(j - 1, 0)]))
        def _():
            for cp, (_, buf, dst) in zip(weight_copies(e), stage):
                cp.wait()
                dst[...] = buf[...].astype(BF)
            nxt = nx_ref[e]

            @pl.when(nxt >= 0)
            def _():
                for cp in weight_copies(nxt):
                    cp.start()

        part_rows = x_ref.shape[0] // EXPERT_PARTS

        def run(sub_blocks):
            def up(x_sub):
                xb = _unpack_words(_load_packed(x_sub)).astype(BF)
                a = _dot(xb, wg_s[...])
                u = _dot(xb, wu_s[...])
                return ((a / (1.0 + jnp.exp(-a))) * u).astype(BF)

            mids = [up(x_sub) for x_sub, _ in sub_blocks]
            for (_, y_sub), mid in zip(sub_blocks, mids):
                _store_packed(y_sub, _pack_words(_dot(mid, wd_s[...])))

        def views(ref, start, rows, n):
            return [ref.at[pl.ds(start + h * rows, rows)] for h in range(n)]

        both = nv_ref[j] > MOE_BLOCK // EXPERT_PARTS

        @pl.when(both)
        def _():
            run(list(zip(views(x_ref, 0, part_rows, EXPERT_PARTS), views(y_ref, 0, part_rows, EXPERT_PARTS))))

        @pl.when(jnp.logical_not(both))
        def _():
            rows = part_rows // EXPERT_SUBBLOCKS
            run(list(zip(views(x_ref, 0, rows, EXPERT_SUBBLOCKS), views(y_ref, 0, rows, EXPERT_SUBBLOCKS))))
            tail = y_ref.at[pl.ds(part_rows, part_rows)]
            tail[...] = jnp.zeros_like(tail)

    @pl.when(j >= nu_ref[0])
    def _():
        y_ref[...] = jnp.zeros_like(y_ref)


def _experts(xg, blk_expert, next_expert, blk_valid, n_used, w_eg, w_eu, w_ed):
    blk = MOE_BLOCK * PACK_ROWS
    nb = xg.shape[0] // blk
    d, de = w_eg.shape[1], w_eg.shape[2]
    rows = lambda j, be, nx, nv, nu: (jnp.minimum(j, nu[0] - 1), 0)
    hbm = pl.BlockSpec(memory_space=pl.ANY)
    return pl.pallas_call(
        _expert_kernel,
        grid_spec=pltpu.PrefetchScalarGridSpec(
            num_scalar_prefetch=4,
            grid=(nb,),
            in_specs=[pl.BlockSpec((blk, LANES), rows), hbm, hbm, hbm],
            out_specs=pl.BlockSpec((blk, LANES), lambda j, be, nx, nv, nu: (j, 0)),
            scratch_shapes=[pltpu.VMEM((d, de), F32), pltpu.VMEM((d, de), F32), pltpu.VMEM((de, d), F32),
                            pltpu.VMEM((d, de), BF), pltpu.VMEM((d, de), BF), pltpu.VMEM((de, d), BF),
                            pltpu.SemaphoreType.DMA((3,))],
        ),
        out_shape=jax.ShapeDtypeStruct(xg.shape, U32),
        compiler_params=_cparams("arbitrary"),
        name="experts",
    )(blk_expert, next_expert, blk_valid, n_used, xg, w_eg, w_eu, w_ed)


def _combine_kernel(d1_ref, d2_ref, d1_next_ref, d2_next_ref, wt_ref, ht_ref, yb_ref, g_ref, b_ref, o_ref,
                    buf, sems):
    i = pl.program_id(0)
    n = pl.num_programs(0)
    tm = d1_ref.shape[0]

    def gather_tile(da_ref, db_ref, slot, start):
        def copies(t):
            if not start:
                return [_packed_row_copy(yb_ref, 0, buf.at[slot, a], 0, sems.at[slot]) for a in range(2)]
            return [_packed_row_copy(yb_ref, da_ref[t], buf.at[slot, 0], t, sems.at[slot]),
                    _packed_row_copy(yb_ref, db_ref[t], buf.at[slot, 1], t, sems.at[slot])]
        _row_groups(tm, copies, start)

    def gathered(slot, a):
        return _unpack_words(_load_packed(buf.at[slot, a]))

    slot = i % 2

    @pl.when(i == 0)
    def _():
        gather_tile(d1_ref, d2_ref, 0, True)

    @pl.when(i + 1 < n)
    def _():
        gather_tile(d1_next_ref, d2_next_ref, 1 - slot, True)

    gather_tile(d1_ref, d2_ref, slot, False)
    wt = wt_ref[...]
    ffn = wt[:, 0:1] * gathered(slot, 0) + wt[:, 1:2] * gathered(slot, 1)
    o_ref[...] = _layer_norm(ALPHA * _load_tile_rows(ht_ref) + ffn, g_ref[...], b_ref[...])


def _combine(ht, yb, d1, d2, wts, ln_g, ln_b):
    t = d1.shape[0]
    d = SUBLANES * LANES
    tm = TM_COMBINE
    n = t // tm
    tok = lambda i: (i, 0)
    const = lambda i: (0, 0)
    cur = lambda: pl.BlockSpec((tm,), lambda i: (i,), memory_space=pltpu.SMEM)
    nxt = lambda: pl.BlockSpec((tm,), lambda i: (jnp.minimum(i + 1, n - 1),), memory_space=pltpu.SMEM)
    return pl.pallas_call(
        _combine_kernel,
        grid=(n,),
        in_specs=[cur(), cur(), nxt(), nxt(),
                  pl.BlockSpec((tm, 8), tok),
                  pl.BlockSpec((tm * SUBLANES, LANES), tok),
                  pl.BlockSpec(memory_space=pl.ANY),
                  pl.BlockSpec((1, d), const), pl.BlockSpec((1, d), const)],
        out_specs=pl.BlockSpec((tm, d), tok),
        out_shape=jax.ShapeDtypeStruct((t, d), F32),
        scratch_shapes=[pltpu.VMEM((2, 2, tm * PACK_ROWS, LANES), U32), pltpu.SemaphoreType.DMA((2,))],
        compiler_params=_cparams("arbitrary"),
        name="combine",
    )(d1, d2, d1, d2, wts, ht, yb, ln_g, ln_b)


def _token_mixing(x2d, mem2d, batch, seq, m_len, w_in, w_gate, b_gate, w_mem_kv, rpb, w_fo, w_na, w_mo, w_out,
                  ln_g, ln_b):
    k_mem, v_mem = _memkv(mem2d, w_mem_kv.astype(BF), m_len)
    u_f, k, q3, v3, mo = _proj(x2d, w_in, k_mem, v_mem, seq, m_len)
    fo = _fourier(u_f, batch, seq)
    na = _natten(q3, k, v3, rpb, batch, seq)
    return _merge(x2d, fo, na, mo, w_gate.astype(BF), b_gate.astype(F32)[None, :], w_fo.astype(BF),
                  w_na.astype(BF), w_mo.astype(BF), w_out.astype(BF), ln_g.astype(F32)[None, :],
                  ln_b.astype(F32)[None, :])


def _moe(ht, w_rg, b_rg, w_re, b_re, w_eg, w_eu, w_ed, ln_g, ln_b):
    t = ht.shape[0] // SUBLANES
    idx, wts, cnt = _router(ht, w_rg, b_rg, w_re, b_re)
    counts = cnt[:, 0].astype(I32)
    padded = (counts + MOE_BLOCK - 1) // MOE_BLOCK * MOE_BLOCK
    pad_end = jnp.cumsum(padded).astype(I32)
    pad_start = pad_end - padded
    n_pad = 2 * t + N_EXPERTS * MOE_BLOCK
    nb = n_pad // MOE_BLOCK
    blk_start = jnp.arange(nb, dtype=I32) * MOE_BLOCK
    blk_expert = jnp.minimum(jnp.sum((pad_end[None, :] <= blk_start[:, None]).astype(I32), axis=1), N_EXPERTS - 1)
    n_used = pad_end[-1:] // MOE_BLOCK
    seg_left = counts[None, :] - (blk_start[:, None] - pad_start[None, :])
    own = blk_expert[:, None] == jnp.arange(N_EXPERTS, dtype=I32)[None, :]
    blk_valid = jnp.clip(jnp.sum(jnp.where(own, seg_left, 0), axis=1), 0, MOE_BLOCK).astype(I32)
    ids = jnp.arange(N_EXPERTS, dtype=I32)
    later_used = (padded[None, :] > 0) & (ids[None, :] > ids[:, None])
    next_expert = jnp.min(jnp.where(later_used, ids[None, :], N_EXPERTS), axis=1)
    next_expert = jnp.where(next_expert == N_EXPERTS, -1, next_expert).astype(I32)
    d1, d2 = _dests(idx, pad_start)
    xg = _dispatch(ht, d1, d2, pad_start, pad_end, n_pad)
    yb = _experts(xg, blk_expert, next_expert, blk_valid, n_used, w_eg, w_eu, w_ed)
    return _combine(ht, yb, d1, d2, wts, ln_g.astype(F32)[None, :], ln_b.astype(F32)[None, :])


def kernel(x, mem, w_in, w_gate, b_gate, w_mem_kv, rpb, w_fourier_o, w_na_o, w_mem_o, w_out, ln1_g, ln1_b,
           w_router_group, b_router_group, w_router_expert, b_router_expert, w_exp_gate, w_exp_up,
           w_exp_down, ln2_g, ln2_b):
    batch, seq, d = x.shape
    m_len = mem.shape[1]
    x2d = x.reshape(batch * seq, d)
    mem2d = mem.reshape(batch * m_len, d)
    for l in range(w_in.shape[0]):
        ht = _token_mixing(x2d, mem2d, batch, seq, m_len, w_in[l], w_gate[l], b_gate[l], w_mem_kv[l], rpb[l],
                           w_fourier_o[l], w_na_o[l], w_mem_o[l], w_out[l], ln1_g[l], ln1_b[l])
        x2d = _moe(ht, w_router_group[l], b_router_group[l], w_router_expert[l], b_router_expert[l],
                   w_exp_gate[l], w_exp_up[l], w_exp_down[l], ln2_g[l], ln2_b[l])
    return x2d.reshape(batch, seq, d)
```

```python
import functools

import numpy as np
import jax
import jax.numpy as jnp
from jax import lax
from jax.experimental import pallas as pl
from jax.experimental.pallas import tpu as pltpu

BF = jnp.bfloat16
F32 = jnp.float32
I32 = jnp.int32
U32 = jnp.uint32

GRID_W = 64
MEM_HEADS = 4
MEM_HEAD_DIM = 128
MEM_DIM = MEM_HEADS * MEM_HEAD_DIM
FOURIER_GROUPS = 4
FOURIER_GROUP_DIM = 128
FOURIER_DIM = FOURIER_GROUPS * FOURIER_GROUP_DIM
NA_HEADS = 8
NA_HEAD_DIM = 64
NA_DIM = NA_HEADS * NA_HEAD_DIM
NA_ROW_WIN = 8
NA_COL_WIN = 16
N_GROUPS = 8
EXPERTS_PER_GROUP = 8
N_EXPERTS = N_GROUPS * EXPERTS_PER_GROUP
DEPTH = 1
ALPHA = (2.0 * DEPTH) ** 0.25
LN_EPS = 1e-5
NA_SCALE = NA_HEAD_DIM ** -0.5
LOG2_E = 1.4426950408889634
MEM_SCALE = MEM_HEAD_DIM ** -0.5
MASK_NEG = -1e30

LANES = 128
SUBLANES = 8
FFT_N2 = 128
TM_PROJ = 512
TM_MERGE = 512
MERGE_SUBTILES = 2
TM_ROUTE = 1024
TM_DISPATCH = 1024
TM_COMBINE = 256
NA_GROUP_ROWS = 4
NA_GROUP_WIN = NA_GROUP_ROWS + NA_ROW_WIN
NA_GROUPS_PER_STEP = 16
NA_SUM_ROWS = 16
F1_POS = 16
F2_SLABS = 16
MOE_BLOCK = 512
EXPERT_PARTS = 2
PACK_ROWS = 4
EXPERT_SUBBLOCKS = 2
ROUTE_ROWS = 128
ROUTE_STEPS_PER_PLANE = SUBLANES * LANES // TM_ROUTE
DMA_QUEUES = 2
ROW_GROUP = 16
VMEM_LIMIT = 56 * 1024 * 1024


def _cparams(*sem):
    return pltpu.CompilerParams(dimension_semantics=sem, vmem_limit_bytes=VMEM_LIMIT)


def _dot(a, b):
    return jnp.dot(a, b, preferred_element_type=F32)


def _dot_nt(a, b):
    return lax.dot_general(a, b, (((1,), (1,)), ((), ())), preferred_element_type=F32)


def _layer_norm(y, g, b):
    mu = jnp.mean(y, axis=-1, keepdims=True)
    yc = y - mu
    var = jnp.mean(yc * yc, axis=-1, keepdims=True)
    return yc * lax.rsqrt(var + LN_EPS) * g + b


def _store_tile_rows(ref, val):
    n = val.shape[0]
    for s in range(SUBLANES):
        ref[pl.ds(s, n, stride=SUBLANES), :] = val[:, s * LANES:(s + 1) * LANES]


def _load_tile_rows(ref):
    n = ref.shape[0] // SUBLANES
    return jnp.concatenate([ref[pl.ds(s, n, stride=SUBLANES), :] for s in range(SUBLANES)], axis=1)


def _memkv_kernel(mem_ref, w_ref, k_ref, v_ref):
    kv = _dot(mem_ref[...].astype(BF), w_ref[...])
    k_ref[...] = kv[:, :MEM_DIM].astype(BF)
    v_ref[...] = kv[:, MEM_DIM:].astype(BF)


def _memkv(mem2d, w_kv, m_len):
    rows, d = mem2d.shape
    return pl.pallas_call(
        _memkv_kernel,
        grid=(rows // m_len,),
        in_specs=[pl.BlockSpec((m_len, d), lambda i: (i, 0)),
                  pl.BlockSpec((d, 2 * MEM_DIM), lambda i: (0, 0))],
        out_specs=[pl.BlockSpec((m_len, MEM_DIM), lambda i: (i, 0))] * 2,
        out_shape=[jax.ShapeDtypeStruct((rows, MEM_DIM), BF)] * 2,
        compiler_params=_cparams("parallel"),
        name="memkv",
    )(mem2d, w_kv)


def _proj_kernel(x_ref, w_ref, wt_ref, km_ref, vm_ref, uf_ref, k_ref, q3_ref, v3_ref, mo_ref):
    xb = x_ref[...].astype(BF)

    def seg(j):
        return _dot(xb, w_ref[:, j * 512:(j + 1) * 512])

    heads = [slice(h * MEM_HEAD_DIM, (h + 1) * MEM_HEAD_DIM) for h in range(MEM_HEADS)]
    qm = seg(2).astype(BF)
    scores = [_dot_nt(qm[:, sl], km_ref[:, sl]) * MEM_SCALE for sl in heads]
    uf_ref[...] = seg(0)
    k_ref[...] = seg(1).astype(BF)
    probs = [jnp.exp(s - jnp.max(s, axis=-1, keepdims=True)) for s in scores]
    q_t = _dot_nt(wt_ref[:NA_DIM, :], xb) * (NA_SCALE * LOG2_E)
    v_t = _dot_nt(wt_ref[NA_DIM:, :], xb)
    for s in range(q3_ref.shape[0]):
        q3_ref[s] = q_t[:, s * LANES:(s + 1) * LANES].astype(BF)
        v3_ref[s] = v_t[:, s * LANES:(s + 1) * LANES].astype(BF)
    for sl, p in zip(heads, probs):
        l = jnp.sum(p, axis=-1, keepdims=True)
        mo_ref[:, sl] = (_dot(p.astype(BF), vm_ref[:, sl]) / l).astype(BF)


def _proj(x2d, w_in, k_mem, v_mem, seq, m_len):
    t, d = x2d.shape
    tm = TM_PROJ
    tiles_per_batch = seq // tm
    w_tok = jnp.concatenate([w_in[:, 0:512], w_in[:, 1024:1536], w_in[:, 2048:2560]], axis=1).astype(BF)
    w_chan = jnp.concatenate([w_in[:, 512:1024], w_in[:, 1536:2048]], axis=1).T.astype(BF)
    tok = lambda i: (i, 0)
    slab = lambda i: (i, 0, 0)
    memb = lambda i: (i // tiles_per_batch, 0)
    const = lambda i: (0, 0)
    tok_out = pl.BlockSpec((tm, 512), tok)
    slab_out = pl.BlockSpec((tm // LANES, NA_DIM, LANES), slab)
    slab_shape = jax.ShapeDtypeStruct((t // LANES, NA_DIM, LANES), BF)
    return pl.pallas_call(
        _proj_kernel,
        grid=(t // tm,),
        in_specs=[pl.BlockSpec((tm, d), tok),
                  pl.BlockSpec(w_tok.shape, const),
                  pl.BlockSpec(w_chan.shape, const),
                  pl.BlockSpec((m_len, MEM_DIM), memb),
                  pl.BlockSpec((m_len, MEM_DIM), memb)],
        out_specs=[tok_out, tok_out, slab_out, slab_out, tok_out],
        out_shape=[jax.ShapeDtypeStruct((t, 512), F32), jax.ShapeDtypeStruct((t, 512), BF)] + [slab_shape] * 2
                  + [jax.ShapeDtypeStruct((t, 512), BF)],
        compiler_params=_cparams("parallel"),
        name="proj",
    )(x2d, w_tok, w_chan, k_mem, v_mem)


def _fourier_tables(n1, n2):
    n = n1 * n2
    k1 = np.arange(n1)
    ang1 = 2.0 * np.pi * ((k1[:, None] * k1[None, :]) % n1) / n1
    norm = 1.0 / np.sqrt(float(n) * FOURIER_GROUP_DIM)
    m1 = np.concatenate([np.cos(ang1), -np.sin(ang1)], axis=0) * norm
    kk = k1[:, None, None] + n1 * np.arange(n2)[None, :, None]
    nn = np.arange(n2)[None, None, :]
    ang2 = 2.0 * np.pi * ((kk * nn) % n) / n
    c2, s2 = np.cos(ang2), np.sin(ang2)
    m2 = np.concatenate([np.concatenate([c2, s2], axis=2),
                         np.concatenate([-s2, c2], axis=2)], axis=1)
    c = np.arange(FOURIER_GROUP_DIM)
    angc = 2.0 * np.pi * ((c[:, None] * c[None, :]) % FOURIER_GROUP_DIM) / FOURIER_GROUP_DIM
    mc = np.concatenate([np.cos(angc), np.sin(angc)], axis=0)
    as_bf = lambda a: jnp.asarray(a.astype(np.float32)).astype(BF)
    return as_bf(m1), as_bf(m2), as_bf(mc)


def _f1_kernel(u_ref, m_ref, a_ref):
    n1, pos, _ = u_ref.shape
    x = jnp.concatenate([u_ref[:, p, :] for p in range(pos)], axis=1).astype(BF)
    r = _dot(m_ref[...], x)
    a_ref[0] = r[:n1].astype(BF)
    a_ref[1] = r[n1:].astype(BF)


def _f2_kernel(a_ref, m2_ref, mc_ref, o_ref):
    slabs = a_ref.shape[1]
    n2 = a_ref.shape[2]
    for t in range(slabs):
        a = jnp.concatenate([a_ref[0, t], a_ref[1, t]], axis=0)
        y = _dot(m2_ref[t], a)
        yr = y[:n2].astype(BF)
        yi = y[n2:].astype(BF)
        yy = jnp.concatenate([jnp.concatenate([yr[:, g * LANES:(g + 1) * LANES], yi[:, g * LANES:(g + 1) * LANES]],
                                              axis=1) for g in range(FOURIER_GROUPS)], axis=0)
        z = _dot(yy, mc_ref[...])
        for g in range(FOURIER_GROUPS):
            o_ref[:, t, g * LANES:(g + 1) * LANES] = z[g * n2:(g + 1) * n2]


def _fourier(u_f, batch, seq):
    n2 = FFT_N2
    n1 = seq // n2
    m1, m2, mc = _fourier_tables(n1, n2)
    cols = n2 * FOURIER_DIM
    pos = min(F1_POS, n2)
    a = pl.pallas_call(
        _f1_kernel,
        grid=(batch, n2 // pos),
        in_specs=[pl.BlockSpec((None, n1, pos, FOURIER_DIM), lambda b, j: (b, 0, j, 0)),
                  pl.BlockSpec((2 * n1, n1), lambda b, j: (0, 0))],
        out_specs=pl.BlockSpec((None, 2, n1, pos * FOURIER_DIM), lambda b, j: (b, 0, 0, j)),
        out_shape=jax.ShapeDtypeStruct((batch, 2, n1, cols), BF),
        compiler_params=_cparams("parallel", "parallel"),
        name="fourier1",
    )(u_f.reshape(batch, n1, n2, FOURIER_DIM), m1)
    kb = min(F2_SLABS, n1)
    out = pl.pallas_call(
        _f2_kernel,
        grid=(n1 // kb, batch),
        in_specs=[pl.BlockSpec((None, 2, kb, n2, FOURIER_DIM), lambda j, b: (b, 0, j, 0, 0)),
                  pl.BlockSpec((kb, 2 * n2, 2 * n2), lambda j, b: (j, 0, 0)),
                  pl.BlockSpec((2 * FOURIER_GROUP_DIM, FOURIER_GROUP_DIM), lambda j, b: (0, 0))],
        out_specs=pl.BlockSpec((None, n2, kb, FOURIER_DIM), lambda j, b: (b, 0, j, 0)),
        out_shape=jax.ShapeDtypeStruct((batch, n2, n1, FOURIER_DIM), F32),
        compiler_params=_cparams("parallel", "parallel"),
        name="fourier2",
    )(a.reshape(batch, 2, n1, n2, FOURIER_DIM), m2, mc)
    return out.reshape(batch * seq, FOURIER_DIM)


def _na_group_start(r0, n_rows):
    rs0 = min(max(r0 - NA_ROW_WIN // 2, 0), n_rows - NA_ROW_WIN)
    return min(rs0, n_rows - NA_GROUP_WIN)


def _na_bias_table(rpb, n_rows):
    g = NA_GROUP_ROWS
    assert n_rows >= 2 * NA_GROUP_WIN and n_rows % g == 0 and (NA_ROW_WIN // 2) % g == 0
    cols = np.arange(GRID_W)
    cs = np.clip(cols - NA_COL_WIN // 2, 0, GRID_W - NA_COL_WIN)
    kc = np.arange(GRID_W)
    in_win = (kc[None, :] >= cs[:, None]) & (kc[None, :] < cs[:, None] + NA_COL_WIN)
    dc = kc[None, :] - cols[:, None] + (NA_COL_WIN - 1)
    n_dc = 2 * NA_COL_WIN - 1
    pick = ((dc.reshape(-1)[None, :] == np.arange(n_dc)[:, None]) & in_win.reshape(-1)[None, :])
    picked = jnp.dot(rpb.astype(F32).reshape(-1, n_dc), jnp.asarray(pick.astype(np.float32)),
                     precision=lax.Precision.HIGHEST)
    picked = picked.reshape(NA_HEADS, 2 * NA_ROW_WIN - 1, GRID_W, GRID_W)
    full_t = jnp.where(in_win[None, None], picked * LOG2_E, MASK_NEG).transpose(0, 1, 3, 2)
    masked = jnp.full((NA_HEADS, GRID_W, GRID_W), MASK_NEG, F32)
    half = NA_ROW_WIN // 2
    group_rows = list(range(0, half, g)) + [half] + list(range(n_rows - half, n_rows, g))
    tabs = []
    for r0 in group_rows:
        start = _na_group_start(r0, n_rows)
        key_rows = []
        for kr in range(NA_GROUP_WIN):
            blocks = []
            for rr in range(g):
                r = r0 + rr
                rs = min(max(r - half, 0), n_rows - NA_ROW_WIN)
                in_rows = rs <= start + kr < rs + NA_ROW_WIN
                blocks.append(full_t[:, start + kr - r + NA_ROW_WIN - 1] if in_rows else masked)
            key_rows.append(jnp.concatenate(blocks, axis=2))
        tabs.append(jnp.concatenate(key_rows, axis=1))
    return jnp.stack(tabs, axis=0)


def _na_kernel(q3_ref, k_ref, v3_ref, tab_ref, o_ref, *, n_rows):
    i = pl.program_id(2)
    g = NA_GROUP_ROWS
    slabs_per_group = g * GRID_W // LANES
    groups = q3_ref.shape[0] // slabs_per_group
    win = NA_GROUP_WIN * GRID_W
    tokens = g * GRID_W
    half = NA_ROW_WIN // 2
    n_top = half // g
    chan = lax.broadcasted_iota(I32, (2 * NA_HEAD_DIM, tokens), 0)
    first_head = chan < NA_HEAD_DIM

    def scores(jj):
        r0 = (i * groups + jj) * g
        start = jnp.minimum(jnp.clip(r0 - half, 0, n_rows - NA_ROW_WIN), n_rows - NA_GROUP_WIN)
        cfg = jnp.where(r0 < half, r0 // g,
                        jnp.where(r0 >= n_rows - half, n_top + 1 + (r0 - (n_rows - half)) // g, n_top))
        kw = k_ref[pl.ds(pl.multiple_of(start * GRID_W, LANES), win), :]
        qt = jnp.concatenate([q3_ref[jj * slabs_per_group + t] for t in range(slabs_per_group)], axis=1)
        zero = jnp.zeros_like(qt)
        q_bd = jnp.concatenate([jnp.where(first_head, qt, zero), jnp.where(first_head, zero, qt)], axis=1)
        s = _dot(kw, q_bd) + jnp.concatenate([tab_ref[cfg, 0], tab_ref[cfg, 1]], axis=1)
        return s, start

    ones_rows = jnp.ones((NA_SUM_ROWS, win), BF)

    def outputs(jj, p, start):
        slab = start // 2
        vw = jnp.concatenate([v3_ref[slab + t] for t in range(NA_GROUP_WIN // 2)] , axis=1)
        o_l = _dot(jnp.concatenate([vw, ones_rows], axis=0), p)
        o_t = o_l[:2 * NA_HEAD_DIM] / o_l[2 * NA_HEAD_DIM:2 * NA_HEAD_DIM + 1]
        o = jnp.where(first_head, o_t[:, :tokens], o_t[:, tokens:])
        o_ref[jj * tokens:(jj + 1) * tokens, :] = jnp.transpose(o).astype(BF)

    pending = None
    for jj in range(groups):
        s, start = scores(jj)
        if pending is not None:
            outputs(*pending)
        p = jnp.exp2(s - jnp.max(s, axis=0, keepdims=True))
        pending = (jj, p.astype(BF), start)
    outputs(*pending)


def _natten(q3, k, v3, rpb, batch, seq):
    assert 2 * GRID_W == LANES and 2 * NA_HEAD_DIM == LANES and NA_GROUP_ROWS % 2 == 0
    n_rows = seq // GRID_W
    slabs = n_rows // 2
    step_slabs = min(NA_GROUPS_PER_STEP * NA_GROUP_ROWS // 2, slabs)
    steps = slabs // step_slabs
    hp = NA_HEADS // 2
    tab = _na_bias_table(rpb, n_rows)
    return pl.pallas_call(
        functools.partial(_na_kernel, n_rows=n_rows),
        grid=(hp, batch, steps),
        in_specs=[pl.BlockSpec((step_slabs, LANES, LANES), lambda p, b, i: (b * steps + i, p, 0)),
                  pl.BlockSpec((seq, LANES), lambda p, b, i: (b, p)),
                  pl.BlockSpec((slabs, LANES, LANES), lambda p, b, i: (b, p, 0)),
                  pl.BlockSpec((tab.shape[0], 2) + tab.shape[2:], lambda p, b, i: (0, p, 0, 0))],
        out_specs=pl.BlockSpec((step_slabs * LANES, LANES), lambda p, b, i: (b * steps + i, p)),
        out_shape=jax.ShapeDtypeStruct((batch * seq, NA_DIM), BF),
        compiler_params=_cparams("parallel", "parallel", "parallel"),
        name="natten",
    )(q3, k, v3, tab)


def _merge_kernel(x_ref, fo_ref, na_ref, mo_ref, wg_ref, bg_ref, wf_ref, wn_ref, wm_ref, wo_ref,
                  g_ref, b_ref, ht_ref):
    d = x_ref.shape[1]
    rows = x_ref.shape[0] // MERGE_SUBTILES

    def pre_norm(h):
        sl = pl.ds(h * rows, rows)
        x = x_ref[sl, :]
        xb = x.astype(BF)
        merged = None
        for j, (br_ref, w_ref) in enumerate(((fo_ref, wf_ref), (na_ref, wn_ref), (mo_ref, wm_ref))):
            z = _dot(xb, wg_ref[:, j * d:(j + 1) * d]) + bg_ref[:, j * d:(j + 1) * d]
            gate = 1.0 / (1.0 + jnp.exp(-z))
            term = gate * _dot(br_ref[sl, :].astype(BF), w_ref[...])
            merged = term if merged is None else merged + term
        return ALPHA * x + _dot(merged.astype(BF), wo_ref[...])

    ys = [pre_norm(h) for h in range(MERGE_SUBTILES)]
    for h, y in enumerate(ys):
        _store_tile_rows(ht_ref.at[pl.ds(h * rows * SUBLANES, rows * SUBLANES)],
                         _layer_norm(y, g_ref[...], b_ref[...]))


def _merge(x2d, fo, na, mo, w_gate, b_gate, w_fo, w_na, w_mo, w_out, ln_g, ln_b):
    t, d = x2d.shape
    assert d == SUBLANES * LANES
    tm = TM_MERGE
    tok = lambda i: (i, 0)
    full = lambda a: pl.BlockSpec(a.shape, lambda i: (0, 0))
    return pl.pallas_call(
        _merge_kernel,
        grid=(t // tm,),
        in_specs=[pl.BlockSpec((tm, d), tok)] + [pl.BlockSpec((tm, 512), tok)] * 3
                 + [full(a) for a in (w_gate, b_gate, w_fo, w_na, w_mo, w_out, ln_g, ln_b)],
        out_specs=pl.BlockSpec((tm * SUBLANES, LANES), tok),
        out_shape=jax.ShapeDtypeStruct((t * SUBLANES, LANES), F32),
        compiler_params=_cparams("parallel"),
        name="merge",
    )(x2d, fo, na, mo, w_gate, b_gate, w_fo, w_na, w_mo, w_out, ln_g, ln_b)


def _router_kernel(ht_ref, whi_ref, wlo_ref, b_ref, oi_ref, of_ref, cnt_ref, carry_ref):
    tm = ht_ref.shape[0] // SUBLANES

    @pl.when(pl.program_id(0) == 0)
    def _():
        carry_ref[...] = jnp.zeros_like(carry_ref)

    h = _load_tile_rows(ht_ref)
    hh = h.astype(BF)
    hl = (h - hh.astype(F32)).astype(BF)
    logits = (_dot_nt(whi_ref[...], hh) + _dot_nt(wlo_ref[...], hh) + _dot_nt(whi_ref[...], hl)
              + b_ref[...])
    gl = logits[:N_GROUPS, :]
    g_row = lax.broadcasted_iota(I32, (N_GROUPS, tm), 0)
    gmax = jnp.max(gl, axis=0, keepdims=True)
    g_idx = jnp.min(jnp.where(gl == gmax, g_row, N_GROUPS), axis=0, keepdims=True)
    p_group = 1.0 / jnp.sum(jnp.exp(gl - gmax), axis=0, keepdims=True)
    e_row = lax.broadcasted_iota(I32, (N_EXPERTS, tm), 0)
    el = jnp.where((e_row >> 3) == g_idx, logits[N_GROUPS:N_GROUPS + N_EXPERTS, :], MASK_NEG)
    v1 = jnp.max(el, axis=0, keepdims=True)
    i1 = jnp.min(jnp.where(el == v1, e_row, N_EXPERTS), axis=0, keepdims=True)
    el2 = jnp.where(e_row == i1, MASK_NEG, el)
    v2 = jnp.max(el2, axis=0, keepdims=True)
    i2 = jnp.min(jnp.where(el2 == v2, e_row, N_EXPERTS), axis=0, keepdims=True)
    tt = jnp.exp(v2 - v1)
    w1 = p_group / (1.0 + tt)
    w2 = p_group * tt / (1.0 + tt)
    sel1 = e_row == i1
    sel2 = e_row == i2
    onehot = jnp.where(sel1 | sel2, 1.0, 0.0)
    t_in = lax.broadcasted_iota(I32, (tm, tm), 0)
    t_out = lax.broadcasted_iota(I32, (tm, tm), 1)
    tri = jnp.where(t_in < t_out, 1.0, 0.0).astype(BF)
    prefix = _dot(onehot.astype(BF), tri) + carry_ref[...]
    r1 = jnp.sum(jnp.where(sel1, prefix, 0.0), axis=0, keepdims=True)
    r2 = jnp.sum(jnp.where(sel2, prefix, 0.0), axis=0, keepdims=True)
    carry_ref[...] += jnp.sum(onehot, axis=1, keepdims=True)
    cnt_ref[...] = carry_ref[...]
    sub = pl.program_id(0) % ROUTE_STEPS_PER_PLANE
    for f, vals in enumerate((i1, i2, r1.astype(I32), r2.astype(I32))):
        for c in range(tm // LANES):
            oi_ref[f, pl.ds(sub * (tm // LANES) + c, 1), :] = vals[:, c * LANES:(c + 1) * LANES]
    w_row = lax.broadcasted_iota(I32, (LANES, tm), 0)
    w_rows = jnp.where(w_row == 0, w1, jnp.where(w_row == 1, w2, 0.0))
    of_ref[...] = jnp.transpose(w_rows)[:, :8]


def _router(ht, w_rg, b_rg, w_re, b_re):
    t = ht.shape[0] // SUBLANES
    d = w_rg.shape[0]
    tm = TM_ROUTE
    pad = ROUTE_ROWS - N_GROUPS - N_EXPERTS
    w = jnp.concatenate([w_rg.astype(F32), w_re.astype(F32), jnp.zeros((d, pad), F32)], axis=1).T
    b = jnp.concatenate([b_rg.astype(F32), b_re.astype(F32), jnp.zeros((pad,), F32)])[:, None]
    w_hi = w.astype(BF)
    w_lo = (w - w_hi.astype(F32)).astype(BF)
    tok = lambda i: (i, 0)
    const = lambda i: (0, 0)
    return pl.pallas_call(
        _router_kernel,
        grid=(t // tm,),
        in_specs=[pl.BlockSpec((tm * SUBLANES, LANES), tok),
                  pl.BlockSpec((ROUTE_ROWS, d), const),
                  pl.BlockSpec((ROUTE_ROWS, d), const),
                  pl.BlockSpec((ROUTE_ROWS, 1), const)],
        out_specs=[pl.BlockSpec((4, SUBLANES, LANES), lambda i: (0, i // ROUTE_STEPS_PER_PLANE, 0)),
                   pl.BlockSpec((tm, 8), tok),
                   pl.BlockSpec((N_EXPERTS, 1), const)],
        out_shape=[jax.ShapeDtypeStruct((4, t // LANES, LANES), I32), jax.ShapeDtypeStruct((t, 8), F32),
                   jax.ShapeDtypeStruct((N_EXPERTS, 1), F32)],
        scratch_shapes=[pltpu.VMEM((N_EXPERTS, 1), F32)],
        compiler_params=_cparams("arbitrary"),
        name="router",
    )(ht, w_hi, w_lo, b)


def _row_groups(n_rows, copies_of_row, start):
    def group(g, c):
        copies = []
        for u in range(ROW_GROUP):
            copies.extend(copies_of_row(g * ROW_GROUP + u))
        for k, (src, dst, sem) in enumerate(copies):
            if start:
                pltpu.async_copy(src, dst, sem, priority=k % DMA_QUEUES)
            else:
                pltpu.make_async_copy(src, dst, sem).wait()
        return c

    lax.fori_loop(0, n_rows // ROW_GROUP, group, 0)


def _dests_kernel(ps_ref, idx_ref, d_ref):
    for a in range(2):
        e = idx_ref[a]
        d = idx_ref[2 + a]
        for k in range(N_EXPERTS):
            d = d + jnp.where(e == k, ps_ref[k], 0)
        d_ref[a] = d


def _dests(idx, pad_start):
    planes = idx.shape[1]
    out = pl.pallas_call(
        _dests_kernel,
        grid_spec=pltpu.PrefetchScalarGridSpec(
            num_scalar_prefetch=1,
            grid=(1,),
            in_specs=[pl.BlockSpec(idx.shape, lambda i, ps: (0, 0, 0))],
            out_specs=pl.BlockSpec((2, planes, LANES), lambda i, ps: (0, 0, 0)),
        ),
        out_shape=jax.ShapeDtypeStruct((2, planes, LANES), I32),
        compiler_params=_cparams("arbitrary"),
        name="dests",
    )(pad_start, idx)
    return out[0].reshape(-1), out[1].reshape(-1)


def _pack_words(val):
    words = []
    for c in range(PACK_ROWS):
        hi = pltpu.bitcast(val[:, c * LANES:(c + 1) * LANES].astype(BF).astype(F32), U32)
        lo = pltpu.bitcast(val[:, (c + PACK_ROWS) * LANES:(c + PACK_ROWS + 1) * LANES].astype(BF).astype(F32), U32)
        words.append(hi | (lo >> 16))
    return words


def _unpack_words(words):
    his = [pltpu.bitcast(w & jnp.uint32(0xFFFF0000), F32) for w in words]
    los = [pltpu.bitcast(w << 16, F32) for w in words]
    return jnp.concatenate(his + los, axis=1)


def _load_packed(ref):
    n = ref.shape[0] // PACK_ROWS
    return [ref[pl.ds(c, n, stride=PACK_ROWS), :] for c in range(PACK_ROWS)]


def _store_packed(ref, words):
    n = ref.shape[0] // PACK_ROWS
    for c, w in enumerate(words):
        ref[pl.ds(c, n, stride=PACK_ROWS), :] = w


def _packed_row_copy(src_ref, src_row, dst_ref, dst_row, sem):
    src = pl.multiple_of(src_row * PACK_ROWS, PACK_ROWS)
    dst = pl.multiple_of(dst_row * PACK_ROWS, PACK_ROWS)
    return src_ref.at[pl.ds(src, PACK_ROWS)], dst_ref.at[pl.ds(dst, PACK_ROWS)], sem


def _dispatch_kernel(ps_ref, pe_ref, d1_ref, d2_ref, ht_ref, xg_ref, pk, zbuf, sem, zsem):
    i = pl.program_id(0)
    tm = d1_ref.shape[0]
    blk = MOE_BLOCK * PACK_ROWS
    n_blocks = xg_ref.shape[0] // blk

    @pl.when(i == 0)
    def _():
        zbuf[...] = jnp.zeros_like(zbuf)

        def zero_copy(block):
            return pltpu.make_async_copy(zbuf, xg_ref.at[pl.ds(pl.multiple_of(block * blk, blk), blk)], zsem)

        def seg_issue(e, c):
            @pl.when(pe_ref[e] > ps_ref[e])
            def _():
                zero_copy(pe_ref[e] // MOE_BLOCK - 1).start()
            return c

        def seg_drain(e, c):
            @pl.when(pe_ref[e] > ps_ref[e])
            def _():
                zero_copy(pe_ref[e] // MOE_BLOCK - 1).wait()
            return c

        def tail_issue(b, c):
            zero_copy(b).start()
            return c

        def tail_drain(b, c):
            zero_copy(b).wait()
            return c

        first_tail = pe_ref[N_EXPERTS - 1] // MOE_BLOCK
        lax.fori_loop(0, N_EXPERTS, seg_issue, 0)
        lax.fori_loop(first_tail, n_blocks, tail_issue, 0)
        lax.fori_loop(0, N_EXPERTS, seg_drain, 0)
        lax.fori_loop(first_tail, n_blocks, tail_drain, 0)

    _store_packed(pk, _pack_words(_load_tile_rows(ht_ref)))
    _row_groups(tm, lambda t: [_packed_row_copy(pk, t, xg_ref, d1_ref[t], sem),
                               _packed_row_copy(pk, t, xg_ref, d2_ref[t], sem)], True)
    _row_groups(tm, lambda t: [_packed_row_copy(pk, 0, xg_ref, 0, sem)] * 2, False)


def _dispatch(ht, d1, d2, pad_start, pad_end, n_pad):
    t = d1.shape[0]
    tm = TM_DISPATCH
    smem = lambda: pl.BlockSpec((tm,), lambda i, ps, pe: (i,), memory_space=pltpu.SMEM)
    return pl.pallas_call(
        _dispatch_kernel,
        grid_spec=pltpu.PrefetchScalarGridSpec(
            num_scalar_prefetch=2,
            grid=(t // tm,),
            in_specs=[smem(), smem(), pl.BlockSpec((tm * SUBLANES, LANES), lambda i, ps, pe: (i, 0))],
            out_specs=pl.BlockSpec(memory_space=pl.ANY),
            scratch_shapes=[pltpu.VMEM((tm * PACK_ROWS, LANES), U32), pltpu.VMEM((MOE_BLOCK * PACK_ROWS, LANES), U32),
                            pltpu.SemaphoreType.DMA(()), pltpu.SemaphoreType.DMA(())],
        ),
        out_shape=jax.ShapeDtypeStruct((n_pad * PACK_ROWS, LANES), U32),
        compiler_params=_cparams("arbitrary"),
        name="dispatch",
    )(pad_start, pad_end, d1, d2, ht)


def _expert_kernel(be_ref, nx_ref, nv_ref, nu_ref, x_ref, wg_hbm, wu_hbm, wd_hbm, y_ref,
                   wg_f, wu_f, wd_f, wg_s, wu_s, wd_s, sems):
    j = pl.program_id(0)
    stage = ((wg_hbm, wg_f, wg_s), (wu_hbm, wu_f, wu_s), (wd_hbm, wd_f, wd_s))

    def weight_copies(e):
        return [pltpu.make_async_copy(hbm.at[e], buf, sems.at[k]) for k, (hbm, buf, _) in enumerate(stage)]

    @pl.when(j < nu_ref[0])
    def _():
        e = be_ref[j]

        @pl.when(j == 0)
        def _():
            for cp in weight_copies(e):
                cp.start()

        @pl.when((j == 0) | (e != be_ref[jnp.maximum(j - 1, 0)]))
        def _():
            for cp, (_, buf, dst) in zip(weight_copies(e), stage):
                cp.wait()
                dst[...] = buf[...].astype(BF)
            nxt = nx_ref[e]

            @pl.when(nxt >= 0)
            def _():
                for cp in weight_copies(nxt):
                    cp.start()

        part_rows = x_ref.shape[0] // EXPERT_PARTS

        def run(sub_blocks):
            def up(x_sub):
                xb = _unpack_words(_load_packed(x_sub)).astype(BF)
                a = _dot(xb, wg_s[...])
                u = _dot(xb, wu_s[...])
                return ((a / (1.0 + jnp.exp(-a))) * u).astype(BF)

            mids = [up(x_sub) for x_sub, _ in sub_blocks]
            for (_, y_sub), mid in zip(sub_blocks, mids):
                _store_packed(y_sub, _pack_words(_dot(mid, wd_s[...])))

        def views(ref, start, rows, n):
            return [ref.at[pl.ds(start + h * rows, rows)] for h in range(n)]

        both = nv_ref[j] > MOE_BLOCK // EXPERT_PARTS

        @pl.when(both)
        def _():
            run(list(zip(views(x_ref, 0, part_rows, EXPERT_PARTS), views(y_ref, 0, part_rows, EXPERT_PARTS))))

        @pl.when(jnp.logical_not(both))
        def _():
            rows = part_rows // EXPERT_SUBBLOCKS
            run(list(zip(views(x_ref, 0, rows, EXPERT_SUBBLOCKS), views(y_ref, 0, rows, EXPERT_SUBBLOCKS))))
            tail = y_ref.at[pl.ds(part_rows, part_rows)]
            tail[...] = jnp.zeros_like(tail)

    @pl.when(j >= nu_ref[0])
    def _():
        y_ref[...] = jnp.zeros_like(y_ref)


def _experts(xg, blk_expert, next_expert, blk_valid, n_used, w_eg, w_eu, w_ed):
    blk = MOE_BLOCK * PACK_ROWS
    nb = xg.shape[0] // blk
    d, de = w_eg.shape[1], w_eg.shape[2]
    rows = lambda j, be, nx, nv, nu: (jnp.minimum(j, nu[0] - 1), 0)
    hbm = pl.BlockSpec(memory_space=pl.ANY)
    return pl.pallas_call(
        _expert_kernel,
        grid_spec=pltpu.PrefetchScalarGridSpec(
            num_scalar_prefetch=4,
            grid=(nb,),
            in_specs=[pl.BlockSpec((blk, LANES), rows), hbm, hbm, hbm],
            out_specs=pl.BlockSpec((blk, LANES), lambda j, be, nx, nv, nu: (j, 0)),
            scratch_shapes=[pltpu.VMEM((d, de), F32), pltpu.VMEM((d, de), F32), pltpu.VMEM((de, d), F32),
                            pltpu.VMEM((d, de), BF), pltpu.VMEM((d, de), BF), pltpu.VMEM((de, d), BF),
                            pltpu.SemaphoreType.DMA((3,))],
        ),
        out_shape=jax.ShapeDtypeStruct(xg.shape, U32),
        compiler_params=_cparams("arbitrary"),
        name="experts",
    )(blk_expert, next_expert, blk_valid, n_used, xg, w_eg, w_eu, w_ed)


def _combine_kernel(d1_ref, d2_ref, d1_next_ref, d2_next_ref, wt_ref, ht_ref, yb_ref, g_ref, b_ref, o_ref,
                    buf, sems):
    i = pl.program_id(0)
    n = pl.num_programs(0)
    tm = d1_ref.shape[0]

    def gather_tile(da_ref, db_ref, slot, start):
        def copies(t):
            if not start:
                return [_packed_row_copy(yb_ref, 0, buf.at[slot, a], 0, sems.at[slot]) for a in range(2)]
            return [_packed_row_copy(yb_ref, da_ref[t], buf.at[slot, 0], t, sems.at[slot]),
                    _packed_row_copy(yb_ref, db_ref[t], buf.at[slot, 1], t, sems.at[slot])]
        _row_groups(tm, copies, start)

    def gathered(slot, a):
        return _unpack_words(_load_packed(buf.at[slot, a]))

    slot = i % 2

    @pl.when(i == 0)
    def _():
        gather_tile(d1_ref, d2_ref, 0, True)

    @pl.when(i + 1 < n)
    def _():
        gather_tile(d1_next_ref, d2_next_ref, 1 - slot, True)

    gather_tile(d1_ref, d2_ref, slot, False)
    wt = wt_ref[...]
    ffn = wt[:, 0:1] * gathered(slot, 0) + wt[:, 1:2] * gathered(slot, 1)
    o_ref[...] = _layer_norm(ALPHA * _load_tile_rows(ht_ref) + ffn, g_ref[...], b_ref[...])


def _combine(ht, yb, d1, d2, wts, ln_g, ln_b):
    t = d1.shape[0]
    d = SUBLANES * LANES
    tm = TM_COMBINE
    n = t // tm
    tok = lambda i: (i, 0)
    const = lambda i: (0, 0)
    cur = lambda: pl.BlockSpec((tm,), lambda i: (i,), memory_space=pltpu.SMEM)
    nxt = lambda: pl.BlockSpec((tm,), lambda i: (jnp.minimum(i + 1, n - 1),), memory_space=pltpu.SMEM)
    return pl.pallas_call(
        _combine_kernel,
        grid=(n,),
        in_specs=[cur(), cur(), nxt(), nxt(),
                  pl.BlockSpec((tm, 8), tok),
                  pl.BlockSpec((tm * SUBLANES, LANES), tok),
                  pl.BlockSpec(memory_space=pl.ANY),
                  pl.BlockSpec((1, d), const), pl.BlockSpec((1, d), const)],
        out_specs=pl.BlockSpec((tm, d), tok),
        out_shape=jax.ShapeDtypeStruct((t, d), F32),
        scratch_shapes=[pltpu.VMEM((2, 2, tm * PACK_ROWS, LANES), U32), pltpu.SemaphoreType.DMA((2,))],
        compiler_params=_cparams("arbitrary"),
        name="combine",
    )(d1, d2, d1, d2, wts, ht, yb, ln_g, ln_b)


def _token_mixing(x2d, mem2d, batch, seq, m_len, w_in, w_gate, b_gate, w_mem_kv, rpb, w_fo, w_na, w_mo, w_out,
                  ln_g, ln_b):
    k_mem, v_mem = _memkv(mem2d, w_mem_kv.astype(BF), m_len)
    u_f, k, q3, v3, mo = _proj(x2d, w_in, k_mem, v_mem, seq, m_len)
    fo = _fourier(u_f, batch, seq)
    na = _natten(q3, k, v3, rpb, batch, seq)
    return _merge(x2d, fo, na, mo, w_gate.astype(BF), b_gate.astype(F32)[None, :], w_fo.astype(BF),
                  w_na.astype(BF), w_mo.astype(BF), w_out.astype(BF), ln_g.astype(F32)[None, :],
                  ln_b.astype(F32)[None, :])


def _moe(ht, w_rg, b_rg, w_re, b_re, w_eg, w_eu, w_ed, ln_g, ln_b):
    t = ht.shape[0] // SUBLANES
    idx, wts, cnt = _router(ht, w_rg, b_rg, w_re, b_re)
    counts = cnt[:, 0].astype(I32)
    padded = (counts + MOE_BLOCK - 1) // MOE_BLOCK * MOE_BLOCK
    pad_end = jnp.cumsum(padded).astype(I32)
    pad_start = pad_end - padded
    n_pad = 2 * t + N_EXPERTS * MOE_BLOCK
    nb = n_pad // MOE_BLOCK
    blk_start = jnp.arange(nb, dtype=I32) * MOE_BLOCK
    blk_expert = jnp.minimum(jnp.sum((pad_end[None, :] <= blk_start[:, None]).astype(I32), axis=1), N_EXPERTS - 1)
    n_used = pad_end[-1:] // MOE_BLOCK
    seg_left = counts[None, :] - (blk_start[:, None] - pad_start[None, :])
    own = blk_expert[:, None] == jnp.arange(N_EXPERTS, dtype=I32)[None, :]
    blk_valid = jnp.clip(jnp.sum(jnp.where(own, seg_left, 0), axis=1), 0, MOE_BLOCK).astype(I32)
    ids = jnp.arange(N_EXPERTS, dtype=I32)
    later_used = (padded[None, :] > 0) & (ids[None, :] > ids[:, None])
    next_expert = jnp.min(jnp.where(later_used, ids[None, :], N_EXPERTS), axis=1)
    next_expert = jnp.where(next_expert == N_EXPERTS, -1, next_expert).astype(I32)
    d1, d2 = _dests(idx, pad_start)
    xg = _dispatch(ht, d1, d2, pad_start, pad_end, n_pad)
    yb = _experts(xg, blk_expert, next_expert, blk_valid, n_used, w_eg, w_eu, w_ed)
    return _combine(ht, yb, d1, d2, wts, ln_g.astype(F32)[None, :], ln_b.astype(F32)[None, :])


def kernel(x, mem, w_in, w_gate, b_gate, w_mem_kv, rpb, w_fourier_o, w_na_o, w_mem_o, w_out, ln1_g, ln1_b,
           w_router_group, b_router_group, w_router_expert, b_router_expert, w_exp_gate, w_exp_up,
           w_exp_down, ln2_g, ln2_b):
    batch, seq, d = x.shape
    m_len = mem.shape[1]
    x2d = x.reshape(batch * seq, d)
    mem2d = mem.reshape(batch * m_len, d)
    for l in range(w_in.shape[0]):
        ht = _token_mixing(x2d, mem2d, batch, seq, m_len, w_in[l], w_gate[l], b_gate[l], w_mem_kv[l], rpb[l],
                           w_fourier_o[l], w_na_o[l], w_mem_o[l], w_out[l], ln1_g[l], ln1_b[l])
        x2d = _moe(ht, w_router_group[l], b_router_group[l], w_router_expert[l], b_router_expert[l],
                   w_exp_gate[l], w_exp_up[l], w_exp_down[l], ln2_g[l], ln2_b[l])
    return x2d.reshape(batch, seq, d)
```
